```python
import jax, jax.numpy as jnp
from jax import lax
import numpy as np

D_MODEL = 1024
BATCH = 32
SEQ = 256
DEPTH = 2
DEC_BATCH = 4
DEC_SEQ = 1024
PAST_LEN = 512

GRID_W = 64
HEAD_DIM = 64
FOURIER_W = D_MODEL // 4
FOURIER_GROUPS = FOURIER_W // HEAD_DIM
HGRN_W = D_MODEL // 4
HGRN_HEADS = HGRN_W // HEAD_DIM
ATTN_W = D_MODEL // 2
ATTN_HEADS = ATTN_W // HEAD_DIM
ATTN_KV_HEADS = 2
ATTN_GROUP = ATTN_HEADS // ATTN_KV_HEADS
KV_W = ATTN_KV_HEADS * HEAD_DIM
MIX_W = FOURIER_W + HGRN_W + ATTN_W
SPLIT_SIZES = (FOURIER_W, HGRN_W, HGRN_W, HGRN_W, HGRN_W, HGRN_W, ATTN_W, KV_W, KV_W)
IN_W = sum(SPLIT_SIZES)
WINDOW = 128
BLOCK = 128
CHUNK = 16
D_FF = 2816
ROPE_BASE = 10000.0
EPS = 1e-6
N_MOD = 9
NEG_BIG = -1e30
LB_FLOOR = 1e-30

kernel_name = 'hybrid_fourier_hgrn2_swa_diffusion_step'


def rmsnorm(x, w):
    xf = x.astype(jnp.float32)
    y = xf * lax.rsqrt(jnp.mean(xf * xf, axis=-1, keepdims=True) + EPS)
    return (y * w.astype(jnp.float32)).astype(x.dtype)


def modulate(x, g, shift, scale):
    return rmsnorm(x, g) * (1 + scale) + shift


def adaln(cond, w_ada, b_ada):
    m = jax.nn.silu(cond) @ w_ada + b_ada
    return m.reshape(cond.shape[0], 1, N_MOD, D_MODEL)


def swiglu(h, w_gu, w_down):
    g, u = jnp.split(h @ w_gu, 2, axis=-1)
    return (jax.nn.silu(g) * u) @ w_down


def split_in(z):
    idx = np.cumsum(SPLIT_SIZES)[:-1].tolist()
    return jnp.split(z, idx, axis=-1)


def fourier_mix(u):
    B, L, _ = u.shape
    ug = u.astype(jnp.float32).reshape(B, L, FOURIER_GROUPS, HEAD_DIM)
    f = jnp.fft.fft2(ug, axes=(1, 3), norm='ortho')
    return jnp.real(f).reshape(B, L, FOURIER_W).astype(u.dtype)


def gla_chunk_scan(q, k, v, logf, s0):
    B, L, H, DK = q.shape
    n = L // CHUNK
    rs = lambda t: t.reshape(B, n, CHUNK, H, t.shape[-1])
    qc, kc, vc, gc = rs(q), rs(k), rs(v), rs(logf)
    b = jnp.cumsum(gc, axis=2)
    bT = b[:, :, -1]
    causal = jnp.tril(jnp.ones((CHUNK, CHUNK), dtype=bool))[:, :, None, None]
    diff = b[:, :, :, None] - b[:, :, None, :]
    dec = jnp.where(causal, jnp.exp(jnp.where(causal, diff, 0.0)), 0.0)
    A = jnp.einsum('bnihk,bnijhk,bnjhk->bnhij', qc, dec, kc)
    o_intra = jnp.einsum('bnhij,bnjhv->bnihv', A, vc)
    q_in = qc * jnp.exp(b)
    k_out = kc * jnp.exp(bT[:, :, None] - b)

    def step(S, xs):
        qi, ki, vi, dT = xs
        o = jnp.einsum('bchk,bhkv->bchv', qi, S)
        S = S * jnp.exp(dT)[..., None] + jnp.einsum('bchk,bchv->bhkv', ki, vi)
        return S, o

    xs = (jnp.moveaxis(q_in, 1, 0), jnp.moveaxis(k_out, 1, 0),
          jnp.moveaxis(vc, 1, 0), jnp.moveaxis(bT, 1, 0))
    s_fin, o_inter = lax.scan(step, s0, xs)
    o = o_intra + jnp.moveaxis(o_inter, 0, 1)
    return o.reshape(B, L, H, v.shape[-1]), s_fin


def hgrn_mixer(hq, hi, hf_fwd, hf_bwd, hg, lb, norm_w, s0_fwd, s0_bwd):
    B, L, _ = hq.shape
    f32 = jnp.float32
    heads = lambda t: t.astype(f32).reshape(B, L, HGRN_HEADS, HEAD_DIM)
    q = heads(hq) * (HEAD_DIM ** -0.5)
    v = heads(hi)

    def gates(hf, lbd):
        z = hf.astype(f32)
        logf = jnp.logaddexp(jnp.log(jnp.maximum(lbd, LB_FLOOR)), jnp.log1p(-lbd) + jax.nn.log_sigmoid(z))
        kk = (1.0 - lbd) * jax.nn.sigmoid(-z)
        return heads(kk), heads(logf)

    k_f, g_f = gates(hf_fwd, lb[0])
    k_b, g_b = gates(hf_bwd, lb[1])
    o_f, s_f = gla_chunk_scan(q, k_f, v, g_f, s0_fwd.astype(f32))
    flip = lambda t: jnp.flip(t, axis=1)
    o_b, s_b = gla_chunk_scan(flip(q), flip(k_b), flip(v), flip(g_b), s0_bwd.astype(f32))
    o = o_f + flip(o_b)
    o = rmsnorm(o, norm_w.reshape(HGRN_HEADS, HEAD_DIM)) * jax.nn.silu(heads(hg))
    return o.reshape(B, L, HGRN_W).astype(hq.dtype), jnp.stack([s_f, s_b], axis=1)


def qkv_heads(aq, ak, av, q_norm, k_norm):
    B, L, _ = aq.shape
    q = rmsnorm(aq.reshape(B, L, ATTN_HEADS, HEAD_DIM), q_norm)
    q = q.reshape(B, L, ATTN_KV_HEADS, ATTN_GROUP, HEAD_DIM)
    k = rmsnorm(ak.reshape(B, L, ATTN_KV_HEADS, HEAD_DIM), k_norm)
    v = av.reshape(B, L, ATTN_KV_HEADS, HEAD_DIM)
    return q, k, v


def axial_rope(x):
    T = x.shape[1]
    rows = T // GRID_W
    row = jnp.repeat(jnp.arange(rows), GRID_W)
    col = jnp.tile(jnp.arange(GRID_W), rows)
    half = HEAD_DIM // 2
    inv = ROPE_BASE ** (-jnp.arange(0, half, 2, dtype=jnp.float32) / half)
    bshape = (T,) + (1,) * (x.ndim - 3) + (half // 2,)

    def rot(xa, pos):
        ang = pos.astype(jnp.float32)[:, None] * inv[None]
        cos = jnp.cos(ang).reshape(bshape)
        sin = jnp.sin(ang).reshape(bshape)
        x1, x2 = jnp.split(xa.astype(jnp.float32), 2, axis=-1)
        return jnp.concatenate([x1 * cos - x2 * sin, x2 * cos + x1 * sin], axis=-1)

    xr, xc = jnp.split(x, 2, axis=-1)
    return jnp.concatenate([rot(xr, row), rot(xc, col)], axis=-1).astype(x.dtype)


def context_attention(q, k, v, sink):
    scale = HEAD_DIM ** -0.5
    s = jnp.einsum('bqkgd,bskd->bkgqs', q, k).astype(jnp.float32) * scale
    sk = jnp.broadcast_to(sink.astype(jnp.float32).reshape(ATTN_KV_HEADS, ATTN_GROUP, 1, 1), s.shape[:-1] + (1,))
    p = jax.nn.softmax(jnp.concatenate([s, sk], axis=-1), axis=-1)[..., :-1]
    o = jnp.einsum('bkgqs,bskd->bqkgd', p.astype(v.dtype), v)
    B, L = q.shape[:2]
    return o.reshape(B, L, ATTN_W)


def latent_attention(q, k, v, kc, vc, sink):
    scale = HEAD_DIM ** -0.5
    B, T = q.shape[:2]
    nb = T // BLOCK
    qb = q.reshape(B, nb, BLOCK, ATTN_KV_HEADS, ATTN_GROUP, HEAD_DIM)
    pad = lambda t: jnp.pad(t, ((0, 0), (BLOCK, BLOCK), (0, 0), (0, 0))).reshape(B, nb + 2, BLOCK, ATTN_KV_HEADS, HEAD_DIM)
    win = lambda t: jnp.concatenate([t[:, :-2], t[:, 1:-1], t[:, 2:]], axis=2)
    kw, vw = win(pad(k)), win(pad(v))
    qpos = jnp.arange(nb)[:, None] * BLOCK + jnp.arange(BLOCK)[None]
    kpos = (jnp.arange(nb)[:, None] - 1) * BLOCK + jnp.arange(3 * BLOCK)[None]
    mask = (jnp.abs(qpos[:, :, None] - kpos[:, None, :]) <= WINDOW) & (kpos[:, None, :] >= 0) & (kpos[:, None, :] < T)
    s_loc = jnp.einsum('bnqkgd,bnskd->bnkgqs', qb, kw).astype(jnp.float32) * scale
    s_loc = jnp.where(mask[None, :, None, None], s_loc, NEG_BIG)
    s_ctx = jnp.einsum('bnqkgd,bskd->bnkgqs', qb, kc).astype(jnp.float32) * scale
    sk = jnp.broadcast_to(sink.astype(jnp.float32).reshape(ATTN_KV_HEADS, ATTN_GROUP, 1, 1), s_loc.shape[:-1] + (1,))
    p = jax.nn.softmax(jnp.concatenate([s_loc, s_ctx, sk], axis=-1), axis=-1)
    n_loc = 3 * BLOCK
    n_ctx = kc.shape[1]
    p_loc = p[..., :n_loc].astype(v.dtype)
    p_ctx = p[..., n_loc:n_loc + n_ctx].astype(v.dtype)
    o = jnp.einsum('bnkgqs,bnskd->bnqkgd', p_loc, vw) + jnp.einsum('bnkgqs,bskd->bnqkgd', p_ctx, vc)
    return o.reshape(B, T, ATTN_W)


def context_layer(x, mod, p):
    B = x.shape[0]
    x = x + 0.5 * mod[:, :, 2] * swiglu(modulate(x, p['norm_ffn1'], mod[:, :, 0], mod[:, :, 1]), p['ffn1_w_gate_up'], p['ffn1_w_down'])
    h = modulate(x, p['norm_mix'], mod[:, :, 3], mod[:, :, 4])
    u, hq, hi, hff, hfb, hg, aq, ak, av = split_in(h @ p['w_in'])
    fo = fourier_mix(u)
    zero = jnp.zeros((B, HGRN_HEADS, HEAD_DIM, HEAD_DIM), jnp.float32)
    ho, s_hgrn = hgrn_mixer(hq, hi, hff, hfb, hg, p['lb'], p['hgrn_norm'], zero, zero)
    q, k, v = qkv_heads(aq, ak, av, p['q_norm'], p['k_norm'])
    ao = context_attention(q, k, v, p['attn_sink'])
    x = x + mod[:, :, 5] * (jnp.concatenate([fo, ho, ao], axis=-1) @ p['w_out'])
    x = x + 0.5 * mod[:, :, 8] * swiglu(modulate(x, p['norm_ffn2'], mod[:, :, 6], mod[:, :, 7]), p['ffn2_w_gate_up'], p['ffn2_w_down'])
    return x, k, v, s_hgrn


def latent_layer(x, mod, kc, vc, s_ctx, p):
    x = x + 0.5 * mod[:, :, 2] * swiglu(modulate(x, p['norm_ffn1'], mod[:, :, 0], mod[:, :, 1]), p['ffn1_w_gate_up'], p['ffn1_w_down'])
    h = modulate(x, p['norm_mix'], mod[:, :, 3], mod[:, :, 4])
    u, hq, hi, hff, hfb, hg, aq, ak, av = split_in(h @ p['w_in'])
    fo = fourier_mix(u)
    ho, _ = hgrn_mixer(hq, hi, hff, hfb, hg, p['lb'], p['hgrn_norm'], s_ctx[:, 0], s_ctx[:, 1])
    q, k, v = qkv_heads(aq, ak, av, p['q_norm'], p['k_norm'])
    q, k = axial_rope(q), axial_rope(k)
    ao = latent_attention(q, k, v, kc, vc, p['attn_sink'])
    x = x + mod[:, :, 5] * (jnp.concatenate([fo, ho, ao], axis=-1) @ p['w_out'])
    x = x + 0.5 * mod[:, :, 8] * swiglu(modulate(x, p['norm_ffn2'], mod[:, :, 6], mod[:, :, 7]), p['ffn2_w_gate_up'], p['ffn2_w_down'])
    return x


def setup_inputs(seed: int = 0) -> dict:
    key = jax.random.key(seed)
    ks = jax.random.split(key, 23)
    nrm = lambda k, shape, s: jax.random.normal(k, shape, jnp.float32) * s
    gain = lambda k, shape: 1.0 + 0.05 * jax.random.normal(k, shape, jnp.float32)
    return {
        'x_prompt': nrm(ks[0], (BATCH, SEQ, D_MODEL), 1.0),
        'x_sample': nrm(ks[1], (DEC_BATCH, DEC_SEQ, D_MODEL), 1.0),
        'c': nrm(ks[2], (DEC_BATCH, D_MODEL), 1.0),
        'cache_attn_k': nrm(ks[3], (DEC_BATCH, DEPTH, PAST_LEN, ATTN_KV_HEADS, HEAD_DIM), 1.0),
        'cache_attn_v': nrm(ks[4], (DEC_BATCH, DEPTH, PAST_LEN, ATTN_KV_HEADS, HEAD_DIM), 1.0),
        'state_hgrn': nrm(ks[5], (DEC_BATCH, DEPTH, 2, HGRN_HEADS, HEAD_DIM, HEAD_DIM), 0.5),
        'c_ctx': nrm(ks[6], (D_MODEL,), 1.0),
        'w_ada': nrm(ks[7], (DEPTH, D_MODEL, N_MOD * D_MODEL), 0.5 * D_MODEL ** -0.5),
        'b_ada': nrm(ks[8], (DEPTH, N_MOD * D_MODEL), 0.01),
        'norm_ffn1': gain(ks[9], (DEPTH, D_MODEL)),
        'norm_mix': gain(ks[10], (DEPTH, D_MODEL)),
        'norm_ffn2': gain(ks[11], (DEPTH, D_MODEL)),
        'ffn1_w_gate_up': nrm(ks[12], (DEPTH, D_MODEL, 2 * D_FF), D_MODEL ** -0.5),
        'ffn1_w_down': nrm(ks[13], (DEPTH, D_FF, D_MODEL), D_FF ** -0.5),
        'ffn2_w_gate_up': nrm(ks[14], (DEPTH, D_MODEL, 2 * D_FF), D_MODEL ** -0.5),
        'ffn2_w_down': nrm(ks[15], (DEPTH, D_FF, D_MODEL), D_FF ** -0.5),
        'w_in': nrm(ks[16], (DEPTH, D_MODEL, IN_W), D_MODEL ** -0.5),
        'w_out': nrm(ks[17], (DEPTH, MIX_W, D_MODEL), MIX_W ** -0.5),
        'hgrn_lower_bounds': nrm(ks[18], (DEPTH, 2, HGRN_W), 1.0),
        'hgrn_norm': gain(ks[19], (DEPTH, HGRN_W)),
        'q_norm': gain(ks[20], (DEPTH, HEAD_DIM)),
        'k_norm': gain(ks[21], (DEPTH, HEAD_DIM)),
        'attn_sink': nrm(ks[22], (DEPTH, ATTN_HEADS), 0.5),
    }


def reference(x_prompt, x_sample, c, cache_attn_k, cache_attn_v, state_hgrn, c_ctx, w_ada, b_ada,
              norm_ffn1, norm_mix, norm_ffn2, ffn1_w_gate_up, ffn1_w_down, ffn2_w_gate_up, ffn2_w_down,
              w_in, w_out, hgrn_lower_bounds, hgrn_norm, q_norm, k_norm, attn_sink):
    lb_soft = jax.nn.softmax(hgrn_lower_bounds.astype(jnp.float32), axis=0)
    lower_bounds = jnp.cumsum(lb_soft, axis=0) - lb_soft[0]
    y_p = x_prompt
    y_s = x_sample
    k_list, v_list, s_list = [], [], []
    for l in range(DEPTH):
        p = {
            'norm_ffn1': norm_ffn1[l], 'norm_mix': norm_mix[l], 'norm_ffn2': norm_ffn2[l],
            'ffn1_w_gate_up': ffn1_w_gate_up[l], 'ffn1_w_down': ffn1_w_down[l],
            'ffn2_w_gate_up': ffn2_w_gate_up[l], 'ffn2_w_down': ffn2_w_down[l],
            'w_in': w_in[l], 'w_out': w_out[l], 'lb': lower_bounds[l], 'hgrn_norm': hgrn_norm[l],
            'q_norm': q_norm[l], 'k_norm': k_norm[l], 'attn_sink': attn_sink[l],
        }
        mod_ctx = adaln(c_ctx[None], w_ada[l], b_ada[l])
        mod_lat = adaln(c, w_ada[l], b_ada[l])
        y_p, k_l, v_l, s_l = context_layer(y_p, mod_ctx, p)
        y_s = latent_layer(y_s, mod_lat, cache_attn_k[:, l], cache_attn_v[:, l], state_hgrn[:, l], p)
        k_list.append(k_l)
        v_list.append(v_l)
        s_list.append(s_l)
    new_cache_attn_k = jnp.stack(k_list, axis=1)
    new_cache_attn_v = jnp.stack(v_list, axis=1)
    new_state_hgrn = jnp.stack(s_list, axis=1)
    return (y_p, y_s, new_cache_attn_k, new_cache_attn_v, new_state_hgrn)
```

```python
import functools

import numpy as np
import jax
import jax.numpy as jnp
from jax import lax
from jax.experimental import pallas as pl
from jax.experimental.pallas import tpu as pltpu

F32 = jnp.float32
BF16 = jnp.bfloat16

D_MODEL = 1024
DEPTH = 2
HEAD_DIM = 64
FOURIER_W = 256
HGRN_W = 256
HGRN_HEADS = 4
ATTN_W = 512
ATTN_HEADS = 8
ATTN_KV_HEADS = 2
ATTN_GROUP = 4
KV_W = 128
MIX_W = 1024
IN_W = 2304
ZH_W = 5 * HGRN_W
WINDOW = 128
ATTN_BLOCK = 128
CHUNK = 16
D_FF = 2816
GRID_W = 64
ROPE_BASE = 10000.0
EPS = 1e-6
N_MOD = 9
NEG_BIG = -1e30
LB_FLOOR = 1e-30

VMEM_LIMIT_BYTES = 56 * 1024 * 1024

ADA_ROWS = 8
ADA_TN = 1536
FFN_TM = 512
FFN_FC = 256
HGRN_TILE = 128
WIN_KEYS = 3 * ATTN_BLOCK


def _params(*semantics):
    return pltpu.CompilerParams(dimension_semantics=semantics, vmem_limit_bytes=VMEM_LIMIT_BYTES)


def _mod_norm(x, norm_w, shift, scale):
    ms = jnp.mean(x * x, axis=-1, keepdims=True)
    y = x * lax.rsqrt(ms + EPS) * norm_w
    return y * (1.0 + scale) + shift


def _head_ones(width):
    r = lax.broadcasted_iota(jnp.int32, (width, width), 0) // HEAD_DIM
    c = lax.broadcasted_iota(jnp.int32, (width, width), 1) // HEAD_DIM
    return jnp.where(r == c, 1.0, 0.0).astype(BF16)


def _head_rmsnorm(x, w):
    sq = (x * x).astype(BF16)
    ms = jnp.dot(sq, _head_ones(x.shape[-1]), preferred_element_type=F32) * (1.0 / HEAD_DIM)
    return x * lax.rsqrt(ms + EPS) * w


def _ada_kernel(cond_ref, w_ref, b_ref, o_ref):
    cnd = cond_ref[...]
    act = (cnd * jax.nn.sigmoid(cnd)).astype(BF16)
    o_ref[...] = jnp.dot(act, w_ref[...].astype(BF16), preferred_element_type=F32) + b_ref[...]


def _ada_call(cond, w_ada, b_ada):
    n_out = N_MOD * D_MODEL
    return pl.pallas_call(
        _ada_kernel,
        grid=(DEPTH, n_out // ADA_TN),
        in_specs=[
            pl.BlockSpec((ADA_ROWS, D_MODEL), lambda l, j: (0, 0)),
            pl.BlockSpec((None, D_MODEL, ADA_TN), lambda l, j: (l, 0, j)),
            pl.BlockSpec((None, 1, ADA_TN), lambda l, j: (l, 0, j)),
        ],
        out_specs=pl.BlockSpec((None, ADA_ROWS, ADA_TN), lambda l, j: (l, 0, j)),
        out_shape=jax.ShapeDtypeStruct((DEPTH, ADA_ROWS, n_out), F32),
        compiler_params=_params("arbitrary", "arbitrary"),
        name="adaln",
    )(cond, w_ada, b_ada.reshape(DEPTH, 1, n_out))


def _swiglu_chunks(h, wgu_ref, wd_ref):
    acc = None
    for c in range(D_FF // FFN_FC):
        g = jnp.dot(h, wgu_ref[:, c * FFN_FC:(c + 1) * FFN_FC], preferred_element_type=F32)
        u = jnp.dot(h, wgu_ref[:, D_FF + c * FFN_FC:D_FF + (c + 1) * FFN_FC], preferred_element_type=F32)
        act = (g * jax.nn.sigmoid(g) * u).astype(BF16)
        part = jnp.dot(act, wd_ref[c * FFN_FC:(c + 1) * FFN_FC, :], preferred_element_type=F32)
        acc = part if acc is None else acc + part
    return acc


def _mod_spec(layer, mod_row0, rows_per_mod, tm):
    return pl.BlockSpec((None, None, N_MOD, D_MODEL),
                        lambda i, *_: (layer, mod_row0 + (i * tm) // rows_per_mod, 0, 0))


def _resident(shape, layer):
    return pl.BlockSpec((None,) + shape, lambda i: (layer,) + (0,) * len(shape), pipeline_mode=pl.Buffered(1))


def _ffn_in_kernel(x_ref, mod_ref, n1_ref, nm_ref, wgu_ref, wd_ref, win_ref,
                   x1_ref, zu_ref, zh_ref, zq_ref, zkv_ref):
    x = x_ref[...]
    h = _mod_norm(x, n1_ref[...], mod_ref[0:1, :], mod_ref[1:2, :]).astype(BF16)
    x1 = x + 0.5 * mod_ref[2:3, :] * _swiglu_chunks(h, wgu_ref, wd_ref)
    x1_ref[...] = x1
    h2 = _mod_norm(x1, nm_ref[...], mod_ref[3:4, :], mod_ref[4:5, :]).astype(BF16)
    z = jnp.dot(h2, win_ref[...], preferred_element_type=F32)
    zu_ref[...] = z[:, :FOURIER_W]
    zh_ref[...] = z[:, FOURIER_W:FOURIER_W + ZH_W]
    zq_ref[...] = z[:, FOURIER_W + ZH_W:FOURIER_W + ZH_W + ATTN_W]
    zkv_ref[...] = z[:, FOURIER_W + ZH_W + ATTN_W:]


def _ffn_in_call(x, mod, n1, nm, w_gu, w_down, w_in, *, layer, mod_row0, rows_per_mod):
    n = x.shape[0]
    tm = FFN_TM
    widths = (D_MODEL, FOURIER_W, ZH_W, ATTN_W, 2 * KV_W)
    return pl.pallas_call(
        _ffn_in_kernel,
        grid=(n // tm,),
        in_specs=[
            pl.BlockSpec((tm, D_MODEL), lambda i: (i, 0)),
            _mod_spec(layer, mod_row0, rows_per_mod, tm),
            pl.BlockSpec((None, 1, D_MODEL), lambda i: (layer, 0, 0)),
            pl.BlockSpec((None, 1, D_MODEL), lambda i: (layer, 0, 0)),
            _resident((D_MODEL, 2 * D_FF), layer),
            _resident((D_FF, D_MODEL), layer),
            _resident((D_MODEL, IN_W), layer),
        ],
        out_specs=[pl.BlockSpec((tm, w), lambda i: (i, 0)) for w in widths],
        out_shape=[jax.ShapeDtypeStruct((n, w), F32) for w in widths],
        compiler_params=_params("arbitrary"),
        name="ffn_in",
    )(x, mod, n1, nm, w_gu, w_down, w_in)


def _out_ffn_kernel(x_ref, mod_ref, n2_ref, fo_ref, ho_ref, ao_ref, wout_ref, wgu_ref, wd_ref, o_ref):
    y = jnp.dot(fo_ref[...].astype(BF16), wout_ref[0:FOURIER_W, :], preferred_element_type=F32)
    y += jnp.dot(ho_ref[...].astype(BF16), wout_ref[FOURIER_W:FOURIER_W + HGRN_W, :], preferred_element_type=F32)
    y += jnp.dot(ao_ref[...].astype(BF16), wout_ref[FOURIER_W + HGRN_W:, :], preferred_element_type=F32)
    xm = x_ref[...] + mod_ref[5:6, :] * y
    h = _mod_norm(xm, n2_ref[...], mod_ref[6:7, :], mod_ref[7:8, :]).astype(BF16)
    o_ref[...] = xm + 0.5 * mod_ref[8:9, :] * _swiglu_chunks(h, wgu_ref, wd_ref)


def _out_ffn_call(x, mod, n2, fo, ho, ao, w_out, w_gu, w_down, *, layer, mod_row0, rows_per_mod):
    n = x.shape[0]
    tm = FFN_TM
    return pl.pallas_call(
        _out_ffn_kernel,
        grid=(n // tm,),
        in_specs=[
            pl.BlockSpec((tm, D_MODEL), lambda i: (i, 0)),
            _mod_spec(layer, mod_row0, rows_per_mod, tm),
            pl.BlockSpec((None, 1, D_MODEL), lambda i: (layer, 0, 0)),
            pl.BlockSpec((tm, FOURIER_W), lambda i: (i, 0)),
            pl.BlockSpec((tm, HGRN_W), lambda i: (i, 0)),
            pl.BlockSpec((tm, ATTN_W), lambda i: (i, 0)),
            _resident((MIX_W, D_MODEL), layer),
            _resident((D_MODEL, 2 * D_FF), layer),
            _resident((D_FF, D_MODEL), layer),
        ],
        out_specs=pl.BlockSpec((tm, D_MODEL), lambda i: (i, 0)),
        out_shape=jax.ShapeDtypeStruct((n, D_MODEL), F32),
        compiler_params=_params("arbitrary"),
        name="out_ffn",
    )(x, mod, n2, fo, ho, ao, w_out, w_gu, w_down)


def _dft_tables(seq_len):
    c = np.arange(HEAD_DIM)
    ang_c = 2.0 * np.pi * np.outer(c, c) / HEAD_DIM
    eye = np.eye(FOURIER_W // HEAD_DIM)
    chan = np.concatenate([np.kron(eye, np.cos(ang_c)), np.kron(eye, np.sin(ang_c))], axis=1) / np.sqrt(HEAD_DIM)
    n = np.arange(seq_len)
    ang_l = 2.0 * np.pi * (np.outer(n, n) % seq_len) / seq_len
    pos = np.concatenate([np.cos(ang_l), -np.sin(ang_l)], axis=1) / np.sqrt(seq_len)
    return jnp.asarray(chan, dtype=F32), jnp.asarray(pos, dtype=F32)


def _fourier_kernel(u_ref, chan_ref, pos_ref, o_ref):
    t = jnp.dot(u_ref[...].astype(BF16), chan_ref[...].astype(BF16), preferred_element_type=F32)
    stacked = jnp.concatenate([t[:, :FOURIER_W], t[:, FOURIER_W:]], axis=0).astype(BF16)
    o_ref[...] = jnp.dot(pos_ref[...].astype(BF16), stacked, preferred_element_type=F32)


def _fourier_call(zu, seq_len):
    n = zu.shape[0]
    chan, pos = _dft_tables(seq_len)
    return pl.pallas_call(
        _fourier_kernel,
        grid=(n // seq_len,),
        in_specs=[
            pl.BlockSpec((seq_len, FOURIER_W), lambda b: (b, 0)),
            pl.BlockSpec((FOURIER_W, 2 * FOURIER_W), lambda b: (0, 0), pipeline_mode=pl.Buffered(1)),
            pl.BlockSpec((seq_len, 2 * seq_len), lambda b: (0, 0), pipeline_mode=pl.Buffered(1)),
        ],
        out_specs=pl.BlockSpec((seq_len, FOURIER_W), lambda b: (b, 0)),
        out_shape=jax.ShapeDtypeStruct((n, FOURIER_W), F32),
        compiler_params=_params("arbitrary"),
        name="fourier",
    )(zu, chan, pos)


def _split3(x):
    hi = x.astype(BF16)
    r1 = x - hi.astype(F32)
    mid = r1.astype(BF16)
    lo = (r1 - mid.astype(F32)).astype(BF16)
    return hi, mid, lo


def _hgrn_kernel(*refs, seq_len, layer, has_s0, emit_state):
    refs = list(refs)
    zh_ref, lb_ref, nw_ref = refs[:3]
    pos = 3
    s0_ref = sf_ref = None
    if has_s0:
        s0_ref = refs[pos]
        pos += 1
    ho_ref = refs[pos]
    pos += 1
    if emit_state:
        sf_ref = refs[pos]
        pos += 1
    qin_ref, dec_ref, kvt_ref, st_ref, oacc_ref = refs[pos:]

    width = HGRN_W
    n_chunks = seq_len // CHUNK
    n_tiles = seq_len // HGRN_TILE
    cpt = HGRN_TILE // CHUNK

    raw = lb_ref[...]
    e = jnp.exp(raw - jnp.max(raw, axis=0, keepdims=True))
    soft = e / jnp.sum(e, axis=0, keepdims=True)
    cum = soft[0]
    for l in range(1, layer + 1):
        cum = cum + soft[l]
    lower = cum - soft[0]

    ones_bd = _head_ones(width)
    ti = lax.broadcasted_iota(jnp.int32, (HGRN_TILE, HGRN_TILE), 0)
    tj = lax.broadcasted_iota(jnp.int32, (HGRN_TILE, HGRN_TILE), 1)
    same_chunk = (ti // CHUNK) == (tj // CHUNK)
    tri = [jnp.where(same_chunk & (tj <= ti), 1.0, 0.0).astype(BF16),
           jnp.where(same_chunk & (tj >= ti), 1.0, 0.0).astype(BF16)]
    row_in_chunk = lax.broadcasted_iota(jnp.int32, (1, CHUNK, 1), 1)
    lane_head = lax.broadcasted_iota(jnp.int32, (HEAD_DIM, width), 1) // HEAD_DIM
    bd_mask = (lax.broadcasted_iota(jnp.int32, (width, width), 0) // HEAD_DIM
               == lax.broadcasted_iota(jnp.int32, (width, width), 1) // HEAD_DIM)
    tile_lane_chunk = lax.broadcasted_iota(jnp.int32, (width, HGRN_TILE), 1) // CHUNK
    nt_dims = (((1,), (1,)), ((), ()))

    def tile_body(t, carry):
        rows = pl.ds(pl.multiple_of(t * HGRN_TILE, HGRN_TILE), HGRN_TILE)
        q = zh_ref[rows, 0:width] * (HEAD_DIM ** -0.5)
        v = zh_ref[rows, width:2 * width]
        q3 = q.reshape(cpt, CHUNK, width)
        v3 = v.reshape(cpt, CHUNK, width)
        vt = v.T
        o_acc = jnp.zeros((cpt, CHUNK, width), F32)
        for direction in range(2):
            fwd = direction == 0
            z = zh_ref[rows, (2 + direction) * width:(3 + direction) * width]
            lbd = lower[direction:direction + 1, :]
            log_lb = jnp.log(jnp.maximum(lbd, LB_FLOOR))
            log_sig = jnp.minimum(z, 0.0) - jnp.log1p(jnp.exp(-jnp.abs(z)))
            other = jnp.log1p(-lbd) + log_sig
            logf = jnp.maximum(log_lb, other) + jnp.log1p(jnp.exp(-jnp.abs(log_lb - other)))
            kk = (1.0 - lbd) * jax.nn.sigmoid(-z)

            pieces = jnp.concatenate(_split3(logf), axis=1)
            acc = jnp.dot(tri[direction], pieces, preferred_element_type=F32)
            b = acc[:, :width] + acc[:, width:2 * width] + acc[:, 2 * width:]
            b3 = b.reshape(cpt, CHUNK, width)
            k3 = kk.reshape(cpt, CHUNK, width)
            edge = CHUNK - 1 if fwd else 0
            b_tot = b3[:, edge:edge + 1, :]

            for jj in range(CHUNK):
                valid = (row_in_chunk >= jj) if fwd else (row_in_chunk <= jj)
                diff = jnp.where(valid, b3 - b3[:, jj:jj + 1, :], 0.0)
                p = jnp.where(valid, q3 * k3[:, jj:jj + 1, :] * jnp.exp(diff), 0.0)
                a = jnp.dot(p.reshape(HGRN_TILE, width).astype(BF16), ones_bd, preferred_element_type=F32)
                o_acc = o_acc + a.reshape(cpt, CHUNK, width) * v3[:, jj:jj + 1, :]

            qin_ref[direction, rows, :] = (q3 * jnp.exp(b3)).reshape(HGRN_TILE, width).astype(BF16)
            kout = (k3 * jnp.exp(b_tot - b3)).reshape(HGRN_TILE, width).astype(BF16)
            dec = jnp.exp(b_tot)
            for cc in range(cpt):
                nn = t * cpt + cc
                dec_ref[direction, nn] = dec[cc]
                vm = jnp.where(tile_lane_chunk == cc, vt, 0.0).astype(BF16)
                full = jnp.dot(vm, kout, preferred_element_type=F32)
                kvt = full[0:HEAD_DIM, :]
                for hh in range(1, HGRN_HEADS):
                    kvt = jnp.where(lane_head == hh, full[hh * HEAD_DIM:(hh + 1) * HEAD_DIM, :], kvt)
                kvt_ref[direction, nn] = kvt
        oacc_ref[rows, :] = o_acc.reshape(HGRN_TILE, width)
        return carry

    lax.fori_loop(0, n_tiles, tile_body, 0)

    for direction in range(2):
        st0 = s0_ref[direction] if has_s0 else jnp.zeros((HEAD_DIM, width), F32)

        def scan_step(i, st, direction=direction):
            nn = i if direction == 0 else n_chunks - 1 - i
            st_ref[direction, nn] = st
            return st * dec_ref[direction, nn] + kvt_ref[direction, nn]

        st_fin = lax.fori_loop(0, n_chunks, scan_step, st0)
        if emit_state:
            sf_ref[direction] = st_fin

    def out_body(t, carry):
        r0 = pl.multiple_of(t * HGRN_TILE, HGRN_TILE)
        rows = pl.ds(r0, HGRN_TILE)
        o = oacc_ref[rows, :]
        for direction in range(2):
            parts = []
            for cc in range(cpt):
                st = st_ref[direction, t * cpt + cc]
                bdt = jnp.where(bd_mask, jnp.concatenate([st] * HGRN_HEADS, axis=0), 0.0).astype(BF16)
                qin = qin_ref[direction, pl.ds(r0 + cc * CHUNK, CHUNK), :]
                parts.append(lax.dot_general(qin, bdt, nt_dims, preferred_element_type=F32))
            o = o + jnp.concatenate(parts, axis=0)
        hg = zh_ref[rows, 4 * width:5 * width]
        ho_ref[rows, :] = _head_rmsnorm(o, nw_ref[...]) * (hg * jax.nn.sigmoid(hg))
        return carry

    lax.fori_loop(0, n_tiles, out_body, 0)


def _hgrn_call(zh, lb_raw, norm_w, s0t, *, seq_len, layer, emit_state):
    n = zh.shape[0]
    n_seq = n // seq_len
    has_s0 = s0t is not None
    n_chunks = seq_len // CHUNK
    in_specs = [
        pl.BlockSpec((seq_len, ZH_W), lambda b: (b, 0)),
        pl.BlockSpec((DEPTH, 2, HGRN_W), lambda b: (0, 0, 0)),
        pl.BlockSpec((None, 1, HGRN_W), lambda b: (layer, 0, 0)),
    ]
    args = [zh, lb_raw, norm_w]
    if has_s0:
        in_specs.append(pl.BlockSpec((None, None, 2, HEAD_DIM, HGRN_W), lambda b: (b, layer, 0, 0, 0)))
        args.append(s0t)
    out_specs = [pl.BlockSpec((seq_len, HGRN_W), lambda b: (b, 0))]
    out_shape = [jax.ShapeDtypeStruct((n, HGRN_W), F32)]
    if emit_state:
        out_specs.append(pl.BlockSpec((None, 2, HEAD_DIM, HGRN_W), lambda b: (b, 0, 0, 0)))
        out_shape.append(jax.ShapeDtypeStruct((n_seq, 2, HEAD_DIM, HGRN_W), F32))
    return pl.pallas_call(
        functools.partial(_hgrn_kernel, seq_len=seq_len, layer=layer, has_s0=has_s0, emit_state=emit_state),
        grid=(n_seq,),
        in_specs=in_specs,
        out_specs=out_specs,
        out_shape=out_shape,
        scratch_shapes=[
            pltpu.VMEM((2, seq_len, HGRN_W), BF16),
            pltpu.VMEM((2, n_chunks, 1, HGRN_W), F32),
            pltpu.VMEM((2, n_chunks, HEAD_DIM, HGRN_W), F32),
            pltpu.VMEM((2, n_chunks, HEAD_DIM, HGRN_W), F32),
            pltpu.VMEM((seq_len, HGRN_W), F32),
        ],
        compiler_params=_params("arbitrary"),
        name="hgrn",
    )(*args)


def _sink_row(sink_ref, kv_head, rows):
    grp = lax.broadcasted_iota(jnp.int32, (1, ATTN_GROUP * rows), 1) // rows
    base = kv_head * ATTN_GROUP
    out = jnp.broadcast_to(sink_ref[:, base:base + 1], (1, ATTN_GROUP * rows))
    for g in range(1, ATTN_GROUP):
        out = jnp.where(grp == g, sink_ref[:, base + g:base + g + 1], out)
    return out


def _group_queries_t(qt, kv_head):
    base = kv_head * ATTN_GROUP
    return jnp.concatenate([qt[(base + g) * HEAD_DIM:(base + g + 1) * HEAD_DIM, :] for g in range(ATTN_GROUP)], axis=1)


def _ctx_attn_kernel(zq_ref, zkv_ref, qn_ref, kn_ref, sink_ref, ao_ref, kc_ref, vc_ref):
    rows = zq_ref.shape[0]
    q = _head_rmsnorm(zq_ref[...], qn_ref[...]) * (HEAD_DIM ** -0.5)
    k = _head_rmsnorm(zkv_ref[:, :KV_W], kn_ref[...])
    v = zkv_ref[:, KV_W:]
    kc_ref[...] = k
    vc_ref[...] = v
    qt = q.T.astype(BF16)
    kb = k.astype(BF16)
    vt = v.T.astype(BF16)
    out_t = []
    for kh in range(ATTN_KV_HEADS):
        hs = slice(kh * HEAD_DIM, (kh + 1) * HEAD_DIM)
        s = jnp.dot(kb[:, hs], _group_queries_t(qt, kh), preferred_element_type=F32)
        sink = _sink_row(sink_ref, kh, rows)
        m = jnp.maximum(jnp.max(s, axis=0, keepdims=True), sink)
        p = jnp.exp(s - m)
        denom = jnp.sum(p, axis=0, keepdims=True) + jnp.exp(sink - m)
        pn = (p * (1.0 / denom)).astype(BF16)
        o_t = jnp.dot(vt[hs, :], pn, preferred_element_type=F32)
        out_t.extend(o_t[:, g * rows:(g + 1) * rows] for g in range(ATTN_GROUP))
    ao_ref[...] = jnp.concatenate(out_t, axis=0).T


def _ctx_attn_call(zq, zkv, q_norm, k_norm, sink, *, seq_len, layer):
    n = zq.shape[0]
    return pl.pallas_call(
        _ctx_attn_kernel,
        grid=(n // seq_len,),
        in_specs=[
            pl.BlockSpec((seq_len, ATTN_W), lambda b: (b, 0)),
            pl.BlockSpec((seq_len, 2 * KV_W), lambda b: (b, 0)),
            pl.BlockSpec((None, 1, ATTN_W), lambda b: (layer, 0, 0)),
            pl.BlockSpec((None, 1, KV_W), lambda b: (layer, 0, 0)),
            pl.BlockSpec((None, 1, ATTN_HEADS), lambda b: (layer, 0, 0)),
        ],
        out_specs=[
            pl.BlockSpec((seq_len, ATTN_W), lambda b: (b, 0)),
            pl.BlockSpec((seq_len, KV_W), lambda b: (b, 0)),
            pl.BlockSpec((seq_len, KV_W), lambda b: (b, 0)),
        ],
        out_shape=[
            jax.ShapeDtypeStruct((n, ATTN_W), F32),
            jax.ShapeDtypeStruct((n, KV_W), F32),
            jax.ShapeDtypeStruct((n, KV_W), F32),
        ],
        compiler_params=_params("arbitrary"),
        name="ctx_attn",
    )(zq, zkv, q_norm, k_norm, sink)


def _rope_tables(seq_len, n_heads):
    t = np.arange(seq_len)
    row, col = t // GRID_W, t % GRID_W
    half = HEAD_DIM // 2
    inv = (ROPE_BASE ** (-np.arange(0, half, 2, dtype=np.float32) / half)).astype(np.float32)

    def one(pos):
        ang = pos.astype(np.float32)[:, None] * inv[None]
        c, s = np.cos(ang), np.sin(ang)
        return np.concatenate([c, c], axis=1), np.concatenate([-s, s], axis=1)

    cr, sr = one(row)
    cc, sc = one(col)
    cos = np.tile(np.concatenate([cr, cc], axis=1), (1, n_heads)).astype(np.float32)
    sin = np.tile(np.concatenate([sr, sc], axis=1), (1, n_heads)).astype(np.float32)
    return jnp.asarray(cos), jnp.asarray(sin)


def _rope(x, cos, sin_signed):
    width = x.shape[-1]
    lane = lax.broadcasted_iota(jnp.int32, x.shape, 1)
    quarter = HEAD_DIM // 4
    up = pltpu.roll(x, width - quarter, axis=1)
    down = pltpu.roll(x, quarter, axis=1)
    partner = jnp.where((lane % (2 * quarter)) < quarter, up, down)
    return x * cos + partner * sin_signed


def _lat_attn_kernel(zq_ref, zkv_ref, kc_ref, vc_ref, qn_ref, kn_ref, sink_ref,
                     cosq_ref, sinq_ref, cosk_ref, sink_tab_ref, ao_ref, *, seq_len):
    blk = pl.program_id(1)
    rows = ATTN_BLOCK
    q = _head_rmsnorm(zq_ref[...], qn_ref[...])
    q = _rope(q, cosq_ref[...], sinq_ref[...]) * (HEAD_DIM ** -0.5)
    start = jnp.clip((blk - 1) * ATTN_BLOCK, 0, seq_len - WIN_KEYS)
    start = pl.multiple_of(start, ATTN_BLOCK)
    kwin = _head_rmsnorm(zkv_ref[pl.ds(start, WIN_KEYS), 0:KV_W], kn_ref[...])
    kwin = _rope(kwin, cosk_ref[pl.ds(start, WIN_KEYS), :], sink_tab_ref[pl.ds(start, WIN_KEYS), :])
    vwin = zkv_ref[pl.ds(start, WIN_KEYS), KV_W:2 * KV_W]

    stacked = (WIN_KEYS, ATTN_GROUP * rows)
    kpos = start + lax.broadcasted_iota(jnp.int32, stacked, 0)
    qpos = blk * ATTN_BLOCK + lax.broadcasted_iota(jnp.int32, stacked, 1) % rows
    band = jnp.abs(qpos - kpos) <= WINDOW

    qt = q.T.astype(BF16)
    kwb = kwin.astype(BF16)
    kcb = kc_ref[...].astype(BF16)
    vwt = vwin.T.astype(BF16)
    vct = vc_ref[...].T.astype(BF16)
    out_t = []
    for kh in range(ATTN_KV_HEADS):
        hs = slice(kh * HEAD_DIM, (kh + 1) * HEAD_DIM)
        qg = _group_queries_t(qt, kh)
        s_loc = jnp.where(band, jnp.dot(kwb[:, hs], qg, preferred_element_type=F32), NEG_BIG)
        s_ctx = jnp.dot(kcb[:, hs], qg, preferred_element_type=F32)
        sink = _sink_row(sink_ref, kh, rows)
        m = jnp.maximum(jnp.maximum(jnp.max(s_loc, axis=0, keepdims=True),
                                    jnp.max(s_ctx, axis=0, keepdims=True)), sink)
        p_loc = jnp.exp(s_loc - m)
        p_ctx = jnp.exp(s_ctx - m)
        denom = (jnp.sum(p_loc, axis=0, keepdims=True) + jnp.sum(p_ctx, axis=0, keepdims=True)
                 + jnp.exp(sink - m))
        inv = 1.0 / denom
        o_t = (jnp.dot(vwt[hs, :], (p_loc * inv).astype(BF16), preferred_element_type=F32)
               + jnp.dot(vct[hs, :], (p_ctx * inv).astype(BF16), preferred_element_type=F32))
        out_t.extend(o_t[:, g * rows:(g + 1) * rows] for g in range(ATTN_GROUP))
    ao_ref[...] = jnp.concatenate(out_t, axis=0).T


def _lat_attn_call(zq, zkv, cache_k, cache_v, q_norm, k_norm, sink, *, seq_len, layer):
    n = zq.shape[0]
    n_blk = seq_len // ATTN_BLOCK
    past = cache_k.shape[2]
    cosq, sinq = _rope_tables(seq_len, ATTN_HEADS)
    cosk, sink_tab = _rope_tables(seq_len, ATTN_KV_HEADS)
    return pl.pallas_call(
        functools.partial(_lat_attn_kernel, seq_len=seq_len),
        grid=(n // seq_len, n_blk),
        in_specs=[
            pl.BlockSpec((ATTN_BLOCK, ATTN_W), lambda b, i: (b * n_blk + i, 0)),
            pl.BlockSpec((seq_len, 2 * KV_W), lambda b, i: (b, 0)),
            pl.BlockSpec((None, None, past, KV_W), lambda b, i: (b, layer, 0, 0)),
            pl.BlockSpec((None, None, past, KV_W), lambda b, i: (b, layer, 0, 0)),
            pl.BlockSpec((None, 1, ATTN_W), lambda b, i: (layer, 0, 0)),
            pl.BlockSpec((None, 1, KV_W), lambda b, i: (layer, 0, 0)),
            pl.BlockSpec((None, 1, ATTN_HEADS), lambda b, i: (layer, 0, 0)),
            pl.BlockSpec((ATTN_BLOCK, ATTN_W), lambda b, i: (i, 0)),
            pl.BlockSpec((ATTN_BLOCK, ATTN_W), lambda b, i: (i, 0)),
            pl.BlockSpec((seq_len, KV_W), lambda b, i: (0, 0)),
            pl.BlockSpec((seq_len, KV_W), lambda b, i: (0, 0)),
        ],
        out_specs=pl.BlockSpec((ATTN_BLOCK, ATTN_W), lambda b, i: (b * n_blk + i, 0)),
        out_shape=jax.ShapeDtypeStruct((n, ATTN_W), F32),
        compiler_params=_params("arbitrary", "arbitrary"),
        name="lat_attn",
    )(zq, zkv, cache_k, cache_v, q_norm, k_norm, sink, cosq, sinq, cosk, sink_tab)


def kernel(x_prompt, x_sample, c, cache_attn_k, cache_attn_v, state_hgrn, c_ctx, w_ada, b_ada, norm_ffn1, norm_mix, norm_ffn2, ffn1_w_gate_up, ffn1_w_down, ffn2_w_gate_up, ffn2_w_down, w_in, w_out, hgrn_lower_bounds, hgrn_norm, q_norm, k_norm, attn_sink):
    batch, seq, _ = x_prompt.shape
    dec_batch, dec_seq, _ = x_sample.shape
    past = cache_attn_k.shape[2]
    assert dec_batch + 1 <= ADA_ROWS

    cond = jnp.concatenate([c_ctx[None], c, jnp.zeros((ADA_ROWS - 1 - dec_batch, D_MODEL), F32)], axis=0)
    mod = _ada_call(cond, w_ada, b_ada).reshape(DEPTH, ADA_ROWS, N_MOD, D_MODEL)

    bf = lambda w: w.astype(BF16)
    w_gu1, w_d1, w_gu2, w_d2 = bf(ffn1_w_gate_up), bf(ffn1_w_down), bf(ffn2_w_gate_up), bf(ffn2_w_down)
    w_in_b, w_out_b = bf(w_in), bf(w_out)
    row3 = lambda w: w.reshape(DEPTH, 1, w.shape[-1])
    n1, nm, n2 = row3(norm_ffn1), row3(norm_mix), row3(norm_ffn2)
    hn = row3(hgrn_norm)
    qn = row3(jnp.tile(q_norm, (1, ATTN_HEADS)))
    kn = row3(jnp.tile(k_norm, (1, ATTN_KV_HEADS)))
    sink = row3(attn_sink)
    cache_k = cache_attn_k.reshape(dec_batch, DEPTH, past, KV_W)
    cache_v = cache_attn_v.reshape(dec_batch, DEPTH, past, KV_W)
    s0t = state_hgrn.transpose(0, 1, 2, 5, 3, 4).reshape(dec_batch, DEPTH, 2, HEAD_DIM, HGRN_W)

    streams = {
        "ctx": dict(x=x_prompt.reshape(batch * seq, D_MODEL), seq_len=seq, mod_row0=0, rows_per_mod=batch * seq),
        "lat": dict(x=x_sample.reshape(dec_batch * dec_seq, D_MODEL), seq_len=dec_seq, mod_row0=1, rows_per_mod=dec_seq),
    }
    k_list, v_list, s_list = [], [], []
    for layer in range(DEPTH):
        for name, st in streams.items():
            mk = dict(layer=layer, mod_row0=st["mod_row0"], rows_per_mod=st["rows_per_mod"])
            seq_len = st["seq_len"]
            x, zu, zh, zq, zkv = _ffn_in_call(st["x"], mod, n1, nm, w_gu1, w_d1, w_in_b, **mk)
            fo = _fourier_call(zu, seq_len)
            if name == "ctx":
                ho, sfin = _hgrn_call(zh, hgrn_lower_bounds, hn, None, seq_len=seq_len, layer=layer, emit_state=True)
                ao, k_new, v_new = _ctx_attn_call(zq, zkv, qn, kn, sink, seq_len=seq_len, layer=layer)
                k_list.append(k_new.reshape(batch, seq, ATTN_KV_HEADS, HEAD_DIM))
                v_list.append(v_new.reshape(batch, seq, ATTN_KV_HEADS, HEAD_DIM))
                s_list.append(sfin.reshape(batch, 2, HEAD_DIM, HGRN_HEADS, HEAD_DIM).transpose(0, 1, 3, 4, 2))
            else:
                (ho,) = _hgrn_call(zh, hgrn_lower_bounds, hn, s0t, seq_len=seq_len, layer=layer, emit_state=False)
                ao = _lat_attn_call(zq, zkv, cache_k, cache_v, qn, kn, sink, seq_len=seq_len, layer=layer)
            st["x"] = _out_ffn_call(x, mod, n2, fo, ho, ao, w_out_b, w_gu2, w_d2, **mk)

    y_p = streams["ctx"]["x"].reshape(batch, seq, D_MODEL)
    y_s = streams["lat"]["x"].reshape(dec_batch, dec_seq, D_MODEL)
    return (y_p, y_s, jnp.stack(k_list, axis=1), jnp.stack(v_list, axis=1), jnp.stack(s_list, axis=1))
```

```python
import functools

import numpy as np
import jax
import jax.numpy as jnp
from jax import lax
from jax.experimental import pallas as pl
from jax.experimental.pallas import tpu as pltpu

F32 = jnp.float32
BF16 = jnp.bfloat16

D_MODEL = 1024
DEPTH = 2
HEAD_DIM = 64
FOURIER_W = 256
HGRN_W = 256
HGRN_HEADS = 4
ATTN_W = 512
ATTN_HEADS = 8
ATTN_KV_HEADS = 2
ATTN_GROUP = 4
KV_W = 128
MIX_W = 1024
WINDOW = 128
ATTN_BLOCK = 128
D_FF = 2816
GRID_W = 64
ROPE_BASE = 10000.0
EPS = 1e-6
N_MOD = 9
NEG_BIG = -1e30
LB_FLOOR = 1e-30
LOG2_E = 1.4426950408889634

HG_IL = 2 * HGRN_W
ZH_W = HG_IL + HGRN_W + HG_IL + HGRN_W
IN_W = FOURIER_W + ZH_W + ATTN_W + 2 * KV_W

SUBLANES = 8
VMEM_LIMIT_BYTES = 56 * 1024 * 1024

ADA_ROWS = 8
ADA_TN = 1536
FFN_TM = 512
FFN_FC = 256
HGRN_TILE = 128
HGRN_LEVELS = (128, 64, 32, 16, 8, 4, 2)
WIN_KEYS = 3 * ATTN_BLOCK


def _params(*semantics):
    return pltpu.CompilerParams(dimension_semantics=semantics, vmem_limit_bytes=VMEM_LIMIT_BYTES)


def _mod_norm(x, norm_w, shift, scale):
    ms = jnp.mean(x * x, axis=-1, keepdims=True)
    y = x * lax.rsqrt(ms + EPS) * norm_w
    return y * (1.0 + scale) + shift


def _head_ones(width):
    r = lax.broadcasted_iota(jnp.int32, (width, width), 0) // HEAD_DIM
    c = lax.broadcasted_iota(jnp.int32, (width, width), 1) // HEAD_DIM
    return jnp.where(r == c, 1.0, 0.0).astype(BF16)


def _head_rmsnorm(x, w):
    sq = (x * x).astype(BF16)
    ms = jnp.dot(sq, _head_ones(x.shape[-1]), preferred_element_type=F32) * (1.0 / HEAD_DIM)
    return x * lax.rsqrt(ms + EPS) * w


def _ada_kernel(cond_ref, w_ref, b_ref, o_ref):
    cnd = cond_ref[...]
    act = (cnd * jax.nn.sigmoid(cnd)).astype(BF16)
    o_ref[...] = jnp.dot(act, w_ref[...].astype(BF16), preferred_element_type=F32) + b_ref[...]


def _ada_call(cond, w_ada, b_ada):
    n_out = N_MOD * D_MODEL
    return pl.pallas_call(
        _ada_kernel,
        grid=(DEPTH, n_out // ADA_TN),
        in_specs=[
            pl.BlockSpec((ADA_ROWS, D_MODEL), lambda l, j: (0, 0)),
            pl.BlockSpec((None, D_MODEL, ADA_TN), lambda l, j: (l, 0, j)),
            pl.BlockSpec((None, 1, ADA_TN), lambda l, j: (l, 0, j)),
        ],
        out_specs=pl.BlockSpec((None, ADA_ROWS, ADA_TN), lambda l, j: (l, 0, j)),
        out_shape=jax.ShapeDtypeStruct((DEPTH, ADA_ROWS, n_out), F32),
        compiler_params=_params("arbitrary", "arbitrary"),
        name="adaln",
    )(cond, w_ada, b_ada.reshape(DEPTH, 1, n_out))


def _swiglu_chunks(h, wgu_ref, wd_ref):
    acc = None
    for c in range(D_FF // FFN_FC):
        g = jnp.dot(h, wgu_ref[:, c * FFN_FC:(c + 1) * FFN_FC], preferred_element_type=F32)
        u = jnp.dot(h, wgu_ref[:, D_FF + c * FFN_FC:D_FF + (c + 1) * FFN_FC], preferred_element_type=F32)
        act = (g * jax.nn.sigmoid(g) * u).astype(BF16)
        part = jnp.dot(act, wd_ref[c * FFN_FC:(c + 1) * FFN_FC, :], preferred_element_type=F32)
        acc = part if acc is None else acc + part
    return acc


def _mod_spec(layer, mod_row0, rows_per_mod, tm):
    return pl.BlockSpec((None, None, N_MOD, D_MODEL),
                        lambda i, *_: (layer, mod_row0 + (i * tm) // rows_per_mod, 0, 0))


def _resident(shape, layer):
    return pl.BlockSpec((None,) + shape, lambda i: (layer,) + (0,) * len(shape), pipeline_mode=pl.Buffered(1))


def _ffn_in_kernel(x_ref, mod_ref, n1_ref, nm_ref, wgu_ref, wd_ref, win_ref,
                   x1_ref, zu_ref, zh_ref, zq_ref, zkv_ref):
    x = x_ref[...]
    h = _mod_norm(x, n1_ref[...], mod_ref[0:1, :], mod_ref[1:2, :]).astype(BF16)
    x1 = x + 0.5 * mod_ref[2:3, :] * _swiglu_chunks(h, wgu_ref, wd_ref)
    x1_ref[...] = x1
    h2 = _mod_norm(x1, nm_ref[...], mod_ref[3:4, :], mod_ref[4:5, :]).astype(BF16)
    z = jnp.dot(h2, win_ref[...], preferred_element_type=F32)
    zu_ref[...] = z[:, :FOURIER_W]
    zh_ref[...] = z[:, FOURIER_W:FOURIER_W + ZH_W]
    zq_ref[...] = z[:, FOURIER_W + ZH_W:FOURIER_W + ZH_W + ATTN_W]
    zkv_ref[...] = z[:, FOURIER_W + ZH_W + ATTN_W:]


def _ffn_in_call(x, mod, n1, nm, w_gu, w_down, w_in, *, layer, mod_row0, rows_per_mod):
    n = x.shape[0]
    tm = FFN_TM
    widths = (D_MODEL, FOURIER_W, ZH_W, ATTN_W, 2 * KV_W)
    return pl.pallas_call(
        _ffn_in_kernel,
        grid=(n // tm,),
        in_specs=[
            pl.BlockSpec((tm, D_MODEL), lambda i: (i, 0)),
            _mod_spec(layer, mod_row0, rows_per_mod, tm),
            pl.BlockSpec((None, 1, D_MODEL), lambda i: (layer, 0, 0)),
            pl.BlockSpec((None, 1, D_MODEL), lambda i: (layer, 0, 0)),
            _resident((D_MODEL, 2 * D_FF), layer),
            _resident((D_FF, D_MODEL), layer),
            _resident((D_MODEL, IN_W), layer),
        ],
        out_specs=[pl.BlockSpec((tm, w), lambda i: (i, 0)) for w in widths],
        out_shape=[jax.ShapeDtypeStruct((n, w), F32) for w in widths],
        compiler_params=_params("arbitrary"),
        name="ffn_in",
    )(x, mod, n1, nm, w_gu, w_down, w_in)


def _out_ffn_kernel(x_ref, mod_ref, n2_ref, fo_ref, ho_ref, ao_ref, wout_ref, wgu_ref, wd_ref, o_ref):
    y = jnp.dot(fo_ref[...].astype(BF16), wout_ref[0:FOURIER_W, :], preferred_element_type=F32)
    y += jnp.dot(ho_ref[...].astype(BF16), wout_ref[FOURIER_W:FOURIER_W + HGRN_W, :], preferred_element_type=F32)
    y += jnp.dot(ao_ref[...].astype(BF16), wout_ref[FOURIER_W + HGRN_W:, :], preferred_element_type=F32)
    xm = x_ref[...] + mod_ref[5:6, :] * y
    h = _mod_norm(xm, n2_ref[...], mod_ref[6:7, :], mod_ref[7:8, :]).astype(BF16)
    o_ref[...] = xm + 0.5 * mod_ref[8:9, :] * _swiglu_chunks(h, wgu_ref, wd_ref)


def _out_ffn_call(x, mod, n2, fo, ho, ao, w_out, w_gu, w_down, *, layer, mod_row0, rows_per_mod):
    n = x.shape[0]
    tm = FFN_TM
    return pl.pallas_call(
        _out_ffn_kernel,
        grid=(n // tm,),
        in_specs=[
            pl.BlockSpec((tm, D_MODEL), lambda i: (i, 0)),
            _mod_spec(layer, mod_row0, rows_per_mod, tm),
            pl.BlockSpec((None, 1, D_MODEL), lambda i: (layer, 0, 0)),
            pl.BlockSpec((tm, FOURIER_W), lambda i: (i, 0)),
            pl.BlockSpec((tm, HGRN_W), lambda i: (i, 0)),
            pl.BlockSpec((tm, ATTN_W), lambda i: (i, 0)),
            _resident((MIX_W, D_MODEL), layer),
            _resident((D_MODEL, 2 * D_FF), layer),
            _resident((D_FF, D_MODEL), layer),
        ],
        out_specs=pl.BlockSpec((tm, D_MODEL), lambda i: (i, 0)),
        out_shape=jax.ShapeDtypeStruct((n, D_MODEL), F32),
        compiler_params=_params("arbitrary"),
        name="out_ffn",
    )(x, mod, n2, fo, ho, ao, w_out, w_gu, w_down)


def _dft_tables(seq_len):
    c = np.arange(HEAD_DIM)
    ang_c = 2.0 * np.pi * np.outer(c, c) / HEAD_DIM
    eye = np.eye(FOURIER_W // HEAD_DIM)
    chan = np.concatenate([np.kron(eye, np.cos(ang_c)), np.kron(eye, np.sin(ang_c))], axis=1) / np.sqrt(HEAD_DIM)
    n = np.arange(seq_len)
    ang_l = 2.0 * np.pi * (np.outer(n, n) % seq_len) / seq_len
    pos = np.concatenate([np.cos(ang_l), -np.sin(ang_l)], axis=1) / np.sqrt(seq_len)
    return jnp.asarray(chan, dtype=F32), jnp.asarray(pos, dtype=F32)


def _fourier_kernel(u_ref, chan_ref, pos_ref, o_ref):
    t = jnp.dot(u_ref[...].astype(BF16), chan_ref[...].astype(BF16), preferred_element_type=F32)
    stacked = jnp.concatenate([t[:, :FOURIER_W], t[:, FOURIER_W:]], axis=0).astype(BF16)
    o_ref[...] = jnp.dot(pos_ref[...].astype(BF16), stacked, preferred_element_type=F32)


def _fourier_call(zu, seq_len):
    n = zu.shape[0]
    chan, pos = _dft_tables(seq_len)
    return pl.pallas_call(
        _fourier_kernel,
        grid=(n // seq_len,),
        in_specs=[
            pl.BlockSpec((seq_len, FOURIER_W), lambda b: (b, 0)),
            pl.BlockSpec((FOURIER_W, 2 * FOURIER_W), lambda b: (0, 0), pipeline_mode=pl.Buffered(1)),
            pl.BlockSpec((seq_len, 2 * seq_len), lambda b: (0, 0), pipeline_mode=pl.Buffered(1)),
        ],
        out_specs=pl.BlockSpec((seq_len, FOURIER_W), lambda b: (b, 0)),
        out_shape=jax.ShapeDtypeStruct((n, FOURIER_W), F32),
        compiler_params=_params("arbitrary"),
        name="fourier",
    )(zu, chan, pos)


def _split3(x):
    hi = x.astype(BF16)
    r1 = x - hi.astype(F32)
    mid = r1.astype(BF16)
    lo = (r1 - mid.astype(F32)).astype(BF16)
    return hi, mid, lo


def _level_mid(b, s, lane_bwd3):
    tile, lanes = b.shape
    if s >= SUBLANES:
        b3 = b.reshape(tile // s, s, lanes)
        mid = jnp.where(lane_bwd3, b3[:, s // 2:s // 2 + 1, :], b3[:, s // 2 - 1:s // 2, :])
        return jnp.broadcast_to(mid, b3.shape).reshape(tile, lanes)
    assert s == 4
    b3 = b.reshape(tile // SUBLANES, SUBLANES, lanes)
    sub = lax.broadcasted_iota(jnp.int32, (1, SUBLANES, 1), 1)
    lo = jnp.where(lane_bwd3, b3[:, 2:3, :], b3[:, 1:2, :])
    hi = jnp.where(lane_bwd3, b3[:, 6:7, :], b3[:, 5:6, :])
    return jnp.where(sub < 4, lo, hi).reshape(tile, lanes)


def _hgrn_kernel(*refs, seq_len, layer, has_s0, emit_state):
    refs = list(refs)
    zh_ref, lb_ref, nw_ref, tri_ref, ones_ref, sgn_ref, qm_ref = refs[:7]
    pos = 7
    s0_ref = sf_ref = None
    if has_s0:
        s0_ref = refs[pos]
        pos += 1
    ho_ref = refs[pos]
    pos += 1
    if emit_state:
        sf_ref = refs[pos]
        pos += 1
    qin_ref, dec_ref, kvt_ref, stf_ref, stb_ref, oacc_ref = refs[pos:]

    tile = HGRN_TILE
    n_tiles = seq_len // tile
    c_q, c_v, c_z, c_g = 0, HG_IL, HG_IL + HGRN_W, 2 * HG_IL + HGRN_W

    raw = lb_ref[...]
    e = jnp.exp(raw - jnp.max(raw, axis=0, keepdims=True))
    soft = e / jnp.sum(e, axis=0, keepdims=True)
    cum = soft[0]
    for l in range(1, layer + 1):
        cum = cum + soft[l]
    lbd = cum - soft[0]
    lb_floor = jnp.maximum(lbd, LB_FLOOR)
    one_m_lb = 1.0 - lbd

    lane1 = lax.broadcasted_iota(jnp.int32, (1, HG_IL), 1)
    lane_bwd1 = (lane1 // HEAD_DIM) % 2 == 1
    lane_bwd3 = lane_bwd1.reshape(1, 1, HG_IL)
    row = lax.broadcasted_iota(jnp.int32, (tile, HG_IL), 0)
    lane = lax.broadcasted_iota(jnp.int32, (tile, HG_IL), 1)
    lane_bwd = (lane // HEAD_DIM) % 2 == 1
    pair_xor = row ^ (lane % tile)
    ev_mask = (lax.broadcasted_iota(jnp.int32, (HGRN_HEADS * tile, HGRN_W), 0) // tile
               == lax.broadcasted_iota(jnp.int32, (HGRN_HEADS * tile, HGRN_W), 1) // HEAD_DIM)
    bd_mask = (lax.broadcasted_iota(jnp.int32, (HGRN_W, HG_IL), 0) // HEAD_DIM
               == lax.broadcasted_iota(jnp.int32, (HGRN_W, HG_IL), 1) // (2 * HEAD_DIM))
    lane_head = lax.broadcasted_iota(jnp.int32, (HEAD_DIM, HG_IL), 1) // (2 * HEAD_DIM)
    lane_bwd_s = (lax.broadcasted_iota(jnp.int32, (HEAD_DIM, HG_IL), 1) // HEAD_DIM) % 2 == 1
    nt_dims = (((1,), (1,)), ((), ()))
    head_w = 2 * HEAD_DIM

    def tile_body(t, carry):
        rows = pl.ds(pl.multiple_of(t * tile, tile), tile)
        q = zh_ref[rows, c_q:c_q + HG_IL] * (HEAD_DIM ** -0.5)
        v = zh_ref[rows, c_v:c_v + HGRN_W]
        z = zh_ref[rows, c_z:c_z + HG_IL]
        ez = jnp.exp(-jnp.abs(z))
        r = 1.0 / (1.0 + ez)
        tr = ez * r
        pos_z = z >= 0.0
        f = lb_floor + one_m_lb * jnp.where(pos_z, r, tr)
        logf = jnp.log(f)
        kk = one_m_lb * jnp.where(pos_z, tr, r)

        pieces = jnp.concatenate(_split3(logf), axis=1)
        acc = jnp.dot(tri_ref[...], pieces, preferred_element_type=F32)
        prefix = acc[:, :HG_IL] + acc[:, HG_IL:2 * HG_IL] + acc[:, 2 * HG_IL:]
        total = prefix[tile - 1:tile, :]
        b = jnp.where(lane_bwd, total - prefix + logf, prefix)

        a0 = jnp.dot((q * kk).astype(BF16), ones_ref[...], preferred_element_type=F32)
        o = a0 * v

        b2 = b * LOG2_E
        scores = None
        for li, s in enumerate(HGRN_LEVELS):
            qm = qm_ref[li]
            if s == 2:
                x = f * qm + (1.0 - qm)
            else:
                x = jnp.exp2(sgn_ref[li] * (b2 - _level_mid(b2, s, lane_bwd3)))
            xq = x * qm
            qt = (q * xq).astype(BF16)
            kt = (kk * (x - xq)).astype(BF16)
            sc = jnp.concatenate(
                [lax.dot_general(qt[:, h * head_w:(h + 1) * head_w], kt[:, h * head_w:(h + 1) * head_w],
                                 nt_dims, preferred_element_type=F32) for h in range(HGRN_HEADS)], axis=1)
            scores = sc if scores is None else jnp.where(pair_xor < s, sc, scores)
        ev = jnp.where(ev_mask, jnp.concatenate([v] * HGRN_HEADS, axis=0), 0.0).astype(BF16)
        o = o + jnp.dot(scores.astype(BF16), ev, preferred_element_type=F32)
        oacc_ref[rows, :] = o

        qin_ref[rows, :] = (q * jnp.exp(b)).astype(BF16)
        kout = (kk * jnp.exp(total - b)).astype(BF16)
        dec_ref[t] = jnp.exp(total)
        full = jnp.dot(v.T.astype(BF16), kout, preferred_element_type=F32)
        kvt = full[0:HEAD_DIM, :]
        for hh in range(1, HGRN_HEADS):
            kvt = jnp.where(lane_head == hh, full[hh * HEAD_DIM:(hh + 1) * HEAD_DIM, :], kvt)
        kvt_ref[t] = kvt
        return carry

    lax.fori_loop(0, n_tiles, tile_body, 0)

    st0 = s0_ref[...] if has_s0 else jnp.zeros((HEAD_DIM, HG_IL), F32)

    def scan_step(i, st):
        tf, tb = i, n_tiles - 1 - i
        stf_ref[tf] = st
        stb_ref[tb] = st
        dec = jnp.where(lane_bwd1, dec_ref[tb], dec_ref[tf])
        kv = jnp.where(lane_bwd_s, kvt_ref[tb], kvt_ref[tf])
        return st * dec + kv

    st_fin = lax.fori_loop(0, n_tiles, scan_step, st0)
    if emit_state:
        sf_ref[...] = st_fin

    def out_tile(t):
        rows = pl.ds(pl.multiple_of(t * tile, tile), tile)
        st = jnp.where(lane_bwd_s, stb_ref[t], stf_ref[t])
        bd = jnp.where(bd_mask, jnp.concatenate([st] * HGRN_HEADS, axis=0), 0.0).T.astype(BF16)
        o = oacc_ref[rows, :] + jnp.dot(qin_ref[rows, :], bd, preferred_element_type=F32)
        hg = zh_ref[rows, c_g:c_g + HGRN_W]
        ho_ref[rows, :] = _head_rmsnorm(o, nw_ref[...]) * (hg * jax.nn.sigmoid(hg))

    def out_body(i, carry):
        out_tile(2 * i)
        out_tile(2 * i + 1)
        return carry

    assert n_tiles % 2 == 0
    lax.fori_loop(0, n_tiles // 2, out_body, 0)


def _hgrn_consts():
    i = np.arange(HGRN_TILE)
    tri = (i[None, :] <= i[:, None]).astype(np.float32)
    r = np.arange(HG_IL) // (2 * HEAD_DIM)
    c = np.arange(HGRN_W) // HEAD_DIM
    ones = (r[:, None] == c[None, :]).astype(np.float32)
    lane_bwd = (np.arange(HG_IL) // HEAD_DIM) % 2 == 1
    qm = np.stack([(((i[:, None] % s) >= s // 2) != lane_bwd[None, :]) for s in HGRN_LEVELS]).astype(np.float32)
    return (jnp.asarray(tri, dtype=BF16), jnp.asarray(ones, dtype=BF16),
            jnp.asarray(2.0 * qm - 1.0, dtype=F32), jnp.asarray(qm, dtype=F32))


def _hgrn_call(zh, lb_il, norm_w, s0_il, *, seq_len, layer, emit_state):
    n = zh.shape[0]
    n_seq = n // seq_len
    has_s0 = s0_il is not None
    n_tiles = seq_len // HGRN_TILE
    tri, ones, lvl_sgn, lvl_qm = _hgrn_consts()
    n_lvl = len(HGRN_LEVELS)
    in_specs = [
        pl.BlockSpec((seq_len, ZH_W), lambda b: (b, 0)),
        pl.BlockSpec((DEPTH, 1, HG_IL), lambda b: (0, 0, 0)),
        pl.BlockSpec((None, 1, HGRN_W), lambda b: (layer, 0, 0)),
        pl.BlockSpec((HGRN_TILE, HGRN_TILE), lambda b: (0, 0)),
        pl.BlockSpec((HG_IL, HGRN_W), lambda b: (0, 0)),
        pl.BlockSpec((n_lvl, HGRN_TILE, HG_IL), lambda b: (0, 0, 0), pipeline_mode=pl.Buffered(1)),
        pl.BlockSpec((n_lvl, HGRN_TILE, HG_IL), lambda b: (0, 0, 0), pipeline_mode=pl.Buffered(1)),
    ]
    args = [zh, lb_il, norm_w, tri, ones, lvl_sgn, lvl_qm]
    if has_s0:
        in_specs.append(pl.BlockSpec((None, None, HEAD_DIM, HG_IL), lambda b: (b, layer, 0, 0)))
        args.append(s0_il)
    out_specs = [pl.BlockSpec((seq_len, HGRN_W), lambda b: (b, 0))]
    out_shape = [jax.ShapeDtypeStruct((n, HGRN_W), F32)]
    if emit_state:
        out_specs.append(pl.BlockSpec((None, HEAD_DIM, HG_IL), lambda b: (b, 0, 0)))
        out_shape.append(jax.ShapeDtypeStruct((n_seq, HEAD_DIM, HG_IL), F32))
    return pl.pallas_call(
        functools.partial(_hgrn_kernel, seq_len=seq_len, layer=layer, has_s0=has_s0, emit_state=emit_state),
        grid=(n_seq,),
        in_specs=in_specs,
        out_specs=out_specs,
        out_shape=out_shape,
        scratch_shapes=[
            pltpu.VMEM((seq_len, HG_IL), BF16),
            pltpu.VMEM((n_tiles, 1, HG_IL), F32),
            pltpu.VMEM((n_tiles, HEAD_DIM, HG_IL), F32),
            pltpu.VMEM((n_tiles, HEAD_DIM, HG_IL), F32),
            pltpu.VMEM((n_tiles, HEAD_DIM, HG_IL), F32),
            pltpu.VMEM((seq_len, HGRN_W), F32),
        ],
        compiler_params=_params("arbitrary"),
        name="hgrn",
    )(*args)


def _sink_row(sink_ref, kv_head, rows):
    grp = lax.broadcasted_iota(jnp.int32, (1, ATTN_GROUP * rows), 1) // rows
    base = kv_head * ATTN_GROUP
    out = jnp.broadcast_to(sink_ref[:, base:base + 1], (1, ATTN_GROUP * rows))
    for g in range(1, ATTN_GROUP):
        out = jnp.where(grp == g, sink_ref[:, base + g:base + g + 1], out)
    return out


def _group_queries_t(qt, kv_head):
    base = kv_head * ATTN_GROUP
    return jnp.concatenate([qt[(base + g) * HEAD_DIM:(base + g + 1) * HEAD_DIM, :] for g in range(ATTN_GROUP)], axis=1)


def _ctx_attn_kernel(zq_ref, zkv_ref, qn_ref, kn_ref, sink_ref, ao_ref, kc_ref, vc_ref):
    rows = zq_ref.shape[0]
    q = _head_rmsnorm(zq_ref[...], qn_ref[...]) * (HEAD_DIM ** -0.5)
    k = _head_rmsnorm(zkv_ref[:, :KV_W], kn_ref[...])
    v = zkv_ref[:, KV_W:]
    kc_ref[...] = k
    vc_ref[...] = v
    qt = q.T.astype(BF16)
    kb = k.astype(BF16)
    vt = v.T.astype(BF16)
    out_t = []
    for kh in range(ATTN_KV_HEADS):
        hs = slice(kh * HEAD_DIM, (kh + 1) * HEAD_DIM)
        s = jnp.dot(kb[:, hs], _group_queries_t(qt, kh), preferred_element_type=F32)
        sink = _sink_row(sink_ref, kh, rows)
        m = jnp.maximum(jnp.max(s, axis=0, keepdims=True), sink)
        p = jnp.exp(s - m)
        denom = jnp.sum(p, axis=0, keepdims=True) + jnp.exp(sink - m)
        pn = (p * (1.0 / denom)).astype(BF16)
        o_t = jnp.dot(vt[hs, :], pn, preferred_element_type=F32)
        out_t.extend(o_t[:, g * rows:(g + 1) * rows] for g in range(ATTN_GROUP))
    ao_ref[...] = jnp.concatenate(out_t, axis=0).T


def _ctx_attn_call(zq, zkv, q_norm, k_norm, sink, *, seq_len, layer):
    n = zq.shape[0]
    return pl.pallas_call(
        _ctx_attn_kernel,
        grid=(n // seq_len,),
        in_specs=[
            pl.BlockSpec((seq_len, ATTN_W), lambda b: (b, 0)),
            pl.BlockSpec((seq_len, 2 * KV_W), lambda b: (b, 0)),
            pl.BlockSpec((None, 1, ATTN_W), lambda b: (layer, 0, 0)),
            pl.BlockSpec((None, 1, KV_W), lambda b: (layer, 0, 0)),
            pl.BlockSpec((None, 1, ATTN_HEADS), lambda b: (layer, 0, 0)),
        ],
        out_specs=[
            pl.BlockSpec((seq_len, ATTN_W), lambda b: (b, 0)),
            pl.BlockSpec((seq_len, KV_W), lambda b: (b, 0)),
            pl.BlockSpec((seq_len, KV_W), lambda b: (b, 0)),
        ],
        out_shape=[
            jax.ShapeDtypeStruct((n, ATTN_W), F32),
            jax.ShapeDtypeStruct((n, KV_W), F32),
            jax.ShapeDtypeStruct((n, KV_W), F32),
        ],
        compiler_params=_params("arbitrary"),
        name="ctx_attn",
    )(zq, zkv, q_norm, k_norm, sink)


def _rope_tables(seq_len, n_heads):
    t = np.arange(seq_len)
    row, col = t // GRID_W, t % GRID_W
    half = HEAD_DIM // 2
    inv = (ROPE_BASE ** (-np.arange(0, half, 2, dtype=np.float32) / half)).astype(np.float32)

    def one(pos):
        ang = pos.astype(np.float32)[:, None] * inv[None]
        c, s = np.cos(ang), np.sin(ang)
        return np.concatenate([c, c], axis=1), np.concatenate([-s, s], axis=1)

    cr, sr = one(row)
    cc, sc = one(col)
    cos = np.tile(np.concatenate([cr, cc], axis=1), (1, n_heads)).astype(np.float32)
    sin = np.tile(np.concatenate([sr, sc], axis=1), (1, n_heads)).astype(np.float32)
    return jnp.asarray(cos), jnp.asarray(sin)


def _rope(x, cos, sin_signed):
    width = x.shape[-1]
    lane = lax.broadcasted_iota(jnp.int32, x.shape, 1)
    quarter = HEAD_DIM // 4
    up = pltpu.roll(x, width - quarter, axis=1)
    down = pltpu.roll(x, quarter, axis=1)
    partner = jnp.where((lane % (2 * quarter)) < quarter, up, down)
    return x * cos + partner * sin_signed


def _lat_attn_kernel(zq_ref, zkv_ref, kc_ref, vc_ref, qn_ref, kn_ref, sink_ref,
                     cosq_ref, sinq_ref, cosk_ref, sink_tab_ref, ao_ref, *, seq_len):
    blk = pl.program_id(1)
    rows = ATTN_BLOCK
    q = _head_rmsnorm(zq_ref[...], qn_ref[...])
    q = _rope(q, cosq_ref[...], sinq_ref[...]) * (HEAD_DIM ** -0.5)
    start = jnp.clip((blk - 1) * ATTN_BLOCK, 0, seq_len - WIN_KEYS)
    start = pl.multiple_of(start, ATTN_BLOCK)
    kwin = _head_rmsnorm(zkv_ref[pl.ds(start, WIN_KEYS), 0:KV_W], kn_ref[...])
    kwin = _rope(kwin, cosk_ref[pl.ds(start, WIN_KEYS), :], sink_tab_ref[pl.ds(start, WIN_KEYS), :])
    vwin = zkv_ref[pl.ds(start, WIN_KEYS), KV_W:2 * KV_W]

    stacked = (WIN_KEYS, ATTN_GROUP * rows)
    kpos = start + lax.broadcasted_iota(jnp.int32, stacked, 0)
    qpos = blk * ATTN_BLOCK + lax.broadcasted_iota(jnp.int32, stacked, 1) % rows
    band = jnp.abs(qpos - kpos) <= WINDOW

    qt = q.T.astype(BF16)
    kwb = kwin.astype(BF16)
    kcb = kc_ref[...].astype(BF16)
    vwt = vwin.T.astype(BF16)
    vct = vc_ref[...].T.astype(BF16)
    out_t = []
    for kh in range(ATTN_KV_HEADS):
        hs = slice(kh * HEAD_DIM, (kh + 1) * HEAD_DIM)
        qg = _group_queries_t(qt, kh)
        s_loc = jnp.where(band, jnp.dot(kwb[:, hs], qg, preferred_element_type=F32), NEG_BIG)
        s_ctx = jnp.dot(kcb[:, hs], qg, preferred_element_type=F32)
        sink = _sink_row(sink_ref, kh, rows)
        m = jnp.maximum(jnp.maximum(jnp.max(s_loc, axis=0, keepdims=True),
                                    jnp.max(s_ctx, axis=0, keepdims=True)), sink)
        p_loc = jnp.exp(s_loc - m)
        p_ctx = jnp.exp(s_ctx - m)
        denom = (jnp.sum(p_loc, axis=0, keepdims=True) + jnp.sum(p_ctx, axis=0, keepdims=True)
                 + jnp.exp(sink - m))
        inv = 1.0 / denom
        o_t = (jnp.dot(vwt[hs, :], (p_loc * inv).astype(BF16), preferred_element_type=F32)
               + jnp.dot(vct[hs, :], (p_ctx * inv).astype(BF16), preferred_element_type=F32))
        out_t.extend(o_t[:, g * rows:(g + 1) * rows] for g in range(ATTN_GROUP))
    ao_ref[...] = jnp.concatenate(out_t, axis=0).T


def _lat_attn_call(zq, zkv, cache_k, cache_v, q_norm, k_norm, sink, *, seq_len, layer):
    n = zq.shape[0]
    n_blk = seq_len // ATTN_BLOCK
    past = cache_k.shape[2]
    cosq, sinq = _rope_tables(seq_len, ATTN_HEADS)
    cosk, sink_tab = _rope_tables(seq_len, ATTN_KV_HEADS)
    return pl.pallas_call(
        functools.partial(_lat_attn_kernel, seq_len=seq_len),
        grid=(n // seq_len, n_blk),
        in_specs=[
            pl.BlockSpec((ATTN_BLOCK, ATTN_W), lambda b, i: (b * n_blk + i, 0)),
            pl.BlockSpec((seq_len, 2 * KV_W), lambda b, i: (b, 0)),
            pl.BlockSpec((None, None, past, KV_W), lambda b, i: (b, layer, 0, 0)),
            pl.BlockSpec((None, None, past, KV_W), lambda b, i: (b, layer, 0, 0)),
            pl.BlockSpec((None, 1, ATTN_W), lambda b, i: (layer, 0, 0)),
            pl.BlockSpec((None, 1, KV_W), lambda b, i: (layer, 0, 0)),
            pl.BlockSpec((None, 1, ATTN_HEADS), lambda b, i: (layer, 0, 0)),
            pl.BlockSpec((ATTN_BLOCK, ATTN_W), lambda b, i: (i, 0)),
            pl.BlockSpec((ATTN_BLOCK, ATTN_W), lambda b, i: (i, 0)),
            pl.BlockSpec((seq_len, KV_W), lambda b, i: (0, 0)),
            pl.BlockSpec((seq_len, KV_W), lambda b, i: (0, 0)),
        ],
        out_specs=pl.BlockSpec((ATTN_BLOCK, ATTN_W), lambda b, i: (b * n_blk + i, 0)),
        out_shape=jax.ShapeDtypeStruct((n, ATTN_W), F32),
        compiler_params=_params("arbitrary", "arbitrary"),
        name="lat_attn",
    )(zq, zkv, cache_k, cache_v, q_norm, k_norm, sink, cosq, sinq, cosk, sink_tab)


def _in_proj_columns():
    u0, hq0, hi0, hff0, hfb0, hg0, aq0 = 0, 256, 512, 768, 1024, 1280, 1536
    head = lambda base, h: np.arange(base + h * HEAD_DIM, base + (h + 1) * HEAD_DIM)
    q_il = np.concatenate([np.concatenate([head(hq0, h), head(hq0, h)]) for h in range(HGRN_HEADS)])
    z_il = np.concatenate([np.concatenate([head(hff0, h), head(hfb0, h)]) for h in range(HGRN_HEADS)])
    cols = np.concatenate([np.arange(u0, u0 + FOURIER_W), q_il, np.arange(hi0, hi0 + HGRN_W), z_il,
                           np.arange(hg0, hg0 + HGRN_W), np.arange(aq0, aq0 + ATTN_W + 2 * KV_W)])
    assert cols.shape[0] == IN_W
    return cols


def kernel(x_prompt, x_sample, c, cache_attn_k, cache_attn_v, state_hgrn, c_ctx, w_ada, b_ada, norm_ffn1, norm_mix, norm_ffn2, ffn1_w_gate_up, ffn1_w_down, ffn2_w_gate_up, ffn2_w_down, w_in, w_out, hgrn_lower_bounds, hgrn_norm, q_norm, k_norm, attn_sink):
    batch, seq, _ = x_prompt.shape
    dec_batch, dec_seq, _ = x_sample.shape
    past = cache_attn_k.shape[2]
    assert dec_batch + 1 <= ADA_ROWS

    cond = jnp.concatenate([c_ctx[None], c, jnp.zeros((ADA_ROWS - 1 - dec_batch, D_MODEL), F32)], axis=0)
    mod = _ada_call(cond, w_ada, b_ada).reshape(DEPTH, ADA_ROWS, N_MOD, D_MODEL)

    bf = lambda w: w.astype(BF16)
    w_gu1, w_d1, w_gu2, w_d2 = bf(ffn1_w_gate_up), bf(ffn1_w_down), bf(ffn2_w_gate_up), bf(ffn2_w_down)
    w_in_b, w_out_b = bf(w_in)[:, :, _in_proj_columns()], bf(w_out)
    row3 = lambda w: w.reshape(DEPTH, 1, w.shape[-1])
    n1, nm, n2 = row3(norm_ffn1), row3(norm_mix), row3(norm_ffn2)
    hn = row3(hgrn_norm)
    qn = row3(jnp.tile(q_norm, (1, ATTN_HEADS)))
    kn = row3(jnp.tile(k_norm, (1, ATTN_KV_HEADS)))
    sink = row3(attn_sink)
    cache_k = cache_attn_k.reshape(dec_batch, DEPTH, past, KV_W)
    cache_v = cache_attn_v.reshape(dec_batch, DEPTH, past, KV_W)
    lb_il = hgrn_lower_bounds.reshape(DEPTH, 2, HGRN_HEADS, HEAD_DIM).transpose(0, 2, 1, 3).reshape(DEPTH, 1, HG_IL)
    s0_il = state_hgrn.transpose(0, 1, 5, 3, 2, 4).reshape(dec_batch, DEPTH, HEAD_DIM, HG_IL)

    streams = {
        "ctx": dict(x=x_prompt.reshape(batch * seq, D_MODEL), seq_len=seq, mod_row0=0, rows_per_mod=batch * seq),
        "lat": dict(x=x_sample.reshape(dec_batch * dec_seq, D_MODEL), seq_len=dec_seq, mod_row0=1, rows_per_mod=dec_seq),
    }
    k_list, v_list, s_list = [], [], []
    for layer in range(DEPTH):
        for name, st in streams.items():
            mk = dict(layer=layer, mod_row0=st["mod_row0"], rows_per_mod=st["rows_per_mod"])
            seq_len = st["seq_len"]
            x, zu, zh, zq, zkv = _ffn_in_call(st["x"], mod, n1, nm, w_gu1, w_d1, w_in_b, **mk)
            fo = _fourier_call(zu, seq_len)
            if name == "ctx":
                ho, sfin = _hgrn_call(zh, lb_il, hn, None, seq_len=seq_len, layer=layer, emit_state=True)
                ao, k_new, v_new = _ctx_attn_call(zq, zkv, qn, kn, sink, seq_len=seq_len, layer=layer)
                k_list.append(k_new.reshape(batch, seq, ATTN_KV_HEADS, HEAD_DIM))
                v_list.append(v_new.reshape(batch, seq, ATTN_KV_HEADS, HEAD_DIM))
                s_list.append(sfin.reshape(batch, HEAD_DIM, HGRN_HEADS, 2, HEAD_DIM).transpose(0, 3, 2, 4, 1))
            else:
                (ho,) = _hgrn_call(zh, lb_il, hn, s0_il, seq_len=seq_len, layer=layer, emit_state=False)
                ao = _lat_attn_call(zq, zkv, cache_k, cache_v, qn, kn, sink, seq_len=seq_len, layer=layer)
            st["x"] = _out_ffn_call(x, mod, n2, fo, ho, ao, w_out_b, w_gu2, w_d2, **mk)

    y_p = streams["ctx"]["x"].reshape(batch, seq, D_MODEL)
    y_s = streams["lat"]["x"].reshape(dec_batch, dec_seq, D_MODEL)
    return (y_p, y_s, jnp.stack(k_list, axis=1), jnp.stack(v_list, axis=1), jnp.stack(s_list, axis=1))
```

```python
import functools

import numpy as np
import jax
import jax.numpy as jnp
from jax import lax
from jax.experimental import pallas as pl
from jax.experimental.pallas import tpu as pltpu

F32 = jnp.float32
BF16 = jnp.bfloat16

D_MODEL = 1024
DEPTH = 2
HEAD_DIM = 64
FOURIER_W = 256
HGRN_W = 256
HGRN_HEADS = 4
ATTN_W = 512
ATTN_HEADS = 8
ATTN_KV_HEADS = 2
ATTN_GROUP = 4
KV_W = 128
MIX_W = 1024
WINDOW = 128
ATTN_BLOCK = 128
D_FF = 2816
GRID_W = 64
ROPE_BASE = 10000.0
EPS = 1e-6
N_MOD = 9
NEG_BIG = -1e30
LB_FLOOR = 1e-30
LOG2_E = 1.4426950408889634

HG_IL = 2 * HGRN_W
ZH_W = HG_IL + HGRN_W + HG_IL + HGRN_W
IN_W = FOURIER_W + ZH_W + ATTN_W + 2 * KV_W

SUBLANES = 8
VMEM_LIMIT_BYTES = 56 * 1024 * 1024

ADA_ROWS = 8
ADA_TN = 1536
FFN_TM = 512
FFN_FC = 256
SMALL_SEQ_ROWS = 1024
HGRN_TILE = 128
HGRN_LEVELS = (128, 64, 32, 16, 8, 4, 2)
WIN_KEYS = 3 * ATTN_BLOCK


def _params(*semantics):
    return pltpu.CompilerParams(dimension_semantics=semantics, vmem_limit_bytes=VMEM_LIMIT_BYTES)


def _mod_norm(x, norm_w, shift, scale):
    ms = jnp.mean(x * x, axis=-1, keepdims=True)
    y = x * lax.rsqrt(ms + EPS) * norm_w
    return y * (1.0 + scale) + shift


def _head_ones(width):
    r = lax.broadcasted_iota(jnp.int32, (width, width), 0) // HEAD_DIM
    c = lax.broadcasted_iota(jnp.int32, (width, width), 1) // HEAD_DIM
    return jnp.where(r == c, 1.0, 0.0).astype(BF16)


def _head_rmsnorm(x, w):
    sq = (x * x).astype(BF16)
    ms = jnp.dot(sq, _head_ones(x.shape[-1]), preferred_element_type=F32) * (1.0 / HEAD_DIM)
    return x * lax.rsqrt(ms + EPS) * w


def _ada_kernel(cond_ref, w_ref, b_ref, o_ref):
    cnd = cond_ref[...]
    act = (cnd * jax.nn.sigmoid(cnd)).astype(BF16)
    o_ref[...] = jnp.dot(act, w_ref[...].astype(BF16), preferred_element_type=F32) + b_ref[...]


def _ada_call(cond, w_ada, b_ada):
    n_out = N_MOD * D_MODEL
    return pl.pallas_call(
        _ada_kernel,
        grid=(DEPTH, n_out // ADA_TN),
        in_specs=[
            pl.BlockSpec((ADA_ROWS, D_MODEL), lambda l, j: (0, 0)),
            pl.BlockSpec((None, D_MODEL, ADA_TN), lambda l, j: (l, 0, j)),
            pl.BlockSpec((None, 1, ADA_TN), lambda l, j: (l, 0, j)),
        ],
        out_specs=pl.BlockSpec((None, ADA_ROWS, ADA_TN), lambda l, j: (l, 0, j)),
        out_shape=jax.ShapeDtypeStruct((DEPTH, ADA_ROWS, n_out), F32),
        compiler_params=_params("arbitrary", "arbitrary"),
        name="adaln",
    )(cond, w_ada, b_ada.reshape(DEPTH, 1, n_out))


def _swiglu_chunks(h, wgu_ref, wd_ref):
    acc = None
    for c in range(D_FF // FFN_FC):
        g = jnp.dot(h, wgu_ref[:, c * FFN_FC:(c + 1) * FFN_FC], preferred_element_type=F32)
        u = jnp.dot(h, wgu_ref[:, D_FF + c * FFN_FC:D_FF + (c + 1) * FFN_FC], preferred_element_type=F32)
        act = (g * jax.nn.sigmoid(g) * u).astype(BF16)
        part = jnp.dot(act, wd_ref[c * FFN_FC:(c + 1) * FFN_FC, :], preferred_element_type=F32)
        acc = part if acc is None else acc + part
    return acc


def _mod_spec(layer, mod_row0, rows_per_mod, tm):
    return pl.BlockSpec((None, None, N_MOD, D_MODEL),
                        lambda i, *_: (layer, mod_row0 + (i * tm) // rows_per_mod, 0, 0))


def _resident(shape, layer):
    return pl.BlockSpec((None,) + shape, lambda i: (layer,) + (0,) * len(shape), pipeline_mode=pl.Buffered(1))


def _ffn_in_kernel(x_ref, mod_ref, n1_ref, nm_ref, wgu_ref, wd_ref, win_ref,
                   x1_ref, zu_ref, zh_ref, zq_ref, zkv_ref):
    x = x_ref[...]
    h = _mod_norm(x, n1_ref[...], mod_ref[0:1, :], mod_ref[1:2, :]).astype(BF16)
    x1 = x + 0.5 * mod_ref[2:3, :] * _swiglu_chunks(h, wgu_ref, wd_ref)
    x1_ref[...] = x1
    h2 = _mod_norm(x1, nm_ref[...], mod_ref[3:4, :], mod_ref[4:5, :]).astype(BF16)
    z = jnp.dot(h2, win_ref[...], preferred_element_type=F32)
    zu_ref[...] = z[:, :FOURIER_W]
    zh_ref[...] = z[:, FOURIER_W:FOURIER_W + ZH_W]
    zq_ref[...] = z[:, FOURIER_W + ZH_W:FOURIER_W + ZH_W + ATTN_W]
    zkv_ref[...] = z[:, FOURIER_W + ZH_W + ATTN_W:]


def _ffn_in_call(x, mod, n1, nm, w_gu, w_down, w_in, *, layer, mod_row0, rows_per_mod):
    n = x.shape[0]
    tm = FFN_TM
    widths = (D_MODEL, FOURIER_W, ZH_W, ATTN_W, 2 * KV_W)
    return pl.pallas_call(
        _ffn_in_kernel,
        grid=(n // tm,),
        in_specs=[
            pl.BlockSpec((tm, D_MODEL), lambda i: (i, 0)),
            _mod_spec(layer, mod_row0, rows_per_mod, tm),
            pl.BlockSpec((None, 1, D_MODEL), lambda i: (layer, 0, 0)),
            pl.BlockSpec((None, 1, D_MODEL), lambda i: (layer, 0, 0)),
            _resident((D_MODEL, 2 * D_FF), layer),
            _resident((D_FF, D_MODEL), layer),
            _resident((D_MODEL, IN_W), layer),
        ],
        out_specs=[pl.BlockSpec((tm, w), lambda i: (i, 0)) for w in widths],
        out_shape=[jax.ShapeDtypeStruct((n, w), F32) for w in widths],
        compiler_params=_params("arbitrary"),
        name="ffn_in",
    )(x, mod, n1, nm, w_gu, w_down, w_in)


def _out_ffn_kernel(x_ref, mod_ref, n2_ref, fo_ref, ho_ref, ao_ref, wout_ref, wgu_ref, wd_ref, o_ref):
    y = jnp.dot(fo_ref[...].astype(BF16), wout_ref[0:FOURIER_W, :], preferred_element_type=F32)
    y += jnp.dot(ho_ref[...].astype(BF16), wout_ref[FOURIER_W:FOURIER_W + HGRN_W, :], preferred_element_type=F32)
    y += jnp.dot(ao_ref[...].astype(BF16), wout_ref[FOURIER_W + HGRN_W:, :], preferred_element_type=F32)
    xm = x_ref[...] + mod_ref[5:6, :] * y
    h = _mod_norm(xm, n2_ref[...], mod_ref[6:7, :], mod_ref[7:8, :]).astype(BF16)
    o_ref[...] = xm + 0.5 * mod_ref[8:9, :] * _swiglu_chunks(h, wgu_ref, wd_ref)


def _out_ffn_call(x, mod, n2, fo, ho, ao, w_out, w_gu, w_down, *, layer, mod_row0, rows_per_mod):
    n = x.shape[0]
    tm = FFN_TM
    return pl.pallas_call(
        _out_ffn_kernel,
        grid=(n // tm,),
        in_specs=[
            pl.BlockSpec((tm, D_MODEL), lambda i: (i, 0)),
            _mod_spec(layer, mod_row0, rows_per_mod, tm),
            pl.BlockSpec((None, 1, D_MODEL), lambda i: (layer, 0, 0)),
            pl.BlockSpec((tm, FOURIER_W), lambda i: (i, 0)),
            pl.BlockSpec((tm, HGRN_W), lambda i: (i, 0)),
            pl.BlockSpec((tm, ATTN_W), lambda i: (i, 0)),
            _resident((MIX_W, D_MODEL), layer),
            _resident((D_MODEL, 2 * D_FF), layer),
            _resident((D_FF, D_MODEL), layer),
        ],
        out_specs=pl.BlockSpec((tm, D_MODEL), lambda i: (i, 0)),
        out_shape=jax.ShapeDtypeStruct((n, D_MODEL), F32),
        compiler_params=_params("arbitrary"),
        name="out_ffn",
    )(x, mod, n2, fo, ho, ao, w_out, w_gu, w_down)


def _dft_tables(seq_len):
    c = np.arange(HEAD_DIM)
    ang_c = 2.0 * np.pi * np.outer(c, c) / HEAD_DIM
    eye = np.eye(FOURIER_W // HEAD_DIM)
    chan = np.concatenate([np.kron(eye, np.cos(ang_c)), np.kron(eye, np.sin(ang_c))], axis=1) / np.sqrt(HEAD_DIM)
    n = np.arange(seq_len)
    ang_l = 2.0 * np.pi * (np.outer(n, n) % seq_len) / seq_len
    pos = np.concatenate([np.cos(ang_l), -np.sin(ang_l)], axis=1) / np.sqrt(seq_len)
    return jnp.asarray(chan, dtype=F32), jnp.asarray(pos, dtype=F32)


def _fourier_kernel(u_ref, chan_ref, pos_ref, o_ref, *, seq_len):
    t = jnp.dot(u_ref[...].astype(BF16), chan_ref[...].astype(BF16), preferred_element_type=F32)
    pos = pos_ref[...].astype(BF16)
    for r0 in range(0, u_ref.shape[0], seq_len):
        tb = t[r0:r0 + seq_len, :]
        stacked = jnp.concatenate([tb[:, :FOURIER_W], tb[:, FOURIER_W:]], axis=0).astype(BF16)
        o_ref[r0:r0 + seq_len, :] = jnp.dot(pos, stacked, preferred_element_type=F32)


def _fourier_call(zu, seq_len):
    n = zu.shape[0]
    chan, pos = _dft_tables(seq_len)
    rows = max(seq_len, SMALL_SEQ_ROWS)
    return pl.pallas_call(
        functools.partial(_fourier_kernel, seq_len=seq_len),
        grid=(n // rows,),
        in_specs=[
            pl.BlockSpec((rows, FOURIER_W), lambda b: (b, 0)),
            pl.BlockSpec((FOURIER_W, 2 * FOURIER_W), lambda b: (0, 0), pipeline_mode=pl.Buffered(1)),
            pl.BlockSpec((seq_len, 2 * seq_len), lambda b: (0, 0), pipeline_mode=pl.Buffered(1)),
        ],
        out_specs=pl.BlockSpec((rows, FOURIER_W), lambda b: (b, 0)),
        out_shape=jax.ShapeDtypeStruct((n, FOURIER_W), F32),
        compiler_params=_params("arbitrary"),
        name="fourier",
    )(zu, chan, pos)


def _split3(x):
    hi = x.astype(BF16)
    r1 = x - hi.astype(F32)
    mid = r1.astype(BF16)
    lo = (r1 - mid.astype(F32)).astype(BF16)
    return hi, mid, lo


def _level_mid(b, s, lane_bwd3):
    tile, lanes = b.shape
    if s >= SUBLANES:
        b3 = b.reshape(tile // s, s, lanes)
        mid = jnp.where(lane_bwd3, b3[:, s // 2:s // 2 + 1, :], b3[:, s // 2 - 1:s // 2, :])
        return jnp.broadcast_to(mid, b3.shape).reshape(tile, lanes)
    assert s == 4
    b3 = b.reshape(tile // SUBLANES, SUBLANES, lanes)
    sub = lax.broadcasted_iota(jnp.int32, (1, SUBLANES, 1), 1)
    lo = jnp.where(lane_bwd3, b3[:, 2:3, :], b3[:, 1:2, :])
    hi = jnp.where(lane_bwd3, b3[:, 6:7, :], b3[:, 5:6, :])
    return jnp.where(sub < 4, lo, hi).reshape(tile, lanes)


def _hgrn_kernel(*refs, seq_len, layer, has_s0, emit_state):
    refs = list(refs)
    zh_ref, lb_ref, nw_ref, tri_ref, ones_ref, sgn_ref, qm_ref = refs[:7]
    pos = 7
    s0_ref = sf_ref = None
    if has_s0:
        s0_ref = refs[pos]
        pos += 1
    ho_ref = refs[pos]
    pos += 1
    if emit_state:
        sf_ref = refs[pos]
        pos += 1
    qin_ref, dec_ref, kvt_ref, stf_ref, stb_ref, oacc_ref = refs[pos:]

    tile = HGRN_TILE
    n_tiles = seq_len // tile
    c_q, c_v, c_z, c_g = 0, HG_IL, HG_IL + HGRN_W, 2 * HG_IL + HGRN_W

    raw = lb_ref[...]
    e = jnp.exp(raw - jnp.max(raw, axis=0, keepdims=True))
    soft = e / jnp.sum(e, axis=0, keepdims=True)
    cum = soft[0]
    for l in range(1, layer + 1):
        cum = cum + soft[l]
    lbd = cum - soft[0]
    lb_floor = jnp.maximum(lbd, LB_FLOOR)
    one_m_lb = 1.0 - lbd

    lane1 = lax.broadcasted_iota(jnp.int32, (1, HG_IL), 1)
    lane_bwd1 = (lane1 // HEAD_DIM) % 2 == 1
    lane_bwd3 = lane_bwd1.reshape(1, 1, HG_IL)
    row = lax.broadcasted_iota(jnp.int32, (tile, HG_IL), 0)
    lane = lax.broadcasted_iota(jnp.int32, (tile, HG_IL), 1)
    lane_bwd = (lane // HEAD_DIM) % 2 == 1
    pair_xor = row ^ (lane % tile)
    ev_mask = (lax.broadcasted_iota(jnp.int32, (HGRN_HEADS * tile, HGRN_W), 0) // tile
               == lax.broadcasted_iota(jnp.int32, (HGRN_HEADS * tile, HGRN_W), 1) // HEAD_DIM)
    bd_mask = (lax.broadcasted_iota(jnp.int32, (HGRN_W, HG_IL), 0) // HEAD_DIM
               == lax.broadcasted_iota(jnp.int32, (HGRN_W, HG_IL), 1) // (2 * HEAD_DIM))
    lane_head = lax.broadcasted_iota(jnp.int32, (HEAD_DIM, HG_IL), 1) // (2 * HEAD_DIM)
    lane_bwd_s = (lax.broadcasted_iota(jnp.int32, (HEAD_DIM, HG_IL), 1) // HEAD_DIM) % 2 == 1
    nt_dims = (((1,), (1,)), ((), ()))
    head_w = 2 * HEAD_DIM

    def tile_body(t, carry):
        rows = pl.ds(pl.multiple_of(t * tile, tile), tile)
        q = zh_ref[rows, c_q:c_q + HG_IL] * (HEAD_DIM ** -0.5)
        v = zh_ref[rows, c_v:c_v + HGRN_W]
        z = zh_ref[rows, c_z:c_z + HG_IL]
        ez = jnp.exp(-jnp.abs(z))
        r = 1.0 / (1.0 + ez)
        tr = ez * r
        pos_z = z >= 0.0
        f = lb_floor + one_m_lb * jnp.where(pos_z, r, tr)
        logf = jnp.log(f)
        kk = one_m_lb * jnp.where(pos_z, tr, r)

        pieces = jnp.concatenate(_split3(logf), axis=1)
        acc = jnp.dot(tri_ref[...], pieces, preferred_element_type=F32)
        prefix = acc[:, :HG_IL] + acc[:, HG_IL:2 * HG_IL] + acc[:, 2 * HG_IL:]
        total = prefix[tile - 1:tile, :]
        b = jnp.where(lane_bwd, total - prefix + logf, prefix)

        a0 = jnp.dot((q * kk).astype(BF16), ones_ref[...], preferred_element_type=F32)
        o = a0 * v

        b2 = b * LOG2_E
        scores = None
        for li, s in enumerate(HGRN_LEVELS):
            qm = qm_ref[li]
            if s == 2:
                x = f * qm + (1.0 - qm)
            else:
                x = jnp.exp2(sgn_ref[li] * (b2 - _level_mid(b2, s, lane_bwd3)))
            xq = x * qm
            qt = (q * xq).astype(BF16)
            kt = (kk * (x - xq)).astype(BF16)
            sc = jnp.concatenate(
                [lax.dot_general(qt[:, h * head_w:(h + 1) * head_w], kt[:, h * head_w:(h + 1) * head_w],
                                 nt_dims, preferred_element_type=F32) for h in range(HGRN_HEADS)], axis=1)
            scores = sc if scores is None else jnp.where(pair_xor < s, sc, scores)
        ev = jnp.where(ev_mask, jnp.concatenate([v] * HGRN_HEADS, axis=0), 0.0).astype(BF16)
        o = o + jnp.dot(scores.astype(BF16), ev, preferred_element_type=F32)
        oacc_ref[rows, :] = o

        qin_ref[rows, :] = (q * jnp.exp(b)).astype(BF16)
        kout = (kk * jnp.exp(total - b)).astype(BF16)
        dec_ref[t] = jnp.exp(total)
        full = jnp.dot(v.T.astype(BF16), kout, preferred_element_type=F32)
        kvt = full[0:HEAD_DIM, :]
        for hh in range(1, HGRN_HEADS):
            kvt = jnp.where(lane_head == hh, full[hh * HEAD_DIM:(hh + 1) * HEAD_DIM, :], kvt)
        kvt_ref[t] = kvt
        return carry

    lax.fori_loop(0, n_tiles, tile_body, 0)

    if has_s0:
        st0 = jnp.concatenate([s0_ref[d, h] for h in range(HGRN_HEADS) for d in range(2)], axis=0).T
    else:
        st0 = jnp.zeros((HEAD_DIM, HG_IL), F32)

    def scan_step(i, st):
        tf, tb = i, n_tiles - 1 - i
        stf_ref[tf] = st
        stb_ref[tb] = st
        dec = jnp.where(lane_bwd1, dec_ref[tb], dec_ref[tf])
        kv = jnp.where(lane_bwd_s, kvt_ref[tb], kvt_ref[tf])
        return st * dec + kv

    st_fin = lax.fori_loop(0, n_tiles, scan_step, st0)
    if emit_state:
        st_t = st_fin.T
        for h in range(HGRN_HEADS):
            for d in range(2):
                r0 = (2 * h + d) * HEAD_DIM
                sf_ref[d, h] = st_t[r0:r0 + HEAD_DIM, :]

    def out_tile(t):
        rows = pl.ds(pl.multiple_of(t * tile, tile), tile)
        st = jnp.where(lane_bwd_s, stb_ref[t], stf_ref[t])
        bd = jnp.where(bd_mask, jnp.concatenate([st] * HGRN_HEADS, axis=0), 0.0).T.astype(BF16)
        o = oacc_ref[rows, :] + jnp.dot(qin_ref[rows, :], bd, preferred_element_type=F32)
        hg = zh_ref[rows, c_g:c_g + HGRN_W]
        ho_ref[rows, :] = _head_rmsnorm(o, nw_ref[...]) * (hg * jax.nn.sigmoid(hg))

    def out_body(i, carry):
        out_tile(2 * i)
        out_tile(2 * i + 1)
        return carry

    assert n_tiles % 2 == 0
    lax.fori_loop(0, n_tiles // 2, out_body, 0)


def _hgrn_consts():
    i = np.arange(HGRN_TILE)
    tri = (i[None, :] <= i[:, None]).astype(np.float32)
    r = np.arange(HG_IL) // (2 * HEAD_DIM)
    c = np.arange(HGRN_W) // HEAD_DIM
    ones = (r[:, None] == c[None, :]).astype(np.float32)
    lane_bwd = (np.arange(HG_IL) // HEAD_DIM) % 2 == 1
    qm = np.stack([(((i[:, None] % s) >= s // 2) != lane_bwd[None, :]) for s in HGRN_LEVELS]).astype(np.float32)
    return (jnp.asarray(tri, dtype=BF16), jnp.asarray(ones, dtype=BF16),
            jnp.asarray(2.0 * qm - 1.0, dtype=F32), jnp.asarray(qm, dtype=F32))


def _hgrn_call(zh, lb_il, norm_w, s0, *, seq_len, layer, emit_state):
    n = zh.shape[0]
    n_seq = n // seq_len
    has_s0 = s0 is not None
    state_block = (2, HGRN_HEADS, HEAD_DIM, HEAD_DIM)
    n_tiles = seq_len // HGRN_TILE
    tri, ones, lvl_sgn, lvl_qm = _hgrn_consts()
    n_lvl = len(HGRN_LEVELS)
    in_specs = [
        pl.BlockSpec((seq_len, ZH_W), lambda b: (b, 0)),
        pl.BlockSpec((DEPTH, 1, HG_IL), lambda b: (0, 0, 0)),
        pl.BlockSpec((None, 1, HGRN_W), lambda b: (layer, 0, 0)),
        pl.BlockSpec((HGRN_TILE, HGRN_TILE), lambda b: (0, 0)),
        pl.BlockSpec((HG_IL, HGRN_W), lambda b: (0, 0)),
        pl.BlockSpec((n_lvl, HGRN_TILE, HG_IL), lambda b: (0, 0, 0), pipeline_mode=pl.Buffered(1)),
        pl.BlockSpec((n_lvl, HGRN_TILE, HG_IL), lambda b: (0, 0, 0), pipeline_mode=pl.Buffered(1)),
    ]
    args = [zh, lb_il, norm_w, tri, ones, lvl_sgn, lvl_qm]
    if has_s0:
        in_specs.append(pl.BlockSpec((None, None) + state_block, lambda b: (b, layer, 0, 0, 0, 0)))
        args.append(s0)
    out_specs = [pl.BlockSpec((seq_len, HGRN_W), lambda b: (b, 0))]
    out_shape = [jax.ShapeDtypeStruct((n, HGRN_W), F32)]
    if emit_state:
        out_specs.append(pl.BlockSpec((None,) + state_block, lambda b: (b, 0, 0, 0, 0)))
        out_shape.append(jax.ShapeDtypeStruct((n_seq,) + state_block, F32))
    return pl.pallas_call(
        functools.partial(_hgrn_kernel, seq_len=seq_len, layer=layer, has_s0=has_s0, emit_state=emit_state),
        grid=(n_seq,),
        in_specs=in_specs,
        out_specs=out_specs,
        out_shape=out_shape,
        scratch_shapes=[
            pltpu.VMEM((seq_len, HG_IL), BF16),
            pltpu.VMEM((n_tiles, 1, HG_IL), F32),
            pltpu.VMEM((n_tiles, HEAD_DIM, HG_IL), F32),
            pltpu.VMEM((n_tiles, HEAD_DIM, HG_IL), F32),
            pltpu.VMEM((n_tiles, HEAD_DIM, HG_IL), F32),
            pltpu.VMEM((seq_len, HGRN_W), F32),
        ],
        compiler_params=_params("arbitrary"),
        name="hgrn",
    )(*args)


def _sink_row(sink_ref, kv_head, rows):
    grp = lax.broadcasted_iota(jnp.int32, (1, ATTN_GROUP * rows), 1) // rows
    base = kv_head * ATTN_GROUP
    out = jnp.broadcast_to(sink_ref[:, base:base + 1], (1, ATTN_GROUP * rows))
    for g in range(1, ATTN_GROUP):
        out = jnp.where(grp == g, sink_ref[:, base + g:base + g + 1], out)
    return out


def _group_queries_t(qt, kv_head):
    base = kv_head * ATTN_GROUP
    return jnp.concatenate([qt[(base + g) * HEAD_DIM:(base + g + 1) * HEAD_DIM, :] for g in range(ATTN_GROUP)], axis=1)


def _ctx_attn_kernel(zq_ref, zkv_ref, qn_ref, kn_ref, sink_ref, ao_ref, kc_ref, vc_ref, *, seq_len):
    rows = seq_len
    q_all = _head_rmsnorm(zq_ref[...], qn_ref[...]) * (HEAD_DIM ** -0.5)
    k_all = _head_rmsnorm(zkv_ref[:, :KV_W], kn_ref[...])
    v_all = zkv_ref[:, KV_W:]
    kc_ref[...] = k_all
    vc_ref[...] = v_all
    for r0 in range(0, zq_ref.shape[0], seq_len):
        qt = q_all[r0:r0 + rows, :].T.astype(BF16)
        kb = k_all[r0:r0 + rows, :].astype(BF16)
        vt = v_all[r0:r0 + rows, :].T.astype(BF16)
        out_t = []
        for kh in range(ATTN_KV_HEADS):
            hs = slice(kh * HEAD_DIM, (kh + 1) * HEAD_DIM)
            s = jnp.dot(kb[:, hs], _group_queries_t(qt, kh), preferred_element_type=F32)
            sink = _sink_row(sink_ref, kh, rows)
            m = jnp.maximum(jnp.max(s, axis=0, keepdims=True), sink)
            p = jnp.exp(s - m)
            denom = jnp.sum(p, axis=0, keepdims=True) + jnp.exp(sink - m)
            pn = (p * (1.0 / denom)).astype(BF16)
            o_t = jnp.dot(vt[hs, :], pn, preferred_element_type=F32)
            out_t.extend(o_t[:, g * rows:(g + 1) * rows] for g in range(ATTN_GROUP))
        ao_ref[r0:r0 + rows, :] = jnp.concatenate(out_t, axis=0).T


def _ctx_attn_call(zq, zkv, q_norm, k_norm, sink, *, seq_len, layer):
    n = zq.shape[0]
    rows = max(seq_len, SMALL_SEQ_ROWS)
    return pl.pallas_call(
        functools.partial(_ctx_attn_kernel, seq_len=seq_len),
        grid=(n // rows,),
        in_specs=[
            pl.BlockSpec((rows, ATTN_W), lambda b: (b, 0)),
            pl.BlockSpec((rows, 2 * KV_W), lambda b: (b, 0)),
            pl.BlockSpec((None, 1, ATTN_W), lambda b: (layer, 0, 0)),
            pl.BlockSpec((None, 1, KV_W), lambda b: (layer, 0, 0)),
            pl.BlockSpec((None, 1, ATTN_HEADS), lambda b: (layer, 0, 0)),
        ],
        out_specs=[
            pl.BlockSpec((rows, ATTN_W), lambda b: (b, 0)),
            pl.BlockSpec((rows, KV_W), lambda b: (b, 0)),
            pl.BlockSpec((rows, KV_W), lambda b: (b, 0)),
        ],
        out_shape=[
            jax.ShapeDtypeStruct((n, ATTN_W), F32),
            jax.ShapeDtypeStruct((n, KV_W), F32),
            jax.ShapeDtypeStruct((n, KV_W), F32),
        ],
        compiler_params=_params("arbitrary"),
        name="ctx_attn",
    )(zq, zkv, q_norm, k_norm, sink)


def _rope_tables(seq_len, n_heads):
    t = np.arange(seq_len)
    row, col = t // GRID_W, t % GRID_W
    half = HEAD_DIM // 2
    inv = (ROPE_BASE ** (-np.arange(0, half, 2, dtype=np.float32) / half)).astype(np.float32)

    def one(pos):
        ang = pos.astype(np.float32)[:, None] * inv[None]
        c, s = np.cos(ang), np.sin(ang)
        return np.concatenate([c, c], axis=1), np.concatenate([-s, s], axis=1)

    cr, sr = one(row)
    cc, sc = one(col)
    cos = np.tile(np.concatenate([cr, cc], axis=1), (1, n_heads)).astype(np.float32)
    sin = np.tile(np.concatenate([sr, sc], axis=1), (1, n_heads)).astype(np.float32)
    return jnp.asarray(cos), jnp.asarray(sin)


def _rope(x, cos, sin_signed):
    width = x.shape[-1]
    lane = lax.broadcasted_iota(jnp.int32, x.shape, 1)
    quarter = HEAD_DIM // 4
    up = pltpu.roll(x, width - quarter, axis=1)
    down = pltpu.roll(x, quarter, axis=1)
    partner = jnp.where((lane % (2 * quarter)) < quarter, up, down)
    return x * cos + partner * sin_signed


def _lat_attn_kernel(zq_ref, zkv_ref, kc_ref, vc_ref, qn_ref, kn_ref, sink_ref,
                     cosq_ref, sinq_ref, cosk_ref, sink_tab_ref, ao_ref, kb_ref, kcb_ref, vct_ref, *, seq_len):
    blk = pl.program_id(1)
    rows = ATTN_BLOCK

    @pl.when(blk == 0)
    def _():
        k_all = _head_rmsnorm(zkv_ref[:, 0:KV_W], kn_ref[...])
        kb_ref[...] = _rope(k_all, cosk_ref[...], sink_tab_ref[...]).astype(BF16)
        kcb_ref[...] = kc_ref[...].astype(BF16)
        vct_ref[...] = vc_ref[...].T.astype(BF16)

    q = _head_rmsnorm(zq_ref[...], qn_ref[...])
    q = _rope(q, cosq_ref[...], sinq_ref[...]) * (HEAD_DIM ** -0.5)
    start = jnp.clip((blk - 1) * ATTN_BLOCK, 0, seq_len - WIN_KEYS)
    start = pl.multiple_of(start, ATTN_BLOCK)
    vwin = zkv_ref[pl.ds(start, WIN_KEYS), KV_W:2 * KV_W]

    stacked = (WIN_KEYS, ATTN_GROUP * rows)
    kpos = start + lax.broadcasted_iota(jnp.int32, stacked, 0)
    qpos = blk * ATTN_BLOCK + lax.broadcasted_iota(jnp.int32, stacked, 1) % rows
    band = jnp.abs(qpos - kpos) <= WINDOW

    qt = q.T.astype(BF16)
    kwb = kb_ref[pl.ds(start, WIN_KEYS), :]
    kcb = kcb_ref[...]
    vwt = vwin.T.astype(BF16)
    vct = vct_ref[...]
    out_t = []
    for kh in range(ATTN_KV_HEADS):
        hs = slice(kh * HEAD_DIM, (kh + 1) * HEAD_DIM)
        qg = _group_queries_t(qt, kh)
        s_loc = jnp.where(band, jnp.dot(kwb[:, hs], qg, preferred_element_type=F32), NEG_BIG)
        s_ctx = jnp.dot(kcb[:, hs], qg, preferred_element_type=F32)
        sink = _sink_row(sink_ref, kh, rows)
        m = jnp.maximum(jnp.maximum(jnp.max(s_loc, axis=0, keepdims=True),
                                    jnp.max(s_ctx, axis=0, keepdims=True)), sink)
        p_loc = jnp.exp(s_loc - m)
        p_ctx = jnp.exp(s_ctx - m)
        denom = (jnp.sum(p_loc, axis=0, keepdims=True) + jnp.sum(p_ctx, axis=0, keepdims=True)
                 + jnp.exp(sink - m))
        inv = 1.0 / denom
        o_t = (jnp.dot(vwt[hs, :], (p_loc * inv).astype(BF16), preferred_element_type=F32)
               + jnp.dot(vct[hs, :], (p_ctx * inv).astype(BF16), preferred_element_type=F32))
        out_t.extend(o_t[:, g * rows:(g + 1) * rows] for g in range(ATTN_GROUP))
    ao_ref[...] = jnp.concatenate(out_t, axis=0).T


def _lat_attn_call(zq, zkv, cache_k, cache_v, q_norm, k_norm, sink, *, seq_len, layer):
    n = zq.shape[0]
    n_blk = seq_len // ATTN_BLOCK
    past = cache_k.shape[2]
    cosq, sinq = _rope_tables(seq_len, ATTN_HEADS)
    cosk, sink_tab = _rope_tables(seq_len, ATTN_KV_HEADS)
    return pl.pallas_call(
        functools.partial(_lat_attn_kernel, seq_len=seq_len),
        grid=(n // seq_len, n_blk),
        in_specs=[
            pl.BlockSpec((ATTN_BLOCK, ATTN_W), lambda b, i: (b * n_blk + i, 0)),
            pl.BlockSpec((seq_len, 2 * KV_W), lambda b, i: (b, 0)),
            pl.BlockSpec((None, None, past, KV_W), lambda b, i: (b, layer, 0, 0)),
            pl.BlockSpec((None, None, past, KV_W), lambda b, i: (b, layer, 0, 0)),
            pl.BlockSpec((None, 1, ATTN_W), lambda b, i: (layer, 0, 0)),
            pl.BlockSpec((None, 1, KV_W), lambda b, i: (layer, 0, 0)),
            pl.BlockSpec((None, 1, ATTN_HEADS), lambda b, i: (layer, 0, 0)),
            pl.BlockSpec((ATTN_BLOCK, ATTN_W), lambda b, i: (i, 0)),
            pl.BlockSpec((ATTN_BLOCK, ATTN_W), lambda b, i: (i, 0)),
            pl.BlockSpec((seq_len, KV_W), lambda b, i: (0, 0)),
            pl.BlockSpec((seq_len, KV_W), lambda b, i: (0, 0)),
        ],
        out_specs=pl.BlockSpec((ATTN_BLOCK, ATTN_W), lambda b, i: (b * n_blk + i, 0)),
        out_shape=jax.ShapeDtypeStruct((n, ATTN_W), F32),
        scratch_shapes=[
            pltpu.VMEM((seq_len, KV_W), BF16),
            pltpu.VMEM((past, KV_W), BF16),
            pltpu.VMEM((KV_W, past), BF16),
        ],
        compiler_params=_params("arbitrary", "arbitrary"),
        name="lat_attn",
    )(zq, zkv, cache_k, cache_v, q_norm, k_norm, sink, cosq, sinq, cosk, sink_tab)


def _arrange_in_proj(w):
    lead = w.shape[:-1]
    u, hq, hi, hff, hfb, hg, rest = jnp.split(w, [256, 512, 768, 1024, 1280, 1536], axis=-1)
    heads = lambda t: t.reshape(lead + (HGRN_HEADS, 1, HEAD_DIM))
    q_il = jnp.concatenate([heads(hq), heads(hq)], axis=-2).reshape(lead + (HG_IL,))
    z_il = jnp.concatenate([heads(hff), heads(hfb)], axis=-2).reshape(lead + (HG_IL,))
    out = jnp.concatenate([u, q_il, hi, z_il, hg, rest], axis=-1)
    assert out.shape[-1] == IN_W
    return out


def kernel(x_prompt, x_sample, c, cache_attn_k, cache_attn_v, state_hgrn, c_ctx, w_ada, b_ada, norm_ffn1, norm_mix, norm_ffn2, ffn1_w_gate_up, ffn1_w_down, ffn2_w_gate_up, ffn2_w_down, w_in, w_out, hgrn_lower_bounds, hgrn_norm, q_norm, k_norm, attn_sink):
    batch, seq, _ = x_prompt.shape
    dec_batch, dec_seq, _ = x_sample.shape
    past = cache_attn_k.shape[2]
    assert dec_batch + 1 <= ADA_ROWS

    cond = jnp.concatenate([c_ctx[None], c, jnp.zeros((ADA_ROWS - 1 - dec_batch, D_MODEL), F32)], axis=0)
    mod = _ada_call(cond, w_ada, b_ada).reshape(DEPTH, ADA_ROWS, N_MOD, D_MODEL)

    bf = lambda w: w.astype(BF16)
    w_gu1, w_d1, w_gu2, w_d2 = bf(ffn1_w_gate_up), bf(ffn1_w_down), bf(ffn2_w_gate_up), bf(ffn2_w_down)
    w_in_b, w_out_b = _arrange_in_proj(bf(w_in)), bf(w_out)
    row3 = lambda w: w.reshape(DEPTH, 1, w.shape[-1])
    n1, nm, n2 = row3(norm_ffn1), row3(norm_mix), row3(norm_ffn2)
    hn = row3(hgrn_norm)
    qn = row3(jnp.tile(q_norm, (1, ATTN_HEADS)))
    kn = row3(jnp.tile(k_norm, (1, ATTN_KV_HEADS)))
    sink = row3(attn_sink)
    cache_k = cache_attn_k.reshape(dec_batch, DEPTH, past, KV_W)
    cache_v = cache_attn_v.reshape(dec_batch, DEPTH, past, KV_W)
    lb_il = hgrn_lower_bounds.reshape(DEPTH, 2, HGRN_HEADS, HEAD_DIM).transpose(0, 2, 1, 3).reshape(DEPTH, 1, HG_IL)

    streams = {
        "ctx": dict(x=x_prompt.reshape(batch * seq, D_MODEL), seq_len=seq, mod_row0=0, rows_per_mod=batch * seq),
        "lat": dict(x=x_sample.reshape(dec_batch * dec_seq, D_MODEL), seq_len=dec_seq, mod_row0=1, rows_per_mod=dec_seq),
    }
    k_list, v_list, s_list = [], [], []
    for layer in range(DEPTH):
        for name, st in streams.items():
            mk = dict(layer=layer, mod_row0=st["mod_row0"], rows_per_mod=st["rows_per_mod"])
            seq_len = st["seq_len"]
            x, zu, zh, zq, zkv = _ffn_in_call(st["x"], mod, n1, nm, w_gu1, w_d1, w_in_b, **mk)
            fo = _fourier_call(zu, seq_len)
            if name == "ctx":
                ho, sfin = _hgrn_call(zh, lb_il, hn, None, seq_len=seq_len, layer=layer, emit_state=True)
                ao, k_new, v_new = _ctx_attn_call(zq, zkv, qn, kn, sink, seq_len=seq_len, layer=layer)
                k_list.append(k_new.reshape(batch, seq, ATTN_KV_HEADS, HEAD_DIM))
                v_list.append(v_new.reshape(batch, seq, ATTN_KV_HEADS, HEAD_DIM))
                s_list.append(sfin)
            else:
                (ho,) = _hgrn_call(zh, lb_il, hn, state_hgrn, seq_len=seq_len, layer=layer, emit_state=False)
                ao = _lat_attn_call(zq, zkv, cache_k, cache_v, qn, kn, sink, seq_len=seq_len, layer=layer)
            st["x"] = _out_ffn_call(x, mod, n2, fo, ho, ao, w_out_b, w_gu2, w_d2, **mk)

    y_p = streams["ctx"]["x"].reshape(batch, seq, D_MODEL)
    y_s = streams["lat"]["x"].reshape(dec_batch, dec_seq, D_MODEL)
    return (y_p, y_s, jnp.stack(k_list, axis=1), jnp.stack(v_list, axis=1), jnp.stack(s_list, axis=1))
```

```python
import functools

import numpy as np
import jax
import jax.numpy as jnp
from jax import lax
from jax.experimental import pallas as pl
from jax.experimental.pallas import tpu as pltpu

F32 = jnp.float32
BF16 = jnp.bfloat16

D_MODEL = 1024
DEPTH = 2
HEAD_DIM = 64
FOURIER_W = 256
HGRN_W = 256
HGRN_HEADS = 4
ATTN_W = 512
ATTN_HEADS = 8
ATTN_KV_HEADS = 2
ATTN_GROUP = 4
KV_W = 128
MIX_W = 1024
WINDOW = 128
ATTN_BLOCK = 128
D_FF = 2816
GRID_W = 64
ROPE_BASE = 10000.0
EPS = 1e-6
N_MOD = 9
NEG_BIG = -1e30
LB_FLOOR = 1e-30
LOG2_E = 1.4426950408889634

HG_IL = 2 * HGRN_W
ZH_W = HG_IL + HGRN_W + HG_IL + HGRN_W
IN_W = FOURIER_W + ZH_W + ATTN_W + 2 * KV_W

SUBLANES = 8
VMEM_LIMIT_BYTES = 60 * 1024 * 1024

ADA_ROWS = 8
ADA_TN = 1536
FFN_TM = 512
FFN_FC = 256
SMALL_SEQ_ROWS = 1024
HGRN_TILE = 128
HGRN_LEVELS = (128, 64, 32, 16, 8, 4, 2)
WIN_KEYS = 3 * ATTN_BLOCK


def _params(*semantics):
    return pltpu.CompilerParams(dimension_semantics=semantics, vmem_limit_bytes=VMEM_LIMIT_BYTES)


def _mod_norm(x, norm_w, shift, scale):
    ms = jnp.mean(x * x, axis=-1, keepdims=True)
    y = x * lax.rsqrt(ms + EPS) * norm_w
    return y * (1.0 + scale) + shift


def _head_ones(width):
    r = lax.broadcasted_iota(jnp.int32, (width, width), 0) // HEAD_DIM
    c = lax.broadcasted_iota(jnp.int32, (width, width), 1) // HEAD_DIM
    return jnp.where(r == c, 1.0, 0.0).astype(BF16)


def _head_rmsnorm(x, w):
    sq = (x * x).astype(BF16)
    ms = jnp.dot(sq, _head_ones(x.shape[-1]), preferred_element_type=F32) * (1.0 / HEAD_DIM)
    return x * lax.rsqrt(ms + EPS) * w


def _ada_kernel(cond_ref, w_ref, b_ref, o_ref):
    cnd = cond_ref[...]
    act = (cnd * jax.nn.sigmoid(cnd)).astype(BF16)
    o_ref[...] = jnp.dot(act, w_ref[...].astype(BF16), preferred_element_type=F32) + b_ref[...]


def _ada_call(cond, w_ada, b_ada):
    n_out = N_MOD * D_MODEL
    return pl.pallas_call(
        _ada_kernel,
        grid=(DEPTH, n_out // ADA_TN),
        in_specs=[
            pl.BlockSpec((ADA_ROWS, D_MODEL), lambda l, j: (0, 0)),
            pl.BlockSpec((None, D_MODEL, ADA_TN), lambda l, j: (l, 0, j)),
            pl.BlockSpec((None, 1, ADA_TN), lambda l, j: (l, 0, j)),
        ],
        out_specs=pl.BlockSpec((None, ADA_ROWS, ADA_TN), lambda l, j: (l, 0, j)),
        out_shape=jax.ShapeDtypeStruct((DEPTH, ADA_ROWS, n_out), F32),
        compiler_params=_params("arbitrary", "arbitrary"),
        name="adaln",
    )(cond, w_ada, b_ada.reshape(DEPTH, 1, n_out))


def _weight_chunk_copies(wgu_hbm, wd_hbm, stage_gu, stage_d, sems, layer, c, slot):
    lo, hi = c * FFN_FC, (c + 1) * FFN_FC
    return (
        pltpu.make_async_copy(wgu_hbm.at[layer, :, lo:hi], stage_gu.at[slot, :, 0:FFN_FC], sems.at[slot, 0]),
        pltpu.make_async_copy(wgu_hbm.at[layer, :, D_FF + lo:D_FF + hi], stage_gu.at[slot, :, FFN_FC:2 * FFN_FC],
                              sems.at[slot, 1]),
        pltpu.make_async_copy(wd_hbm.at[layer, lo:hi, :], stage_d.at[slot], sems.at[slot, 2]),
    )


def _swiglu_chunks(h, wgu_ref, wd_ref, fetch=None):
    n_chunks = D_FF // FFN_FC
    if fetch is not None:
        stage_gu, stage_d = fetch[2], fetch[3]
        for cp in _weight_chunk_copies(*fetch, 0, 0):
            cp.start()
    acc = None
    for c in range(n_chunks):
        if fetch is not None:
            slot = c % 2
            if c + 1 < n_chunks:
                for cp in _weight_chunk_copies(*fetch, c + 1, 1 - slot):
                    cp.start()
            for cp in _weight_chunk_copies(*fetch, c, slot):
                cp.wait()
            wgu_ref[:, c * FFN_FC:(c + 1) * FFN_FC] = stage_gu[slot, :, 0:FFN_FC].astype(BF16)
            wgu_ref[:, D_FF + c * FFN_FC:D_FF + (c + 1) * FFN_FC] = stage_gu[slot, :, FFN_FC:2 * FFN_FC].astype(BF16)
            wd_ref[c * FFN_FC:(c + 1) * FFN_FC, :] = stage_d[slot].astype(BF16)
        g = jnp.dot(h, wgu_ref[:, c * FFN_FC:(c + 1) * FFN_FC], preferred_element_type=F32)
        u = jnp.dot(h, wgu_ref[:, D_FF + c * FFN_FC:D_FF + (c + 1) * FFN_FC], preferred_element_type=F32)
        act = (g * jax.nn.sigmoid(g) * u).astype(BF16)
        part = jnp.dot(act, wd_ref[c * FFN_FC:(c + 1) * FFN_FC, :], preferred_element_type=F32)
        acc = part if acc is None else acc + part
    return acc


def _mod_spec(layer, mod_row0, rows_per_mod, tm):
    return pl.BlockSpec((None, None, N_MOD, D_MODEL),
                        lambda i, *_: (layer, mod_row0 + (i * tm) // rows_per_mod, 0, 0))


def _resident(shape, layer):
    return pl.BlockSpec((None,) + shape, lambda i: (layer,) + (0,) * len(shape), pipeline_mode=pl.Buffered(1))


def _first_step_fetches(body, fetch):
    first = pl.program_id(0) == 0
    pl.when(first)(functools.partial(body, fetch))
    pl.when(jnp.logical_not(first))(functools.partial(body, None))


def _ffn_weight_scratch():
    return [
        pltpu.VMEM((D_MODEL, 2 * D_FF), BF16),
        pltpu.VMEM((D_FF, D_MODEL), BF16),
        pltpu.VMEM((2, D_MODEL, 2 * FFN_FC), F32),
        pltpu.VMEM((2, FFN_FC, D_MODEL), F32),
        pltpu.SemaphoreType.DMA((2, 3)),
    ]


def _ffn_in_kernel(x_ref, mod_ref, n1_ref, nm_ref, wgu_hbm, wd_hbm, win_ref,
                   x1_ref, zu_ref, zh_ref, zq_ref, zkv_ref, wgu_ref, wd_ref, stage_gu, stage_d, sems, *, layer):
    def body(fetch):
        x = x_ref[...]
        h = _mod_norm(x, n1_ref[...], mod_ref[0:1, :], mod_ref[1:2, :]).astype(BF16)
        x1 = x + 0.5 * mod_ref[2:3, :] * _swiglu_chunks(h, wgu_ref, wd_ref, fetch)
        x1_ref[...] = x1
        h2 = _mod_norm(x1, nm_ref[...], mod_ref[3:4, :], mod_ref[4:5, :]).astype(BF16)
        z = jnp.dot(h2, win_ref[...], preferred_element_type=F32)
        zu_ref[...] = z[:, :FOURIER_W]
        zh_ref[...] = z[:, FOURIER_W:FOURIER_W + ZH_W]
        zq_ref[...] = z[:, FOURIER_W + ZH_W:FOURIER_W + ZH_W + ATTN_W]
        zkv_ref[...] = z[:, FOURIER_W + ZH_W + ATTN_W:]

    _first_step_fetches(body, (wgu_hbm, wd_hbm, stage_gu, stage_d, sems, layer))


def _ffn_in_call(x, mod, n1, nm, w_gu, w_down, w_in, *, layer, mod_row0, rows_per_mod):
    n = x.shape[0]
    tm = FFN_TM
    widths = (D_MODEL, FOURIER_W, ZH_W, ATTN_W, 2 * KV_W)
    return pl.pallas_call(
        functools.partial(_ffn_in_kernel, layer=layer),
        grid=(n // tm,),
        in_specs=[
            pl.BlockSpec((tm, D_MODEL), lambda i: (i, 0)),
            _mod_spec(layer, mod_row0, rows_per_mod, tm),
            pl.BlockSpec((None, 1, D_MODEL), lambda i: (layer, 0, 0)),
            pl.BlockSpec((None, 1, D_MODEL), lambda i: (layer, 0, 0)),
            pl.BlockSpec(memory_space=pl.ANY),
            pl.BlockSpec(memory_space=pl.ANY),
            _resident((D_MODEL, IN_W), layer),
        ],
        out_specs=[pl.BlockSpec((tm, w), lambda i: (i, 0)) for w in widths],
        out_shape=[jax.ShapeDtypeStruct((n, w), F32) for w in widths],
        scratch_shapes=_ffn_weight_scratch(),
        compiler_params=_params("arbitrary"),
        name="ffn_in",
    )(x, mod, n1, nm, w_gu, w_down, w_in)


def _out_ffn_kernel(x_ref, mod_ref, n2_ref, fo_ref, ho_ref, ao_ref, wout_ref, wgu_hbm, wd_hbm, o_ref,
                    wgu_ref, wd_ref, stage_gu, stage_d, sems, *, layer):
    def body(fetch):
        y = jnp.dot(fo_ref[...].astype(BF16), wout_ref[0:FOURIER_W, :], preferred_element_type=F32)
        y += jnp.dot(ho_ref[...].astype(BF16), wout_ref[FOURIER_W:FOURIER_W + HGRN_W, :], preferred_element_type=F32)
        y += jnp.dot(ao_ref[...].astype(BF16), wout_ref[FOURIER_W + HGRN_W:, :], preferred_element_type=F32)
        xm = x_ref[...] + mod_ref[5:6, :] * y
        h = _mod_norm(xm, n2_ref[...], mod_ref[6:7, :], mod_ref[7:8, :]).astype(BF16)
        o_ref[...] = xm + 0.5 * mod_ref[8:9, :] * _swiglu_chunks(h, wgu_ref, wd_ref, fetch)

    _first_step_fetches(body, (wgu_hbm, wd_hbm, stage_gu, stage_d, sems, layer))


def _out_ffn_call(x, mod, n2, fo, ho, ao, w_out, w_gu, w_down, *, layer, mod_row0, rows_per_mod):
    n = x.shape[0]
    tm = FFN_TM
    return pl.pallas_call(
        functools.partial(_out_ffn_kernel, layer=layer),
        grid=(n // tm,),
        in_specs=[
            pl.BlockSpec((tm, D_MODEL), lambda i: (i, 0)),
            _mod_spec(layer, mod_row0, rows_per_mod, tm),
            pl.BlockSpec((None, 1, D_MODEL), lambda i: (layer, 0, 0)),
            pl.BlockSpec((tm, FOURIER_W), lambda i: (i, 0)),
            pl.BlockSpec((tm, HGRN_W), lambda i: (i, 0)),
            pl.BlockSpec((tm, ATTN_W), lambda i: (i, 0)),
            _resident((MIX_W, D_MODEL), layer),
            pl.BlockSpec(memory_space=pl.ANY),
            pl.BlockSpec(memory_space=pl.ANY),
        ],
        out_specs=pl.BlockSpec((tm, D_MODEL), lambda i: (i, 0)),
        out_shape=jax.ShapeDtypeStruct((n, D_MODEL), F32),
        scratch_shapes=_ffn_weight_scratch(),
        compiler_params=_params("arbitrary"),
        name="out_ffn",
    )(x, mod, n2, fo, ho, ao, w_out, w_gu, w_down)


def _dft_tables(seq_len):
    c = np.arange(HEAD_DIM)
    ang_c = 2.0 * np.pi * np.outer(c, c) / HEAD_DIM
    eye = np.eye(FOURIER_W // HEAD_DIM)
    chan = np.concatenate([np.kron(eye, np.cos(ang_c)), np.kron(eye, np.sin(ang_c))], axis=1) / np.sqrt(HEAD_DIM)
    n = np.arange(seq_len)
    ang_l = 2.0 * np.pi * (np.outer(n, n) % seq_len) / seq_len
    pos = np.concatenate([np.cos(ang_l), -np.sin(ang_l)], axis=1) / np.sqrt(seq_len)
    return jnp.asarray(chan, dtype=F32), jnp.asarray(pos, dtype=F32)


def _fourier_kernel(u_ref, chan_ref, pos_ref, o_ref, *, seq_len):
    t = jnp.dot(u_ref[...].astype(BF16), chan_ref[...].astype(BF16), preferred_element_type=F32)
    pos = pos_ref[...].astype(BF16)
    for r0 in range(0, u_ref.shape[0], seq_len):
        tb = t[r0:r0 + seq_len, :]
        stacked = jnp.concatenate([tb[:, :FOURIER_W], tb[:, FOURIER_W:]], axis=0).astype(BF16)
        o_ref[r0:r0 + seq_len, :] = jnp.dot(pos, stacked, preferred_element_type=F32)


def _fourier_call(zu, seq_len):
    n = zu.shape[0]
    chan, pos = _dft_tables(seq_len)
    rows = max(seq_len, SMALL_SEQ_ROWS)
    return pl.pallas_call(
        functools.partial(_fourier_kernel, seq_len=seq_len),
        grid=(n // rows,),
        in_specs=[
            pl.BlockSpec((rows, FOURIER_W), lambda b: (b, 0)),
            pl.BlockSpec((FOURIER_W, 2 * FOURIER_W), lambda b: (0, 0), pipeline_mode=pl.Buffered(1)),
            pl.BlockSpec((seq_len, 2 * seq_len), lambda b: (0, 0), pipeline_mode=pl.Buffered(1)),
        ],
        out_specs=pl.BlockSpec((rows, FOURIER_W), lambda b: (b, 0)),
        out_shape=jax.ShapeDtypeStruct((n, FOURIER_W), F32),
        compiler_params=_params("arbitrary"),
        name="fourier",
    )(zu, chan, pos)


def _split3(x):
    hi = x.astype(BF16)
    r1 = x - hi.astype(F32)
    mid = r1.astype(BF16)
    lo = (r1 - mid.astype(F32)).astype(BF16)
    return hi, mid, lo


def _level_mid(b, s, lane_bwd3):
    tile, lanes = b.shape
    if s >= SUBLANES:
        b3 = b.reshape(tile // s, s, lanes)
        mid = jnp.where(lane_bwd3, b3[:, s // 2:s // 2 + 1, :], b3[:, s // 2 - 1:s // 2, :])
        return jnp.broadcast_to(mid, b3.shape).reshape(tile, lanes)
    assert s == 4
    b3 = b.reshape(tile // SUBLANES, SUBLANES, lanes)
    sub = lax.broadcasted_iota(jnp.int32, (1, SUBLANES, 1), 1)
    lo = jnp.where(lane_bwd3, b3[:, 2:3, :], b3[:, 1:2, :])
    hi = jnp.where(lane_bwd3, b3[:, 6:7, :], b3[:, 5:6, :])
    return jnp.where(sub < 4, lo, hi).reshape(tile, lanes)


def _hgrn_kernel(*refs, seq_len, layer, has_s0, emit_state):
    refs = list(refs)
    zh_ref, lb_ref, nw_ref, tri_ref, ones_ref, sgn_ref, qm_ref = refs[:7]
    pos = 7
    s0_ref = sf_ref = None
    if has_s0:
        s0_ref = refs[pos]
        pos += 1
    ho_ref = refs[pos]
    pos += 1
    if emit_state:
        sf_ref = refs[pos]
        pos += 1
    qin_ref, dec_ref, kvt_ref, stf_ref, stb_ref, oacc_ref = refs[pos:]

    tile = HGRN_TILE
    n_tiles = seq_len // tile
    c_q, c_v, c_z, c_g = 0, HG_IL, HG_IL + HGRN_W, 2 * HG_IL + HGRN_W

    raw = lb_ref[...]
    e = jnp.exp(raw - jnp.max(raw, axis=0, keepdims=True))
    soft = e / jnp.sum(e, axis=0, keepdims=True)
    cum = soft[0]
    for l in range(1, layer + 1):
        cum = cum + soft[l]
    lbd = cum - soft[0]
    lb_floor = jnp.maximum(lbd, LB_FLOOR)
    one_m_lb = 1.0 - lbd

    lane1 = lax.broadcasted_iota(jnp.int32, (1, HG_IL), 1)
    lane_bwd1 = (lane1 // HEAD_DIM) % 2 == 1
    lane_bwd3 = lane_bwd1.reshape(1, 1, HG_IL)
    row = lax.broadcasted_iota(jnp.int32, (tile, HG_IL), 0)
    lane = lax.broadcasted_iota(jnp.int32, (tile, HG_IL), 1)
    lane_bwd = (lane // HEAD_DIM) % 2 == 1
    pair_xor = row ^ (lane % tile)
    ev_mask = (lax.broadcasted_iota(jnp.int32, (HGRN_HEADS * tile, HGRN_W), 0) // tile
               == lax.broadcasted_iota(jnp.int32, (HGRN_HEADS * tile, HGRN_W), 1) // HEAD_DIM)
    bd_mask = (lax.broadcasted_iota(jnp.int32, (HGRN_W, HG_IL), 0) // HEAD_DIM
               == lax.broadcasted_iota(jnp.int32, (HGRN_W, HG_IL), 1) // (2 * HEAD_DIM))
    lane_head = lax.broadcasted_iota(jnp.int32, (HEAD_DIM, HG_IL), 1) // (2 * HEAD_DIM)
    lane_bwd_s = (lax.broadcasted_iota(jnp.int32, (HEAD_DIM, HG_IL), 1) // HEAD_DIM) % 2 == 1
    nt_dims = (((1,), (1,)), ((), ()))
    head_w = 2 * HEAD_DIM

    def tile_body(t, carry):
        rows = pl.ds(pl.multiple_of(t * tile, tile), tile)
        q = zh_ref[rows, c_q:c_q + HG_IL] * (HEAD_DIM ** -0.5)
        v = zh_ref[rows, c_v:c_v + HGRN_W]
        z = zh_ref[rows, c_z:c_z + HG_IL]
        ez = jnp.exp(-jnp.abs(z))
        r = 1.0 / (1.0 + ez)
        tr = ez * r
        pos_z = z >= 0.0
        f = lb_floor + one_m_lb * jnp.where(pos_z, r, tr)
        logf = jnp.log(f)
        kk = one_m_lb * jnp.where(pos_z, tr, r)

        pieces = jnp.concatenate(_split3(logf), axis=1)
        acc = jnp.dot(tri_ref[...], pieces, preferred_element_type=F32)
        prefix = acc[:, :HG_IL] + acc[:, HG_IL:2 * HG_IL] + acc[:, 2 * HG_IL:]
        total = prefix[tile - 1:tile, :]
        b = jnp.where(lane_bwd, total - prefix + logf, prefix)

        a0 = jnp.dot((q * kk).astype(BF16), ones_ref[...], preferred_element_type=F32)
        o = a0 * v

        b2 = b * LOG2_E
        scores = None
        for li, s in enumerate(HGRN_LEVELS):
            qm = qm_ref[li]
            if s == 2:
                x = f * qm + (1.0 - qm)
            else:
                x = jnp.exp2(sgn_ref[li] * (b2 - _level_mid(b2, s, lane_bwd3)))
            xq = x * qm
            qt = (q * xq).astype(BF16)
            kt = (kk * (x - xq)).astype(BF16)
            sc = jnp.concatenate(
                [lax.dot_general(qt[:, h * head_w:(h + 1) * head_w], kt[:, h * head_w:(h + 1) * head_w],
                                 nt_dims, preferred_element_type=F32) for h in range(HGRN_HEADS)], axis=1)
            scores = sc if scores is None else jnp.where(pair_xor < s, sc, scores)
        ev = jnp.where(ev_mask, jnp.concatenate([v] * HGRN_HEADS, axis=0), 0.0).astype(BF16)
        o = o + jnp.dot(scores.astype(BF16), ev, preferred_element_type=F32)
        oacc_ref[rows, :] = o

        qin_ref[rows, :] = (q * jnp.exp(b)).astype(BF16)
        kout = (kk * jnp.exp(total - b)).astype(BF16)
        dec_ref[t] = jnp.exp(total)
        full = jnp.dot(v.T.astype(BF16), kout, preferred_element_type=F32)
        kvt = full[0:HEAD_DIM, :]
        for hh in range(1, HGRN_HEADS):
            kvt = jnp.where(lane_head == hh, full[hh * HEAD_DIM:(hh + 1) * HEAD_DIM, :], kvt)
        kvt_ref[t] = kvt
        return carry

    lax.fori_loop(0, n_tiles, tile_body, 0)

    if has_s0:
        st0 = jnp.concatenate([s0_ref[d, h] for h in range(HGRN_HEADS) for d in range(2)], axis=0).T
    else:
        st0 = jnp.zeros((HEAD_DIM, HG_IL), F32)

    def scan_step(i, st):
        tf, tb = i, n_tiles - 1 - i
        stf_ref[tf] = st
        stb_ref[tb] = st
        dec = jnp.where(lane_bwd1, dec_ref[tb], dec_ref[tf])
        kv = jnp.where(lane_bwd_s, kvt_ref[tb], kvt_ref[tf])
        return st * dec + kv

    st_fin = lax.fori_loop(0, n_tiles, scan_step, st0)
    if emit_state:
        st_t = st_fin.T
        for h in range(HGRN_HEADS):
            for d in range(2):
                r0 = (2 * h + d) * HEAD_DIM
                sf_ref[d, h] = st_t[r0:r0 + HEAD_DIM, :]

    def out_tile(t):
        rows = pl.ds(pl.multiple_of(t * tile, tile), tile)
        st = jnp.where(lane_bwd_s, stb_ref[t], stf_ref[t])
        bd = jnp.where(bd_mask, jnp.concatenate([st] * HGRN_HEADS, axis=0), 0.0).T.astype(BF16)
        o = oacc_ref[rows, :] + jnp.dot(qin_ref[rows, :], bd, preferred_element_type=F32)
        hg = zh_ref[rows, c_g:c_g + HGRN_W]
        ho_ref[rows, :] = _head_rmsnorm(o, nw_ref[...]) * (hg * jax.nn.sigmoid(hg))

    def out_body(i, carry):
        out_tile(2 * i)
        out_tile(2 * i + 1)
        return carry

    assert n_tiles % 2 == 0
    lax.fori_loop(0, n_tiles // 2, out_body, 0)


def _hgrn_consts():
    i = np.arange(HGRN_TILE)
    tri = (i[None, :] <= i[:, None]).astype(np.float32)
    r = np.arange(HG_IL) // (2 * HEAD_DIM)
    c = np.arange(HGRN_W) // HEAD_DIM
    ones = (r[:, None] == c[None, :]).astype(np.float32)
    lane_bwd = (np.arange(HG_IL) // HEAD_DIM) % 2 == 1
    qm = np.stack([(((i[:, None] % s) >= s // 2) != lane_bwd[None, :]) for s in HGRN_LEVELS]).astype(np.float32)
    return (jnp.asarray(tri, dtype=BF16), jnp.asarray(ones, dtype=BF16),
            jnp.asarray(2.0 * qm - 1.0, dtype=F32), jnp.asarray(qm, dtype=F32))


def _hgrn_call(zh, lb_il, norm_w, s0, *, seq_len, layer, emit_state):
    n = zh.shape[0]
    n_seq = n // seq_len
    has_s0 = s0 is not None
    state_block = (2, HGRN_HEADS, HEAD_DIM, HEAD_DIM)
    n_tiles = seq_len // HGRN_TILE
    tri, ones, lvl_sgn, lvl_qm = _hgrn_consts()
    n_lvl = len(HGRN_LEVELS)
    in_specs = [
        pl.BlockSpec((seq_len, ZH_W), lambda b: (b, 0)),
        pl.BlockSpec((DEPTH, 1, HG_IL), lambda b: (0, 0, 0)),
        pl.BlockSpec((None, 1, HGRN_W), lambda b: (layer, 0, 0)),
        pl.BlockSpec((HGRN_TILE, HGRN_TILE), lambda b: (0, 0)),
        pl.BlockSpec((HG_IL, HGRN_W), lambda b: (0, 0)),
        pl.BlockSpec((n_lvl, HGRN_TILE, HG_IL), lambda b: (0, 0, 0), pipeline_mode=pl.Buffered(1)),
        pl.BlockSpec((n_lvl, HGRN_TILE, HG_IL), lambda b: (0, 0, 0), pipeline_mode=pl.Buffered(1)),
    ]
    args = [zh, lb_il, norm_w, tri, ones, lvl_sgn, lvl_qm]
    if has_s0:
        in_specs.append(pl.BlockSpec((None, None) + state_block, lambda b: (b, layer, 0, 0, 0, 0)))
        args.append(s0)
    out_specs = [pl.BlockSpec((seq_len, HGRN_W), lambda b: (b, 0))]
    out_shape = [jax.ShapeDtypeStruct((n, HGRN_W), F32)]
    if emit_state:
        out_specs.append(pl.BlockSpec((None,) + state_block, lambda b: (b, 0, 0, 0, 0)))
        out_shape.append(jax.ShapeDtypeStruct((n_seq,) + state_block, F32))
    return pl.pallas_call(
        functools.partial(_hgrn_kernel, seq_len=seq_len, layer=layer, has_s0=has_s0, emit_state=emit_state),
        grid=(n_seq,),
        in_specs=in_specs,
        out_specs=out_specs,
        out_shape=out_shape,
        scratch_shapes=[
            pltpu.VMEM((seq_len, HG_IL), BF16),
            pltpu.VMEM((n_tiles, 1, HG_IL), F32),
            pltpu.VMEM((n_tiles, HEAD_DIM, HG_IL), F32),
            pltpu.VMEM((n_tiles, HEAD_DIM, HG_IL), F32),
            pltpu.VMEM((n_tiles, HEAD_DIM, HG_IL), F32),
            pltpu.VMEM((seq_len, HGRN_W), F32),
        ],
        compiler_params=_params("arbitrary"),
        name="hgrn",
    )(*args)


def _sink_row(sink_ref, kv_head, rows):
    grp = lax.broadcasted_iota(jnp.int32, (1, ATTN_GROUP * rows), 1) // rows
    base = kv_head * ATTN_GROUP
    out = jnp.broadcast_to(sink_ref[:, base:base + 1], (1, ATTN_GROUP * rows))
    for g in range(1, ATTN_GROUP):
        out = jnp.where(grp == g, sink_ref[:, base + g:base + g + 1], out)
    return out


def _group_queries_t(qt, kv_head):
    base = kv_head * ATTN_GROUP
    return jnp.concatenate([qt[(base + g) * HEAD_DIM:(base + g + 1) * HEAD_DIM, :] for g in range(ATTN_GROUP)], axis=1)


def _ctx_attn_kernel(zq_ref, zkv_ref, qn_ref, kn_ref, sink_ref, ao_ref, kc_ref, vc_ref, *, seq_len):
    rows = seq_len
    q_all = _head_rmsnorm(zq_ref[...], qn_ref[...]) * (HEAD_DIM ** -0.5)
    k_all = _head_rmsnorm(zkv_ref[:, :KV_W], kn_ref[...])
    v_all = zkv_ref[:, KV_W:]
    kc_ref[...] = k_all
    vc_ref[...] = v_all
    for r0 in range(0, zq_ref.shape[0], seq_len):
        qt = q_all[r0:r0 + rows, :].T.astype(BF16)
        kb = k_all[r0:r0 + rows, :].astype(BF16)
        vt = v_all[r0:r0 + rows, :].T.astype(BF16)
        out_t = []
        for kh in range(ATTN_KV_HEADS):
            hs = slice(kh * HEAD_DIM, (kh + 1) * HEAD_DIM)
            s = jnp.dot(kb[:, hs], _group_queries_t(qt, kh), preferred_element_type=F32)
            sink = _sink_row(sink_ref, kh, rows)
            m = jnp.maximum(jnp.max(s, axis=0, keepdims=True), sink)
            p = jnp.exp(s - m)
            denom = jnp.sum(p, axis=0, keepdims=True) + jnp.exp(sink - m)
            pn = (p * (1.0 / denom)).astype(BF16)
            o_t = jnp.dot(vt[hs, :], pn, preferred_element_type=F32)
            out_t.extend(o_t[:, g * rows:(g + 1) * rows] for g in range(ATTN_GROUP))
        ao_ref[r0:r0 + rows, :] = jnp.concatenate(out_t, axis=0).T


def _ctx_attn_call(zq, zkv, q_norm, k_norm, sink, *, seq_len, layer):
    n = zq.shape[0]
    rows = max(seq_len, SMALL_SEQ_ROWS)
    return pl.pallas_call(
        functools.partial(_ctx_attn_kernel, seq_len=seq_len),
        grid=(n // rows,),
        in_specs=[
            pl.BlockSpec((rows, ATTN_W), lambda b: (b, 0)),
            pl.BlockSpec((rows, 2 * KV_W), lambda b: (b, 0)),
            pl.BlockSpec((None, 1, ATTN_W), lambda b: (layer, 0, 0)),
            pl.BlockSpec((None, 1, KV_W), lambda b: (layer, 0, 0)),
            pl.BlockSpec((None, 1, ATTN_HEADS), lambda b: (layer, 0, 0)),
        ],
        out_specs=[
            pl.BlockSpec((rows, ATTN_W), lambda b: (b, 0)),
            pl.BlockSpec((rows, KV_W), lambda b: (b, 0)),
            pl.BlockSpec((rows, KV_W), lambda b: (b, 0)),
        ],
        out_shape=[
            jax.ShapeDtypeStruct((n, ATTN_W), F32),
            jax.ShapeDtypeStruct((n, KV_W), F32),
            jax.ShapeDtypeStruct((n, KV_W), F32),
        ],
        compiler_params=_params("arbitrary"),
        name="ctx_attn",
    )(zq, zkv, q_norm, k_norm, sink)


def _rope_tables(seq_len, n_heads):
    t = np.arange(seq_len)
    row, col = t // GRID_W, t % GRID_W
    half = HEAD_DIM // 2
    inv = (ROPE_BASE ** (-np.arange(0, half, 2, dtype=np.float32) / half)).astype(np.float32)

    def one(pos):
        ang = pos.astype(np.float32)[:, None] * inv[None]
        c, s = np.cos(ang), np.sin(ang)
        return np.concatenate([c, c], axis=1), np.concatenate([-s, s], axis=1)

    cr, sr = one(row)
    cc, sc = one(col)
    cos = np.tile(np.concatenate([cr, cc], axis=1), (1, n_heads)).astype(np.float32)
    sin = np.tile(np.concatenate([sr, sc], axis=1), (1, n_heads)).astype(np.float32)
    return jnp.asarray(cos), jnp.asarray(sin)


def _rope(x, cos, sin_signed):
    width = x.shape[-1]
    lane = lax.broadcasted_iota(jnp.int32, x.shape, 1)
    quarter = HEAD_DIM // 4
    up = pltpu.roll(x, width - quarter, axis=1)
    down = pltpu.roll(x, quarter, axis=1)
    partner = jnp.where((lane % (2 * quarter)) < quarter, up, down)
    return x * cos + partner * sin_signed


def _lat_attn_kernel(zq_ref, zkv_ref, kc_ref, vc_ref, qn_ref, kn_ref, sink_ref,
                     cosq_ref, sinq_ref, cosk_ref, sink_tab_ref, ao_ref, kb_ref, kcb_ref, vct_ref, *, seq_len):
    blk = pl.program_id(1)
    rows = ATTN_BLOCK

    @pl.when(blk == 0)
    def _():
        k_all = _head_rmsnorm(zkv_ref[:, 0:KV_W], kn_ref[...])
        kb_ref[...] = _rope(k_all, cosk_ref[...], sink_tab_ref[...]).astype(BF16)
        kcb_ref[...] = kc_ref[...].astype(BF16)
        vct_ref[...] = vc_ref[...].T.astype(BF16)

    q = _head_rmsnorm(zq_ref[...], qn_ref[...])
    q = _rope(q, cosq_ref[...], sinq_ref[...]) * (HEAD_DIM ** -0.5)
    start = jnp.clip((blk - 1) * ATTN_BLOCK, 0, seq_len - WIN_KEYS)
    start = pl.multiple_of(start, ATTN_BLOCK)
    vwin = zkv_ref[pl.ds(start, WIN_KEYS), KV_W:2 * KV_W]

    stacked = (WIN_KEYS, ATTN_GROUP * rows)
    kpos = start + lax.broadcasted_iota(jnp.int32, stacked, 0)
    qpos = blk * ATTN_BLOCK + lax.broadcasted_iota(jnp.int32, stacked, 1) % rows
    band = jnp.abs(qpos - kpos) <= WINDOW

    qt = q.T.astype(BF16)
    kwb = kb_ref[pl.ds(start, WIN_KEYS), :]
    kcb = kcb_ref[...]
    vwt = vwin.T.astype(BF16)
    vct = vct_ref[...]
    out_t = []
    for kh in range(ATTN_KV_HEADS):
        hs = slice(kh * HEAD_DIM, (kh + 1) * HEAD_DIM)
        qg = _group_queries_t(qt, kh)
        s_loc = jnp.where(band, jnp.dot(kwb[:, hs], qg, preferred_element_type=F32), NEG_BIG)
        s_ctx = jnp.dot(kcb[:, hs], qg, preferred_element_type=F32)
        sink = _sink_row(sink_ref, kh, rows)
        m = jnp.maximum(jnp.maximum(jnp.max(s_loc, axis=0, keepdims=True),
                                    jnp.max(s_ctx, axis=0, keepdims=True)), sink)
        p_loc = jnp.exp(s_loc - m)
        p_ctx = jnp.exp(s_ctx - m)
        denom = (jnp.sum(p_loc, axis=0, keepdims=True) + jnp.sum(p_ctx, axis=0, keepdims=True)
                 + jnp.exp(sink - m))
        inv = 1.0 / denom
        o_t = (jnp.dot(vwt[hs, :], (p_loc * inv).astype(BF16), preferred_element_type=F32)
               + jnp.dot(vct[hs, :], (p_ctx * inv).astype(BF16), preferred_element_type=F32))
        out_t.extend(o_t[:, g * rows:(g + 1) * rows] for g in range(ATTN_GROUP))
    ao_ref[...] = jnp.concatenate(out_t, axis=0).T


def _lat_attn_call(zq, zkv, cache_k, cache_v, q_norm, k_norm, sink, *, seq_len, layer):
    n = zq.shape[0]
    n_blk = seq_len // ATTN_BLOCK
    past = cache_k.shape[2]
    cosq, sinq = _rope_tables(seq_len, ATTN_HEADS)
    cosk, sink_tab = _rope_tables(seq_len, ATTN_KV_HEADS)
    return pl.pallas_call(
        functools.partial(_lat_attn_kernel, seq_len=seq_len),
        grid=(n // seq_len, n_blk),
        in_specs=[
            pl.BlockSpec((ATTN_BLOCK, ATTN_W), lambda b, i: (b * n_blk + i, 0)),
            pl.BlockSpec((seq_len, 2 * KV_W), lambda b, i: (b, 0)),
            pl.BlockSpec((None, None, past, KV_W), lambda b, i: (b, layer, 0, 0)),
            pl.BlockSpec((None, None, past, KV_W), lambda b, i: (b, layer, 0, 0)),
            pl.BlockSpec((None, 1, ATTN_W), lambda b, i: (layer, 0, 0)),
            pl.BlockSpec((None, 1, KV_W), lambda b, i: (layer, 0, 0)),
            pl.BlockSpec((None, 1, ATTN_HEADS), lambda b, i: (layer, 0, 0)),
            pl.BlockSpec((ATTN_BLOCK, ATTN_W), lambda b, i: (i, 0)),
            pl.BlockSpec((ATTN_BLOCK, ATTN_W), lambda b, i: (i, 0)),
            pl.BlockSpec((seq_len, KV_W), lambda b, i: (0, 0)),
            pl.BlockSpec((seq_len, KV_W), lambda b, i: (0, 0)),
        ],
        out_specs=pl.BlockSpec((ATTN_BLOCK, ATTN_W), lambda b, i: (b * n_blk + i, 0)),
        out_shape=jax.ShapeDtypeStruct((n, ATTN_W), F32),
        scratch_shapes=[
            pltpu.VMEM((seq_len, KV_W), BF16),
            pltpu.VMEM((past, KV_W), BF16),
            pltpu.VMEM((KV_W, past), BF16),
        ],
        compiler_params=_params("arbitrary", "arbitrary"),
        name="lat_attn",
    )(zq, zkv, cache_k, cache_v, q_norm, k_norm, sink, cosq, sinq, cosk, sink_tab)


def _arrange_in_proj(w):
    lead = w.shape[:-1]
    u, hq, hi, hff, hfb, hg, rest = jnp.split(w, [256, 512, 768, 1024, 1280, 1536], axis=-1)
    heads = lambda t: t.reshape(lead + (HGRN_HEADS, 1, HEAD_DIM))
    q_il = jnp.concatenate([heads(hq), heads(hq)], axis=-2).reshape(lead + (HG_IL,))
    z_il = jnp.concatenate([heads(hff), heads(hfb)], axis=-2).reshape(lead + (HG_IL,))
    out = jnp.concatenate([u, q_il, hi, z_il, hg, rest], axis=-1)
    assert out.shape[-1] == IN_W
    return out


def kernel(x_prompt, x_sample, c, cache_attn_k, cache_attn_v, state_hgrn, c_ctx, w_ada, b_ada, norm_ffn1, norm_mix, norm_ffn2, ffn1_w_gate_up, ffn1_w_down, ffn2_w_gate_up, ffn2_w_down, w_in, w_out, hgrn_lower_bounds, hgrn_norm, q_norm, k_norm, attn_sink):
    batch, seq, _ = x_prompt.shape
    dec_batch, dec_seq, _ = x_sample.shape
    past = cache_attn_k.shape[2]
    assert dec_batch + 1 <= ADA_ROWS

    cond = jnp.concatenate([c_ctx[None], c, jnp.zeros((ADA_ROWS - 1 - dec_batch, D_MODEL), F32)], axis=0)
    mod = _ada_call(cond, w_ada, b_ada).reshape(DEPTH, ADA_ROWS, N_MOD, D_MODEL)

    bf = lambda w: w.astype(BF16)
    w_gu1, w_d1, w_gu2, w_d2 = ffn1_w_gate_up, ffn1_w_down, ffn2_w_gate_up, ffn2_w_down
    w_in_b, w_out_b = _arrange_in_proj(bf(w_in)), bf(w_out)
    row3 = lambda w: w.reshape(DEPTH, 1, w.shape[-1])
    n1, nm, n2 = row3(norm_ffn1), row3(norm_mix), row3(norm_ffn2)
    hn = row3(hgrn_norm)
    qn = row3(jnp.tile(q_norm, (1, ATTN_HEADS)))
    kn = row3(jnp.tile(k_norm, (1, ATTN_KV_HEADS)))
    sink = row3(attn_sink)
    cache_k = cache_attn_k.reshape(dec_batch, DEPTH, past, KV_W)
    cache_v = cache_attn_v.reshape(dec_batch, DEPTH, past, KV_W)
    lb_il = hgrn_lower_bounds.reshape(DEPTH, 2, HGRN_HEADS, HEAD_DIM).transpose(0, 2, 1, 3).reshape(DEPTH, 1, HG_IL)

    streams = {
        "ctx": dict(x=x_prompt.reshape(batch * seq, D_MODEL), seq_len=seq, mod_row0=0, rows_per_mod=batch * seq),
        "lat": dict(x=x_sample.reshape(dec_batch * dec_seq, D_MODEL), seq_len=dec_seq, mod_row0=1, rows_per_mod=dec_seq),
    }
    k_list, v_list, s_list = [], [], []
    for layer in range(DEPTH):
        for name, st in streams.items():
            mk = dict(layer=layer, mod_row0=st["mod_row0"], rows_per_mod=st["rows_per_mod"])
            seq_len = st["seq_len"]
            x, zu, zh, zq, zkv = _ffn_in_call(st["x"], mod, n1, nm, w_gu1, w_d1, w_in_b, **mk)
            fo = _fourier_call(zu, seq_len)
            if name == "ctx":
                ho, sfin = _hgrn_call(zh, lb_il, hn, None, seq_len=seq_len, layer=layer, emit_state=True)
                ao, k_new, v_new = _ctx_attn_call(zq, zkv, qn, kn, sink, seq_len=seq_len, layer=layer)
                k_list.append(k_new.reshape(batch, seq, KV_W))
                v_list.append(v_new.reshape(batch, seq, KV_W))
                s_list.append(sfin)
            else:
                (ho,) = _hgrn_call(zh, lb_il, hn, state_hgrn, seq_len=seq_len, layer=layer, emit_state=False)
                ao = _lat_attn_call(zq, zkv, cache_k, cache_v, qn, kn, sink, seq_len=seq_len, layer=layer)
            st["x"] = _out_ffn_call(x, mod, n2, fo, ho, ao, w_out_b, w_gu2, w_d2, **mk)

    y_p = streams["ctx"]["x"].reshape(batch, seq, D_MODEL)
    y_s = streams["lat"]["x"].reshape(dec_batch, dec_seq, D_MODEL)
    cache = lambda parts: jnp.stack(parts, axis=1).reshape(batch, DEPTH, seq, ATTN_KV_HEADS, HEAD_DIM)
    return (y_p, y_s, cache(k_list), cache(v_list), jnp.stack(s_list, axis=1))
```

```python
import functools

import numpy as np
import jax
import jax.numpy as jnp
from jax import lax
from jax.experimental import pallas as pl
from jax.experimental.pallas import tpu as pltpu

F32 = jnp.float32
BF16 = jnp.bfloat16

D_MODEL = 1024
DEPTH = 2
HEAD_DIM = 64
FOURIER_W = 256
HGRN_W = 256
HGRN_HEADS = 4
ATTN_W = 512
ATTN_HEADS = 8
ATTN_KV_HEADS = 2
ATTN_GROUP = 4
KV_W = 128
MIX_W = 1024
WINDOW = 128
ATTN_BLOCK = 128
D_FF = 2816
GRID_W = 64
ROPE_BASE = 10000.0
EPS = 1e-6
N_MOD = 9
NEG_BIG = -1e30
LB_FLOOR = 1e-30
LOG2_E = 1.4426950408889634

HG_IL = 2 * HGRN_W
ZH_W = HG_IL + HGRN_W + HG_IL + HGRN_W
IN_W = FOURIER_W + ZH_W + ATTN_W + 2 * KV_W

SUBLANES = 8
VMEM_LIMIT_BYTES = 60 * 1024 * 1024

ADA_ROWS = 8
ADA_TN = 1536
FFN_TM = 512
FFN_FC = 256
SMALL_SEQ_ROWS = 1024
HGRN_TILE = 128
HGRN_LEVELS = (128, 64, 32, 16, 8, 4, 2)
WIN_KEYS = 3 * ATTN_BLOCK


def _params(*semantics):
    return pltpu.CompilerParams(dimension_semantics=semantics, vmem_limit_bytes=VMEM_LIMIT_BYTES)


def _mod_norm(x, norm_w, shift, scale):
    ms = jnp.mean(x * x, axis=-1, keepdims=True)
    y = x * lax.rsqrt(ms + EPS) * norm_w
    return y * (1.0 + scale) + shift


def _head_ones(width):
    r = lax.broadcasted_iota(jnp.int32, (width, width), 0) // HEAD_DIM
    c = lax.broadcasted_iota(jnp.int32, (width, width), 1) // HEAD_DIM
    return jnp.where(r == c, 1.0, 0.0).astype(BF16)


def _head_rmsnorm(x, w):
    sq = (x * x).astype(BF16)
    ms = jnp.dot(sq, _head_ones(x.shape[-1]), preferred_element_type=F32) * (1.0 / HEAD_DIM)
    return x * lax.rsqrt(ms + EPS) * w


def _ada_kernel(cond_ref, w_ref, b_ref, o_ref):
    cnd = cond_ref[...]
    act = (cnd * jax.nn.sigmoid(cnd)).astype(BF16)
    o_ref[...] = jnp.dot(act, w_ref[...].astype(BF16), preferred_element_type=F32) + b_ref[...]


def _ada_call(cond, w_ada, b_ada):
    n_out = N_MOD * D_MODEL
    return pl.pallas_call(
        _ada_kernel,
        grid=(DEPTH, n_out // ADA_TN),
        in_specs=[
            pl.BlockSpec((ADA_ROWS, D_MODEL), lambda l, j: (0, 0)),
            pl.BlockSpec((None, D_MODEL, ADA_TN), lambda l, j: (l, 0, j)),
            pl.BlockSpec((None, 1, ADA_TN), lambda l, j: (l, 0, j)),
        ],
        out_specs=pl.BlockSpec((None, ADA_ROWS, ADA_TN), lambda l, j: (l, 0, j)),
        out_shape=jax.ShapeDtypeStruct((DEPTH, ADA_ROWS, n_out), F32),
        compiler_params=_params("arbitrary", "arbitrary"),
        name="adaln",
    )(cond, w_ada, b_ada.reshape(DEPTH, 1, n_out))


def _weight_chunk_copies(wgu_hbm, wd_hbm, stage_gu, stage_d, sems, layer, c, slot):
    lo, hi = c * FFN_FC, (c + 1) * FFN_FC
    return (
        pltpu.make_async_copy(wgu_hbm.at[layer, :, lo:hi], stage_gu.at[slot, :, 0:FFN_FC], sems.at[slot, 0]),
        pltpu.make_async_copy(wgu_hbm.at[layer, :, D_FF + lo:D_FF + hi], stage_gu.at[slot, :, FFN_FC:2 * FFN_FC],
                              sems.at[slot, 1]),
        pltpu.make_async_copy(wd_hbm.at[layer, lo:hi, :], stage_d.at[slot], sems.at[slot, 2]),
    )


def _swiglu_chunks(h, wgu_ref, wd_ref, fetch=None):
    n_chunks = D_FF // FFN_FC
    if fetch is not None:
        stage_gu, stage_d = fetch[2], fetch[3]
        for cp in _weight_chunk_copies(*fetch, 0, 0):
            cp.start()
    acc = None
    for c in range(n_chunks):
        if fetch is not None:
            slot = c % 2
            if c + 1 < n_chunks:
                for cp in _weight_chunk_copies(*fetch, c + 1, 1 - slot):
                    cp.start()
            for cp in _weight_chunk_copies(*fetch, c, slot):
                cp.wait()
            wgu_ref[:, c * FFN_FC:(c + 1) * FFN_FC] = stage_gu[slot, :, 0:FFN_FC].astype(BF16)
            wgu_ref[:, D_FF + c * FFN_FC:D_FF + (c + 1) * FFN_FC] = stage_gu[slot, :, FFN_FC:2 * FFN_FC].astype(BF16)
            wd_ref[c * FFN_FC:(c + 1) * FFN_FC, :] = stage_d[slot].astype(BF16)
        g = jnp.dot(h, wgu_ref[:, c * FFN_FC:(c + 1) * FFN_FC], preferred_element_type=F32)
        u = jnp.dot(h, wgu_ref[:, D_FF + c * FFN_FC:D_FF + (c + 1) * FFN_FC], preferred_element_type=F32)
        act = (g * jax.nn.sigmoid(g) * u).astype(BF16)
        part = jnp.dot(act, wd_ref[c * FFN_FC:(c + 1) * FFN_FC, :], preferred_element_type=F32)
        acc = part if acc is None else acc + part
    return acc


def _mod_spec(layer, mod_row0, rows_per_mod, tm):
    return pl.BlockSpec((None, None, N_MOD, D_MODEL),
                        lambda i, *_: (layer, mod_row0 + (i * tm) // rows_per_mod, 0, 0))


def _resident(shape, layer):
    return pl.BlockSpec((None,) + shape, lambda i: (layer,) + (0,) * len(shape), pipeline_mode=pl.Buffered(1))


def _first_step_fetches(body, fetch):
    first = pl.program_id(0) == 0
    pl.when(first)(functools.partial(body, fetch))
    pl.when(jnp.logical_not(first))(functools.partial(body, None))


def _ffn_weight_scratch():
    return [
        pltpu.VMEM((D_MODEL, 2 * D_FF), BF16),
        pltpu.VMEM((D_FF, D_MODEL), BF16),
        pltpu.VMEM((2, D_MODEL, 2 * FFN_FC), F32),
        pltpu.VMEM((2, FFN_FC, D_MODEL), F32),
        pltpu.SemaphoreType.DMA((2, 3)),
    ]


def _ffn_in_kernel(x_ref, mod_ref, n1_ref, nm_ref, wgu_hbm, wd_hbm, win_ref,
                   x1_ref, zu_ref, zh_ref, zq_ref, zkv_ref, wgu_ref, wd_ref, stage_gu, stage_d, sems, *, layer):
    def body(fetch):
        x = x_ref[...]
        h = _mod_norm(x, n1_ref[...], mod_ref[0:1, :], mod_ref[1:2, :]).astype(BF16)
        x1 = x + 0.5 * mod_ref[2:3, :] * _swiglu_chunks(h, wgu_ref, wd_ref, fetch)
        x1_ref[...] = x1
        h2 = _mod_norm(x1, nm_ref[...], mod_ref[3:4, :], mod_ref[4:5, :]).astype(BF16)
        z = jnp.dot(h2, win_ref[...], preferred_element_type=F32)
        zu_ref[...] = z[:, :FOURIER_W]
        zh_ref[...] = z[:, FOURIER_W:FOURIER_W + ZH_W]
        zq_ref[...] = z[:, FOURIER_W + ZH_W:FOURIER_W + ZH_W + ATTN_W]
        zkv_ref[...] = z[:, FOURIER_W + ZH_W + ATTN_W:]

    _first_step_fetches(body, (wgu_hbm, wd_hbm, stage_gu, stage_d, sems, layer))


def _ffn_in_call(x, mod, n1, nm, w_gu, w_down, w_in, *, layer, mod_row0, rows_per_mod):
    n = x.shape[0]
    tm = FFN_TM
    widths = (D_MODEL, FOURIER_W, ZH_W, ATTN_W, 2 * KV_W)
    return pl.pallas_call(
        functools.partial(_ffn_in_kernel, layer=layer),
        grid=(n // tm,),
        in_specs=[
            pl.BlockSpec((tm, D_MODEL), lambda i: (i, 0)),
            _mod_spec(layer, mod_row0, rows_per_mod, tm),
            pl.BlockSpec((None, 1, D_MODEL), lambda i: (layer, 0, 0)),
            pl.BlockSpec((None, 1, D_MODEL), lambda i: (layer, 0, 0)),
            pl.BlockSpec(memory_space=pl.ANY),
            pl.BlockSpec(memory_space=pl.ANY),
            _resident((D_MODEL, IN_W), layer),
        ],
        out_specs=[pl.BlockSpec((tm, w), lambda i: (i, 0)) for w in widths],
        out_shape=[jax.ShapeDtypeStruct((n, w), F32) for w in widths],
        scratch_shapes=_ffn_weight_scratch(),
        compiler_params=_params("arbitrary"),
        name="ffn_in",
    )(x, mod, n1, nm, w_gu, w_down, w_in)


def _out_ffn_kernel(x_ref, mod_ref, n2_ref, fo_ref, ho_ref, ao_ref, wout_ref, wgu_hbm, wd_hbm, o_ref,
                    wgu_ref, wd_ref, stage_gu, stage_d, sems, *, layer):
    def body(fetch):
        y = jnp.dot(fo_ref[...].astype(BF16), wout_ref[0:FOURIER_W, :], preferred_element_type=F32)
        y += jnp.dot(ho_ref[...].astype(BF16), wout_ref[FOURIER_W:FOURIER_W + HGRN_W, :], preferred_element_type=F32)
        y += jnp.dot(ao_ref[...].astype(BF16), wout_ref[FOURIER_W + HGRN_W:, :], preferred_element_type=F32)
        xm = x_ref[...] + mod_ref[5:6, :] * y
        h = _mod_norm(xm, n2_ref[...], mod_ref[6:7, :], mod_ref[7:8, :]).astype(BF16)
        o_ref[...] = xm + 0.5 * mod_ref[8:9, :] * _swiglu_chunks(h, wgu_ref, wd_ref, fetch)

    _first_step_fetches(body, (wgu_hbm, wd_hbm, stage_gu, stage_d, sems, layer))


def _out_ffn_call(x, mod, n2, fo, ho, ao, w_out, w_gu, w_down, *, layer, mod_row0, rows_per_mod):
    n = x.shape[0]
    tm = FFN_TM
    return pl.pallas_call(
        functools.partial(_out_ffn_kernel, layer=layer),
        grid=(n // tm,),
        in_specs=[
            pl.BlockSpec((tm, D_MODEL), lambda i: (i, 0)),
            _mod_spec(layer, mod_row0, rows_per_mod, tm),
            pl.BlockSpec((None, 1, D_MODEL), lambda i: (layer, 0, 0)),
            pl.BlockSpec((tm, FOURIER_W), lambda i: (i, 0)),
            pl.BlockSpec((tm, HGRN_W), lambda i: (i, 0)),
            pl.BlockSpec((tm, ATTN_W), lambda i: (i, 0)),
            _resident((MIX_W, D_MODEL), layer),
            pl.BlockSpec(memory_space=pl.ANY),
            pl.BlockSpec(memory_space=pl.ANY),
        ],
        out_specs=pl.BlockSpec((tm, D_MODEL), lambda i: (i, 0)),
        out_shape=jax.ShapeDtypeStruct((n, D_MODEL), F32),
        scratch_shapes=_ffn_weight_scratch(),
        compiler_params=_params("arbitrary"),
        name="out_ffn",
    )(x, mod, n2, fo, ho, ao, w_out, w_gu, w_down)


def _dft_tables(seq_len):
    c = np.arange(HEAD_DIM)
    ang_c = 2.0 * np.pi * np.outer(c, c) / HEAD_DIM
    eye = np.eye(FOURIER_W // HEAD_DIM)
    chan = np.concatenate([np.kron(eye, np.cos(ang_c)), np.kron(eye, np.sin(ang_c))], axis=1) / np.sqrt(HEAD_DIM)
    n = np.arange(seq_len)
    ang_l = 2.0 * np.pi * (np.outer(n, n) % seq_len) / seq_len
    pos = np.concatenate([np.cos(ang_l), -np.sin(ang_l)], axis=1) / np.sqrt(seq_len)
    return jnp.asarray(chan, dtype=F32), jnp.asarray(pos, dtype=F32)


def _fourier_kernel(u_ref, chan_ref, pos_ref, o_ref, chan_bf, pos_bf, *, seq_len):
    @pl.when(pl.program_id(0) == 0)
    def _():
        chan_bf[...] = chan_ref[...].astype(BF16)
        pos_bf[...] = pos_ref[...].astype(BF16)

    t = jnp.dot(u_ref[...].astype(BF16), chan_bf[...], preferred_element_type=F32)
    pos = pos_bf[...]
    for r0 in range(0, u_ref.shape[0], seq_len):
        tb = t[r0:r0 + seq_len, :]
        stacked = jnp.concatenate([tb[:, :FOURIER_W], tb[:, FOURIER_W:]], axis=0).astype(BF16)
        o_ref[r0:r0 + seq_len, :] = jnp.dot(pos, stacked, preferred_element_type=F32)


def _fourier_call(zu, seq_len):
    n = zu.shape[0]
    chan, pos = _dft_tables(seq_len)
    rows = max(seq_len, SMALL_SEQ_ROWS)
    return pl.pallas_call(
        functools.partial(_fourier_kernel, seq_len=seq_len),
        grid=(n // rows,),
        in_specs=[
            pl.BlockSpec((rows, FOURIER_W), lambda b: (b, 0)),
            pl.BlockSpec((FOURIER_W, 2 * FOURIER_W), lambda b: (0, 0), pipeline_mode=pl.Buffered(1)),
            pl.BlockSpec((seq_len, 2 * seq_len), lambda b: (0, 0), pipeline_mode=pl.Buffered(1)),
        ],
        out_specs=pl.BlockSpec((rows, FOURIER_W), lambda b: (b, 0)),
        out_shape=jax.ShapeDtypeStruct((n, FOURIER_W), F32),
        scratch_shapes=[pltpu.VMEM((FOURIER_W, 2 * FOURIER_W), BF16), pltpu.VMEM((seq_len, 2 * seq_len), BF16)],
        compiler_params=_params("arbitrary"),
        name="fourier",
    )(zu, chan, pos)


def _split3(x):
    hi = x.astype(BF16)
    r1 = x - hi.astype(F32)
    mid = r1.astype(BF16)
    lo = (r1 - mid.astype(F32)).astype(BF16)
    return hi, mid, lo


def _level_mid(b, s, lane_bwd3):
    tile, lanes = b.shape
    if s >= SUBLANES:
        b3 = b.reshape(tile // s, s, lanes)
        mid = jnp.where(lane_bwd3, b3[:, s // 2:s // 2 + 1, :], b3[:, s // 2 - 1:s // 2, :])
        return jnp.broadcast_to(mid, b3.shape).reshape(tile, lanes)
    assert s == 4
    b3 = b.reshape(tile // SUBLANES, SUBLANES, lanes)
    sub = lax.broadcasted_iota(jnp.int32, (1, SUBLANES, 1), 1)
    lo = jnp.where(lane_bwd3, b3[:, 2:3, :], b3[:, 1:2, :])
    hi = jnp.where(lane_bwd3, b3[:, 6:7, :], b3[:, 5:6, :])
    return jnp.where(sub < 4, lo, hi).reshape(tile, lanes)


def _hgrn_kernel(*refs, seq_len, layer, has_s0, emit_state):
    refs = list(refs)
    zh_ref, lb_ref, nw_ref, tri_ref, ones_ref, sgn_ref, qm_ref = refs[:7]
    pos = 7
    s0_ref = sf_ref = None
    if has_s0:
        s0_ref = refs[pos]
        pos += 1
    ho_ref = refs[pos]
    pos += 1
    if emit_state:
        sf_ref = refs[pos]
        pos += 1
    qin_ref, dec_ref, kvt_ref, stf_ref, stb_ref, oacc_ref = refs[pos:]

    tile = HGRN_TILE
    n_tiles = seq_len // tile
    c_q, c_v, c_z, c_g = 0, HG_IL, HG_IL + HGRN_W, 2 * HG_IL + HGRN_W

    raw = lb_ref[...]
    e = jnp.exp(raw - jnp.max(raw, axis=0, keepdims=True))
    soft = e / jnp.sum(e, axis=0, keepdims=True)
    cum = soft[0]
    for l in range(1, layer + 1):
        cum = cum + soft[l]
    lbd = cum - soft[0]
    lb_floor = jnp.maximum(lbd, LB_FLOOR)
    one_m_lb = 1.0 - lbd

    lane1 = lax.broadcasted_iota(jnp.int32, (1, HG_IL), 1)
    lane_bwd1 = (lane1 // HEAD_DIM) % 2 == 1
    lane_bwd3 = lane_bwd1.reshape(1, 1, HG_IL)
    row = lax.broadcasted_iota(jnp.int32, (tile, HG_IL), 0)
    lane = lax.broadcasted_iota(jnp.int32, (tile, HG_IL), 1)
    lane_bwd = (lane // HEAD_DIM) % 2 == 1
    pair_xor = row ^ (lane % tile)
    ev_mask = (lax.broadcasted_iota(jnp.int32, (HGRN_HEADS * tile, HGRN_W), 0) // tile
               == lax.broadcasted_iota(jnp.int32, (HGRN_HEADS * tile, HGRN_W), 1) // HEAD_DIM)
    bd_mask = (lax.broadcasted_iota(jnp.int32, (HGRN_W, HG_IL), 0) // HEAD_DIM
               == lax.broadcasted_iota(jnp.int32, (HGRN_W, HG_IL), 1) // (2 * HEAD_DIM))
    lane_head = lax.broadcasted_iota(jnp.int32, (HEAD_DIM, HG_IL), 1) // (2 * HEAD_DIM)
    lane_bwd_s = (lax.broadcasted_iota(jnp.int32, (HEAD_DIM, HG_IL), 1) // HEAD_DIM) % 2 == 1
    nt_dims = (((1,), (1,)), ((), ()))
    head_w = 2 * HEAD_DIM

    def tile_body(t, carry):
        rows = pl.ds(pl.multiple_of(t * tile, tile), tile)
        q = zh_ref[rows, c_q:c_q + HG_IL] * (HEAD_DIM ** -0.5)
        v = zh_ref[rows, c_v:c_v + HGRN_W]
        z = zh_ref[rows, c_z:c_z + HG_IL]
        ez = jnp.exp(-jnp.abs(z))
        r = 1.0 / (1.0 + ez)
        tr = ez * r
        pos_z = z >= 0.0
        f = lb_floor + one_m_lb * jnp.where(pos_z, r, tr)
        logf = jnp.log(f)
        kk = one_m_lb * jnp.where(pos_z, tr, r)

        pieces = jnp.concatenate(_split3(logf), axis=1)
        acc = jnp.dot(tri_ref[...], pieces, preferred_element_type=F32)
        prefix = acc[:, :HG_IL] + acc[:, HG_IL:2 * HG_IL] + acc[:, 2 * HG_IL:]
        total = prefix[tile - 1:tile, :]
        b = jnp.where(lane_bwd, total - prefix + logf, prefix)

        a0 = jnp.dot((q * kk).astype(BF16), ones_ref[...], preferred_element_type=F32)
        o = a0 * v

        b2 = b * LOG2_E
        scores = None
        for li, s in enumerate(HGRN_LEVELS):
            qm = qm_ref[li]
            if s == 2:
                x = f * qm + (1.0 - qm)
            else:
                x = jnp.exp2(sgn_ref[li] * (b2 - _level_mid(b2, s, lane_bwd3)))
            xq = x * qm
            qt = (q * xq).astype(BF16)
            kt = (kk * (x - xq)).astype(BF16)
            sc = jnp.concatenate(
                [lax.dot_general(qt[:, h * head_w:(h + 1) * head_w], kt[:, h * head_w:(h + 1) * head_w],
                                 nt_dims, preferred_element_type=F32) for h in range(HGRN_HEADS)], axis=1)
            scores = sc if scores is None else jnp.where(pair_xor < s, sc, scores)
        ev = jnp.where(ev_mask, jnp.concatenate([v] * HGRN_HEADS, axis=0), 0.0).astype(BF16)
        o = o + jnp.dot(scores.astype(BF16), ev, preferred_element_type=F32)
        oacc_ref[rows, :] = o

        qin_ref[rows, :] = (q * jnp.exp(b)).astype(BF16)
        kout = (kk * jnp.exp(total - b)).astype(BF16)
        dec_ref[t] = jnp.exp(total)
        full = jnp.dot(v.T.astype(BF16), kout, preferred_element_type=F32)
        kvt = full[0:HEAD_DIM, :]
        for hh in range(1, HGRN_HEADS):
            kvt = jnp.where(lane_head == hh, full[hh * HEAD_DIM:(hh + 1) * HEAD_DIM, :], kvt)
        kvt_ref[t] = kvt
        return carry

    lax.fori_loop(0, n_tiles, tile_body, 0)

    if has_s0:
        st0 = jnp.concatenate([s0_ref[d, h] for h in range(HGRN_HEADS) for d in range(2)], axis=0).T
    else:
        st0 = jnp.zeros((HEAD_DIM, HG_IL), F32)

    def scan_step(i, st):
        tf, tb = i, n_tiles - 1 - i
        stf_ref[tf] = st
        stb_ref[tb] = st
        dec = jnp.where(lane_bwd1, dec_ref[tb], dec_ref[tf])
        kv = jnp.where(lane_bwd_s, kvt_ref[tb], kvt_ref[tf])
        return st * dec + kv

    st_fin = lax.fori_loop(0, n_tiles, scan_step, st0)
    if emit_state:
        st_t = st_fin.T
        for h in range(HGRN_HEADS):
            for d in range(2):
                r0 = (2 * h + d) * HEAD_DIM
                sf_ref[d, h] = st_t[r0:r0 + HEAD_DIM, :]

    def out_tile(t):
        rows = pl.ds(pl.multiple_of(t * tile, tile), tile)
        st = jnp.where(lane_bwd_s, stb_ref[t], stf_ref[t])
        bd = jnp.where(bd_mask, jnp.concatenate([st] * HGRN_HEADS, axis=0), 0.0).T.astype(BF16)
        o = oacc_ref[rows, :] + jnp.dot(qin_ref[rows, :], bd, preferred_element_type=F32)
        hg = zh_ref[rows, c_g:c_g + HGRN_W]
        ho_ref[rows, :] = _head_rmsnorm(o, nw_ref[...]) * (hg * jax.nn.sigmoid(hg))

    def out_body(i, carry):
        out_tile(2 * i)
        out_tile(2 * i + 1)
        return carry

    assert n_tiles % 2 == 0
    lax.fori_loop(0, n_tiles // 2, out_body, 0)


def _hgrn_consts():
    i = np.arange(HGRN_TILE)
    tri = (i[None, :] <= i[:, None]).astype(np.float32)
    r = np.arange(HG_IL) // (2 * HEAD_DIM)
    c = np.arange(HGRN_W) // HEAD_DIM
    ones = (r[:, None] == c[None, :]).astype(np.float32)
    lane_bwd = (np.arange(HG_IL) // HEAD_DIM) % 2 == 1
    qm = np.stack([(((i[:, None] % s) >= s // 2) != lane_bwd[None, :]) for s in HGRN_LEVELS]).astype(np.float32)
    return (jnp.asarray(tri, dtype=BF16), jnp.asarray(ones, dtype=BF16),
            jnp.asarray(2.0 * qm - 1.0, dtype=F32), jnp.asarray(qm, dtype=F32))


def _hgrn_call(zh, lb_il, norm_w, s0, *, seq_len, layer, emit_state):
    n = zh.shape[0]
    n_seq = n // seq_len
    has_s0 = s0 is not None
    state_block = (2, HGRN_HEADS, HEAD_DIM, HEAD_DIM)
    n_tiles = seq_len // HGRN_TILE
    tri, ones, lvl_sgn, lvl_qm = _hgrn_consts()
    n_lvl = len(HGRN_LEVELS)
    in_specs = [
        pl.BlockSpec((seq_len, ZH_W), lambda b: (b, 0)),
        pl.BlockSpec((DEPTH, 1, HG_IL), lambda b: (0, 0, 0)),
        pl.BlockSpec((None, 1, HGRN_W), lambda b: (layer, 0, 0)),
        pl.BlockSpec((HGRN_TILE, HGRN_TILE), lambda b: (0, 0)),
        pl.BlockSpec((HG_IL, HGRN_W), lambda b: (0, 0)),
        pl.BlockSpec((n_lvl, HGRN_TILE, HG_IL), lambda b: (0, 0, 0), pipeline_mode=pl.Buffered(1)),
        pl.BlockSpec((n_lvl, HGRN_TILE, HG_IL), lambda b: (0, 0, 0), pipeline_mode=pl.Buffered(1)),
    ]
    args = [zh, lb_il, norm_w, tri, ones, lvl_sgn, lvl_qm]
    if has_s0:
        in_specs.append(pl.BlockSpec((None, None) + state_block, lambda b: (b, layer, 0, 0, 0, 0)))
        args.append(s0)
    out_specs = [pl.BlockSpec((seq_len, HGRN_W), lambda b: (b, 0))]
    out_shape = [jax.ShapeDtypeStruct((n, HGRN_W), F32)]
    if emit_state:
        out_specs.append(pl.BlockSpec((None,) + state_block, lambda b: (b, 0, 0, 0, 0)))
        out_shape.append(jax.ShapeDtypeStruct((n_seq,) + state_block, F32))
    return pl.pallas_call(
        functools.partial(_hgrn_kernel, seq_len=seq_len, layer=layer, has_s0=has_s0, emit_state=emit_state),
        grid=(n_seq,),
        in_specs=in_specs,
        out_specs=out_specs,
        out_shape=out_shape,
        scratch_shapes=[
            pltpu.VMEM((seq_len, HG_IL), BF16),
            pltpu.VMEM((n_tiles, 1, HG_IL), F32),
            pltpu.VMEM((n_tiles, HEAD_DIM, HG_IL), F32),
            pltpu.VMEM((n_tiles, HEAD_DIM, HG_IL), F32),
            pltpu.VMEM((n_tiles, HEAD_DIM, HG_IL), F32),
            pltpu.VMEM((seq_len, HGRN_W), F32),
        ],
        compiler_params=_params("arbitrary"),
        name="hgrn",
    )(*args)


def _sink_row(sink_ref, kv_head, rows):
    grp = lax.broadcasted_iota(jnp.int32, (1, ATTN_GROUP * rows), 1) // rows
    base = kv_head * ATTN_GROUP
    out = jnp.broadcast_to(sink_ref[:, base:base + 1], (1, ATTN_GROUP * rows))
    for g in range(1, ATTN_GROUP):
        out = jnp.where(grp == g, sink_ref[:, base + g:base + g + 1], out)
    return out


def _group_queries_t(qt, kv_head):
    base = kv_head * ATTN_GROUP
    return jnp.concatenate([qt[(base + g) * HEAD_DIM:(base + g + 1) * HEAD_DIM, :] for g in range(ATTN_GROUP)], axis=1)


def _ctx_attn_kernel(zq_ref, zkv_ref, qn_ref, kn_ref, sink_ref, ao_ref, kc_ref, vc_ref, *, seq_len):
    rows = seq_len
    q_all = _head_rmsnorm(zq_ref[...], qn_ref[...]) * (HEAD_DIM ** -0.5)
    k_all = _head_rmsnorm(zkv_ref[:, :KV_W], kn_ref[...])
    v_all = zkv_ref[:, KV_W:]
    kc_ref[...] = k_all
    vc_ref[...] = v_all
    for r0 in range(0, zq_ref.shape[0], seq_len):
        qt = q_all[r0:r0 + rows, :].T.astype(BF16)
        kb = k_all[r0:r0 + rows, :].astype(BF16)
        vt = v_all[r0:r0 + rows, :].T.astype(BF16)
        out_t = []
        for kh in range(ATTN_KV_HEADS):
            hs = slice(kh * HEAD_DIM, (kh + 1) * HEAD_DIM)
            s = jnp.dot(kb[:, hs], _group_queries_t(qt, kh), preferred_element_type=F32)
            sink = _sink_row(sink_ref, kh, rows)
            m = jnp.maximum(jnp.max(s, axis=0, keepdims=True), sink)
            p = jnp.exp(s - m)
            denom = jnp.sum(p, axis=0, keepdims=True) + jnp.exp(sink - m)
            o_t = jnp.dot(vt[hs, :], p.astype(BF16), preferred_element_type=F32) * (1.0 / denom)
            out_t.extend(o_t[:, g * rows:(g + 1) * rows] for g in range(ATTN_GROUP))
        ao_ref[r0:r0 + rows, :] = jnp.concatenate(out_t, axis=0).T


def _ctx_attn_call(zq, zkv, q_norm, k_norm, sink, *, seq_len, layer):
    n = zq.shape[0]
    rows = max(seq_len, SMALL_SEQ_ROWS)
    return pl.pallas_call(
        functools.partial(_ctx_attn_kernel, seq_len=seq_len),
        grid=(n // rows,),
        in_specs=[
            pl.BlockSpec((rows, ATTN_W), lambda b: (b, 0)),
            pl.BlockSpec((rows, 2 * KV_W), lambda b: (b, 0)),
            pl.BlockSpec((None, 1, ATTN_W), lambda b: (layer, 0, 0)),
            pl.BlockSpec((None, 1, KV_W), lambda b: (layer, 0, 0)),
            pl.BlockSpec((None, 1, ATTN_HEADS), lambda b: (layer, 0, 0)),
        ],
        out_specs=[
            pl.BlockSpec((rows, ATTN_W), lambda b: (b, 0)),
            pl.BlockSpec((rows, KV_W), lambda b: (b, 0)),
            pl.BlockSpec((rows, KV_W), lambda b: (b, 0)),
        ],
        out_shape=[
            jax.ShapeDtypeStruct((n, ATTN_W), F32),
            jax.ShapeDtypeStruct((n, KV_W), F32),
            jax.ShapeDtypeStruct((n, KV_W), F32),
        ],
        compiler_params=_params("arbitrary"),
        name="ctx_attn",
    )(zq, zkv, q_norm, k_norm, sink)


def _rope_tables(seq_len, n_heads):
    t = np.arange(seq_len)
    row, col = t // GRID_W, t % GRID_W
    half = HEAD_DIM // 2
    inv = (ROPE_BASE ** (-np.arange(0, half, 2, dtype=np.float32) / half)).astype(np.float32)

    def one(pos):
        ang = pos.astype(np.float32)[:, None] * inv[None]
        c, s = np.cos(ang), np.sin(ang)
        return np.concatenate([c, c], axis=1), np.concatenate([-s, s], axis=1)

    cr, sr = one(row)
    cc, sc = one(col)
    cos = np.tile(np.concatenate([cr, cc], axis=1), (1, n_heads)).astype(np.float32)
    sin = np.tile(np.concatenate([sr, sc], axis=1), (1, n_heads)).astype(np.float32)
    return jnp.asarray(cos), jnp.asarray(sin)


def _rope(x, cos, sin_signed):
    width = x.shape[-1]
    lane = lax.broadcasted_iota(jnp.int32, x.shape, 1)
    quarter = HEAD_DIM // 4
    up = pltpu.roll(x, width - quarter, axis=1)
    down = pltpu.roll(x, quarter, axis=1)
    partner = jnp.where((lane % (2 * quarter)) < quarter, up, down)
    return x * cos + partner * sin_signed


def _lat_attn_kernel(zq_ref, zkv_ref, kc_ref, vc_ref, qn_ref, kn_ref, sink_ref,
                     cosq_ref, sinq_ref, cosk_ref, sink_tab_ref, band_ref, ao_ref, kb_ref, kcb_ref, vct_ref,
                     *, seq_len):
    blk = pl.program_id(1)
    rows = ATTN_BLOCK

    @pl.when(blk == 0)
    def _():
        k_all = _head_rmsnorm(zkv_ref[:, 0:KV_W], kn_ref[...])
        kb_ref[...] = _rope(k_all, cosk_ref[...], sink_tab_ref[...]).astype(BF16)
        kcb_ref[...] = kc_ref[...].astype(BF16)
        vct_ref[...] = vc_ref[...].T.astype(BF16)

    q = _head_rmsnorm(zq_ref[...], qn_ref[...])
    q = _rope(q, cosq_ref[...], sinq_ref[...]) * (HEAD_DIM ** -0.5)
    start = jnp.clip((blk - 1) * ATTN_BLOCK, 0, seq_len - WIN_KEYS)
    start = pl.multiple_of(start, ATTN_BLOCK)
    vwin = zkv_ref[pl.ds(start, WIN_KEYS), KV_W:2 * KV_W]

    band_bias = band_ref[blk - start // ATTN_BLOCK]

    qt = q.T.astype(BF16)
    kwb = kb_ref[pl.ds(start, WIN_KEYS), :]
    kcb = kcb_ref[...]
    vwt = vwin.T.astype(BF16)
    vct = vct_ref[...]
    out_t = []
    for kh in range(ATTN_KV_HEADS):
        hs = slice(kh * HEAD_DIM, (kh + 1) * HEAD_DIM)
        qg = _group_queries_t(qt, kh)
        s_loc = jnp.dot(kwb[:, hs], qg, preferred_element_type=F32) + band_bias
        s_ctx = jnp.dot(kcb[:, hs], qg, preferred_element_type=F32)
        sink = _sink_row(sink_ref, kh, rows)
        m = jnp.maximum(jnp.maximum(jnp.max(s_loc, axis=0, keepdims=True),
                                    jnp.max(s_ctx, axis=0, keepdims=True)), sink)
        p_loc = jnp.exp(s_loc - m)
        p_ctx = jnp.exp(s_ctx - m)
        denom = (jnp.sum(p_loc, axis=0, keepdims=True) + jnp.sum(p_ctx, axis=0, keepdims=True)
                 + jnp.exp(sink - m))
        o_t = (jnp.dot(vwt[hs, :], p_loc.astype(BF16), preferred_element_type=F32)
               + jnp.dot(vct[hs, :], p_ctx.astype(BF16), preferred_element_type=F32)) * (1.0 / denom)
        out_t.extend(o_t[:, g * rows:(g + 1) * rows] for g in range(ATTN_GROUP))
    ao_ref[...] = jnp.concatenate(out_t, axis=0).T


def _lat_attn_call(zq, zkv, cache_k, cache_v, q_norm, k_norm, sink, *, seq_len, layer):
    n = zq.shape[0]
    n_blk = seq_len // ATTN_BLOCK
    past = cache_k.shape[2]
    cosq, sinq = _rope_tables(seq_len, ATTN_HEADS)
    cosk, sink_tab = _rope_tables(seq_len, ATTN_KV_HEADS)
    kr = np.arange(WIN_KEYS)[:, None]
    qi = np.arange(ATTN_GROUP * ATTN_BLOCK)[None, :] % ATTN_BLOCK
    band = np.stack([np.where(np.abs(qi + off * ATTN_BLOCK - kr) <= WINDOW, 0.0, NEG_BIG) for off in range(3)])
    band = jnp.asarray(band, dtype=F32)
    return pl.pallas_call(
        functools.partial(_lat_attn_kernel, seq_len=seq_len),
        grid=(n // seq_len, n_blk),
        in_specs=[
            pl.BlockSpec((ATTN_BLOCK, ATTN_W), lambda b, i: (b * n_blk + i, 0)),
            pl.BlockSpec((seq_len, 2 * KV_W), lambda b, i: (b, 0)),
            pl.BlockSpec((None, None, past, KV_W), lambda b, i: (b, layer, 0, 0)),
            pl.BlockSpec((None, None, past, KV_W), lambda b, i: (b, layer, 0, 0)),
            pl.BlockSpec((None, 1, ATTN_W), lambda b, i: (layer, 0, 0)),
            pl.BlockSpec((None, 1, KV_W), lambda b, i: (layer, 0, 0)),
            pl.BlockSpec((None, 1, ATTN_HEADS), lambda b, i: (layer, 0, 0)),
            pl.BlockSpec((ATTN_BLOCK, ATTN_W), lambda b, i: (i, 0)),
            pl.BlockSpec((ATTN_BLOCK, ATTN_W), lambda b, i: (i, 0)),
            pl.BlockSpec((seq_len, KV_W), lambda b, i: (0, 0)),
            pl.BlockSpec((seq_len, KV_W), lambda b, i: (0, 0)),
            pl.BlockSpec((3, WIN_KEYS, ATTN_GROUP * ATTN_BLOCK), lambda b, i: (0, 0, 0), pipeline_mode=pl.Buffered(1)),
        ],
        out_specs=pl.BlockSpec((ATTN_BLOCK, ATTN_W), lambda b, i: (b * n_blk + i, 0)),
        out_shape=jax.ShapeDtypeStruct((n, ATTN_W), F32),
        scratch_shapes=[
            pltpu.VMEM((seq_len, KV_W), BF16),
            pltpu.VMEM((past, KV_W), BF16),
            pltpu.VMEM((KV_W, past), BF16),
        ],
        compiler_params=_params("arbitrary", "arbitrary"),
        name="lat_attn",
    )(zq, zkv, cache_k, cache_v, q_norm, k_norm, sink, cosq, sinq, cosk, sink_tab, band)


def _arrange_in_proj(w):
    lead = w.shape[:-1]
    u, hq, hi, hff, hfb, hg, rest = jnp.split(w, [256, 512, 768, 1024, 1280, 1536], axis=-1)
    heads = lambda t: t.reshape(lead + (HGRN_HEADS, 1, HEAD_DIM))
    q_il = jnp.concatenate([heads(hq), heads(hq)], axis=-2).reshape(lead + (HG_IL,))
    z_il = jnp.concatenate([heads(hff), heads(hfb)], axis=-2).reshape(lead + (HG_IL,))
    out = jnp.concatenate([u, q_il, hi, z_il, hg, rest], axis=-1)
    assert out.shape[-1] == IN_W
    return out


def kernel(x_prompt, x_sample, c, cache_attn_k, cache_attn_v, state_hgrn, c_ctx, w_ada, b_ada, norm_ffn1, norm_mix, norm_ffn2, ffn1_w_gate_up, ffn1_w_down, ffn2_w_gate_up, ffn2_w_down, w_in, w_out, hgrn_lower_bounds, hgrn_norm, q_norm, k_norm, attn_sink):
    batch, seq, _ = x_prompt.shape
    dec_batch, dec_seq, _ = x_sample.shape
    past = cache_attn_k.shape[2]
    assert dec_batch + 1 <= ADA_ROWS

    cond = jnp.concatenate([c_ctx[None], c, jnp.zeros((ADA_ROWS - 1 - dec_batch, D_MODEL), F32)], axis=0)
    mod = _ada_call(cond, w_ada, b_ada).reshape(DEPTH, ADA_ROWS, N_MOD, D_MODEL)

    bf = lambda w: w.astype(BF16)
    w_gu1, w_d1, w_gu2, w_d2 = ffn1_w_gate_up, ffn1_w_down, ffn2_w_gate_up, ffn2_w_down
    w_in_b, w_out_b = _arrange_in_proj(bf(w_in)), bf(w_out)
    row3 = lambda w: w.reshape(DEPTH, 1, w.shape[-1])
    n1, nm, n2 = row3(norm_ffn1), row3(norm_mix), row3(norm_ffn2)
    hn = row3(hgrn_norm)
    qn = row3(jnp.tile(q_norm, (1, ATTN_HEADS)))
    kn = row3(jnp.tile(k_norm, (1, ATTN_KV_HEADS)))
    sink = row3(attn_sink)
    cache_k = cache_attn_k.reshape(dec_batch, DEPTH, past, KV_W)
    cache_v = cache_attn_v.reshape(dec_batch, DEPTH, past, KV_W)
    lb_il = hgrn_lower_bounds.reshape(DEPTH, 2, HGRN_HEADS, HEAD_DIM).transpose(0, 2, 1, 3).reshape(DEPTH, 1, HG_IL)

    streams = {
        "ctx": dict(x=x_prompt.reshape(batch * seq, D_MODEL), seq_len=seq, mod_row0=0, rows_per_mod=batch * seq),
        "lat": dict(x=x_sample.reshape(dec_batch * dec_seq, D_MODEL), seq_len=dec_seq, mod_row0=1, rows_per_mod=dec_seq),
    }
    k_list, v_list, s_list = [], [], []
    for layer in range(DEPTH):
        for name, st in streams.items():
            mk = dict(layer=layer, mod_row0=st["mod_row0"], rows_per_mod=st["rows_per_mod"])
            seq_len = st["seq_len"]
            x, zu, zh, zq, zkv = _ffn_in_call(st["x"], mod, n1, nm, w_gu1, w_d1, w_in_b, **mk)
            fo = _fourier_call(zu, seq_len)
            if name == "ctx":
                ho, sfin = _hgrn_call(zh, lb_il, hn, None, seq_len=seq_len, layer=layer, emit_state=True)
                ao, k_new, v_new = _ctx_attn_call(zq, zkv, qn, kn, sink, seq_len=seq_len, layer=layer)
                k_list.append(k_new.reshape(batch, seq, KV_W))
                v_list.append(v_new.reshape(batch, seq, KV_W))
                s_list.append(sfin)
            else:
                (ho,) = _hgrn_call(zh, lb_il, hn, state_hgrn, seq_len=seq_len, layer=layer, emit_state=False)
                ao = _lat_attn_call(zq, zkv, cache_k, cache_v, qn, kn, sink, seq_len=seq_len, layer=layer)
            st["x"] = _out_ffn_call(x, mod, n2, fo, ho, ao, w_out_b, w_gu2, w_d2, **mk)

    y_p = streams["ctx"]["x"].reshape(batch, seq, D_MODEL)
    y_s = streams["lat"]["x"].reshape(dec_batch, dec_seq, D_MODEL)
    cache = lambda parts: jnp.stack(parts, axis=1).reshape(batch, DEPTH, seq, ATTN_KV_HEADS, HEAD_DIM)
    return (y_p, y_s, cache(k_list), cache(v_list), jnp.stack(s_list, axis=1))
```

```python
import functools

import numpy as np
import jax
import jax.numpy as jnp
from jax import lax
from jax.experimental import pallas as pl
from jax.experimental.pallas import tpu as pltpu

F32 = jnp.float32
BF16 = jnp.bfloat16

D_MODEL = 1024
DEPTH = 2
HEAD_DIM = 64
FOURIER_W = 256
HGRN_W = 256
HGRN_HEADS = 4
ATTN_W = 512
ATTN_HEADS = 8
ATTN_KV_HEADS = 2
ATTN_GROUP = 4
KV_W = 128
MIX_W = 1024
WINDOW = 128
ATTN_BLOCK = 128
D_FF = 2816
GRID_W = 64
ROPE_BASE = 10000.0
EPS = 1e-6
N_MOD = 9
NEG_BIG = -1e30
LB_FLOOR = 1e-30
LOG2_E = 1.4426950408889634

HG_2 = 2 * HGRN_W
ZH_W = 5 * HGRN_W
IN_W = FOURIER_W + ZH_W + ATTN_W + 2 * KV_W

SUBLANES = 8
VMEM_LIMIT_BYTES = 60 * 1024 * 1024

ADA_ROWS = 8
ADA_TN = 1536
FFN_TM = 512
FFN_FC = 256
SMALL_SEQ_ROWS = 1024
HGRN_TILE = 128
HGRN_LEVELS = (128, 64, 32, 16, 8, 4, 2)
WIN_KEYS = 3 * ATTN_BLOCK


def _params(*semantics):
    return pltpu.CompilerParams(dimension_semantics=semantics, vmem_limit_bytes=VMEM_LIMIT_BYTES)


def _mod_norm(x, norm_w, shift, scale):
    ms = jnp.mean(x * x, axis=-1, keepdims=True)
    y = x * lax.rsqrt(ms + EPS) * norm_w
    return y * (1.0 + scale) + shift


def _head_ones(width):
    r = lax.broadcasted_iota(jnp.int32, (width, width), 0) // HEAD_DIM
    c = lax.broadcasted_iota(jnp.int32, (width, width), 1) // HEAD_DIM
    return jnp.where(r == c, 1.0, 0.0).astype(BF16)


def _head_rmsnorm(x, w):
    sq = (x * x).astype(BF16)
    ms = jnp.dot(sq, _head_ones(x.shape[-1]), preferred_element_type=F32) * (1.0 / HEAD_DIM)
    return x * lax.rsqrt(ms + EPS) * w


def _ada_kernel(cond_ref, w_ref, b_ref, o_ref):
    cnd = cond_ref[...]
    act = (cnd * jax.nn.sigmoid(cnd)).astype(BF16)
    o_ref[...] = jnp.dot(act, w_ref[...].astype(BF16), preferred_element_type=F32) + b_ref[...]


def _ada_call(cond, w_ada, b_ada):
    n_out = N_MOD * D_MODEL
    return pl.pallas_call(
        _ada_kernel,
        grid=(DEPTH, n_out // ADA_TN),
        in_specs=[
            pl.BlockSpec((ADA_ROWS, D_MODEL), lambda l, j: (0, 0)),
            pl.BlockSpec((None, D_MODEL, ADA_TN), lambda l, j: (l, 0, j)),
            pl.BlockSpec((None, 1, ADA_TN), lambda l, j: (l, 0, j)),
        ],
        out_specs=pl.BlockSpec((None, ADA_ROWS, ADA_TN), lambda l, j: (l, 0, j)),
        out_shape=jax.ShapeDtypeStruct((DEPTH, ADA_ROWS, n_out), F32),
        compiler_params=_params("arbitrary", "arbitrary"),
        name="adaln",
    )(cond, w_ada, b_ada.reshape(DEPTH, 1, n_out))


def _weight_chunk_copies(wgu_hbm, wd_hbm, stage_gu, stage_d, sems, layer, c, slot):
    lo, hi = c * FFN_FC, (c + 1) * FFN_FC
    return (
        pltpu.make_async_copy(wgu_hbm.at[layer, :, lo:hi], stage_gu.at[slot, :, 0:FFN_FC], sems.at[slot, 0]),
        pltpu.make_async_copy(wgu_hbm.at[layer, :, D_FF + lo:D_FF + hi], stage_gu.at[slot, :, FFN_FC:2 * FFN_FC],
                              sems.at[slot, 1]),
        pltpu.make_async_copy(wd_hbm.at[layer, lo:hi, :], stage_d.at[slot], sems.at[slot, 2]),
    )


def _swiglu_chunks(h, wgu_ref, wd_ref, fetch=None):
    n_chunks = D_FF // FFN_FC
    if fetch is not None:
        stage_gu, stage_d = fetch[2], fetch[3]
        for cp in _weight_chunk_copies(*fetch, 0, 0):
            cp.start()
    acc = None
    for c in range(n_chunks):
        if fetch is not None:
            slot = c % 2
            if c + 1 < n_chunks:
                for cp in _weight_chunk_copies(*fetch, c + 1, 1 - slot):
                    cp.start()
            for cp in _weight_chunk_copies(*fetch, c, slot):
                cp.wait()
            wgu_ref[:, c * FFN_FC:(c + 1) * FFN_FC] = stage_gu[slot, :, 0:FFN_FC].astype(BF16)
            wgu_ref[:, D_FF + c * FFN_FC:D_FF + (c + 1) * FFN_FC] = stage_gu[slot, :, FFN_FC:2 * FFN_FC].astype(BF16)
            wd_ref[c * FFN_FC:(c + 1) * FFN_FC, :] = stage_d[slot].astype(BF16)
        g = jnp.dot(h, wgu_ref[:, c * FFN_FC:(c + 1) * FFN_FC], preferred_element_type=F32)
        u = jnp.dot(h, wgu_ref[:, D_FF + c * FFN_FC:D_FF + (c + 1) * FFN_FC], preferred_element_type=F32)
        act = (g * jax.nn.sigmoid(g) * u).astype(BF16)
        part = jnp.dot(act, wd_ref[c * FFN_FC:(c + 1) * FFN_FC, :], preferred_element_type=F32)
        acc = part if acc is None else acc + part
    return acc


def _mod_spec(layer, mod_row0, rows_per_mod, tm):
    return pl.BlockSpec((None, None, N_MOD, D_MODEL),
                        lambda i, *_: (layer, mod_row0 + (i * tm) // rows_per_mod, 0, 0))


def _resident(shape, layer):
    return pl.BlockSpec((None,) + shape, lambda i: (layer,) + (0,) * len(shape), pipeline_mode=pl.Buffered(1))


def _first_step_fetches(body, fetch):
    first = pl.program_id(0) == 0
    pl.when(first)(functools.partial(body, fetch))
    pl.when(jnp.logical_not(first))(functools.partial(body, None))


def _ffn_weight_scratch():
    return [
        pltpu.VMEM((D_MODEL, 2 * D_FF), BF16),
        pltpu.VMEM((D_FF, D_MODEL), BF16),
        pltpu.VMEM((2, D_MODEL, 2 * FFN_FC), F32),
        pltpu.VMEM((2, FFN_FC, D_MODEL), F32),
        pltpu.SemaphoreType.DMA((2, 3)),
    ]


def _ffn_in_kernel(x_ref, mod_ref, n1_ref, nm_ref, wgu_hbm, wd_hbm, win_ref,
                   x1_ref, zu_ref, zh_ref, zq_ref, zkv_ref, wgu_ref, wd_ref, stage_gu, stage_d, sems, *, layer):
    def body(fetch):
        x = x_ref[...]
        h = _mod_norm(x, n1_ref[...], mod_ref[0:1, :], mod_ref[1:2, :]).astype(BF16)
        x1 = x + 0.5 * mod_ref[2:3, :] * _swiglu_chunks(h, wgu_ref, wd_ref, fetch)
        x1_ref[...] = x1
        h2 = _mod_norm(x1, nm_ref[...], mod_ref[3:4, :], mod_ref[4:5, :]).astype(BF16)
        z = jnp.dot(h2, win_ref[...], preferred_element_type=F32)
        zu_ref[...] = z[:, :FOURIER_W]
        zh_ref[...] = z[:, FOURIER_W:FOURIER_W + ZH_W]
        zq_ref[...] = z[:, FOURIER_W + ZH_W:FOURIER_W + ZH_W + ATTN_W]
        zkv_ref[...] = z[:, FOURIER_W + ZH_W + ATTN_W:]

    _first_step_fetches(body, (wgu_hbm, wd_hbm, stage_gu, stage_d, sems, layer))


def _ffn_in_call(x, mod, n1, nm, w_gu, w_down, w_in, *, layer, mod_row0, rows_per_mod):
    n = x.shape[0]
    tm = FFN_TM
    widths = (D_MODEL, FOURIER_W, ZH_W, ATTN_W, 2 * KV_W)
    return pl.pallas_call(
        functools.partial(_ffn_in_kernel, layer=layer),
        grid=(n // tm,),
        in_specs=[
            pl.BlockSpec((tm, D_MODEL), lambda i: (i, 0)),
            _mod_spec(layer, mod_row0, rows_per_mod, tm),
            pl.BlockSpec((None, 1, D_MODEL), lambda i: (layer, 0, 0)),
            pl.BlockSpec((None, 1, D_MODEL), lambda i: (layer, 0, 0)),
            pl.BlockSpec(memory_space=pl.ANY),
            pl.BlockSpec(memory_space=pl.ANY),
            _resident((D_MODEL, IN_W), layer),
        ],
        out_specs=[pl.BlockSpec((tm, w), lambda i: (i, 0)) for w in widths],
        out_shape=[jax.ShapeDtypeStruct((n, w), F32) for w in widths],
        scratch_shapes=_ffn_weight_scratch(),
        compiler_params=_params("arbitrary"),
        name="ffn_in",
    )(x, mod, n1, nm, w_gu, w_down, w_in)


def _out_ffn_kernel(x_ref, mod_ref, n2_ref, fo_ref, ho_ref, ao_ref, wout_ref, wgu_hbm, wd_hbm, o_ref,
                    wgu_ref, wd_ref, stage_gu, stage_d, sems, *, layer):
    def body(fetch):
        y = jnp.dot(fo_ref[...].astype(BF16), wout_ref[0:FOURIER_W, :], preferred_element_type=F32)
        y += jnp.dot(ho_ref[...].astype(BF16), wout_ref[FOURIER_W:FOURIER_W + HGRN_W, :], preferred_element_type=F32)
        y += jnp.dot(ao_ref[...].astype(BF16), wout_ref[FOURIER_W + HGRN_W:, :], preferred_element_type=F32)
        xm = x_ref[...] + mod_ref[5:6, :] * y
        h = _mod_norm(xm, n2_ref[...], mod_ref[6:7, :], mod_ref[7:8, :]).astype(BF16)
        o_ref[...] = xm + 0.5 * mod_ref[8:9, :] * _swiglu_chunks(h, wgu_ref, wd_ref, fetch)

    _first_step_fetches(body, (wgu_hbm, wd_hbm, stage_gu, stage_d, sems, layer))


def _out_ffn_call(x, mod, n2, fo, ho, ao, w_out, w_gu, w_down, *, layer, mod_row0, rows_per_mod):
    n = x.shape[0]
    tm = FFN_TM
    return pl.pallas_call(
        functools.partial(_out_ffn_kernel, layer=layer),
        grid=(n // tm,),
        in_specs=[
            pl.BlockSpec((tm, D_MODEL), lambda i: (i, 0)),
            _mod_spec(layer, mod_row0, rows_per_mod, tm),
            pl.BlockSpec((None, 1, D_MODEL), lambda i: (layer, 0, 0)),
            pl.BlockSpec((tm, FOURIER_W), lambda i: (i, 0)),
            pl.BlockSpec((tm, HGRN_W), lambda i: (i, 0)),
            pl.BlockSpec((tm, ATTN_W), lambda i: (i, 0)),
            _resident((MIX_W, D_MODEL), layer),
            pl.BlockSpec(memory_space=pl.ANY),
            pl.BlockSpec(memory_space=pl.ANY),
        ],
        out_specs=pl.BlockSpec((tm, D_MODEL), lambda i: (i, 0)),
        out_shape=jax.ShapeDtypeStruct((n, D_MODEL), F32),
        scratch_shapes=_ffn_weight_scratch(),
        compiler_params=_params("arbitrary"),
        name="out_ffn",
    )(x, mod, n2, fo, ho, ao, w_out, w_gu, w_down)


def _dft_tables(seq_len):
    c = np.arange(HEAD_DIM)
    ang_c = 2.0 * np.pi * np.outer(c, c) / HEAD_DIM
    eye = np.eye(FOURIER_W // HEAD_DIM)
    chan = np.concatenate([np.kron(eye, np.cos(ang_c)), np.kron(eye, np.sin(ang_c))], axis=1) / np.sqrt(HEAD_DIM)
    n = np.arange(seq_len)
    ang_l = 2.0 * np.pi * (np.outer(n, n) % seq_len) / seq_len
    pos = np.concatenate([np.cos(ang_l), -np.sin(ang_l)], axis=1) / np.sqrt(seq_len)
    return jnp.asarray(chan, dtype=F32), jnp.asarray(pos, dtype=F32)


def _fourier_kernel(u_ref, chan_ref, pos_ref, o_ref, chan_bf, pos_bf, *, seq_len):
    @pl.when(pl.program_id(0) == 0)
    def _():
        chan_bf[...] = chan_ref[...].astype(BF16)
        pos_bf[...] = pos_ref[...].astype(BF16)

    t = jnp.dot(u_ref[...].astype(BF16), chan_bf[...], preferred_element_type=F32)
    pos = pos_bf[...]
    for r0 in range(0, u_ref.shape[0], seq_len):
        tb = t[r0:r0 + seq_len, :]
        stacked = jnp.concatenate([tb[:, :FOURIER_W], tb[:, FOURIER_W:]], axis=0).astype(BF16)
        o_ref[r0:r0 + seq_len, :] = jnp.dot(pos, stacked, preferred_element_type=F32)


def _fourier_call(zu, seq_len):
    n = zu.shape[0]
    chan, pos = _dft_tables(seq_len)
    rows = max(seq_len, SMALL_SEQ_ROWS)
    return pl.pallas_call(
        functools.partial(_fourier_kernel, seq_len=seq_len),
        grid=(n // rows,),
        in_specs=[
            pl.BlockSpec((rows, FOURIER_W), lambda b: (b, 0)),
            pl.BlockSpec((FOURIER_W, 2 * FOURIER_W), lambda b: (0, 0), pipeline_mode=pl.Buffered(1)),
            pl.BlockSpec((seq_len, 2 * seq_len), lambda b: (0, 0), pipeline_mode=pl.Buffered(1)),
        ],
        out_specs=pl.BlockSpec((rows, FOURIER_W), lambda b: (b, 0)),
        out_shape=jax.ShapeDtypeStruct((n, FOURIER_W), F32),
        scratch_shapes=[pltpu.VMEM((FOURIER_W, 2 * FOURIER_W), BF16), pltpu.VMEM((seq_len, 2 * seq_len), BF16)],
        compiler_params=_params("arbitrary"),
        name="fourier",
    )(zu, chan, pos)


def _split3(x):
    hi = x.astype(BF16)
    r1 = x - hi.astype(F32)
    mid = r1.astype(BF16)
    lo = (r1 - mid.astype(F32)).astype(BF16)
    return hi, mid, lo


def _level_mid(b, s, pos):
    tile, lanes = b.shape
    if s >= SUBLANES:
        b3 = b.reshape(tile // s, s, lanes)
        return jnp.broadcast_to(b3[:, pos:pos + 1, :], b3.shape).reshape(tile, lanes)
    assert s == 4
    b3 = b.reshape(tile // SUBLANES, SUBLANES, lanes)
    sub = lax.broadcasted_iota(jnp.int32, (1, SUBLANES, 1), 1)
    return jnp.where(sub < 4, b3[:, pos:pos + 1, :], b3[:, 4 + pos:5 + pos, :]).reshape(tile, lanes)


def _hgrn_kernel(*refs, seq_len, layer, has_s0, emit_state):
    refs = list(refs)
    zh_ref, lb_ref, nw_ref, tri_ref = refs[:4]
    pos = 4
    s0_ref = sf_ref = None
    if has_s0:
        s0_ref = refs[pos]
        pos += 1
    ho_ref = refs[pos]
    pos += 1
    if emit_state:
        sf_ref = refs[pos]
        pos += 1
    qin_ref, dec_ref, kvt_ref, stf_ref, stb_ref, oacc_ref = refs[pos:]

    tile = HGRN_TILE
    width = HGRN_W
    n_tiles = seq_len // tile
    c_q, c_v, c_z, c_g = 0, width, 2 * width, 4 * width

    raw = lb_ref[...]
    e = jnp.exp(raw - jnp.max(raw, axis=0, keepdims=True))
    soft = e / jnp.sum(e, axis=0, keepdims=True)
    cum = soft[0]
    for l in range(1, layer + 1):
        cum = cum + soft[l]
    lbd = cum - soft[0]
    lb_floor = jnp.maximum(lbd, LB_FLOOR)
    one_m_lb = 1.0 - lbd

    lane_bwd1 = lax.broadcasted_iota(jnp.int32, (1, HG_2), 1) >= width
    lane_bwd = lax.broadcasted_iota(jnp.int32, (tile, HG_2), 1) >= width
    lane_bwd_s = lax.broadcasted_iota(jnp.int32, (HEAD_DIM, HG_2), 1) >= width
    row = lax.broadcasted_iota(jnp.int32, (tile, 1), 0)
    pair_xor = (lax.broadcasted_iota(jnp.int32, (tile, HGRN_HEADS * tile), 0)
                ^ (lax.broadcasted_iota(jnp.int32, (tile, HGRN_HEADS * tile), 1) % tile))
    ev_mask = (lax.broadcasted_iota(jnp.int32, (HGRN_HEADS * tile, width), 0) // tile
               == lax.broadcasted_iota(jnp.int32, (HGRN_HEADS * tile, width), 1) // HEAD_DIM)
    bd_mask = (lax.broadcasted_iota(jnp.int32, (width, HG_2), 0) // HEAD_DIM
               == (lax.broadcasted_iota(jnp.int32, (width, HG_2), 1) // HEAD_DIM) % HGRN_HEADS)
    lane_head = (lax.broadcasted_iota(jnp.int32, (HEAD_DIM, HG_2), 1) // HEAD_DIM) % HGRN_HEADS
    ones_bd = _head_ones(width)
    nt_dims = (((1,), (1,)), ((), ()))

    def tile_body(t, carry):
        rows = pl.ds(pl.multiple_of(t * tile, tile), tile)
        q = zh_ref[rows, c_q:c_q + width] * (HEAD_DIM ** -0.5)
        v = zh_ref[rows, c_v:c_v + width]
        z = zh_ref[rows, c_z:c_z + HG_2]
        ez = jnp.exp(-jnp.abs(z))
        r = 1.0 / (1.0 + ez)
        tr = ez * r
        pos_z = z >= 0.0
        f = lb_floor + one_m_lb * jnp.where(pos_z, r, tr)
        logf = jnp.log(f)
        kk = one_m_lb * jnp.where(pos_z, tr, r)

        pieces = jnp.concatenate(_split3(logf), axis=1)
        acc = jnp.dot(tri_ref[...], pieces, preferred_element_type=F32)
        prefix = acc[:, :HG_2] + acc[:, HG_2:2 * HG_2] + acc[:, 2 * HG_2:]
        total = prefix[tile - 1:tile, :]
        b = jnp.where(lane_bwd, total - prefix + logf, prefix)
        kk_f, kk_b = kk[:, :width], kk[:, width:]

        a0 = jnp.dot((q * (kk_f + kk_b)).astype(BF16), ones_bd, preferred_element_type=F32)
        o = a0 * v

        b2 = b * LOG2_E
        bf2, bb2 = b2[:, :width], b2[:, width:]
        scores = None
        for s in HGRN_LEVELS:
            second = (row % s) >= s // 2
            k_role = jnp.where(second, kk_b, kk_f)
            if s == 2:
                q_op = q * jnp.where(second, f[:, :width], f[:, width:])
                k_op = k_role
            else:
                mid_f = _level_mid(bf2, s, s // 2 - 1)
                mid_b = _level_mid(bb2, s, s // 2)
                e_q = jnp.where(second, bf2, bb2) - jnp.where(second, mid_f, mid_b)
                e_k = jnp.where(second, mid_b, mid_f) - jnp.where(second, bb2, bf2)
                q_op = q * jnp.exp2(e_q)
                k_op = k_role * jnp.exp2(e_k)
            qt = q_op.astype(BF16)
            kt = k_op.astype(BF16)
            sc = jnp.concatenate(
                [lax.dot_general(qt[:, h * HEAD_DIM:(h + 1) * HEAD_DIM], kt[:, h * HEAD_DIM:(h + 1) * HEAD_DIM],
                                 nt_dims, preferred_element_type=F32) for h in range(HGRN_HEADS)], axis=1)
            scores = sc if scores is None else jnp.where(pair_xor < s, sc, scores)
        scores = jnp.where(pair_xor == 0, 0.0, scores)
        ev = jnp.where(ev_mask, jnp.concatenate([v] * HGRN_HEADS, axis=0), 0.0).astype(BF16)
        o = o + jnp.dot(scores.astype(BF16), ev, preferred_element_type=F32)
        oacc_ref[rows, :] = o

        q2 = jnp.concatenate([q, q], axis=1)
        qin_ref[rows, :] = (q2 * jnp.exp(b)).astype(BF16)
        kout = (kk * jnp.exp(total - b)).astype(BF16)
        dec_ref[t] = jnp.exp(total)
        full = jnp.dot(v.T.astype(BF16), kout, preferred_element_type=F32)
        kvt = full[0:HEAD_DIM, :]
        for hh in range(1, HGRN_HEADS):
            kvt = jnp.where(lane_head == hh, full[hh * HEAD_DIM:(hh + 1) * HEAD_DIM, :], kvt)
        kvt_ref[t] = kvt
        return carry

    lax.fori_loop(0, n_tiles, tile_body, 0)

    if has_s0:
        st0 = jnp.concatenate([s0_ref[d, h] for d in range(2) for h in range(HGRN_HEADS)], axis=0).T
    else:
        st0 = jnp.zeros((HEAD_DIM, HG_2), F32)

    def scan_step(i, st):
        tf, tb = i, n_tiles - 1 - i
        stf_ref[tf] = st
        stb_ref[tb] = st
        dec = jnp.where(lane_bwd1, dec_ref[tb], dec_ref[tf])
        kv = jnp.where(lane_bwd_s, kvt_ref[tb], kvt_ref[tf])
        return st * dec + kv

    st_fin = lax.fori_loop(0, n_tiles, scan_step, st0)
    if emit_state:
        st_t = st_fin.T
        for d in range(2):
            for h in range(HGRN_HEADS):
                r0 = (HGRN_HEADS * d + h) * HEAD_DIM
                sf_ref[d, h] = st_t[r0:r0 + HEAD_DIM, :]

    def out_tile(t):
        rows = pl.ds(pl.multiple_of(t * tile, tile), tile)
        st = jnp.where(lane_bwd_s, stb_ref[t], stf_ref[t])
        bd = jnp.where(bd_mask, jnp.concatenate([st] * HGRN_HEADS, axis=0), 0.0).T.astype(BF16)
        o = oacc_ref[rows, :] + jnp.dot(qin_ref[rows, :], bd, preferred_element_type=F32)
        hg = zh_ref[rows, c_g:c_g + width]
        ho_ref[rows, :] = _head_rmsnorm(o, nw_ref[...]) * (hg * jax.nn.sigmoid(hg))

    def out_body(i, carry):
        out_tile(2 * i)
        out_tile(2 * i + 1)
        return carry

    assert n_tiles % 2 == 0
    lax.fori_loop(0, n_tiles // 2, out_body, 0)


def _hgrn_tri():
    i = np.arange(HGRN_TILE)
    return jnp.asarray((i[None, :] <= i[:, None]).astype(np.float32), dtype=BF16)


def _hgrn_call(zh, lb2, norm_w, s0, *, seq_len, layer, emit_state):
    n = zh.shape[0]
    n_seq = n // seq_len
    has_s0 = s0 is not None
    state_block = (2, HGRN_HEADS, HEAD_DIM, HEAD_DIM)
    n_tiles = seq_len // HGRN_TILE
    in_specs = [
        pl.BlockSpec((seq_len, ZH_W), lambda b: (b, 0)),
        pl.BlockSpec((DEPTH, 1, HG_2), lambda b: (0, 0, 0)),
        pl.BlockSpec((None, 1, HGRN_W), lambda b: (layer, 0, 0)),
        pl.BlockSpec((HGRN_TILE, HGRN_TILE), lambda b: (0, 0)),
    ]
    args = [zh, lb2, norm_w, _hgrn_tri()]
    if has_s0:
        in_specs.append(pl.BlockSpec((None, None) + state_block, lambda b: (b, layer, 0, 0, 0, 0)))
        args.append(s0)
    out_specs = [pl.BlockSpec((seq_len, HGRN_W), lambda b: (b, 0))]
    out_shape = [jax.ShapeDtypeStruct((n, HGRN_W), F32)]
    if emit_state:
        out_specs.append(pl.BlockSpec((None,) + state_block, lambda b: (b, 0, 0, 0, 0)))
        out_shape.append(jax.ShapeDtypeStruct((n_seq,) + state_block, F32))
    return pl.pallas_call(
        functools.partial(_hgrn_kernel, seq_len=seq_len, layer=layer, has_s0=has_s0, emit_state=emit_state),
        grid=(n_seq,),
        in_specs=in_specs,
        out_specs=out_specs,
        out_shape=out_shape,
        scratch_shapes=[
            pltpu.VMEM((seq_len, HG_2), BF16),
            pltpu.VMEM((n_tiles, 1, HG_2), F32),
            pltpu.VMEM((n_tiles, HEAD_DIM, HG_2), F32),
            pltpu.VMEM((n_tiles, HEAD_DIM, HG_2), F32),
            pltpu.VMEM((n_tiles, HEAD_DIM, HG_2), F32),
            pltpu.VMEM((seq_len, HGRN_W), F32),
        ],
        compiler_params=_params("arbitrary"),
        name="hgrn",
    )(*args)


def _sink_row(sink_ref, kv_head, rows):
    grp = lax.broadcasted_iota(jnp.int32, (1, ATTN_GROUP * rows), 1) // rows
    base = kv_head * ATTN_GROUP
    out = jnp.broadcast_to(sink_ref[:, base:base + 1], (1, ATTN_GROUP * rows))
    for g in range(1, ATTN_GROUP):
        out = jnp.where(grp == g, sink_ref[:, base + g:base + g + 1], out)
    return out


def _group_queries_t(qt, kv_head):
    base = kv_head * ATTN_GROUP
    return jnp.concatenate([qt[(base + g) * HEAD_DIM:(base + g + 1) * HEAD_DIM, :] for g in range(ATTN_GROUP)], axis=1)


def _ctx_attn_kernel(zq_ref, zkv_ref, qn_ref, kn_ref, sink_ref, ao_ref, kc_ref, vc_ref, *, seq_len):
    rows = seq_len
    q_all = _head_rmsnorm(zq_ref[...], qn_ref[...]) * (HEAD_DIM ** -0.5)
    k_all = _head_rmsnorm(zkv_ref[:, :KV_W], kn_ref[...])
    v_all = zkv_ref[:, KV_W:]
    kc_ref[...] = k_all
    vc_ref[...] = v_all
    for r0 in range(0, zq_ref.shape[0], seq_len):
        qt = q_all[r0:r0 + rows, :].T.astype(BF16)
        kb = k_all[r0:r0 + rows, :].astype(BF16)
        vt = v_all[r0:r0 + rows, :].T.astype(BF16)
        out_t = []
        for kh in range(ATTN_KV_HEADS):
            hs = slice(kh * HEAD_DIM, (kh + 1) * HEAD_DIM)
            s = jnp.dot(kb[:, hs], _group_queries_t(qt, kh), preferred_element_type=F32)
            sink = _sink_row(sink_ref, kh, rows)
            m = jnp.maximum(jnp.max(s, axis=0, keepdims=True), sink)
            p = jnp.exp(s - m)
            denom = jnp.sum(p, axis=0, keepdims=True) + jnp.exp(sink - m)
            o_t = jnp.dot(vt[hs, :], p.astype(BF16), preferred_element_type=F32) * (1.0 / denom)
            out_t.extend(o_t[:, g * rows:(g + 1) * rows] for g in range(ATTN_GROUP))
        ao_ref[r0:r0 + rows, :] = jnp.concatenate(out_t, axis=0).T


def _ctx_attn_call(zq, zkv, q_norm, k_norm, sink, *, seq_len, layer):
    n = zq.shape[0]
    rows = max(seq_len, SMALL_SEQ_ROWS)
    return pl.pallas_call(
        functools.partial(_ctx_attn_kernel, seq_len=seq_len),
        grid=(n // rows,),
        in_specs=[
            pl.BlockSpec((rows, ATTN_W), lambda b: (b, 0)),
            pl.BlockSpec((rows, 2 * KV_W), lambda b: (b, 0)),
            pl.BlockSpec((None, 1, ATTN_W), lambda b: (layer, 0, 0)),
            pl.BlockSpec((None, 1, KV_W), lambda b: (layer, 0, 0)),
            pl.BlockSpec((None, 1, ATTN_HEADS), lambda b: (layer, 0, 0)),
        ],
        out_specs=[
            pl.BlockSpec((rows, ATTN_W), lambda b: (b, 0)),
            pl.BlockSpec((rows, KV_W), lambda b: (b, 0)),
            pl.BlockSpec((rows, KV_W), lambda b: (b, 0)),
        ],
        out_shape=[
            jax.ShapeDtypeStruct((n, ATTN_W), F32),
            jax.ShapeDtypeStruct((n, KV_W), F32),
            jax.ShapeDtypeStruct((n, KV_W), F32),
        ],
        compiler_params=_params("arbitrary"),
        name="ctx_attn",
    )(zq, zkv, q_norm, k_norm, sink)


def _rope_tables(seq_len, n_heads):
    t = np.arange(seq_len)
    row, col = t // GRID_W, t % GRID_W
    half = HEAD_DIM // 2
    inv = (ROPE_BASE ** (-np.arange(0, half, 2, dtype=np.float32) / half)).astype(np.float32)

    def one(pos):
        ang = pos.astype(np.float32)[:, None] * inv[None]
        c, s = np.cos(ang), np.sin(ang)
        return np.concatenate([c, c], axis=1), np.concatenate([-s, s], axis=1)

    cr, sr = one(row)
    cc, sc = one(col)
    cos = np.tile(np.concatenate([cr, cc], axis=1), (1, n_heads)).astype(np.float32)
    sin = np.tile(np.concatenate([sr, sc], axis=1), (1, n_heads)).astype(np.float32)
    return jnp.asarray(cos), jnp.asarray(sin)


def _rope(x, cos, sin_signed):
    width = x.shape[-1]
    lane = lax.broadcasted_iota(jnp.int32, x.shape, 1)
    quarter = HEAD_DIM // 4
    up = pltpu.roll(x, width - quarter, axis=1)
    down = pltpu.roll(x, quarter, axis=1)
    partner = jnp.where((lane % (2 * quarter)) < quarter, up, down)
    return x * cos + partner * sin_signed


def _lat_attn_kernel(zq_ref, zkv_ref, kc_ref, vc_ref, qn_ref, kn_ref, sink_ref,
                     cosq_ref, sinq_ref, cosk_ref, sink_tab_ref, band_ref, ao_ref, kb_ref, kcb_ref, vct_ref,
                     *, seq_len):
    blk = pl.program_id(1)
    rows = ATTN_BLOCK

    @pl.when(blk == 0)
    def _():
        k_all = _head_rmsnorm(zkv_ref[:, 0:KV_W], kn_ref[...])
        kb_ref[...] = _rope(k_all, cosk_ref[...], sink_tab_ref[...]).astype(BF16)
        kcb_ref[...] = kc_ref[...].astype(BF16)
        vct_ref[...] = vc_ref[...].T.astype(BF16)

    q = _head_rmsnorm(zq_ref[...], qn_ref[...])
    q = _rope(q, cosq_ref[...], sinq_ref[...]) * (HEAD_DIM ** -0.5)
    start = jnp.clip((blk - 1) * ATTN_BLOCK, 0, seq_len - WIN_KEYS)
    start = pl.multiple_of(start, ATTN_BLOCK)
    vwin = zkv_ref[pl.ds(start, WIN_KEYS), KV_W:2 * KV_W]

    band_bias = band_ref[blk - start // ATTN_BLOCK]

    qt = q.T.astype(BF16)
    kwb = kb_ref[pl.ds(start, WIN_KEYS), :]
    kcb = kcb_ref[...]
    vwt = vwin.T.astype(BF16)
    vct = vct_ref[...]
    out_t = []
    for kh in range(ATTN_KV_HEADS):
        hs = slice(kh * HEAD_DIM, (kh + 1) * HEAD_DIM)
        qg = _group_queries_t(qt, kh)
        s_loc = jnp.dot(kwb[:, hs], qg, preferred_element_type=F32) + band_bias
        s_ctx = jnp.dot(kcb[:, hs], qg, preferred_element_type=F32)
        sink = _sink_row(sink_ref, kh, rows)
        m = jnp.maximum(jnp.maximum(jnp.max(s_loc, axis=0, keepdims=True),
                                    jnp.max(s_ctx, axis=0, keepdims=True)), sink)
        p_loc = jnp.exp(s_loc - m)
        p_ctx = jnp.exp(s_ctx - m)
        denom = (jnp.sum(p_loc, axis=0, keepdims=True) + jnp.sum(p_ctx, axis=0, keepdims=True)
                 + jnp.exp(sink - m))
        o_t = (jnp.dot(vwt[hs, :], p_loc.astype(BF16), preferred_element_type=F32)
               + jnp.dot(vct[hs, :], p_ctx.astype(BF16), preferred_element_type=F32)) * (1.0 / denom)
        out_t.extend(o_t[:, g * rows:(g + 1) * rows] for g in range(ATTN_GROUP))
    ao_ref[...] = jnp.concatenate(out_t, axis=0).T


def _lat_attn_call(zq, zkv, cache_k, cache_v, q_norm, k_norm, sink, *, seq_len, layer):
    n = zq.shape[0]
    n_blk = seq_len // ATTN_BLOCK
    past = cache_k.shape[2]
    cosq, sinq = _rope_tables(seq_len, ATTN_HEADS)
    cosk, sink_tab = _rope_tables(seq_len, ATTN_KV_HEADS)
    kr = np.arange(WIN_KEYS)[:, None]
    qi = np.arange(ATTN_GROUP * ATTN_BLOCK)[None, :] % ATTN_BLOCK
    band = np.stack([np.where(np.abs(qi + off * ATTN_BLOCK - kr) <= WINDOW, 0.0, NEG_BIG) for off in range(3)])
    band = jnp.asarray(band, dtype=F32)
    return pl.pallas_call(
        functools.partial(_lat_attn_kernel, seq_len=seq_len),
        grid=(n // seq_len, n_blk),
        in_specs=[
            pl.BlockSpec((ATTN_BLOCK, ATTN_W), lambda b, i: (b * n_blk + i, 0)),
            pl.BlockSpec((seq_len, 2 * KV_W), lambda b, i: (b, 0)),
            pl.BlockSpec((None, None, past, KV_W), lambda b, i: (b, layer, 0, 0)),
            pl.BlockSpec((None, None, past, KV_W), lambda b, i: (b, layer, 0, 0)),
            pl.BlockSpec((None, 1, ATTN_W), lambda b, i: (layer, 0, 0)),
            pl.BlockSpec((None, 1, KV_W), lambda b, i: (layer, 0, 0)),
            pl.BlockSpec((None, 1, ATTN_HEADS), lambda b, i: (layer, 0, 0)),
            pl.BlockSpec((ATTN_BLOCK, ATTN_W), lambda b, i: (i, 0)),
            pl.BlockSpec((ATTN_BLOCK, ATTN_W), lambda b, i: (i, 0)),
            pl.BlockSpec((seq_len, KV_W), lambda b, i: (0, 0)),
            pl.BlockSpec((seq_len, KV_W), lambda b, i: (0, 0)),
            pl.BlockSpec((3, WIN_KEYS, ATTN_GROUP * ATTN_BLOCK), lambda b, i: (0, 0, 0), pipeline_mode=pl.Buffered(1)),
        ],
        out_specs=pl.BlockSpec((ATTN_BLOCK, ATTN_W), lambda b, i: (b * n_blk + i, 0)),
        out_shape=jax.ShapeDtypeStruct((n, ATTN_W), F32),
        scratch_shapes=[
            pltpu.VMEM((seq_len, KV_W), BF16),
            pltpu.VMEM((past, KV_W), BF16),
            pltpu.VMEM((KV_W, past), BF16),
        ],
        compiler_params=_params("arbitrary", "arbitrary"),
        name="lat_attn",
    )(zq, zkv, cache_k, cache_v, q_norm, k_norm, sink, cosq, sinq, cosk, sink_tab, band)


def kernel(x_prompt, x_sample, c, cache_attn_k, cache_attn_v, state_hgrn, c_ctx, w_ada, b_ada, norm_ffn1, norm_mix, norm_ffn2, ffn1_w_gate_up, ffn1_w_down, ffn2_w_gate_up, ffn2_w_down, w_in, w_out, hgrn_lower_bounds, hgrn_norm, q_norm, k_norm, attn_sink):
    batch, seq, _ = x_prompt.shape
    dec_batch, dec_seq, _ = x_sample.shape
    past = cache_attn_k.shape[2]
    assert dec_batch + 1 <= ADA_ROWS

    cond = jnp.concatenate([c_ctx[None], c, jnp.zeros((ADA_ROWS - 1 - dec_batch, D_MODEL), F32)], axis=0)
    mod = _ada_call(cond, w_ada, b_ada).reshape(DEPTH, ADA_ROWS, N_MOD, D_MODEL)

    bf = lambda w: w.astype(BF16)
    w_gu1, w_d1, w_gu2, w_d2 = ffn1_w_gate_up, ffn1_w_down, ffn2_w_gate_up, ffn2_w_down
    w_in_b, w_out_b = bf(w_in), bf(w_out)
    row3 = lambda w: w.reshape(DEPTH, 1, w.shape[-1])
    n1, nm, n2 = row3(norm_ffn1), row3(norm_mix), row3(norm_ffn2)
    hn = row3(hgrn_norm)
    qn = row3(jnp.tile(q_norm, (1, ATTN_HEADS)))
    kn = row3(jnp.tile(k_norm, (1, ATTN_KV_HEADS)))
    sink = row3(attn_sink)
    cache_k = cache_attn_k.reshape(dec_batch, DEPTH, past, KV_W)
    cache_v = cache_attn_v.reshape(dec_batch, DEPTH, past, KV_W)
    lb2 = hgrn_lower_bounds.reshape(DEPTH, 1, HG_2)

    streams = {
        "ctx": dict(x=x_prompt.reshape(batch * seq, D_MODEL), seq_len=seq, mod_row0=0, rows_per_mod=batch * seq),
        "lat": dict(x=x_sample.reshape(dec_batch * dec_seq, D_MODEL), seq_len=dec_seq, mod_row0=1, rows_per_mod=dec_seq),
    }
    k_list, v_list, s_list = [], [], []
    for layer in range(DEPTH):
        for name, st in streams.items():
            mk = dict(layer=layer, mod_row0=st["mod_row0"], rows_per_mod=st["rows_per_mod"])
            seq_len = st["seq_len"]
            x, zu, zh, zq, zkv = _ffn_in_call(st["x"], mod, n1, nm, w_gu1, w_d1, w_in_b, **mk)
            fo = _fourier_call(zu, seq_len)
            if name == "ctx":
                ho, sfin = _hgrn_call(zh, lb2, hn, None, seq_len=seq_len, layer=layer, emit_state=True)
                ao, k_new, v_new = _ctx_attn_call(zq, zkv, qn, kn, sink, seq_len=seq_len, layer=layer)
                k_list.append(k_new.reshape(batch, seq, KV_W))
                v_list.append(v_new.reshape(batch, seq, KV_W))
                s_list.append(sfin)
            else:
                (ho,) = _hgrn_call(zh, lb2, hn, state_hgrn, seq_len=seq_len, layer=layer, emit_state=False)
                ao = _lat_attn_call(zq, zkv, cache_k, cache_v, qn, kn, sink, seq_len=seq_len, layer=layer)
            st["x"] = _out_ffn_call(x, mod, n2, fo, ho, ao, w_out_b, w_gu2, w_d2, **mk)

    y_p = streams["ctx"]["x"].reshape(batch, seq, D_MODEL)
    y_s = streams["lat"]["x"].reshape(dec_batch, dec_seq, D_MODEL)
    cache = lambda parts: jnp.stack(parts, axis=1).reshape(batch, DEPTH, seq, ATTN_KV_HEADS, HEAD_DIM)
    return (y_p, y_s, cache(k_list), cache(v_list), jnp.stack(s_list, axis=1))
```

```python
import functools

import numpy as np
import jax
import jax.numpy as jnp
from jax import lax
from jax.experimental import pallas as pl
from jax.experimental.pallas import tpu as pltpu

F32 = jnp.float32
BF16 = jnp.bfloat16

D_MODEL = 1024
DEPTH = 2
HEAD_DIM = 64
FOURIER_W = 256
HGRN_W = 256
HGRN_HEADS = 4
ATTN_W = 512
ATTN_HEADS = 8
ATTN_KV_HEADS = 2
ATTN_GROUP = 4
KV_W = 128
MIX_W = 1024
WINDOW = 128
ATTN_BLOCK = 128
D_FF = 2816
GRID_W = 64
ROPE_BASE = 10000.0
EPS = 1e-6
N_MOD = 9
NEG_BIG = -1e30
LB_FLOOR = 1e-30
LOG2_E = 1.4426950408889634

HG_2 = 2 * HGRN_W
ZH_W = 5 * HGRN_W
IN_W = FOURIER_W + ZH_W + ATTN_W + 2 * KV_W

SUBLANES = 8
VMEM_LIMIT_BYTES = 60 * 1024 * 1024

ADA_ROWS = 8
ADA_TN = 1536
FFN_TM = 512
FFN_FC = 256
SMALL_SEQ_ROWS = 1024
HGRN_TILE = 128
HGRN_LEVELS = (128, 64, 32, 16, 8, 4, 2)
WIN_KEYS = 3 * ATTN_BLOCK


def _params(*semantics):
    return pltpu.CompilerParams(dimension_semantics=semantics, vmem_limit_bytes=VMEM_LIMIT_BYTES)


def _mod_norm(x, norm_w, shift, scale):
    ms = jnp.mean(x * x, axis=-1, keepdims=True)
    y = x * lax.rsqrt(ms + EPS) * norm_w
    return y * (1.0 + scale) + shift


def _head_ones(width):
    r = lax.broadcasted_iota(jnp.int32, (width, width), 0) // HEAD_DIM
    c = lax.broadcasted_iota(jnp.int32, (width, width), 1) // HEAD_DIM
    return jnp.where(r == c, 1.0, 0.0).astype(BF16)


def _head_rmsnorm(x, w):
    sq = (x * x).astype(BF16)
    ms = jnp.dot(sq, _head_ones(x.shape[-1]), preferred_element_type=F32) * (1.0 / HEAD_DIM)
    return x * lax.rsqrt(ms + EPS) * w


def _ada_kernel(cond_ref, w_ref, b_ref, o_ref):
    cnd = cond_ref[...]
    act = (cnd * jax.nn.sigmoid(cnd)).astype(BF16)
    o_ref[...] = jnp.dot(act, w_ref[...].astype(BF16), preferred_element_type=F32) + b_ref[...]


def _ada_call(cond, w_ada, b_ada):
    n_out = N_MOD * D_MODEL
    return pl.pallas_call(
        _ada_kernel,
        grid=(DEPTH, n_out // ADA_TN),
        in_specs=[
            pl.BlockSpec((ADA_ROWS, D_MODEL), lambda l, j: (0, 0)),
            pl.BlockSpec((None, D_MODEL, ADA_TN), lambda l, j: (l, 0, j)),
            pl.BlockSpec((None, 1, ADA_TN), lambda l, j: (l, 0, j)),
        ],
        out_specs=pl.BlockSpec((None, ADA_ROWS, ADA_TN), lambda l, j: (l, 0, j)),
        out_shape=jax.ShapeDtypeStruct((DEPTH, ADA_ROWS, n_out), F32),
        compiler_params=_params("arbitrary", "arbitrary"),
        name="adaln",
    )(cond, w_ada, b_ada.reshape(DEPTH, 1, n_out))


def _weight_chunk_copies(wgu_hbm, wd_hbm, stage_gu, stage_d, sems, layer, c, slot):
    lo, hi = c * FFN_FC, (c + 1) * FFN_FC
    return (
        pltpu.make_async_copy(wgu_hbm.at[layer, :, lo:hi], stage_gu.at[slot, :, 0:FFN_FC], sems.at[slot, 0]),
        pltpu.make_async_copy(wgu_hbm.at[layer, :, D_FF + lo:D_FF + hi], stage_gu.at[slot, :, FFN_FC:2 * FFN_FC],
                              sems.at[slot, 1]),
        pltpu.make_async_copy(wd_hbm.at[layer, lo:hi, :], stage_d.at[slot], sems.at[slot, 2]),
    )


def _swiglu_chunks(h, wgu_ref, wd_ref, fetch=None):
    n_chunks = D_FF // FFN_FC
    if fetch is not None:
        stage_gu, stage_d = fetch[2], fetch[3]
        for cp in _weight_chunk_copies(*fetch, 0, 0):
            cp.start()
    acc = None
    for c in range(n_chunks):
        if fetch is not None:
            slot = c % 2
            if c + 1 < n_chunks:
                for cp in _weight_chunk_copies(*fetch, c + 1, 1 - slot):
                    cp.start()
            for cp in _weight_chunk_copies(*fetch, c, slot):
                cp.wait()
            wgu_ref[:, c * FFN_FC:(c + 1) * FFN_FC] = stage_gu[slot, :, 0:FFN_FC].astype(BF16)
            wgu_ref[:, D_FF + c * FFN_FC:D_FF + (c + 1) * FFN_FC] = stage_gu[slot, :, FFN_FC:2 * FFN_FC].astype(BF16)
            wd_ref[c * FFN_FC:(c + 1) * FFN_FC, :] = stage_d[slot].astype(BF16)
        g = jnp.dot(h, wgu_ref[:, c * FFN_FC:(c + 1) * FFN_FC], preferred_element_type=F32)
        u = jnp.dot(h, wgu_ref[:, D_FF + c * FFN_FC:D_FF + (c + 1) * FFN_FC], preferred_element_type=F32)
        act = (g * jax.nn.sigmoid(g) * u).astype(BF16)
        part = jnp.dot(act, wd_ref[c * FFN_FC:(c + 1) * FFN_FC, :], preferred_element_type=F32)
        acc = part if acc is None else acc + part
    return acc


def _mod_spec(layer, mod_row0, rows_per_mod, tm):
    return pl.BlockSpec((None, None, N_MOD, D_MODEL),
                        lambda i, *_: (layer, mod_row0 + (i * tm) // rows_per_mod, 0, 0))


def _resident(shape, layer):
    return pl.BlockSpec((None,) + shape, lambda i: (layer,) + (0,) * len(shape), pipeline_mode=pl.Buffered(1))


def _first_step_fetches(body, fetch):
    first = pl.program_id(0) == 0
    pl.when(first)(functools.partial(body, fetch))
    pl.when(jnp.logical_not(first))(functools.partial(body, None))


def _ffn_weight_scratch():
    return [
        pltpu.VMEM((D_MODEL, 2 * D_FF), BF16),
        pltpu.VMEM((D_FF, D_MODEL), BF16),
        pltpu.VMEM((2, D_MODEL, 2 * FFN_FC), F32),
        pltpu.VMEM((2, FFN_FC, D_MODEL), F32),
        pltpu.SemaphoreType.DMA((2, 3)),
    ]


def _ffn_in_kernel(x_ref, mod_ref, n1_ref, nm_ref, wgu_hbm, wd_hbm, win_ref,
                   x1_ref, zu_ref, zh_ref, zq_ref, zkv_ref, wgu_ref, wd_ref, stage_gu, stage_d, sems, *, layer):
    def body(fetch):
        x = x_ref[...]
        h = _mod_norm(x, n1_ref[...], mod_ref[0:1, :], mod_ref[1:2, :]).astype(BF16)
        x1 = x + 0.5 * mod_ref[2:3, :] * _swiglu_chunks(h, wgu_ref, wd_ref, fetch)
        x1_ref[...] = x1
        h2 = _mod_norm(x1, nm_ref[...], mod_ref[3:4, :], mod_ref[4:5, :]).astype(BF16)
        z = jnp.dot(h2, win_ref[...], preferred_element_type=F32)
        zu_ref[...] = z[:, :FOURIER_W]
        zh_ref[...] = z[:, FOURIER_W:FOURIER_W + ZH_W]
        zq_ref[...] = z[:, FOURIER_W + ZH_W:FOURIER_W + ZH_W + ATTN_W]
        zkv_ref[...] = z[:, FOURIER_W + ZH_W + ATTN_W:]

    _first_step_fetches(body, (wgu_hbm, wd_hbm, stage_gu, stage_d, sems, layer))


def _ffn_in_call(x, mod, n1, nm, w_gu, w_down, w_in, *, layer, mod_row0, rows_per_mod):
    n = x.shape[0]
    tm = FFN_TM
    widths = (D_MODEL, FOURIER_W, ZH_W, ATTN_W, 2 * KV_W)
    return pl.pallas_call(
        functools.partial(_ffn_in_kernel, layer=layer),
        grid=(n // tm,),
        in_specs=[
            pl.BlockSpec((tm, D_MODEL), lambda i: (i, 0)),
            _mod_spec(layer, mod_row0, rows_per_mod, tm),
            pl.BlockSpec((None, 1, D_MODEL), lambda i: (layer, 0, 0)),
            pl.BlockSpec((None, 1, D_MODEL), lambda i: (layer, 0, 0)),
            pl.BlockSpec(memory_space=pl.ANY),
            pl.BlockSpec(memory_space=pl.ANY),
            _resident((D_MODEL, IN_W), layer),
        ],
        out_specs=[pl.BlockSpec((tm, w), lambda i: (i, 0)) for w in widths],
        out_shape=[jax.ShapeDtypeStruct((n, w), F32) for w in widths],
        scratch_shapes=_ffn_weight_scratch(),
        compiler_params=_params("arbitrary"),
        name="ffn_in",
    )(x, mod, n1, nm, w_gu, w_down, w_in)


def _out_ffn_kernel(x_ref, mod_ref, n2_ref, fo_ref, ho_ref, ao_ref, wout_ref, wgu_hbm, wd_hbm, o_ref,
                    wgu_ref, wd_ref, stage_gu, stage_d, sems, *, layer):
    def body(fetch):
        y = jnp.dot(fo_ref[...].astype(BF16), wout_ref[0:FOURIER_W, :], preferred_element_type=F32)
        y += jnp.dot(ho_ref[...].astype(BF16), wout_ref[FOURIER_W:FOURIER_W + HGRN_W, :], preferred_element_type=F32)
        y += jnp.dot(ao_ref[...].astype(BF16), wout_ref[FOURIER_W + HGRN_W:, :], preferred_element_type=F32)
        xm = x_ref[...] + mod_ref[5:6, :] * y
        h = _mod_norm(xm, n2_ref[...], mod_ref[6:7, :], mod_ref[7:8, :]).astype(BF16)
        o_ref[...] = xm + 0.5 * mod_ref[8:9, :] * _swiglu_chunks(h, wgu_ref, wd_ref, fetch)

    _first_step_fetches(body, (wgu_hbm, wd_hbm, stage_gu, stage_d, sems, layer))


def _out_ffn_call(x, mod, n2, fo, ho, ao, w_out, w_gu, w_down, *, layer, mod_row0, rows_per_mod):
    n = x.shape[0]
    tm = FFN_TM
    return pl.pallas_call(
        functools.partial(_out_ffn_kernel, layer=layer),
        grid=(n // tm,),
        in_specs=[
            pl.BlockSpec((tm, D_MODEL), lambda i: (i, 0)),
            _mod_spec(layer, mod_row0, rows_per_mod, tm),
            pl.BlockSpec((None, 1, D_MODEL), lambda i: (layer, 0, 0)),
            pl.BlockSpec((tm, FOURIER_W), lambda i: (i, 0)),
            pl.BlockSpec((tm, HGRN_W), lambda i: (i, 0)),
            pl.BlockSpec((tm, ATTN_W), lambda i: (i, 0)),
            _resident((MIX_W, D_MODEL), layer),
            pl.BlockSpec(memory_space=pl.ANY),
            pl.BlockSpec(memory_space=pl.ANY),
        ],
        out_specs=pl.BlockSpec((tm, D_MODEL), lambda i: (i, 0)),
        out_shape=jax.ShapeDtypeStruct((n, D_MODEL), F32),
        scratch_shapes=_ffn_weight_scratch(),
        compiler_params=_params("arbitrary"),
        name="out_ffn",
    )(x, mod, n2, fo, ho, ao, w_out, w_gu, w_down)


def _dft_tables(seq_len):
    c = np.arange(HEAD_DIM)
    ang_c = 2.0 * np.pi * np.outer(c, c) / HEAD_DIM
    eye = np.eye(FOURIER_W // HEAD_DIM)
    chan = np.concatenate([np.kron(eye, np.cos(ang_c)), np.kron(eye, np.sin(ang_c))], axis=1) / np.sqrt(HEAD_DIM)
    n = np.arange(seq_len)
    ang_l = 2.0 * np.pi * (np.outer(n, n) % seq_len) / seq_len
    pos = np.concatenate([np.cos(ang_l), -np.sin(ang_l)], axis=1) / np.sqrt(seq_len)
    return jnp.asarray(chan, dtype=F32), jnp.asarray(pos, dtype=F32)


def _fourier_kernel(u_ref, chan_ref, pos_ref, o_ref, chan_bf, pos_bf, *, seq_len):
    @pl.when(pl.program_id(0) == 0)
    def _():
        chan_bf[...] = chan_ref[...].astype(BF16)
        pos_bf[...] = pos_ref[...].astype(BF16)

    t = jnp.dot(u_ref[...].astype(BF16), chan_bf[...], preferred_element_type=F32)
    pos = pos_bf[...]
    for r0 in range(0, u_ref.shape[0], seq_len):
        tb = t[r0:r0 + seq_len, :]
        stacked = jnp.concatenate([tb[:, :FOURIER_W], tb[:, FOURIER_W:]], axis=0).astype(BF16)
        o_ref[r0:r0 + seq_len, :] = jnp.dot(pos, stacked, preferred_element_type=F32)


def _fourier_call(zu, seq_len):
    n = zu.shape[0]
    chan, pos = _dft_tables(seq_len)
    rows = max(seq_len, SMALL_SEQ_ROWS)
    return pl.pallas_call(
        functools.partial(_fourier_kernel, seq_len=seq_len),
        grid=(n // rows,),
        in_specs=[
            pl.BlockSpec((rows, FOURIER_W), lambda b: (b, 0)),
            pl.BlockSpec((FOURIER_W, 2 * FOURIER_W), lambda b: (0, 0), pipeline_mode=pl.Buffered(1)),
            pl.BlockSpec((seq_len, 2 * seq_len), lambda b: (0, 0), pipeline_mode=pl.Buffered(1)),
        ],
        out_specs=pl.BlockSpec((rows, FOURIER_W), lambda b: (b, 0)),
        out_shape=jax.ShapeDtypeStruct((n, FOURIER_W), F32),
        scratch_shapes=[pltpu.VMEM((FOURIER_W, 2 * FOURIER_W), BF16), pltpu.VMEM((seq_len, 2 * seq_len), BF16)],
        compiler_params=_params("arbitrary"),
        name="fourier",
    )(zu, chan, pos)


def _split3(x):
    hi = x.astype(BF16)
    r1 = x - hi.astype(F32)
    mid = r1.astype(BF16)
    lo = (r1 - mid.astype(F32)).astype(BF16)
    return hi, mid, lo


def _level_mid(b, s, pos):
    tile, lanes = b.shape
    if s >= SUBLANES:
        b3 = b.reshape(tile // s, s, lanes)
        return jnp.broadcast_to(b3[:, pos:pos + 1, :], b3.shape).reshape(tile, lanes)
    assert s == 4
    b3 = b.reshape(tile // SUBLANES, SUBLANES, lanes)
    sub = lax.broadcasted_iota(jnp.int32, (1, SUBLANES, 1), 1)
    return jnp.where(sub < 4, b3[:, pos:pos + 1, :], b3[:, 4 + pos:5 + pos, :]).reshape(tile, lanes)


def _hgrn_kernel(*refs, seq_len, layer, has_s0, emit_state):
    refs = list(refs)
    zh_ref, lb_ref, nw_ref, tri_ref = refs[:4]
    pos = 4
    s0_ref = sf_ref = None
    if has_s0:
        s0_ref = refs[pos]
        pos += 1
    if emit_state:
        pos += 1
    ho_ref = refs[pos]
    pos += 1
    if emit_state:
        sf_ref = refs[pos]
        pos += 1
    qin_ref, dec_ref, kvt_ref, stf_ref, stb_ref, oacc_ref = refs[pos:]

    tile = HGRN_TILE
    width = HGRN_W
    n_tiles = seq_len // tile
    c_q, c_v, c_z, c_g = 0, width, 2 * width, 4 * width

    raw = lb_ref[...]
    e = jnp.exp(raw - jnp.max(raw, axis=0, keepdims=True))
    soft = e / jnp.sum(e, axis=0, keepdims=True)
    cum = soft[0]
    for l in range(1, layer + 1):
        cum = cum + soft[l]
    lbd = cum - soft[0]
    lb_floor = jnp.maximum(lbd, LB_FLOOR)
    one_m_lb = 1.0 - lbd

    lane_bwd1 = lax.broadcasted_iota(jnp.int32, (1, HG_2), 1) >= width
    lane_bwd = lax.broadcasted_iota(jnp.int32, (tile, HG_2), 1) >= width
    lane_bwd_s = lax.broadcasted_iota(jnp.int32, (HEAD_DIM, HG_2), 1) >= width
    row = lax.broadcasted_iota(jnp.int32, (tile, 1), 0)
    pair_xor = (lax.broadcasted_iota(jnp.int32, (tile, HGRN_HEADS * tile), 0)
                ^ (lax.broadcasted_iota(jnp.int32, (tile, HGRN_HEADS * tile), 1) % tile))
    ev_mask = (lax.broadcasted_iota(jnp.int32, (HGRN_HEADS * tile, width), 0) // tile
               == lax.broadcasted_iota(jnp.int32, (HGRN_HEADS * tile, width), 1) // HEAD_DIM)
    bd_mask = (lax.broadcasted_iota(jnp.int32, (width, HG_2), 0) // HEAD_DIM
               == (lax.broadcasted_iota(jnp.int32, (width, HG_2), 1) // HEAD_DIM) % HGRN_HEADS)
    lane_head = (lax.broadcasted_iota(jnp.int32, (HEAD_DIM, HG_2), 1) // HEAD_DIM) % HGRN_HEADS
    ones_bd = _head_ones(width)
    nt_dims = (((1,), (1,)), ((), ()))

    def tile_body(t, carry):
        rows = pl.ds(pl.multiple_of(t * tile, tile), tile)
        q = zh_ref[rows, c_q:c_q + width] * (HEAD_DIM ** -0.5)
        v = zh_ref[rows, c_v:c_v + width]
        z = zh_ref[rows, c_z:c_z + HG_2]
        ez = jnp.exp(-jnp.abs(z))
        r = 1.0 / (1.0 + ez)
        tr = ez * r
        pos_z = z >= 0.0
        f = lb_floor + one_m_lb * jnp.where(pos_z, r, tr)
        logf = jnp.log(f)
        kk = one_m_lb * jnp.where(pos_z, tr, r)

        pieces = jnp.concatenate(_split3(logf), axis=1)
        acc = jnp.dot(tri_ref[...], pieces, preferred_element_type=F32)
        prefix = acc[:, :HG_2] + acc[:, HG_2:2 * HG_2] + acc[:, 2 * HG_2:]
        total = prefix[tile - 1:tile, :]
        b = jnp.where(lane_bwd, total - prefix + logf, prefix)
        kk_f, kk_b = kk[:, :width], kk[:, width:]

        a0 = jnp.dot((q * (kk_f + kk_b)).astype(BF16), ones_bd, preferred_element_type=F32)
        o = a0 * v

        b2 = b * LOG2_E
        bf2, bb2 = b2[:, :width], b2[:, width:]
        scores = None
        for s in HGRN_LEVELS:
            second = (row % s) >= s // 2
            k_role = jnp.where(second, kk_b, kk_f)
            if s == 2:
                q_op = q * jnp.where(second, f[:, :width], f[:, width:])
                k_op = k_role
            else:
                mid_f = _level_mid(bf2, s, s // 2 - 1)
                mid_b = _level_mid(bb2, s, s // 2)
                e_q = jnp.where(second, bf2, bb2) - jnp.where(second, mid_f, mid_b)
                e_k = jnp.where(second, mid_b, mid_f) - jnp.where(second, bb2, bf2)
                q_op = q * jnp.exp2(e_q)
                k_op = k_role * jnp.exp2(e_k)
            qt = q_op.astype(BF16)
            kt = k_op.astype(BF16)
            sc = jnp.concatenate(
                [lax.dot_general(qt[:, h * HEAD_DIM:(h + 1) * HEAD_DIM], kt[:, h * HEAD_DIM:(h + 1) * HEAD_DIM],
                                 nt_dims, preferred_element_type=F32) for h in range(HGRN_HEADS)], axis=1)
            scores = sc if scores is None else jnp.where(pair_xor < s, sc, scores)
        scores = jnp.where(pair_xor == 0, 0.0, scores)
        ev = jnp.where(ev_mask, jnp.concatenate([v] * HGRN_HEADS, axis=0), 0.0).astype(BF16)
        o = o + jnp.dot(scores.astype(BF16), ev, preferred_element_type=F32)
        oacc_ref[rows, :] = o

        q2 = jnp.concatenate([q, q], axis=1)
        qin_ref[rows, :] = (q2 * jnp.exp(b)).astype(BF16)
        kout = (kk * jnp.exp(total - b)).astype(BF16)
        dec_ref[t] = jnp.exp(total)
        full = jnp.dot(v.T.astype(BF16), kout, preferred_element_type=F32)
        kvt = full[0:HEAD_DIM, :]
        for hh in range(1, HGRN_HEADS):
            kvt = jnp.where(lane_head == hh, full[hh * HEAD_DIM:(hh + 1) * HEAD_DIM, :], kvt)
        kvt_ref[t] = kvt
        return carry

    lax.fori_loop(0, n_tiles, tile_body, 0)

    if has_s0:
        st0 = jnp.concatenate([s0_ref[d, h] for d in range(2) for h in range(HGRN_HEADS)], axis=0).T
    else:
        st0 = jnp.zeros((HEAD_DIM, HG_2), F32)

    def scan_step(i, st):
        tf, tb = i, n_tiles - 1 - i
        stf_ref[tf] = st
        stb_ref[tb] = st
        dec = jnp.where(lane_bwd1, dec_ref[tb], dec_ref[tf])
        kv = jnp.where(lane_bwd_s, kvt_ref[tb], kvt_ref[tf])
        return st * dec + kv

    st_fin = lax.fori_loop(0, n_tiles, scan_step, st0)
    if emit_state:
        st_t = st_fin.T
        for d in range(2):
            for h in range(HGRN_HEADS):
                r0 = (HGRN_HEADS * d + h) * HEAD_DIM
                sf_ref[d, h] = st_t[r0:r0 + HEAD_DIM, :]

    def out_tile(t):
        rows = pl.ds(pl.multiple_of(t * tile, tile), tile)
        st = jnp.where(lane_bwd_s, stb_ref[t], stf_ref[t])
        bd = jnp.where(bd_mask, jnp.concatenate([st] * HGRN_HEADS, axis=0), 0.0).T.astype(BF16)
        o = oacc_ref[rows, :] + jnp.dot(qin_ref[rows, :], bd, preferred_element_type=F32)
        hg = zh_ref[rows, c_g:c_g + width]
        ho_ref[rows, :] = _head_rmsnorm(o, nw_ref[...]) * (hg * jax.nn.sigmoid(hg))

    def out_body(i, carry):
        out_tile(2 * i)
        out_tile(2 * i + 1)
        return carry

    assert n_tiles % 2 == 0
    lax.fori_loop(0, n_tiles // 2, out_body, 0)


def _hgrn_tri():
    i = np.arange(HGRN_TILE)
    return jnp.asarray((i[None, :] <= i[:, None]).astype(np.float32), dtype=BF16)


def _hgrn_call(zh, lb2, norm_w, s0, final_states, *, seq_len, layer):
    n = zh.shape[0]
    n_seq = n // seq_len
    has_s0 = s0 is not None
    emit_state = final_states is not None
    state_block = (2, HGRN_HEADS, HEAD_DIM, HEAD_DIM)
    n_tiles = seq_len // HGRN_TILE
    in_specs = [
        pl.BlockSpec((seq_len, ZH_W), lambda b: (b, 0)),
        pl.BlockSpec((DEPTH, 1, HG_2), lambda b: (0, 0, 0)),
        pl.BlockSpec((None, 1, HGRN_W), lambda b: (layer, 0, 0)),
        pl.BlockSpec((HGRN_TILE, HGRN_TILE), lambda b: (0, 0)),
    ]
    args = [zh, lb2, norm_w, _hgrn_tri()]
    if has_s0:
        in_specs.append(pl.BlockSpec((None, None) + state_block, lambda b: (b, layer, 0, 0, 0, 0)))
        args.append(s0)
    out_specs = [pl.BlockSpec((seq_len, HGRN_W), lambda b: (b, 0))]
    out_shape = [jax.ShapeDtypeStruct((n, HGRN_W), F32)]
    aliases = {}
    if emit_state:
        aliases[len(args)] = 1
        in_specs.append(pl.BlockSpec(memory_space=pl.ANY))
        args.append(final_states)
        out_specs.append(pl.BlockSpec((None, None) + state_block, lambda b: (b, layer, 0, 0, 0, 0)))
        out_shape.append(jax.ShapeDtypeStruct(final_states.shape, F32))
    return pl.pallas_call(
        functools.partial(_hgrn_kernel, seq_len=seq_len, layer=layer, has_s0=has_s0, emit_state=emit_state),
        grid=(n_seq,),
        in_specs=in_specs,
        out_specs=out_specs,
        out_shape=out_shape,
        input_output_aliases=aliases,
        scratch_shapes=[
            pltpu.VMEM((seq_len, HG_2), BF16),
            pltpu.VMEM((n_tiles, 1, HG_2), F32),
            pltpu.VMEM((n_tiles, HEAD_DIM, HG_2), F32),
            pltpu.VMEM((n_tiles, HEAD_DIM, HG_2), F32),
            pltpu.VMEM((n_tiles, HEAD_DIM, HG_2), F32),
            pltpu.VMEM((seq_len, HGRN_W), F32),
        ],
        compiler_params=_params("arbitrary"),
        name="hgrn",
    )(*args)


def _sink_row(sink_ref, kv_head, rows):
    grp = lax.broadcasted_iota(jnp.int32, (1, ATTN_GROUP * rows), 1) // rows
    base = kv_head * ATTN_GROUP
    out = jnp.broadcast_to(sink_ref[:, base:base + 1], (1, ATTN_GROUP * rows))
    for g in range(1, ATTN_GROUP):
        out = jnp.where(grp == g, sink_ref[:, base + g:base + g + 1], out)
    return out


def _group_queries_t(qt, kv_head):
    base = kv_head * ATTN_GROUP
    return jnp.concatenate([qt[(base + g) * HEAD_DIM:(base + g + 1) * HEAD_DIM, :] for g in range(ATTN_GROUP)], axis=1)


def _ctx_attn_kernel(zq_ref, zkv_ref, qn_ref, kn_ref, sink_ref, kc_all_ref, vc_all_ref, ao_ref, kc_ref, vc_ref,
                     *, seq_len):
    del kc_all_ref, vc_all_ref
    rows = seq_len
    q_all = _head_rmsnorm(zq_ref[...], qn_ref[...]) * (HEAD_DIM ** -0.5)
    k_all = _head_rmsnorm(zkv_ref[:, :KV_W], kn_ref[...])
    v_all = zkv_ref[:, KV_W:]
    kc_ref[...] = k_all.reshape(kc_ref.shape)
    vc_ref[...] = v_all.reshape(vc_ref.shape)
    for r0 in range(0, zq_ref.shape[0], seq_len):
        qt = q_all[r0:r0 + rows, :].T.astype(BF16)
        kb = k_all[r0:r0 + rows, :].astype(BF16)
        vt = v_all[r0:r0 + rows, :].T.astype(BF16)
        out_t = []
        for kh in range(ATTN_KV_HEADS):
            hs = slice(kh * HEAD_DIM, (kh + 1) * HEAD_DIM)
            s = jnp.dot(kb[:, hs], _group_queries_t(qt, kh), preferred_element_type=F32)
            sink = _sink_row(sink_ref, kh, rows)
            m = jnp.maximum(jnp.max(s, axis=0, keepdims=True), sink)
            p = jnp.exp(s - m)
            denom = jnp.sum(p, axis=0, keepdims=True) + jnp.exp(sink - m)
            o_t = jnp.dot(vt[hs, :], p.astype(BF16), preferred_element_type=F32) * (1.0 / denom)
            out_t.extend(o_t[:, g * rows:(g + 1) * rows] for g in range(ATTN_GROUP))
        ao_ref[r0:r0 + rows, :] = jnp.concatenate(out_t, axis=0).T


def _ctx_attn_call(zq, zkv, q_norm, k_norm, sink, k_cache, v_cache, *, seq_len, layer):
    n = zq.shape[0]
    rows = max(seq_len, SMALL_SEQ_ROWS)
    seqs = rows // seq_len
    cache_spec = pl.BlockSpec((seqs, None, seq_len, KV_W), lambda b: (b, layer, 0, 0))
    return pl.pallas_call(
        functools.partial(_ctx_attn_kernel, seq_len=seq_len),
        grid=(n // rows,),
        in_specs=[
            pl.BlockSpec((rows, ATTN_W), lambda b: (b, 0)),
            pl.BlockSpec((rows, 2 * KV_W), lambda b: (b, 0)),
            pl.BlockSpec((None, 1, ATTN_W), lambda b: (layer, 0, 0)),
            pl.BlockSpec((None, 1, KV_W), lambda b: (layer, 0, 0)),
            pl.BlockSpec((None, 1, ATTN_HEADS), lambda b: (layer, 0, 0)),
            pl.BlockSpec(memory_space=pl.ANY),
            pl.BlockSpec(memory_space=pl.ANY),
        ],
        out_specs=[pl.BlockSpec((rows, ATTN_W), lambda b: (b, 0)), cache_spec, cache_spec],
        out_shape=[
            jax.ShapeDtypeStruct((n, ATTN_W), F32),
            jax.ShapeDtypeStruct(k_cache.shape, F32),
            jax.ShapeDtypeStruct(v_cache.shape, F32),
        ],
        input_output_aliases={5: 1, 6: 2},
        compiler_params=_params("arbitrary"),
        name="ctx_attn",
    )(zq, zkv, q_norm, k_norm, sink, k_cache, v_cache)


def _rope_tables(seq_len, n_heads):
    t = np.arange(seq_len)
    row, col = t // GRID_W, t % GRID_W
    half = HEAD_DIM // 2
    inv = (ROPE_BASE ** (-np.arange(0, half, 2, dtype=np.float32) / half)).astype(np.float32)

    def one(pos):
        ang = pos.astype(np.float32)[:, None] * inv[None]
        c, s = np.cos(ang), np.sin(ang)
        return np.concatenate([c, c], axis=1), np.concatenate([-s, s], axis=1)

    cr, sr = one(row)
    cc, sc = one(col)
    cos = np.tile(np.concatenate([cr, cc], axis=1), (1, n_heads)).astype(np.float32)
    sin = np.tile(np.concatenate([sr, sc], axis=1), (1, n_heads)).astype(np.float32)
    return jnp.asarray(cos), jnp.asarray(sin)


def _rope(x, cos, sin_signed):
    width = x.shape[-1]
    lane = lax.broadcasted_iota(jnp.int32, x.shape, 1)
    quarter = HEAD_DIM // 4
    up = pltpu.roll(x, width - quarter, axis=1)
    down = pltpu.roll(x, quarter, axis=1)
    partner = jnp.where((lane % (2 * quarter)) < quarter, up, down)
    return x * cos + partner * sin_signed


def _lat_attn_kernel(zq_ref, zkv_ref, kc_ref, vc_ref, qn_ref, kn_ref, sink_ref,
                     cosq_ref, sinq_ref, cosk_ref, sink_tab_ref, band_ref, ao_ref, kb_ref, kcb_ref, vct_ref,
                     *, seq_len):
    blk = pl.program_id(1)
    rows = ATTN_BLOCK

    @pl.when(blk == 0)
    def _():
        k_all = _head_rmsnorm(zkv_ref[:, 0:KV_W], kn_ref[...])
        kb_ref[...] = _rope(k_all, cosk_ref[...], sink_tab_ref[...]).astype(BF16)
        kcb_ref[...] = kc_ref[...].astype(BF16)
        vct_ref[...] = vc_ref[...].T.astype(BF16)

    q = _head_rmsnorm(zq_ref[...], qn_ref[...])
    q = _rope(q, cosq_ref[...], sinq_ref[...]) * (HEAD_DIM ** -0.5)
    start = jnp.clip((blk - 1) * ATTN_BLOCK, 0, seq_len - WIN_KEYS)
    start = pl.multiple_of(start, ATTN_BLOCK)
    vwin = zkv_ref[pl.ds(start, WIN_KEYS), KV_W:2 * KV_W]

    band_bias = band_ref[blk - start // ATTN_BLOCK]

    qt = q.T.astype(BF16)
    kwb = kb_ref[pl.ds(start, WIN_KEYS), :]
    kcb = kcb_ref[...]
    vwt = vwin.T.astype(BF16)
    vct = vct_ref[...]
    out_t = []
    for kh in range(ATTN_KV_HEADS):
        hs = slice(kh * HEAD_DIM, (kh + 1) * HEAD_DIM)
        qg = _group_queries_t(qt, kh)
        s_loc = jnp.dot(kwb[:, hs], qg, preferred_element_type=F32) + band_bias
        s_ctx = jnp.dot(kcb[:, hs], qg, preferred_element_type=F32)
        sink = _sink_row(sink_ref, kh, rows)
        m = jnp.maximum(jnp.maximum(jnp.max(s_loc, axis=0, keepdims=True),
                                    jnp.max(s_ctx, axis=0, keepdims=True)), sink)
        p_loc = jnp.exp(s_loc - m)
        p_ctx = jnp.exp(s_ctx - m)
        denom = (jnp.sum(p_loc, axis=0, keepdims=True) + jnp.sum(p_ctx, axis=0, keepdims=True)
                 + jnp.exp(sink - m))
        o_t = (jnp.dot(vwt[hs, :], p_loc.astype(BF16), preferred_element_type=F32)
               + jnp.dot(vct[hs, :], p_ctx.astype(BF16), preferred_element_type=F32)) * (1.0 / denom)
        out_t.extend(o_t[:, g * rows:(g + 1) * rows] for g in range(ATTN_GROUP))
    ao_ref[...] = jnp.concatenate(out_t, axis=0).T


def _lat_attn_call(zq, zkv, cache_k, cache_v, q_norm, k_norm, sink, *, seq_len, layer):
    n = zq.shape[0]
    n_blk = seq_len // ATTN_BLOCK
    past = cache_k.shape[2]
    cosq, sinq = _rope_tables(seq_len, ATTN_HEADS)
    cosk, sink_tab = _rope_tables(seq_len, ATTN_KV_HEADS)
    kr = np.arange(WIN_KEYS)[:, None]
    qi = np.arange(ATTN_GROUP * ATTN_BLOCK)[None, :] % ATTN_BLOCK
    band = np.stack([np.where(np.abs(qi + off * ATTN_BLOCK - kr) <= WINDOW, 0.0, NEG_BIG) for off in range(3)])
    band = jnp.asarray(band, dtype=F32)
    return pl.pallas_call(
        functools.partial(_lat_attn_kernel, seq_len=seq_len),
        grid=(n // seq_len, n_blk),
        in_specs=[
            pl.BlockSpec((ATTN_BLOCK, ATTN_W), lambda b, i: (b * n_blk + i, 0)),
            pl.BlockSpec((seq_len, 2 * KV_W), lambda b, i: (b, 0)),
            pl.BlockSpec((None, None, past, KV_W), lambda b, i: (b, layer, 0, 0)),
            pl.BlockSpec((None, None, past, KV_W), lambda b, i: (b, layer, 0, 0)),
            pl.BlockSpec((None, 1, ATTN_W), lambda b, i: (layer, 0, 0)),
            pl.BlockSpec((None, 1, KV_W), lambda b, i: (layer, 0, 0)),
            pl.BlockSpec((None, 1, ATTN_HEADS), lambda b, i: (layer, 0, 0)),
            pl.BlockSpec((ATTN_BLOCK, ATTN_W), lambda b, i: (i, 0)),
            pl.BlockSpec((ATTN_BLOCK, ATTN_W), lambda b, i: (i, 0)),
            pl.BlockSpec((seq_len, KV_W), lambda b, i: (0, 0)),
            pl.BlockSpec((seq_len, KV_W), lambda b, i: (0, 0)),
            pl.BlockSpec((3, WIN_KEYS, ATTN_GROUP * ATTN_BLOCK), lambda b, i: (0, 0, 0), pipeline_mode=pl.Buffered(1)),
        ],
        out_specs=pl.BlockSpec((ATTN_BLOCK, ATTN_W), lambda b, i: (b * n_blk + i, 0)),
        out_shape=jax.ShapeDtypeStruct((n, ATTN_W), F32),
        scratch_shapes=[
            pltpu.VMEM((seq_len, KV_W), BF16),
            pltpu.VMEM((past, KV_W), BF16),
            pltpu.VMEM((KV_W, past), BF16),
        ],
        compiler_params=_params("arbitrary", "arbitrary"),
        name="lat_attn",
    )(zq, zkv, cache_k, cache_v, q_norm, k_norm, sink, cosq, sinq, cosk, sink_tab, band)


def kernel(x_prompt, x_sample, c, cache_attn_k, cache_attn_v, state_hgrn, c_ctx, w_ada, b_ada, norm_ffn1, norm_mix, norm_ffn2, ffn1_w_gate_up, ffn1_w_down, ffn2_w_gate_up, ffn2_w_down, w_in, w_out, hgrn_lower_bounds, hgrn_norm, q_norm, k_norm, attn_sink):
    batch, seq, _ = x_prompt.shape
    dec_batch, dec_seq, _ = x_sample.shape
    past = cache_attn_k.shape[2]
    assert dec_batch + 1 <= ADA_ROWS

    cond = jnp.concatenate([c_ctx[None], c, jnp.zeros((ADA_ROWS - 1 - dec_batch, D_MODEL), F32)], axis=0)
    mod = _ada_call(cond, w_ada, b_ada).reshape(DEPTH, ADA_ROWS, N_MOD, D_MODEL)

    bf = lambda w: w.astype(BF16)
    w_gu1, w_d1, w_gu2, w_d2 = ffn1_w_gate_up, ffn1_w_down, ffn2_w_gate_up, ffn2_w_down
    w_in_b, w_out_b = bf(w_in), bf(w_out)
    row3 = lambda w: w.reshape(DEPTH, 1, w.shape[-1])
    n1, nm, n2 = row3(norm_ffn1), row3(norm_mix), row3(norm_ffn2)
    hn = row3(hgrn_norm)
    qn = row3(jnp.tile(q_norm, (1, ATTN_HEADS)))
    kn = row3(jnp.tile(k_norm, (1, ATTN_KV_HEADS)))
    sink = row3(attn_sink)
    cache_k = cache_attn_k.reshape(dec_batch, DEPTH, past, KV_W)
    cache_v = cache_attn_v.reshape(dec_batch, DEPTH, past, KV_W)
    lb2 = hgrn_lower_bounds.reshape(DEPTH, 1, HG_2)

    streams = {
        "ctx": dict(x=x_prompt.reshape(batch * seq, D_MODEL), seq_len=seq, mod_row0=0, rows_per_mod=batch * seq),
        "lat": dict(x=x_sample.reshape(dec_batch * dec_seq, D_MODEL), seq_len=dec_seq, mod_row0=1, rows_per_mod=dec_seq),
    }
    new_k = jnp.zeros((batch, DEPTH, seq, KV_W), F32)
    new_v = jnp.zeros((batch, DEPTH, seq, KV_W), F32)
    new_s = jnp.zeros((batch, DEPTH, 2, HGRN_HEADS, HEAD_DIM, HEAD_DIM), F32)
    for layer in range(DEPTH):
        for name, st in streams.items():
            mk = dict(layer=layer, mod_row0=st["mod_row0"], rows_per_mod=st["rows_per_mod"])
            seq_len = st["seq_len"]
            x, zu, zh, zq, zkv = _ffn_in_call(st["x"], mod, n1, nm, w_gu1, w_d1, w_in_b, **mk)
            fo = _fourier_call(zu, seq_len)
            if name == "ctx":
                ho, new_s = _hgrn_call(zh, lb2, hn, None, new_s, seq_len=seq_len, layer=layer)
                ao, new_k, new_v = _ctx_attn_call(zq, zkv, qn, kn, sink, new_k, new_v, seq_len=seq_len, layer=layer)
            else:
                (ho,) = _hgrn_call(zh, lb2, hn, state_hgrn, None, seq_len=seq_len, layer=layer)
                ao = _lat_attn_call(zq, zkv, cache_k, cache_v, qn, kn, sink, seq_len=seq_len, layer=layer)
            st["x"] = _out_ffn_call(x, mod, n2, fo, ho, ao, w_out_b, w_gu2, w_d2, **mk)

    y_p = streams["ctx"]["x"].reshape(batch, seq, D_MODEL)
    y_s = streams["lat"]["x"].reshape(dec_batch, dec_seq, D_MODEL)
    cache_shape = (batch, DEPTH, seq, ATTN_KV_HEADS, HEAD_DIM)
    return (y_p, y_s, new_k.reshape(cache_shape), new_v.reshape(cache_shape), new_s)
```

```python
import functools

import numpy as np
import jax
import jax.numpy as jnp
from jax import lax
from jax.experimental import pallas as pl
from jax.experimental.pallas import tpu as pltpu

F32 = jnp.float32
BF16 = jnp.bfloat16

D_MODEL = 1024
DEPTH = 2
HEAD_DIM = 64
FOURIER_W = 256
HGRN_W = 256
HGRN_HEADS = 4
ATTN_W = 512
ATTN_HEADS = 8
ATTN_KV_HEADS = 2
ATTN_GROUP = 4
KV_W = 128
MIX_W = 1024
WINDOW = 128
ATTN_BLOCK = 128
D_FF = 2816
GRID_W = 64
ROPE_BASE = 10000.0
EPS = 1e-6
N_MOD = 9
NEG_BIG = -1e30
LB_FLOOR = 1e-30
LOG2_E = 1.4426950408889634

HG_2 = 2 * HGRN_W
ZH_W = 5 * HGRN_W
IN_W = FOURIER_W + ZH_W + ATTN_W + 2 * KV_W

SUBLANES = 8
VMEM_LIMIT_BYTES = 63 * 1024 * 1024

ADA_ROWS = 8
ADA_TN = 1536
FFN_TM = 512
FFN_FC = 256
SMALL_SEQ_ROWS = 1024
HGRN_TILE = 128
HGRN_LEVELS = (128, 64, 32, 16, 8, 4, 2)
WIN_KEYS = 3 * ATTN_BLOCK


def _params(*semantics):
    return pltpu.CompilerParams(dimension_semantics=semantics, vmem_limit_bytes=VMEM_LIMIT_BYTES)


def _mod_norm(x, norm_w, shift, scale):
    ms = jnp.mean(x * x, axis=-1, keepdims=True)
    y = x * lax.rsqrt(ms + EPS) * norm_w
    return y * (1.0 + scale) + shift


def _head_ones(width):
    r = lax.broadcasted_iota(jnp.int32, (width, width), 0) // HEAD_DIM
    c = lax.broadcasted_iota(jnp.int32, (width, width), 1) // HEAD_DIM
    return jnp.where(r == c, 1.0, 0.0).astype(BF16)


def _head_rmsnorm(x, w):
    sq = (x * x).astype(BF16)
    ms = jnp.dot(sq, _head_ones(x.shape[-1]), preferred_element_type=F32) * (1.0 / HEAD_DIM)
    return x * lax.rsqrt(ms + EPS) * w


def _ada_kernel(cond_ref, w_ref, b_ref, o_ref):
    cnd = cond_ref[...]
    act = (cnd * jax.nn.sigmoid(cnd)).astype(BF16)
    o_ref[...] = jnp.dot(act, w_ref[...].astype(BF16), preferred_element_type=F32) + b_ref[...]


def _ada_call(cond, w_ada, b_ada):
    n_out = N_MOD * D_MODEL
    return pl.pallas_call(
        _ada_kernel,
        grid=(DEPTH, n_out // ADA_TN),
        in_specs=[
            pl.BlockSpec((ADA_ROWS, D_MODEL), lambda l, j: (0, 0)),
            pl.BlockSpec((None, D_MODEL, ADA_TN), lambda l, j: (l, 0, j)),
            pl.BlockSpec((None, 1, ADA_TN), lambda l, j: (l, 0, j)),
        ],
        out_specs=pl.BlockSpec((None, ADA_ROWS, ADA_TN), lambda l, j: (l, 0, j)),
        out_shape=jax.ShapeDtypeStruct((DEPTH, ADA_ROWS, n_out), F32),
        compiler_params=_params("arbitrary", "arbitrary"),
        name="adaln",
    )(cond, w_ada, b_ada.reshape(DEPTH, 1, n_out))


def _weight_chunk_copies(wgu_hbm, wd_hbm, stage_gu, stage_d, sems, layer, c, slot):
    lo, hi = c * FFN_FC, (c + 1) * FFN_FC
    return (
        pltpu.make_async_copy(wgu_hbm.at[layer, :, lo:hi], stage_gu.at[slot, :, 0:FFN_FC], sems.at[slot, 0]),
        pltpu.make_async_copy(wgu_hbm.at[layer, :, D_FF + lo:D_FF + hi], stage_gu.at[slot, :, FFN_FC:2 * FFN_FC],
                              sems.at[slot, 1]),
        pltpu.make_async_copy(wd_hbm.at[layer, lo:hi, :], stage_d.at[slot], sems.at[slot, 2]),
    )


def _swiglu_chunks(h, wgu_ref, wd_ref, fetch=None):
    n_chunks = D_FF // FFN_FC
    if fetch is not None:
        stage_gu, stage_d = fetch[2], fetch[3]
        for cp in _weight_chunk_copies(*fetch, 0, 0):
            cp.start()
    acc = None
    for c in range(n_chunks):
        if fetch is not None:
            slot = c % 2
            if c + 1 < n_chunks:
                for cp in _weight_chunk_copies(*fetch, c + 1, 1 - slot):
                    cp.start()
            for cp in _weight_chunk_copies(*fetch, c, slot):
                cp.wait()
            wgu_ref[:, c * FFN_FC:(c + 1) * FFN_FC] = stage_gu[slot, :, 0:FFN_FC].astype(BF16)
            wgu_ref[:, D_FF + c * FFN_FC:D_FF + (c + 1) * FFN_FC] = stage_gu[slot, :, FFN_FC:2 * FFN_FC].astype(BF16)
            wd_ref[c * FFN_FC:(c + 1) * FFN_FC, :] = stage_d[slot].astype(BF16)
        g = jnp.dot(h, wgu_ref[:, c * FFN_FC:(c + 1) * FFN_FC], preferred_element_type=F32)
        u = jnp.dot(h, wgu_ref[:, D_FF + c * FFN_FC:D_FF + (c + 1) * FFN_FC], preferred_element_type=F32)
        act = (g * jax.nn.sigmoid(g) * u).astype(BF16)
        part = jnp.dot(act, wd_ref[c * FFN_FC:(c + 1) * FFN_FC, :], preferred_element_type=F32)
        acc = part if acc is None else acc + part
    return acc


def _mod_spec(layer, tiles_a, tm, rows_per_cond):
    def index(i):
        row = jnp.where(i < tiles_a, 0, 1 + ((i - tiles_a) * tm) // rows_per_cond)
        return (layer, row, 0, 0)
    return pl.BlockSpec((None, None, N_MOD, D_MODEL), index)


def _stream_specs(width, tiles_a, tm):
    return (pl.BlockSpec((tm, width), lambda i: (jnp.minimum(i, tiles_a - 1), 0)),
            pl.BlockSpec((tm, width), lambda i: (jnp.maximum(i - tiles_a, 0), 0)))


def _resident(shape, layer):
    return pl.BlockSpec((None,) + shape, lambda i: (layer,) + (0,) * len(shape), pipeline_mode=pl.Buffered(1))


def _first_step_fetches(body, fetch):
    first = pl.program_id(0) == 0
    pl.when(first)(functools.partial(body, fetch))
    pl.when(jnp.logical_not(first))(functools.partial(body, None))


def _ffn_weight_scratch():
    return [
        pltpu.VMEM((D_MODEL, 2 * D_FF), BF16),
        pltpu.VMEM((D_FF, D_MODEL), BF16),
        pltpu.VMEM((2, D_MODEL, 2 * FFN_FC), F32),
        pltpu.VMEM((2, FFN_FC, D_MODEL), F32),
        pltpu.SemaphoreType.DMA((2, 3)),
    ]


def _ffn_in_kernel(xa_ref, xb_ref, mod_ref, n1_ref, nm_ref, wgu_hbm, wd_hbm, win_ref,
                   x1_ref, zu_ref, zh_ref, zq_ref, zkv_ref, wgu_ref, wd_ref, stage_gu, stage_d, sems, *, layer, tiles_a):
    def body(fetch):
        x = jnp.where(pl.program_id(0) < tiles_a, xa_ref[...], xb_ref[...])
        h = _mod_norm(x, n1_ref[...], mod_ref[0:1, :], mod_ref[1:2, :]).astype(BF16)
        x1 = x + 0.5 * mod_ref[2:3, :] * _swiglu_chunks(h, wgu_ref, wd_ref, fetch)
        x1_ref[...] = x1
        h2 = _mod_norm(x1, nm_ref[...], mod_ref[3:4, :], mod_ref[4:5, :]).astype(BF16)
        z = jnp.dot(h2, win_ref[...], preferred_element_type=F32)
        zu_ref[...] = z[:, :FOURIER_W]
        zh_ref[...] = z[:, FOURIER_W:FOURIER_W + ZH_W]
        zq_ref[...] = z[:, FOURIER_W + ZH_W:FOURIER_W + ZH_W + ATTN_W]
        zkv_ref[...] = z[:, FOURIER_W + ZH_W + ATTN_W:]

    _first_step_fetches(body, (wgu_hbm, wd_hbm, stage_gu, stage_d, sems, layer))


def _ffn_in_call(x_a, x_b, mod, n1, nm, w_gu, w_down, w_in, *, layer, rows_per_cond):
    tm = FFN_TM
    tiles_a = x_a.shape[0] // tm
    n = x_a.shape[0] + x_b.shape[0]
    widths = (D_MODEL, FOURIER_W, ZH_W, ATTN_W, 2 * KV_W)
    return pl.pallas_call(
        functools.partial(_ffn_in_kernel, layer=layer, tiles_a=tiles_a),
        grid=(n // tm,),
        in_specs=[
            *_stream_specs(D_MODEL, tiles_a, tm),
            _mod_spec(layer, tiles_a, tm, rows_per_cond),
            pl.BlockSpec((None, 1, D_MODEL), lambda i: (layer, 0, 0)),
            pl.BlockSpec((None, 1, D_MODEL), lambda i: (layer, 0, 0)),
            pl.BlockSpec(memory_space=pl.ANY),
            pl.BlockSpec(memory_space=pl.ANY),
            _resident((D_MODEL, IN_W), layer),
        ],
        out_specs=[pl.BlockSpec((tm, w), lambda i: (i, 0)) for w in widths],
        out_shape=[jax.ShapeDtypeStruct((n, w), F32) for w in widths],
        scratch_shapes=_ffn_weight_scratch(),
        compiler_params=_params("arbitrary"),
        name="ffn_in",
    )(x_a, x_b, mod, n1, nm, w_gu, w_down, w_in)


def _out_ffn_kernel(x_ref, mod_ref, n2_ref, foa_ref, fob_ref, hoa_ref, hob_ref, aoa_ref, aob_ref,
                    wout_ref, wgu_hbm, wd_hbm, oa_ref, ob_ref,
                    wgu_ref, wd_ref, stage_gu, stage_d, sems, *, layer, tiles_a):
    is_a = pl.program_id(0) < tiles_a

    def body(fetch):
        pick = lambda a_ref, b_ref: jnp.where(is_a, a_ref[...], b_ref[...]).astype(BF16)
        y = jnp.dot(pick(foa_ref, fob_ref), wout_ref[0:FOURIER_W, :], preferred_element_type=F32)
        y += jnp.dot(pick(hoa_ref, hob_ref), wout_ref[FOURIER_W:FOURIER_W + HGRN_W, :], preferred_element_type=F32)
        y += jnp.dot(pick(aoa_ref, aob_ref), wout_ref[FOURIER_W + HGRN_W:, :], preferred_element_type=F32)
        xm = x_ref[...] + mod_ref[5:6, :] * y
        h = _mod_norm(xm, n2_ref[...], mod_ref[6:7, :], mod_ref[7:8, :]).astype(BF16)
        out = xm + 0.5 * mod_ref[8:9, :] * _swiglu_chunks(h, wgu_ref, wd_ref, fetch)

        @pl.when(is_a)
        def _():
            oa_ref[...] = out

        @pl.when(jnp.logical_not(is_a))
        def _():
            ob_ref[...] = out

    _first_step_fetches(body, (wgu_hbm, wd_hbm, stage_gu, stage_d, sems, layer))


def _out_ffn_call(x, mod, n2, fo, ho, ao, w_out, w_gu, w_down, *, layer, rows_per_cond):
    tm = FFN_TM
    n_a, n_b = fo[0].shape[0], fo[1].shape[0]
    tiles_a = n_a // tm
    return pl.pallas_call(
        functools.partial(_out_ffn_kernel, layer=layer, tiles_a=tiles_a),
        grid=((n_a + n_b) // tm,),
        in_specs=[
            pl.BlockSpec((tm, D_MODEL), lambda i: (i, 0)),
            _mod_spec(layer, tiles_a, tm, rows_per_cond),
            pl.BlockSpec((None, 1, D_MODEL), lambda i: (layer, 0, 0)),
            *_stream_specs(FOURIER_W, tiles_a, tm),
            *_stream_specs(HGRN_W, tiles_a, tm),
            *_stream_specs(ATTN_W, tiles_a, tm),
            _resident((MIX_W, D_MODEL), layer),
            pl.BlockSpec(memory_space=pl.ANY),
            pl.BlockSpec(memory_space=pl.ANY),
        ],
        out_specs=list(_stream_specs(D_MODEL, tiles_a, tm)),
        out_shape=[jax.ShapeDtypeStruct((n_a, D_MODEL), F32), jax.ShapeDtypeStruct((n_b, D_MODEL), F32)],
        scratch_shapes=_ffn_weight_scratch(),
        compiler_params=_params("arbitrary"),
        name="out_ffn",
    )(x, mod, n2, fo[0], fo[1], ho[0], ho[1], ao[0], ao[1], w_out, w_gu, w_down)


def _dft_tables(seq_len):
    c = np.arange(HEAD_DIM)
    ang_c = 2.0 * np.pi * np.outer(c, c) / HEAD_DIM
    eye = np.eye(FOURIER_W // HEAD_DIM)
    chan = np.concatenate([np.kron(eye, np.cos(ang_c)), np.kron(eye, np.sin(ang_c))], axis=1) / np.sqrt(HEAD_DIM)
    n = np.arange(seq_len)
    ang_l = 2.0 * np.pi * (np.outer(n, n) % seq_len) / seq_len
    pos = np.concatenate([np.cos(ang_l), -np.sin(ang_l)], axis=1) / np.sqrt(seq_len)
    return jnp.asarray(chan, dtype=F32), jnp.asarray(pos, dtype=F32)


def _fourier_kernel(u_ref, chan_ref, pos_ref, o_ref, chan_bf, pos_bf, *, seq_len):
    @pl.when(pl.program_id(0) == 0)
    def _():
        chan_bf[...] = chan_ref[...].astype(BF16)
        pos_bf[...] = pos_ref[...].astype(BF16)

    t = jnp.dot(u_ref[...].astype(BF16), chan_bf[...], preferred_element_type=F32)
    pos = pos_bf[...]
    for r0 in range(0, u_ref.shape[0], seq_len):
        tb = t[r0:r0 + seq_len, :]
        stacked = jnp.concatenate([tb[:, :FOURIER_W], tb[:, FOURIER_W:]], axis=0).astype(BF16)
        o_ref[r0:r0 + seq_len, :] = jnp.dot(pos, stacked, preferred_element_type=F32)


def _fourier_call(zu, seq_len, span):
    row0, n = span
    chan, pos = _dft_tables(seq_len)
    rows = max(seq_len, SMALL_SEQ_ROWS)
    blk0 = row0 // rows
    return pl.pallas_call(
        functools.partial(_fourier_kernel, seq_len=seq_len),
        grid=(n // rows,),
        in_specs=[
            pl.BlockSpec((rows, FOURIER_W), lambda b: (b + blk0, 0)),
            pl.BlockSpec((FOURIER_W, 2 * FOURIER_W), lambda b: (0, 0), pipeline_mode=pl.Buffered(1)),
            pl.BlockSpec((seq_len, 2 * seq_len), lambda b: (0, 0), pipeline_mode=pl.Buffered(1)),
        ],
        out_specs=pl.BlockSpec((rows, FOURIER_W), lambda b: (b, 0)),
        out_shape=jax.ShapeDtypeStruct((n, FOURIER_W), F32),
        scratch_shapes=[pltpu.VMEM((FOURIER_W, 2 * FOURIER_W), BF16), pltpu.VMEM((seq_len, 2 * seq_len), BF16)],
        compiler_params=_params("arbitrary"),
        name="fourier",
    )(zu, chan, pos)


def _split3(x):
    hi = x.astype(BF16)
    r1 = x - hi.astype(F32)
    mid = r1.astype(BF16)
    lo = (r1 - mid.astype(F32)).astype(BF16)
    return hi, mid, lo


def _level_mid(b, s, pos):
    tile, lanes = b.shape
    if s >= SUBLANES:
        b3 = b.reshape(tile // s, s, lanes)
        return jnp.broadcast_to(b3[:, pos:pos + 1, :], b3.shape).reshape(tile, lanes)
    assert s == 4
    b3 = b.reshape(tile // SUBLANES, SUBLANES, lanes)
    sub = lax.broadcasted_iota(jnp.int32, (1, SUBLANES, 1), 1)
    return jnp.where(sub < 4, b3[:, pos:pos + 1, :], b3[:, 4 + pos:5 + pos, :]).reshape(tile, lanes)


def _hgrn_kernel(*refs, seq_len, layer, has_s0, emit_state):
    refs = list(refs)
    zh_ref, lb_ref, nw_ref, tri_ref = refs[:4]
    pos = 4
    s0_ref = sf_ref = None
    if has_s0:
        s0_ref = refs[pos]
        pos += 1
    if emit_state:
        pos += 1
    ho_ref = refs[pos]
    pos += 1
    if emit_state:
        sf_ref = refs[pos]
        pos += 1
    qin_ref, dec_ref, kvt_ref, stf_ref, stb_ref, oacc_ref = refs[pos:]

    tile = HGRN_TILE
    width = HGRN_W
    n_tiles = seq_len // tile
    c_q, c_v, c_z, c_g = 0, width, 2 * width, 4 * width

    raw = lb_ref[...]
    e = jnp.exp(raw - jnp.max(raw, axis=0, keepdims=True))
    soft = e / jnp.sum(e, axis=0, keepdims=True)
    cum = soft[0]
    for l in range(1, layer + 1):
        cum = cum + soft[l]
    lbd = cum - soft[0]
    lb_floor = jnp.maximum(lbd, LB_FLOOR)
    one_m_lb = 1.0 - lbd

    lane_bwd1 = lax.broadcasted_iota(jnp.int32, (1, HG_2), 1) >= width
    lane_bwd = lax.broadcasted_iota(jnp.int32, (tile, HG_2), 1) >= width
    lane_bwd_s = lax.broadcasted_iota(jnp.int32, (HEAD_DIM, HG_2), 1) >= width
    row = lax.broadcasted_iota(jnp.int32, (tile, 1), 0)
    pair_xor = (lax.broadcasted_iota(jnp.int32, (tile, HGRN_HEADS * tile), 0)
                ^ (lax.broadcasted_iota(jnp.int32, (tile, HGRN_HEADS * tile), 1) % tile))
    ev_mask = (lax.broadcasted_iota(jnp.int32, (HGRN_HEADS * tile, width), 0) // tile
               == lax.broadcasted_iota(jnp.int32, (HGRN_HEADS * tile, width), 1) // HEAD_DIM)
    bd_mask = (lax.broadcasted_iota(jnp.int32, (width, HG_2), 0) // HEAD_DIM
               == (lax.broadcasted_iota(jnp.int32, (width, HG_2), 1) // HEAD_DIM) % HGRN_HEADS)
    lane_head = (lax.broadcasted_iota(jnp.int32, (HEAD_DIM, HG_2), 1) // HEAD_DIM) % HGRN_HEADS
    ones_bd = _head_ones(width)
    nt_dims = (((1,), (1,)), ((), ()))

    def tile_body(t, carry):
        rows = pl.ds(pl.multiple_of(t * tile, tile), tile)
        q = zh_ref[rows, c_q:c_q + width] * (HEAD_DIM ** -0.5)
        v = zh_ref[rows, c_v:c_v + width]
        z = zh_ref[rows, c_z:c_z + HG_2]
        ez = jnp.exp(-jnp.abs(z))
        r = 1.0 / (1.0 + ez)
        tr = ez * r
        pos_z = z >= 0.0
        f = lb_floor + one_m_lb * jnp.where(pos_z, r, tr)
        logf = jnp.log(f)
        kk = one_m_lb * jnp.where(pos_z, tr, r)

        pieces = jnp.concatenate(_split3(logf), axis=1)
        acc = jnp.dot(tri_ref[...], pieces, preferred_element_type=F32)
        prefix = acc[:, :HG_2] + acc[:, HG_2:2 * HG_2] + acc[:, 2 * HG_2:]
        total = prefix[tile - 1:tile, :]
        b = jnp.where(lane_bwd, total - prefix + logf, prefix)
        kk_f, kk_b = kk[:, :width], kk[:, width:]

        a0 = jnp.dot((q * (kk_f + kk_b)).astype(BF16), ones_bd, preferred_element_type=F32)
        o = a0 * v

        b2 = b * LOG2_E
        bf2, bb2 = b2[:, :width], b2[:, width:]
        scores = None
        for s in HGRN_LEVELS:
            second = (row % s) >= s // 2
            k_role = jnp.where(second, kk_b, kk_f)
            if s == 2:
                q_op = q * jnp.where(second, f[:, :width], f[:, width:])
                k_op = k_role
            else:
                mid_f = _level_mid(bf2, s, s // 2 - 1)
                mid_b = _level_mid(bb2, s, s // 2)
                e_q = jnp.where(second, bf2, bb2) - jnp.where(second, mid_f, mid_b)
                e_k = jnp.where(second, mid_b, mid_f) - jnp.where(second, bb2, bf2)
                q_op = q * jnp.exp2(e_q)
                k_op = k_role * jnp.exp2(e_k)
            qt = q_op.astype(BF16)
            kt = k_op.astype(BF16)
            sc = jnp.concatenate(
                [lax.dot_general(qt[:, h * HEAD_DIM:(h + 1) * HEAD_DIM], kt[:, h * HEAD_DIM:(h + 1) * HEAD_DIM],
                                 nt_dims, preferred_element_type=F32) for h in range(HGRN_HEADS)], axis=1)
            scores = sc if scores is None else jnp.where(pair_xor < s, sc, scores)
        scores = jnp.where(pair_xor == 0, 0.0, scores)
        ev = jnp.where(ev_mask, jnp.concatenate([v] * HGRN_HEADS, axis=0), 0.0).astype(BF16)
        o = o + jnp.dot(scores.astype(BF16), ev, preferred_element_type=F32)
        oacc_ref[rows, :] = o

        q2 = jnp.concatenate([q, q], axis=1)
        qin_ref[rows, :] = (q2 * jnp.exp(b)).astype(BF16)
        kout = (kk * jnp.exp(total - b)).astype(BF16)
        dec_ref[t] = jnp.exp(total)
        full = jnp.dot(v.T.astype(BF16), kout, preferred_element_type=F32)
        kvt = full[0:HEAD_DIM, :]
        for hh in range(1, HGRN_HEADS):
            kvt = jnp.where(lane_head == hh, full[hh * HEAD_DIM:(hh + 1) * HEAD_DIM, :], kvt)
        kvt_ref[t] = kvt
        return carry

    lax.fori_loop(0, n_tiles, tile_body, 0)

    if has_s0:
        st0 = jnp.concatenate([s0_ref[d, h] for d in range(2) for h in range(HGRN_HEADS)], axis=0).T
    else:
        st0 = jnp.zeros((HEAD_DIM, HG_2), F32)

    def scan_step(i, st):
        tf, tb = i, n_tiles - 1 - i
        stf_ref[tf] = st
        stb_ref[tb] = st
        dec = jnp.where(lane_bwd1, dec_ref[tb], dec_ref[tf])
        kv = jnp.where(lane_bwd_s, kvt_ref[tb], kvt_ref[tf])
        return st * dec + kv

    st_fin = lax.fori_loop(0, n_tiles, scan_step, st0)
    if emit_state:
        st_t = st_fin.T
        for d in range(2):
            for h in range(HGRN_HEADS):
                r0 = (HGRN_HEADS * d + h) * HEAD_DIM
                sf_ref[d, h] = st_t[r0:r0 + HEAD_DIM, :]

    def out_tile(t):
        rows = pl.ds(pl.multiple_of(t * tile, tile), tile)
        st = jnp.where(lane_bwd_s, stb_ref[t], stf_ref[t])
        bd = jnp.where(bd_mask, jnp.concatenate([st] * HGRN_HEADS, axis=0), 0.0).T.astype(BF16)
        o = oacc_ref[rows, :] + jnp.dot(qin_ref[rows, :], bd, preferred_element_type=F32)
        hg = zh_ref[rows, c_g:c_g + width]
        ho_ref[rows, :] = _head_rmsnorm(o, nw_ref[...]) * (hg * jax.nn.sigmoid(hg))

    def out_body(i, carry):
        out_tile(2 * i)
        out_tile(2 * i + 1)
        return carry

    assert n_tiles % 2 == 0
    lax.fori_loop(0, n_tiles // 2, out_body, 0)


def _hgrn_tri():
    i = np.arange(HGRN_TILE)
    return jnp.asarray((i[None, :] <= i[:, None]).astype(np.float32), dtype=BF16)


def _hgrn_call(zh, lb2, norm_w, s0, final_states, *, seq_len, layer, span):
    row0, n = span
    blk0 = row0 // seq_len
    n_seq = n // seq_len
    has_s0 = s0 is not None
    emit_state = final_states is not None
    state_block = (2, HGRN_HEADS, HEAD_DIM, HEAD_DIM)
    n_tiles = seq_len // HGRN_TILE
    in_specs = [
        pl.BlockSpec((seq_len, ZH_W), lambda b: (b + blk0, 0)),
        pl.BlockSpec((DEPTH, 1, HG_2), lambda b: (0, 0, 0)),
        pl.BlockSpec((None, 1, HGRN_W), lambda b: (layer, 0, 0)),
        pl.BlockSpec((HGRN_TILE, HGRN_TILE), lambda b: (0, 0)),
    ]
    args = [zh, lb2, norm_w, _hgrn_tri()]
    if has_s0:
        in_specs.append(pl.BlockSpec((None, None) + state_block, lambda b: (b, layer, 0, 0, 0, 0)))
        args.append(s0)
    out_specs = [pl.BlockSpec((seq_len, HGRN_W), lambda b: (b, 0))]
    out_shape = [jax.ShapeDtypeStruct((n, HGRN_W), F32)]
    aliases = {}
    if emit_state:
        aliases[len(args)] = 1
        in_specs.append(pl.BlockSpec(memory_space=pl.ANY))
        args.append(final_states)
        out_specs.append(pl.BlockSpec((None, None) + state_block, lambda b: (b, layer, 0, 0, 0, 0)))
        out_shape.append(jax.ShapeDtypeStruct(final_states.shape, F32))
    return pl.pallas_call(
        functools.partial(_hgrn_kernel, seq_len=seq_len, layer=layer, has_s0=has_s0, emit_state=emit_state),
        grid=(n_seq,),
        in_specs=in_specs,
        out_specs=out_specs,
        out_shape=out_shape,
        input_output_aliases=aliases,
        scratch_shapes=[
            pltpu.VMEM((seq_len, HG_2), BF16),
            pltpu.VMEM((n_tiles, 1, HG_2), F32),
            pltpu.VMEM((n_tiles, HEAD_DIM, HG_2), F32),
            pltpu.VMEM((n_tiles, HEAD_DIM, HG_2), F32),
            pltpu.VMEM((n_tiles, HEAD_DIM, HG_2), F32),
            pltpu.VMEM((seq_len, HGRN_W), F32),
        ],
        compiler_params=_params("arbitrary"),
        name="hgrn",
    )(*args)


def _sink_row(sink_ref, kv_head, rows):
    grp = lax.broadcasted_iota(jnp.int32, (1, ATTN_GROUP * rows), 1) // rows
    base = kv_head * ATTN_GROUP
    out = jnp.broadcast_to(sink_ref[:, base:base + 1], (1, ATTN_GROUP * rows))
    for g in range(1, ATTN_GROUP):
        out = jnp.where(grp == g, sink_ref[:, base + g:base + g + 1], out)
    return out


def _group_queries_t(qt, kv_head):
    base = kv_head * ATTN_GROUP
    return jnp.concatenate([qt[(base + g) * HEAD_DIM:(base + g + 1) * HEAD_DIM, :] for g in range(ATTN_GROUP)], axis=1)


def _ctx_attn_kernel(zq_ref, zkv_ref, qn_ref, kn_ref, sink_ref, kc_all_ref, vc_all_ref, ao_ref, kc_ref, vc_ref,
                     *, seq_len):
    del kc_all_ref, vc_all_ref
    rows = seq_len
    q_all = _head_rmsnorm(zq_ref[...], qn_ref[...]) * (HEAD_DIM ** -0.5)
    k_all = _head_rmsnorm(zkv_ref[:, :KV_W], kn_ref[...])
    v_all = zkv_ref[:, KV_W:]
    kc_ref[...] = k_all.reshape(kc_ref.shape)
    vc_ref[...] = v_all.reshape(vc_ref.shape)
    for r0 in range(0, zq_ref.shape[0], seq_len):
        qt = q_all[r0:r0 + rows, :].T.astype(BF16)
        kb = k_all[r0:r0 + rows, :].astype(BF16)
        vt = v_all[r0:r0 + rows, :].T.astype(BF16)
        out_t = []
        for kh in range(ATTN_KV_HEADS):
            hs = slice(kh * HEAD_DIM, (kh + 1) * HEAD_DIM)
            s = jnp.dot(kb[:, hs], _group_queries_t(qt, kh), preferred_element_type=F32)
            sink = _sink_row(sink_ref, kh, rows)
            m = jnp.maximum(jnp.max(s, axis=0, keepdims=True), sink)
            p = jnp.exp(s - m)
            denom = jnp.sum(p, axis=0, keepdims=True) + jnp.exp(sink - m)
            o_t = jnp.dot(vt[hs, :], p.astype(BF16), preferred_element_type=F32) * (1.0 / denom)
            out_t.extend(o_t[:, g * rows:(g + 1) * rows] for g in range(ATTN_GROUP))
        ao_ref[r0:r0 + rows, :] = jnp.concatenate(out_t, axis=0).T


def _ctx_attn_call(zq, zkv, q_norm, k_norm, sink, k_cache, v_cache, *, seq_len, layer, span):
    row0, n = span
    rows = max(seq_len, SMALL_SEQ_ROWS)
    blk0 = row0 // rows
    seqs = rows // seq_len
    cache_spec = pl.BlockSpec((seqs, None, seq_len, KV_W), lambda b: (b, layer, 0, 0))
    return pl.pallas_call(
        functools.partial(_ctx_attn_kernel, seq_len=seq_len),
        grid=(n // rows,),
        in_specs=[
            pl.BlockSpec((rows, ATTN_W), lambda b: (b + blk0, 0)),
            pl.BlockSpec((rows, 2 * KV_W), lambda b: (b + blk0, 0)),
            pl.BlockSpec((None, 1, ATTN_W), lambda b: (layer, 0, 0)),
            pl.BlockSpec((None, 1, KV_W), lambda b: (layer, 0, 0)),
            pl.BlockSpec((None, 1, ATTN_HEADS), lambda b: (layer, 0, 0)),
            pl.BlockSpec(memory_space=pl.ANY),
            pl.BlockSpec(memory_space=pl.ANY),
        ],
        out_specs=[pl.BlockSpec((rows, ATTN_W), lambda b: (b, 0)), cache_spec, cache_spec],
        out_shape=[
            jax.ShapeDtypeStruct((n, ATTN_W), F32),
            jax.ShapeDtypeStruct(k_cache.shape, F32),
            jax.ShapeDtypeStruct(v_cache.shape, F32),
        ],
        input_output_aliases={5: 1, 6: 2},
        compiler_params=_params("arbitrary"),
        name="ctx_attn",
    )(zq, zkv, q_norm, k_norm, sink, k_cache, v_cache)


def _rope_tables(seq_len, n_heads):
    t = np.arange(seq_len)
    row, col = t // GRID_W, t % GRID_W
    half = HEAD_DIM // 2
    inv = (ROPE_BASE ** (-np.arange(0, half, 2, dtype=np.float32) / half)).astype(np.float32)

    def one(pos):
        ang = pos.astype(np.float32)[:, None] * inv[None]
        c, s = np.cos(ang), np.sin(ang)
        return np.concatenate([c, c], axis=1), np.concatenate([-s, s], axis=1)

    cr, sr = one(row)
    cc, sc = one(col)
    cos = np.tile(np.concatenate([cr, cc], axis=1), (1, n_heads)).astype(np.float32)
    sin = np.tile(np.concatenate([sr, sc], axis=1), (1, n_heads)).astype(np.float32)
    return jnp.asarray(cos), jnp.asarray(sin)


def _rope(x, cos, sin_signed):
    width = x.shape[-1]
    lane = lax.broadcasted_iota(jnp.int32, x.shape, 1)
    quarter = HEAD_DIM // 4
    up = pltpu.roll(x, width - quarter, axis=1)
    down = pltpu.roll(x, quarter, axis=1)
    partner = jnp.where((lane % (2 * quarter)) < quarter, up, down)
    return x * cos + partner * sin_signed


def _lat_attn_kernel(zq_ref, zkv_ref, kc_ref, vc_ref, qn_ref, kn_ref, sink_ref,
                     cosq_ref, sinq_ref, cosk_ref, sink_tab_ref, band_ref, ao_ref, kb_ref, kcb_ref, vct_ref,
                     *, seq_len):
    blk = pl.program_id(1)
    rows = ATTN_BLOCK

    @pl.when(blk == 0)
    def _():
        k_all = _head_rmsnorm(zkv_ref[:, 0:KV_W], kn_ref[...])
        kb_ref[...] = _rope(k_all, cosk_ref[...], sink_tab_ref[...]).astype(BF16)
        kcb_ref[...] = kc_ref[...].astype(BF16)
        vct_ref[...] = vc_ref[...].T.astype(BF16)

    q = _head_rmsnorm(zq_ref[...], qn_ref[...])
    q = _rope(q, cosq_ref[...], sinq_ref[...]) * (HEAD_DIM ** -0.5)
    start = jnp.clip((blk - 1) * ATTN_BLOCK, 0, seq_len - WIN_KEYS)
    start = pl.multiple_of(start, ATTN_BLOCK)
    vwin = zkv_ref[pl.ds(start, WIN_KEYS), KV_W:2 * KV_W]

    band_bias = band_ref[blk - start // ATTN_BLOCK]

    qt = q.T.astype(BF16)
    kwb = kb_ref[pl.ds(start, WIN_KEYS), :]
    kcb = kcb_ref[...]
    vwt = vwin.T.astype(BF16)
    vct = vct_ref[...]
    out_t = []
    for kh in range(ATTN_KV_HEADS):
        hs = slice(kh * HEAD_DIM, (kh + 1) * HEAD_DIM)
        qg = _group_queries_t(qt, kh)
        s_loc = jnp.dot(kwb[:, hs], qg, preferred_element_type=F32) + band_bias
        s_ctx = jnp.dot(kcb[:, hs], qg, preferred_element_type=F32)
        sink = _sink_row(sink_ref, kh, rows)
        m = jnp.maximum(jnp.maximum(jnp.max(s_loc, axis=0, keepdims=True),
                                    jnp.max(s_ctx, axis=0, keepdims=True)), sink)
        p_loc = jnp.exp(s_loc - m)
        p_ctx = jnp.exp(s_ctx - m)
        denom = (jnp.sum(p_loc, axis=0, keepdims=True) + jnp.sum(p_ctx, axis=0, keepdims=True)
                 + jnp.exp(sink - m))
        o_t = (jnp.dot(vwt[hs, :], p_loc.astype(BF16), preferred_element_type=F32)
               + jnp.dot(vct[hs, :], p_ctx.astype(BF16), preferred_element_type=F32)) * (1.0 / denom)
        out_t.extend(o_t[:, g * rows:(g + 1) * rows] for g in range(ATTN_GROUP))
    ao_ref[...] = jnp.concatenate(out_t, axis=0).T


def _lat_attn_call(zq, zkv, cache_k, cache_v, q_norm, k_norm, sink, *, seq_len, layer, span):
    row0, n = span
    n_blk = seq_len // ATTN_BLOCK
    qblk0, seq0 = row0 // ATTN_BLOCK, row0 // seq_len
    past = cache_k.shape[2]
    cosq, sinq = _rope_tables(seq_len, ATTN_HEADS)
    cosk, sink_tab = _rope_tables(seq_len, ATTN_KV_HEADS)
    kr = np.arange(WIN_KEYS)[:, None]
    qi = np.arange(ATTN_GROUP * ATTN_BLOCK)[None, :] % ATTN_BLOCK
    band = np.stack([np.where(np.abs(qi + off * ATTN_BLOCK - kr) <= WINDOW, 0.0, NEG_BIG) for off in range(3)])
    band = jnp.asarray(band, dtype=F32)
    return pl.pallas_call(
        functools.partial(_lat_attn_kernel, seq_len=seq_len),
        grid=(n // seq_len, n_blk),
        in_specs=[
            pl.BlockSpec((ATTN_BLOCK, ATTN_W), lambda b, i: (qblk0 + b * n_blk + i, 0)),
            pl.BlockSpec((seq_len, 2 * KV_W), lambda b, i: (seq0 + b, 0)),
            pl.BlockSpec((None, None, past, KV_W), lambda b, i: (b, layer, 0, 0)),
            pl.BlockSpec((None, None, past, KV_W), lambda b, i: (b, layer, 0, 0)),
            pl.BlockSpec((None, 1, ATTN_W), lambda b, i: (layer, 0, 0)),
            pl.BlockSpec((None, 1, KV_W), lambda b, i: (layer, 0, 0)),
            pl.BlockSpec((None, 1, ATTN_HEADS), lambda b, i: (layer, 0, 0)),
            pl.BlockSpec((ATTN_BLOCK, ATTN_W), lambda b, i: (i, 0)),
            pl.BlockSpec((ATTN_BLOCK, ATTN_W), lambda b, i: (i, 0)),
            pl.BlockSpec((seq_len, KV_W), lambda b, i: (0, 0)),
            pl.BlockSpec((seq_len, KV_W), lambda b, i: (0, 0)),
            pl.BlockSpec((3, WIN_KEYS, ATTN_GROUP * ATTN_BLOCK), lambda b, i: (0, 0, 0), pipeline_mode=pl.Buffered(1)),
        ],
        out_specs=pl.BlockSpec((ATTN_BLOCK, ATTN_W), lambda b, i: (b * n_blk + i, 0)),
        out_shape=jax.ShapeDtypeStruct((n, ATTN_W), F32),
        scratch_shapes=[
            pltpu.VMEM((seq_len, KV_W), BF16),
            pltpu.VMEM((past, KV_W), BF16),
            pltpu.VMEM((KV_W, past), BF16),
        ],
        compiler_params=_params("arbitrary", "arbitrary"),
        name="lat_attn",
    )(zq, zkv, cache_k, cache_v, q_norm, k_norm, sink, cosq, sinq, cosk, sink_tab, band)


def kernel(x_prompt, x_sample, c, cache_attn_k, cache_attn_v, state_hgrn, c_ctx, w_ada, b_ada, norm_ffn1, norm_mix, norm_ffn2, ffn1_w_gate_up, ffn1_w_down, ffn2_w_gate_up, ffn2_w_down, w_in, w_out, hgrn_lower_bounds, hgrn_norm, q_norm, k_norm, attn_sink):
    batch, seq, _ = x_prompt.shape
    dec_batch, dec_seq, _ = x_sample.shape
    past = cache_attn_k.shape[2]
    assert dec_batch + 1 <= ADA_ROWS

    cond = jnp.concatenate([c_ctx[None], c, jnp.zeros((ADA_ROWS - 1 - dec_batch, D_MODEL), F32)], axis=0)
    mod = _ada_call(cond, w_ada, b_ada).reshape(DEPTH, ADA_ROWS, N_MOD, D_MODEL)

    bf = lambda w: w.astype(BF16)
    w_gu1, w_d1, w_gu2, w_d2 = ffn1_w_gate_up, ffn1_w_down, ffn2_w_gate_up, ffn2_w_down
    w_in_b, w_out_b = bf(w_in), bf(w_out)
    row3 = lambda w: w.reshape(DEPTH, 1, w.shape[-1])
    n1, nm, n2 = row3(norm_ffn1), row3(norm_mix), row3(norm_ffn2)
    hn = row3(hgrn_norm)
    qn = row3(jnp.tile(q_norm, (1, ATTN_HEADS)))
    kn = row3(jnp.tile(k_norm, (1, ATTN_KV_HEADS)))
    sink = row3(attn_sink)
    cache_k = cache_attn_k.reshape(dec_batch, DEPTH, past, KV_W)
    cache_v = cache_attn_v.reshape(dec_batch, DEPTH, past, KV_W)
    lb2 = hgrn_lower_bounds.reshape(DEPTH, 1, HG_2)

    n_ctx, n_lat = batch * seq, dec_batch * dec_seq
    span_c, span_l = (0, n_ctx), (n_ctx, n_lat)
    x_c, x_l = x_prompt.reshape(n_ctx, D_MODEL), x_sample.reshape(n_lat, D_MODEL)
    new_k = jnp.zeros((batch, DEPTH, seq, KV_W), F32)
    new_v = jnp.zeros((batch, DEPTH, seq, KV_W), F32)
    new_s = jnp.zeros((batch, DEPTH, 2, HGRN_HEADS, HEAD_DIM, HEAD_DIM), F32)
    for layer in range(DEPTH):
        x1, zu, zh, zq, zkv = _ffn_in_call(x_c, x_l, mod, n1, nm, w_gu1, w_d1, w_in_b, layer=layer, rows_per_cond=dec_seq)
        fo_c = _fourier_call(zu, seq, span_c)
        fo_l = _fourier_call(zu, dec_seq, span_l)
        ho_c, new_s = _hgrn_call(zh, lb2, hn, None, new_s, seq_len=seq, layer=layer, span=span_c)
        (ho_l,) = _hgrn_call(zh, lb2, hn, state_hgrn, None, seq_len=dec_seq, layer=layer, span=span_l)
        ao_c, new_k, new_v = _ctx_attn_call(zq, zkv, qn, kn, sink, new_k, new_v, seq_len=seq, layer=layer, span=span_c)
        ao_l = _lat_attn_call(zq, zkv, cache_k, cache_v, qn, kn, sink, seq_len=dec_seq, layer=layer, span=span_l)
        x_c, x_l = _out_ffn_call(x1, mod, n2, (fo_c, fo_l), (ho_c, ho_l), (ao_c, ao_l), w_out_b, w_gu2, w_d2,
                                 layer=layer, rows_per_cond=dec_seq)

    y_p = x_c.reshape(batch, seq, D_MODEL)
    y_s = x_l.reshape(dec_batch, dec_seq, D_MODEL)
    cache_shape = (batch, DEPTH, seq, ATTN_KV_HEADS, HEAD_DIM)
    return (y_p, y_s, new_k.reshape(cache_shape), new_v.reshape(cache_shape), new_s)
```

```python
import functools

import numpy as np
import jax
import jax.numpy as jnp
from jax import lax
from jax.experimental import pallas as pl
from jax.experimental.pallas import tpu as pltpu

F32 = jnp.float32
BF16 = jnp.bfloat16

D_MODEL = 1024
DEPTH = 2
HEAD_DIM = 64
FOURIER_W = 256
HGRN_W = 256
HGRN_HEADS = 4
ATTN_W = 512
ATTN_HEADS = 8
ATTN_KV_HEADS = 2
ATTN_GROUP = 4
KV_W = 128
MIX_W = 1024
WINDOW = 128
ATTN_BLOCK = 128
D_FF = 2816
GRID_W = 64
ROPE_BASE = 10000.0
EPS = 1e-6
N_MOD = 9
NEG_BIG = -1e30
LB_FLOOR = 1e-30
LOG2_E = 1.4426950408889634

HG_2 = 2 * HGRN_W
ZH_W = 5 * HGRN_W
IN_W = FOURIER_W + ZH_W + ATTN_W + 2 * KV_W

SUBLANES = 8
VMEM_LIMIT_BYTES = 63 * 1024 * 1024

ADA_ROWS = 8
ADA_TN = 1536
FFN_TM = 512
FFN_FC = 256
SMALL_SEQ_ROWS = 1024
HGRN_TILE = 128
HGRN_LEVELS = (128, 64, 32, 16, 8, 4, 2)
WIN_KEYS = 3 * ATTN_BLOCK


def _params(*semantics):
    return pltpu.CompilerParams(dimension_semantics=semantics, vmem_limit_bytes=VMEM_LIMIT_BYTES)


def _mod_norm(x, norm_w, shift, scale):
    ms = jnp.mean(x * x, axis=-1, keepdims=True)
    y = x * lax.rsqrt(ms + EPS) * norm_w
    return y * (1.0 + scale) + shift


def _head_ones(width):
    r = lax.broadcasted_iota(jnp.int32, (width, width), 0) // HEAD_DIM
    c = lax.broadcasted_iota(jnp.int32, (width, width), 1) // HEAD_DIM
    return jnp.where(r == c, 1.0, 0.0).astype(BF16)


def _head_rmsnorm(x, w):
    sq = (x * x).astype(BF16)
    ms = jnp.dot(sq, _head_ones(x.shape[-1]), preferred_element_type=F32) * (1.0 / HEAD_DIM)
    return x * lax.rsqrt(ms + EPS) * w


def _ada_kernel(cond_ref, w_ref, b_ref, o_ref):
    cnd = cond_ref[...]
    act = (cnd * jax.nn.sigmoid(cnd)).astype(BF16)
    o_ref[...] = jnp.dot(act, w_ref[...].astype(BF16), preferred_element_type=F32) + b_ref[...]


def _ada_call(cond, w_ada, b_ada):
    n_out = N_MOD * D_MODEL
    return pl.pallas_call(
        _ada_kernel,
        grid=(DEPTH, n_out // ADA_TN),
        in_specs=[
            pl.BlockSpec((ADA_ROWS, D_MODEL), lambda l, j: (0, 0)),
            pl.BlockSpec((None, D_MODEL, ADA_TN), lambda l, j: (l, 0, j)),
            pl.BlockSpec((None, 1, ADA_TN), lambda l, j: (l, 0, j)),
        ],
        out_specs=pl.BlockSpec((None, ADA_ROWS, ADA_TN), lambda l, j: (l, 0, j)),
        out_shape=jax.ShapeDtypeStruct((DEPTH, ADA_ROWS, n_out), F32),
        compiler_params=_params("arbitrary", "arbitrary"),
        name="adaln",
    )(cond, w_ada, b_ada.reshape(DEPTH, 1, n_out))


def _weight_chunk_copies(wgu_hbm, wd_hbm, stage_gu, stage_d, sems, layer, c, slot):
    lo, hi = c * FFN_FC, (c + 1) * FFN_FC
    return (
        pltpu.make_async_copy(wgu_hbm.at[layer, :, lo:hi], stage_gu.at[slot, :, 0:FFN_FC], sems.at[slot, 0]),
        pltpu.make_async_copy(wgu_hbm.at[layer, :, D_FF + lo:D_FF + hi], stage_gu.at[slot, :, FFN_FC:2 * FFN_FC],
                              sems.at[slot, 1]),
        pltpu.make_async_copy(wd_hbm.at[layer, lo:hi, :], stage_d.at[slot], sems.at[slot, 2]),
    )


def _swiglu_chunks(h, wgu_ref, wd_ref, fetch=None):
    n_chunks = D_FF // FFN_FC
    if fetch is not None:
        stage_gu, stage_d = fetch[2], fetch[3]
        for cp in _weight_chunk_copies(*fetch, 0, 0):
            cp.start()
    acc = None
    for c in range(n_chunks):
        if fetch is not None:
            slot = c % 2
            if c + 1 < n_chunks:
                for cp in _weight_chunk_copies(*fetch, c + 1, 1 - slot):
                    cp.start()
            for cp in _weight_chunk_copies(*fetch, c, slot):
                cp.wait()
            wgu_ref[:, c * FFN_FC:(c + 1) * FFN_FC] = stage_gu[slot, :, 0:FFN_FC].astype(BF16)
            wgu_ref[:, D_FF + c * FFN_FC:D_FF + (c + 1) * FFN_FC] = stage_gu[slot, :, FFN_FC:2 * FFN_FC].astype(BF16)
            wd_ref[c * FFN_FC:(c + 1) * FFN_FC, :] = stage_d[slot].astype(BF16)
        g = jnp.dot(h, wgu_ref[:, c * FFN_FC:(c + 1) * FFN_FC], preferred_element_type=F32)
        u = jnp.dot(h, wgu_ref[:, D_FF + c * FFN_FC:D_FF + (c + 1) * FFN_FC], preferred_element_type=F32)
        act = (g * jax.nn.sigmoid(g) * u).astype(BF16)
        part = jnp.dot(act, wd_ref[c * FFN_FC:(c + 1) * FFN_FC, :], preferred_element_type=F32)
        acc = part if acc is None else acc + part
    return acc


def _mod_spec(layer, tiles_a, tm, rows_per_cond):
    def index(i):
        row = jnp.where(i < tiles_a, 0, 1 + ((i - tiles_a) * tm) // rows_per_cond)
        return (layer, row, 0, 0)
    return pl.BlockSpec((None, None, N_MOD, D_MODEL), index)


def _stream_specs(width, tiles_a, tm):
    return (pl.BlockSpec((tm, width), lambda i: (jnp.minimum(i, tiles_a - 1), 0)),
            pl.BlockSpec((tm, width), lambda i: (jnp.maximum(i - tiles_a, 0), 0)))


def _resident(shape, layer):
    return pl.BlockSpec((None,) + shape, lambda i: (layer,) + (0,) * len(shape), pipeline_mode=pl.Buffered(1))


def _first_step_fetches(body, fetch):
    first = pl.program_id(0) == 0
    pl.when(first)(functools.partial(body, fetch))
    pl.when(jnp.logical_not(first))(functools.partial(body, None))


def _ffn_weight_scratch():
    return [
        pltpu.VMEM((D_MODEL, 2 * D_FF), BF16),
        pltpu.VMEM((D_FF, D_MODEL), BF16),
        pltpu.VMEM((2, D_MODEL, 2 * FFN_FC), F32),
        pltpu.VMEM((2, FFN_FC, D_MODEL), F32),
        pltpu.SemaphoreType.DMA((2, 3)),
    ]


def _ffn_in_kernel(xa_ref, xb_ref, mod_ref, n1_ref, nm_ref, wgu_hbm, wd_hbm, win_ref,
                   x1_ref, zu_ref, zh_ref, zq_ref, zkv_ref, wgu_ref, wd_ref, stage_gu, stage_d, sems, *, layer, tiles_a):
    def body(fetch):
        x = jnp.where(pl.program_id(0) < tiles_a, xa_ref[...], xb_ref[...])
        h = _mod_norm(x, n1_ref[...], mod_ref[0:1, :], mod_ref[1:2, :]).astype(BF16)
        x1 = x + 0.5 * mod_ref[2:3, :] * _swiglu_chunks(h, wgu_ref, wd_ref, fetch)
        x1_ref[...] = x1
        h2 = _mod_norm(x1, nm_ref[...], mod_ref[3:4, :], mod_ref[4:5, :]).astype(BF16)
        z = jnp.dot(h2, win_ref[...], preferred_element_type=F32)
        zu_ref[...] = z[:, :FOURIER_W]
        zh_ref[...] = z[:, FOURIER_W:FOURIER_W + ZH_W]
        zq_ref[...] = z[:, FOURIER_W + ZH_W:FOURIER_W + ZH_W + ATTN_W]
        zkv_ref[...] = z[:, FOURIER_W + ZH_W + ATTN_W:]

    _first_step_fetches(body, (wgu_hbm, wd_hbm, stage_gu, stage_d, sems, layer))


def _ffn_in_call(x_a, x_b, mod, n1, nm, w_gu, w_down, w_in, *, layer, rows_per_cond):
    tm = FFN_TM
    tiles_a = x_a.shape[0] // tm
    n = x_a.shape[0] + x_b.shape[0]
    widths = (D_MODEL, FOURIER_W, ZH_W, ATTN_W, 2 * KV_W)
    return pl.pallas_call(
        functools.partial(_ffn_in_kernel, layer=layer, tiles_a=tiles_a),
        grid=(n // tm,),
        in_specs=[
            *_stream_specs(D_MODEL, tiles_a, tm),
            _mod_spec(layer, tiles_a, tm, rows_per_cond),
            pl.BlockSpec((None, 1, D_MODEL), lambda i: (layer, 0, 0)),
            pl.BlockSpec((None, 1, D_MODEL), lambda i: (layer, 0, 0)),
            pl.BlockSpec(memory_space=pl.ANY),
            pl.BlockSpec(memory_space=pl.ANY),
            _resident((D_MODEL, IN_W), layer),
        ],
        out_specs=[pl.BlockSpec((tm, w), lambda i: (i, 0)) for w in widths],
        out_shape=[jax.ShapeDtypeStruct((n, w), F32) for w in widths],
        scratch_shapes=_ffn_weight_scratch(),
        compiler_params=_params("arbitrary"),
        name="ffn_in",
    )(x_a, x_b, mod, n1, nm, w_gu, w_down, w_in)


def _out_ffn_kernel(x_ref, mod_ref, n2_ref, foa_ref, fob_ref, hoa_ref, hob_ref, aoa_ref, aob_ref,
                    wout_ref, wgu_hbm, wd_hbm, oa_ref, ob_ref,
                    wgu_ref, wd_ref, stage_gu, stage_d, sems, *, layer, tiles_a):
    is_a = pl.program_id(0) < tiles_a

    def body(fetch):
        pick = lambda a_ref, b_ref: jnp.where(is_a, a_ref[...], b_ref[...]).astype(BF16)
        y = jnp.dot(pick(foa_ref, fob_ref), wout_ref[0:FOURIER_W, :], preferred_element_type=F32)
        y += jnp.dot(pick(hoa_ref, hob_ref), wout_ref[FOURIER_W:FOURIER_W + HGRN_W, :], preferred_element_type=F32)
        y += jnp.dot(pick(aoa_ref, aob_ref), wout_ref[FOURIER_W + HGRN_W:, :], preferred_element_type=F32)
        xm = x_ref[...] + mod_ref[5:6, :] * y
        h = _mod_norm(xm, n2_ref[...], mod_ref[6:7, :], mod_ref[7:8, :]).astype(BF16)
        out = xm + 0.5 * mod_ref[8:9, :] * _swiglu_chunks(h, wgu_ref, wd_ref, fetch)

        @pl.when(is_a)
        def _():
            oa_ref[...] = out

        @pl.when(jnp.logical_not(is_a))
        def _():
            ob_ref[...] = out

    _first_step_fetches(body, (wgu_hbm, wd_hbm, stage_gu, stage_d, sems, layer))


def _out_ffn_call(x, mod, n2, fo, ho, ao, w_out, w_gu, w_down, *, layer, rows_per_cond):
    tm = FFN_TM
    n_a, n_b = fo[0].shape[0], fo[1].shape[0]
    tiles_a = n_a // tm
    return pl.pallas_call(
        functools.partial(_out_ffn_kernel, layer=layer, tiles_a=tiles_a),
        grid=((n_a + n_b) // tm,),
        in_specs=[
            pl.BlockSpec((tm, D_MODEL), lambda i: (i, 0)),
            _mod_spec(layer, tiles_a, tm, rows_per_cond),
            pl.BlockSpec((None, 1, D_MODEL), lambda i: (layer, 0, 0)),
            *_stream_specs(FOURIER_W, tiles_a, tm),
            *_stream_specs(HGRN_W, tiles_a, tm),
            *_stream_specs(ATTN_W, tiles_a, tm),
            _resident((MIX_W, D_MODEL), layer),
            pl.BlockSpec(memory_space=pl.ANY),
            pl.BlockSpec(memory_space=pl.ANY),
        ],
        out_specs=list(_stream_specs(D_MODEL, tiles_a, tm)),
        out_shape=[jax.ShapeDtypeStruct((n_a, D_MODEL), F32), jax.ShapeDtypeStruct((n_b, D_MODEL), F32)],
        scratch_shapes=_ffn_weight_scratch(),
        compiler_params=_params("arbitrary"),
        name="out_ffn",
    )(x, mod, n2, fo[0], fo[1], ho[0], ho[1], ao[0], ao[1], w_out, w_gu, w_down)


def _dft_tables(seq_len):
    c = np.arange(HEAD_DIM)
    ang_c = 2.0 * np.pi * np.outer(c, c) / HEAD_DIM
    eye = np.eye(FOURIER_W // HEAD_DIM)
    chan = np.concatenate([np.kron(eye, np.cos(ang_c)), np.kron(eye, np.sin(ang_c))], axis=1) / np.sqrt(HEAD_DIM)
    n = np.arange(seq_len)
    ang_l = 2.0 * np.pi * (np.outer(n, n) % seq_len) / seq_len
    pos = np.concatenate([np.cos(ang_l), -np.sin(ang_l)], axis=1) / np.sqrt(seq_len)
    return jnp.asarray(chan, dtype=F32), jnp.asarray(pos, dtype=F32)


def _fourier_kernel(u_ref, chan_ref, pos_ref, o_ref, chan_bf, pos_bf, *, seq_len):
    @pl.when(pl.program_id(0) == 0)
    def _():
        chan_bf[...] = chan_ref[...].astype(BF16)
        pos_bf[...] = pos_ref[...].astype(BF16)

    t = jnp.dot(u_ref[...].astype(BF16), chan_bf[...], preferred_element_type=F32)
    pos = pos_bf[...]
    for r0 in range(0, u_ref.shape[0], seq_len):
        tb = t[r0:r0 + seq_len, :]
        stacked = jnp.concatenate([tb[:, :FOURIER_W], tb[:, FOURIER_W:]], axis=0).astype(BF16)
        o_ref[r0:r0 + seq_len, :] = jnp.dot(pos, stacked, preferred_element_type=F32)


def _fourier_call(zu, seq_len, span):
    row0, n = span
    chan, pos = _dft_tables(seq_len)
    rows = max(seq_len, SMALL_SEQ_ROWS)
    blk0 = row0 // rows
    return pl.pallas_call(
        functools.partial(_fourier_kernel, seq_len=seq_len),
        grid=(n // rows,),
        in_specs=[
            pl.BlockSpec((rows, FOURIER_W), lambda b: (b + blk0, 0)),
            pl.BlockSpec((FOURIER_W, 2 * FOURIER_W), lambda b: (0, 0), pipeline_mode=pl.Buffered(1)),
            pl.BlockSpec((seq_len, 2 * seq_len), lambda b: (0, 0), pipeline_mode=pl.Buffered(1)),
        ],
        out_specs=pl.BlockSpec((rows, FOURIER_W), lambda b: (b, 0)),
        out_shape=jax.ShapeDtypeStruct((n, FOURIER_W), F32),
        scratch_shapes=[pltpu.VMEM((FOURIER_W, 2 * FOURIER_W), BF16), pltpu.VMEM((seq_len, 2 * seq_len), BF16)],
        compiler_params=_params("arbitrary"),
        name="fourier",
    )(zu, chan, pos)


def _split3(x):
    hi = x.astype(BF16)
    r1 = x - hi.astype(F32)
    mid = r1.astype(BF16)
    lo = (r1 - mid.astype(F32)).astype(BF16)
    return hi, mid, lo


def _level_mid(b, s, pos):
    tile, lanes = b.shape
    if s >= SUBLANES:
        b3 = b.reshape(tile // s, s, lanes)
        return jnp.broadcast_to(b3[:, pos:pos + 1, :], b3.shape).reshape(tile, lanes)
    assert s == 4
    b3 = b.reshape(tile // SUBLANES, SUBLANES, lanes)
    sub = lax.broadcasted_iota(jnp.int32, (1, SUBLANES, 1), 1)
    return jnp.where(sub < 4, b3[:, pos:pos + 1, :], b3[:, 4 + pos:5 + pos, :]).reshape(tile, lanes)


def _hgrn_kernel(*refs, seq_len, layer, has_s0, emit_state):
    refs = list(refs)
    zh_ref, lb_ref, nw_ref, tri_ref = refs[:4]
    pos = 4
    s0_ref = sf_ref = None
    if has_s0:
        s0_ref = refs[pos]
        pos += 1
    if emit_state:
        pos += 1
    ho_ref = refs[pos]
    pos += 1
    if emit_state:
        sf_ref = refs[pos]
        pos += 1
    qin_ref, dec_ref, kvt_ref, stf_ref, stb_ref, oacc_ref = refs[pos:]

    tile = HGRN_TILE
    width = HGRN_W
    n_tiles = seq_len // tile
    c_q, c_v, c_z, c_g = 0, width, 2 * width, 4 * width

    raw = lb_ref[...]
    e = jnp.exp(raw - jnp.max(raw, axis=0, keepdims=True))
    soft = e / jnp.sum(e, axis=0, keepdims=True)
    cum = soft[0]
    for l in range(1, layer + 1):
        cum = cum + soft[l]
    lbd = cum - soft[0]
    lb_floor = jnp.maximum(lbd, LB_FLOOR)
    one_m_lb = 1.0 - lbd

    lane_bwd1 = lax.broadcasted_iota(jnp.int32, (1, HG_2), 1) >= width
    lane_bwd = lax.broadcasted_iota(jnp.int32, (tile, HG_2), 1) >= width
    lane_bwd_s = lax.broadcasted_iota(jnp.int32, (HEAD_DIM, HG_2), 1) >= width
    row = lax.broadcasted_iota(jnp.int32, (tile, 1), 0)
    pair_xor = (lax.broadcasted_iota(jnp.int32, (tile, HGRN_HEADS * tile), 0)
                ^ (lax.broadcasted_iota(jnp.int32, (tile, HGRN_HEADS * tile), 1) % tile))
    ev_mask = (lax.broadcasted_iota(jnp.int32, (HGRN_HEADS * tile, width), 0) // tile
               == lax.broadcasted_iota(jnp.int32, (HGRN_HEADS * tile, width), 1) // HEAD_DIM)
    bd_mask = (lax.broadcasted_iota(jnp.int32, (width, HG_2), 0) // HEAD_DIM
               == (lax.broadcasted_iota(jnp.int32, (width, HG_2), 1) // HEAD_DIM) % HGRN_HEADS)
    lane_head = (lax.broadcasted_iota(jnp.int32, (HEAD_DIM, HG_2), 1) // HEAD_DIM) % HGRN_HEADS
    ones_bd = _head_ones(width)
    nt_dims = (((1,), (1,)), ((), ()))

    def tile_body(t, carry):
        rows = pl.ds(pl.multiple_of(t * tile, tile), tile)
        q = zh_ref[rows, c_q:c_q + width] * (HEAD_DIM ** -0.5)
        v = zh_ref[rows, c_v:c_v + width]
        z = zh_ref[rows, c_z:c_z + HG_2]
        ez = jnp.exp(-jnp.abs(z))
        r = 1.0 / (1.0 + ez)
        tr = ez * r
        pos_z = z >= 0.0
        f = lb_floor + one_m_lb * jnp.where(pos_z, r, tr)
        logf = jnp.log(f)
        kk = one_m_lb * jnp.where(pos_z, tr, r)

        pieces = jnp.concatenate(_split3(logf), axis=1)
        acc = jnp.dot(tri_ref[...], pieces, preferred_element_type=F32)
        prefix = acc[:, :HG_2] + acc[:, HG_2:2 * HG_2] + acc[:, 2 * HG_2:]
        total = prefix[tile - 1:tile, :]
        b = jnp.where(lane_bwd, total - prefix + logf, prefix)
        kk_f, kk_b = kk[:, :width], kk[:, width:]

        a0 = jnp.dot((q * (kk_f + kk_b)).astype(BF16), ones_bd, preferred_element_type=F32)
        o = a0 * v

        b2 = b * LOG2_E
        bf2, bb2 = b2[:, :width], b2[:, width:]
        scores = None
        for s in HGRN_LEVELS:
            second = (row % s) >= s // 2
            k_role = jnp.where(second, kk_b, kk_f)
            if s == 2:
                q_op = q * jnp.where(second, f[:, :width], f[:, width:])
                k_op = k_role
            else:
                mid_f = _level_mid(bf2, s, s // 2 - 1)
                mid_b = _level_mid(bb2, s, s // 2)
                e_q = jnp.where(second, bf2, bb2) - jnp.where(second, mid_f, mid_b)
                e_k = jnp.where(second, mid_b, mid_f) - jnp.where(second, bb2, bf2)
                q_op = q * jnp.exp2(e_q)
                k_op = k_role * jnp.exp2(e_k)
            qt = q_op.astype(BF16)
            kt = k_op.astype(BF16)
            sc = jnp.concatenate(
                [lax.dot_general(qt[:, h * HEAD_DIM:(h + 1) * HEAD_DIM], kt[:, h * HEAD_DIM:(h + 1) * HEAD_DIM],
                                 nt_dims, preferred_element_type=F32) for h in range(HGRN_HEADS)], axis=1)
            scores = sc if scores is None else jnp.where(pair_xor < s, sc, scores)
        scores = jnp.where(pair_xor == 0, 0.0, scores)
        ev = jnp.where(ev_mask, jnp.concatenate([v] * HGRN_HEADS, axis=0), 0.0).astype(BF16)
        o = o + jnp.dot(scores.astype(BF16), ev, preferred_element_type=F32)
        oacc_ref[rows, :] = o

        q2 = jnp.concatenate([q, q], axis=1)
        qin_ref[rows, :] = (q2 * jnp.exp(b)).astype(BF16)
        kout = (kk * jnp.exp(total - b)).astype(BF16)
        dec_ref[t] = jnp.exp(total)
        full = jnp.dot(v.T.astype(BF16), kout, preferred_element_type=F32)
        kvt = full[0:HEAD_DIM, :]
        for hh in range(1, HGRN_HEADS):
            kvt = jnp.where(lane_head == hh, full[hh * HEAD_DIM:(hh + 1) * HEAD_DIM, :], kvt)
        kvt_ref[t] = kvt
        return carry

    lax.fori_loop(0, n_tiles, tile_body, 0)

    if has_s0:
        st0 = jnp.concatenate([s0_ref[d, h] for d in range(2) for h in range(HGRN_HEADS)], axis=0).T
    else:
        st0 = jnp.zeros((HEAD_DIM, HG_2), F32)

    def scan_step(i, st):
        tf, tb = i, n_tiles - 1 - i
        stf_ref[tf] = st
        stb_ref[tb] = st
        dec = jnp.where(lane_bwd1, dec_ref[tb], dec_ref[tf])
        kv = jnp.where(lane_bwd_s, kvt_ref[tb], kvt_ref[tf])
        return st * dec + kv

    st_fin = lax.fori_loop(0, n_tiles, scan_step, st0)
    if emit_state:
        st_t = st_fin.T
        for d in range(2):
            for h in range(HGRN_HEADS):
                r0 = (HGRN_HEADS * d + h) * HEAD_DIM
                sf_ref[d, h] = st_t[r0:r0 + HEAD_DIM, :]

    def out_tile(t):
        rows = pl.ds(pl.multiple_of(t * tile, tile), tile)
        st = jnp.where(lane_bwd_s, stb_ref[t], stf_ref[t])
        bd = jnp.where(bd_mask, jnp.concatenate([st] * HGRN_HEADS, axis=0), 0.0).T.astype(BF16)
        o = oacc_ref[rows, :] + jnp.dot(qin_ref[rows, :], bd, preferred_element_type=F32)
        hg = zh_ref[rows, c_g:c_g + width]
        ho_ref[rows, :] = _head_rmsnorm(o, nw_ref[...]) * (hg * jax.nn.sigmoid(hg))

    def out_body(i, carry):
        out_tile(2 * i)
        out_tile(2 * i + 1)
        return carry

    assert n_tiles % 2 == 0
    lax.fori_loop(0, n_tiles // 2, out_body, 0)


def _hgrn_tri():
    i = np.arange(HGRN_TILE)
    return jnp.asarray((i[None, :] <= i[:, None]).astype(np.float32), dtype=BF16)


def _hgrn_call(zh, lb2, norm_w, s0, final_states, *, seq_len, layer, span):
    row0, n = span
    blk0 = row0 // seq_len
    n_seq = n // seq_len
    has_s0 = s0 is not None
    emit_state = final_states is not None
    state_block = (2, HGRN_HEADS, HEAD_DIM, HEAD_DIM)
    n_tiles = seq_len // HGRN_TILE
    in_specs = [
        pl.BlockSpec((seq_len, ZH_W), lambda b: (b + blk0, 0)),
        pl.BlockSpec((DEPTH, 1, HG_2), lambda b: (0, 0, 0)),
        pl.BlockSpec((None, 1, HGRN_W), lambda b: (layer, 0, 0)),
        pl.BlockSpec((HGRN_TILE, HGRN_TILE), lambda b: (0, 0)),
    ]
    args = [zh, lb2, norm_w, _hgrn_tri()]
    if has_s0:
        in_specs.append(pl.BlockSpec((None, None) + state_block, lambda b: (b, layer, 0, 0, 0, 0)))
        args.append(s0)
    out_specs = [pl.BlockSpec((seq_len, HGRN_W), lambda b: (b, 0))]
    out_shape = [jax.ShapeDtypeStruct((n, HGRN_W), F32)]
    aliases = {}
    if emit_state:
        aliases[len(args)] = 1
        in_specs.append(pl.BlockSpec(memory_space=pl.ANY))
        args.append(final_states)
        out_specs.append(pl.BlockSpec((None, None) + state_block, lambda b: (b, layer, 0, 0, 0, 0)))
        out_shape.append(jax.ShapeDtypeStruct(final_states.shape, F32))
    return pl.pallas_call(
        functools.partial(_hgrn_kernel, seq_len=seq_len, layer=layer, has_s0=has_s0, emit_state=emit_state),
        grid=(n_seq,),
        in_specs=in_specs,
        out_specs=out_specs,
        out_shape=out_shape,
        input_output_aliases=aliases,
        scratch_shapes=[
            pltpu.VMEM((seq_len, HG_2), BF16),
            pltpu.VMEM((n_tiles, 1, HG_2), F32),
            pltpu.VMEM((n_tiles, HEAD_DIM, HG_2), F32),
            pltpu.VMEM((n_tiles, HEAD_DIM, HG_2), F32),
            pltpu.VMEM((n_tiles, HEAD_DIM, HG_2), F32),
            pltpu.VMEM((seq_len, HGRN_W), F32),
        ],
        compiler_params=_params("arbitrary"),
        name="hgrn",
    )(*args)


def _sink_row(sink_ref, kv_head, rows):
    grp = lax.broadcasted_iota(jnp.int32, (1, ATTN_GROUP * rows), 1) // rows
    base = kv_head * ATTN_GROUP
    out = jnp.broadcast_to(sink_ref[:, base:base + 1], (1, ATTN_GROUP * rows))
    for g in range(1, ATTN_GROUP):
        out = jnp.where(grp == g, sink_ref[:, base + g:base + g + 1], out)
    return out


def _group_queries_t(qt, kv_head):
    base = kv_head * ATTN_GROUP
    return jnp.concatenate([qt[(base + g) * HEAD_DIM:(base + g + 1) * HEAD_DIM, :] for g in range(ATTN_GROUP)], axis=1)


def _ctx_attn_kernel(zq_ref, zkv_ref, qn_ref, kn_ref, sink_ref, kc_all_ref, vc_all_ref, ao_ref, kc_ref, vc_ref,
                     *, seq_len):
    del kc_all_ref, vc_all_ref
    rows = seq_len
    q_all = _head_rmsnorm(zq_ref[...], qn_ref[...]) * (HEAD_DIM ** -0.5)
    k_all = _head_rmsnorm(zkv_ref[:, :KV_W], kn_ref[...])
    v_all = zkv_ref[:, KV_W:]
    kc_ref[...] = k_all.reshape(kc_ref.shape)
    vc_ref[...] = v_all.reshape(vc_ref.shape)
    for r0 in range(0, zq_ref.shape[0], seq_len):
        qt = q_all[r0:r0 + rows, :].T.astype(BF16)
        kb = k_all[r0:r0 + rows, :].astype(BF16)
        vt = v_all[r0:r0 + rows, :].T.astype(BF16)
        out_t = []
        for kh in range(ATTN_KV_HEADS):
            hs = slice(kh * HEAD_DIM, (kh + 1) * HEAD_DIM)
            s = jnp.dot(kb[:, hs], _group_queries_t(qt, kh), preferred_element_type=F32)
            sink = _sink_row(sink_ref, kh, rows)
            m = jnp.maximum(jnp.max(s, axis=0, keepdims=True), sink)
            p = jnp.exp(s - m)
            denom = jnp.sum(p, axis=0, keepdims=True) + jnp.exp(sink - m)
            o_t = jnp.dot(vt[hs, :], p.astype(BF16), preferred_element_type=F32) * (1.0 / denom)
            out_t.extend(o_t[:, g * rows:(g + 1) * rows] for g in range(ATTN_GROUP))
        ao_ref[r0:r0 + rows, :] = jnp.concatenate(out_t, axis=0).T


def _ctx_attn_call(zq, zkv, q_norm, k_norm, sink, k_cache, v_cache, *, seq_len, layer, span):
    row0, n = span
    rows = max(seq_len, SMALL_SEQ_ROWS)
    blk0 = row0 // rows
    seqs = rows // seq_len
    cache_spec = pl.BlockSpec((seqs, None, seq_len, KV_W), lambda b: (b, layer, 0, 0))
    return pl.pallas_call(
        functools.partial(_ctx_attn_kernel, seq_len=seq_len),
        grid=(n // rows,),
        in_specs=[
            pl.BlockSpec((rows, ATTN_W), lambda b: (b + blk0, 0)),
            pl.BlockSpec((rows, 2 * KV_W), lambda b: (b + blk0, 0)),
            pl.BlockSpec((None, 1, ATTN_W), lambda b: (layer, 0, 0)),
            pl.BlockSpec((None, 1, KV_W), lambda b: (layer, 0, 0)),
            pl.BlockSpec((None, 1, ATTN_HEADS), lambda b: (layer, 0, 0)),
            pl.BlockSpec(memory_space=pl.ANY),
            pl.BlockSpec(memory_space=pl.ANY),
        ],
        out_specs=[pl.BlockSpec((rows, ATTN_W), lambda b: (b, 0)), cache_spec, cache_spec],
        out_shape=[
            jax.ShapeDtypeStruct((n, ATTN_W), F32),
            jax.ShapeDtypeStruct(k_cache.shape, F32),
            jax.ShapeDtypeStruct(v_cache.shape, F32),
        ],
        input_output_aliases={5: 1, 6: 2},
        compiler_params=_params("arbitrary"),
        name="ctx_attn",
    )(zq, zkv, q_norm, k_norm, sink, k_cache, v_cache)


def _rope_tables(seq_len, n_heads):
    t = np.arange(seq_len)
    row, col = t // GRID_W, t % GRID_W
    half = HEAD_DIM // 2
    inv = (ROPE_BASE ** (-np.arange(0, half, 2, dtype=np.float32) / half)).astype(np.float32)

    def one(pos):
        ang = pos.astype(np.float32)[:, None] * inv[None]
        c, s = np.cos(ang), np.sin(ang)
        return np.concatenate([c, c], axis=1), np.concatenate([-s, s], axis=1)

    cr, sr = one(row)
    cc, sc = one(col)
    cos = np.tile(np.concatenate([cr, cc], axis=1), (1, n_heads)).astype(np.float32)
    sin = np.tile(np.concatenate([sr, sc], axis=1), (1, n_heads)).astype(np.float32)
    return jnp.asarray(cos), jnp.asarray(sin)


def _rope(x, cos, sin_signed):
    width = x.shape[-1]
    lane = lax.broadcasted_iota(jnp.int32, x.shape, 1)
    quarter = HEAD_DIM // 4
    up = pltpu.roll(x, width - quarter, axis=1)
    down = pltpu.roll(x, quarter, axis=1)
    partner = jnp.where((lane % (2 * quarter)) < quarter, up, down)
    return x * cos + partner * sin_signed


def _lat_attn_kernel(zq_ref, zkv_ref, kc_ref, vc_ref, qn_ref, kn_ref, sink_ref,
                     cosq_ref, sinq_ref, cosk_ref, sink_tab_ref, band_ref, ao_ref, kb_ref, kcb_ref, vct_ref,
                     *, seq_len):
    rows = ATTN_BLOCK

    k_all = _head_rmsnorm(zkv_ref[:, 0:KV_W], kn_ref[...])
    kb_ref[...] = _rope(k_all, cosk_ref[...], sink_tab_ref[...]).astype(BF16)
    kcb_ref[...] = kc_ref[...].astype(BF16)
    vct_ref[...] = vc_ref[...].T.astype(BF16)

    def query_block(blk, carry):
        qrows = pl.ds(pl.multiple_of(blk * ATTN_BLOCK, ATTN_BLOCK), ATTN_BLOCK)
        q = _head_rmsnorm(zq_ref[qrows, :], qn_ref[...])
        q = _rope(q, cosq_ref[qrows, :], sinq_ref[qrows, :]) * (HEAD_DIM ** -0.5)
        start = jnp.clip((blk - 1) * ATTN_BLOCK, 0, seq_len - WIN_KEYS)
        start = pl.multiple_of(start, ATTN_BLOCK)
        vwin = zkv_ref[pl.ds(start, WIN_KEYS), KV_W:2 * KV_W]

        band_bias = band_ref[blk - start // ATTN_BLOCK]

        qt = q.T.astype(BF16)
        kwb = kb_ref[pl.ds(start, WIN_KEYS), :]
        kcb = kcb_ref[...]
        vwt = vwin.T.astype(BF16)
        vct = vct_ref[...]
        out_t = []
        for kh in range(ATTN_KV_HEADS):
            hs = slice(kh * HEAD_DIM, (kh + 1) * HEAD_DIM)
            qg = _group_queries_t(qt, kh)
            s_loc = jnp.dot(kwb[:, hs], qg, preferred_element_type=F32) + band_bias
            s_ctx = jnp.dot(kcb[:, hs], qg, preferred_element_type=F32)
            sink = _sink_row(sink_ref, kh, rows)
            m = jnp.maximum(jnp.maximum(jnp.max(s_loc, axis=0, keepdims=True),
                                        jnp.max(s_ctx, axis=0, keepdims=True)), sink)
            p_loc = jnp.exp(s_loc - m)
            p_ctx = jnp.exp(s_ctx - m)
            denom = (jnp.sum(p_loc, axis=0, keepdims=True) + jnp.sum(p_ctx, axis=0, keepdims=True)
                     + jnp.exp(sink - m))
            o_t = (jnp.dot(vwt[hs, :], p_loc.astype(BF16), preferred_element_type=F32)
                   + jnp.dot(vct[hs, :], p_ctx.astype(BF16), preferred_element_type=F32)) * (1.0 / denom)
            out_t.extend(o_t[:, g * rows:(g + 1) * rows] for g in range(ATTN_GROUP))
        ao_ref[qrows, :] = jnp.concatenate(out_t, axis=0).T
        return carry

    lax.fori_loop(0, seq_len // ATTN_BLOCK, query_block, 0)


def _lat_attn_call(zq, zkv, cache_k, cache_v, q_norm, k_norm, sink, *, seq_len, layer, span):
    row0, n = span
    seq0 = row0 // seq_len
    past = cache_k.shape[2]
    cosq, sinq = _rope_tables(seq_len, ATTN_HEADS)
    cosk, sink_tab = _rope_tables(seq_len, ATTN_KV_HEADS)
    kr = np.arange(WIN_KEYS)[:, None]
    qi = np.arange(ATTN_GROUP * ATTN_BLOCK)[None, :] % ATTN_BLOCK
    band = np.stack([np.where(np.abs(qi + off * ATTN_BLOCK - kr) <= WINDOW, 0.0, NEG_BIG) for off in range(3)])
    band = jnp.asarray(band, dtype=F32)
    return pl.pallas_call(
        functools.partial(_lat_attn_kernel, seq_len=seq_len),
        grid=(n // seq_len,),
        in_specs=[
            pl.BlockSpec((seq_len, ATTN_W), lambda b: (seq0 + b, 0)),
            pl.BlockSpec((seq_len, 2 * KV_W), lambda b: (seq0 + b, 0)),
            pl.BlockSpec((None, None, past, KV_W), lambda b: (b, layer, 0, 0)),
            pl.BlockSpec((None, None, past, KV_W), lambda b: (b, layer, 0, 0)),
            pl.BlockSpec((None, 1, ATTN_W), lambda b: (layer, 0, 0)),
            pl.BlockSpec((None, 1, KV_W), lambda b: (layer, 0, 0)),
            pl.BlockSpec((None, 1, ATTN_HEADS), lambda b: (layer, 0, 0)),
            pl.BlockSpec((seq_len, ATTN_W), lambda b: (0, 0), pipeline_mode=pl.Buffered(1)),
            pl.BlockSpec((seq_len, ATTN_W), lambda b: (0, 0), pipeline_mode=pl.Buffered(1)),
            pl.BlockSpec((seq_len, KV_W), lambda b: (0, 0), pipeline_mode=pl.Buffered(1)),
            pl.BlockSpec((seq_len, KV_W), lambda b: (0, 0), pipeline_mode=pl.Buffered(1)),
            pl.BlockSpec((3, WIN_KEYS, ATTN_GROUP * ATTN_BLOCK), lambda b: (0, 0, 0), pipeline_mode=pl.Buffered(1)),
        ],
        out_specs=pl.BlockSpec((seq_len, ATTN_W), lambda b: (b, 0)),
        out_shape=jax.ShapeDtypeStruct((n, ATTN_W), F32),
        scratch_shapes=[
            pltpu.VMEM((seq_len, KV_W), BF16),
            pltpu.VMEM((past, KV_W), BF16),
            pltpu.VMEM((KV_W, past), BF16),
        ],
        compiler_params=_params("arbitrary"),
        name="lat_attn",
    )(zq, zkv, cache_k, cache_v, q_norm, k_norm, sink, cosq, sinq, cosk, sink_tab, band)


def kernel(x_prompt, x_sample, c, cache_attn_k, cache_attn_v, state_hgrn, c_ctx, w_ada, b_ada, norm_ffn1, norm_mix, norm_ffn2, ffn1_w_gate_up, ffn1_w_down, ffn2_w_gate_up, ffn2_w_down, w_in, w_out, hgrn_lower_bounds, hgrn_norm, q_norm, k_norm, attn_sink):
    batch, seq, _ = x_prompt.shape
    dec_batch, dec_seq, _ = x_sample.shape
    past = cache_attn_k.shape[2]
    assert dec_batch + 1 <= ADA_ROWS

    cond = jnp.concatenate([c_ctx[None], c, jnp.zeros((ADA_ROWS - 1 - dec_batch, D_MODEL), F32)], axis=0)
    mod = _ada_call(cond, w_ada, b_ada).reshape(DEPTH, ADA_ROWS, N_MOD, D_MODEL)

    bf = lambda w: w.astype(BF16)
    w_gu1, w_d1, w_gu2, w_d2 = ffn1_w_gate_up, ffn1_w_down, ffn2_w_gate_up, ffn2_w_down
    w_in_b, w_out_b = bf(w_in), bf(w_out)
    row3 = lambda w: w.reshape(DEPTH, 1, w.shape[-1])
    n1, nm, n2 = row3(norm_ffn1), row3(norm_mix), row3(norm_ffn2)
    hn = row3(hgrn_norm)
    qn = row3(jnp.tile(q_norm, (1, ATTN_HEADS)))
    kn = row3(jnp.tile(k_norm, (1, ATTN_KV_HEADS)))
    sink = row3(attn_sink)
    cache_k = cache_attn_k.reshape(dec_batch, DEPTH, past, KV_W)
    cache_v = cache_attn_v.reshape(dec_batch, DEPTH, past, KV_W)
    lb2 = hgrn_lower_bounds.reshape(DEPTH, 1, HG_2)

    n_ctx, n_lat = batch * seq, dec_batch * dec_seq
    span_c, span_l = (0, n_ctx), (n_ctx, n_lat)
    x_c, x_l = x_prompt.reshape(n_ctx, D_MODEL), x_sample.reshape(n_lat, D_MODEL)
    new_k = jnp.zeros((batch, DEPTH, seq, KV_W), F32)
    new_v = jnp.zeros((batch, DEPTH, seq, KV_W), F32)
    new_s = jnp.zeros((batch, DEPTH, 2, HGRN_HEADS, HEAD_DIM, HEAD_DIM), F32)
    for layer in range(DEPTH):
        x1, zu, zh, zq, zkv = _ffn_in_call(x_c, x_l, mod, n1, nm, w_gu1, w_d1, w_in_b, layer=layer, rows_per_cond=dec_seq)
        fo_c = _fourier_call(zu, seq, span_c)
        fo_l = _fourier_call(zu, dec_seq, span_l)
        ho_c, new_s = _hgrn_call(zh, lb2, hn, None, new_s, seq_len=seq, layer=layer, span=span_c)
        (ho_l,) = _hgrn_call(zh, lb2, hn, state_hgrn, None, seq_len=dec_seq, layer=layer, span=span_l)
        ao_c, new_k, new_v = _ctx_attn_call(zq, zkv, qn, kn, sink, new_k, new_v, seq_len=seq, layer=layer, span=span_c)
        ao_l = _lat_attn_call(zq, zkv, cache_k, cache_v, qn, kn, sink, seq_len=dec_seq, layer=layer, span=span_l)
        x_c, x_l = _out_ffn_call(x1, mod, n2, (fo_c, fo_l), (ho_c, ho_l), (ao_c, ao_l), w_out_b, w_gu2, w_d2,
                                 layer=layer, rows_per_cond=dec_seq)

    y_p = x_c.reshape(batch, seq, D_MODEL)
    y_s = x_l.reshape(dec_batch, dec_seq, D_MODEL)
    cache_shape = (batch, DEPTH, seq, ATTN_KV_HEADS, HEAD_DIM)
    return (y_p, y_s, new_k.reshape(cache_shape), new_v.reshape(cache_shape), new_s)
```

```python
import functools

import numpy as np
import jax
import jax.numpy as jnp
from jax import lax
from jax.experimental import pallas as pl
from jax.experimental.pallas import tpu as pltpu

F32 = jnp.float32
BF16 = jnp.bfloat16

D_MODEL = 1024
DEPTH = 2
HEAD_DIM = 64
FOURIER_W = 256
HGRN_W = 256
HGRN_HEADS = 4
ATTN_W = 512
ATTN_HEADS = 8
ATTN_KV_HEADS = 2
ATTN_GROUP = 4
KV_W = 128
MIX_W = 1024
WINDOW = 128
ATTN_BLOCK = 128
D_FF = 2816
GRID_W = 64
ROPE_BASE = 10000.0
EPS = 1e-6
N_MOD = 9
NEG_BIG = -1e30
LB_FLOOR = 1e-30
LOG2_E = 1.4426950408889634

HG_2 = 2 * HGRN_W
ZH_W = 5 * HGRN_W
IN_W = FOURIER_W + ZH_W + ATTN_W + 2 * KV_W

SUBLANES = 8
VMEM_LIMIT_BYTES = 63 * 1024 * 1024

ADA_ROWS = 8
ADA_TN = 1536
FFN_TM = 512
FFN_FC = 256
SMALL_SEQ_ROWS = 1024
HGRN_TILE = 128
HGRN_LEVELS = (128, 64, 32, 16, 8, 4, 2)
WIN_KEYS = 3 * ATTN_BLOCK


def _params(*semantics):
    return pltpu.CompilerParams(dimension_semantics=semantics, vmem_limit_bytes=VMEM_LIMIT_BYTES)


def _mod_norm(x, norm_w, shift, scale):
    ms = jnp.mean(x * x, axis=-1, keepdims=True)
    y = x * lax.rsqrt(ms + EPS) * norm_w
    return y * (1.0 + scale) + shift


def _head_ones(width):
    r = lax.broadcasted_iota(jnp.int32, (width, width), 0) // HEAD_DIM
    c = lax.broadcasted_iota(jnp.int32, (width, width), 1) // HEAD_DIM
    return jnp.where(r == c, 1.0, 0.0).astype(BF16)


def _head_rmsnorm(x, w):
    sq = (x * x).astype(BF16)
    ms = jnp.dot(sq, _head_ones(x.shape[-1]), preferred_element_type=F32) * (1.0 / HEAD_DIM)
    return x * lax.rsqrt(ms + EPS) * w


def _ada_kernel(cond_ref, w_ref, b_ref, o_ref):
    cnd = cond_ref[...]
    act = (cnd * jax.nn.sigmoid(cnd)).astype(BF16)
    o_ref[...] = jnp.dot(act, w_ref[...].astype(BF16), preferred_element_type=F32) + b_ref[...]


def _ada_call(cond, w_ada, b_ada):
    n_out = N_MOD * D_MODEL
    return pl.pallas_call(
        _ada_kernel,
        grid=(DEPTH, n_out // ADA_TN),
        in_specs=[
            pl.BlockSpec((ADA_ROWS, D_MODEL), lambda l, j: (0, 0)),
            pl.BlockSpec((None, D_MODEL, ADA_TN), lambda l, j: (l, 0, j)),
            pl.BlockSpec((None, 1, ADA_TN), lambda l, j: (l, 0, j)),
        ],
        out_specs=pl.BlockSpec((None, ADA_ROWS, ADA_TN), lambda l, j: (l, 0, j)),
        out_shape=jax.ShapeDtypeStruct((DEPTH, ADA_ROWS, n_out), F32),
        compiler_params=_params("arbitrary", "arbitrary"),
        name="adaln",
    )(cond, w_ada, b_ada.reshape(DEPTH, 1, n_out))


def _weight_chunk_copies(wgu_hbm, wd_hbm, stage_gu, stage_d, sems, layer, c, slot):
    lo, hi = c * FFN_FC, (c + 1) * FFN_FC
    return (
        pltpu.make_async_copy(wgu_hbm.at[layer, :, lo:hi], stage_gu.at[slot, :, 0:FFN_FC], sems.at[slot, 0]),
        pltpu.make_async_copy(wgu_hbm.at[layer, :, D_FF + lo:D_FF + hi], stage_gu.at[slot, :, FFN_FC:2 * FFN_FC],
                              sems.at[slot, 1]),
        pltpu.make_async_copy(wd_hbm.at[layer, lo:hi, :], stage_d.at[slot], sems.at[slot, 2]),
    )


def _swiglu_chunks(h, wgu_ref, wd_ref, fetch=None):
    n_chunks = D_FF // FFN_FC
    if fetch is not None:
        stage_gu, stage_d = fetch[2], fetch[3]
        for cp in _weight_chunk_copies(*fetch, 0, 0):
            cp.start()
    acc = None
    for c in range(n_chunks):
        if fetch is not None:
            slot = c % 2
            if c + 1 < n_chunks:
                for cp in _weight_chunk_copies(*fetch, c + 1, 1 - slot):
                    cp.start()
            for cp in _weight_chunk_copies(*fetch, c, slot):
                cp.wait()
            wgu_ref[:, c * FFN_FC:(c + 1) * FFN_FC] = stage_gu[slot, :, 0:FFN_FC].astype(BF16)
            wgu_ref[:, D_FF + c * FFN_FC:D_FF + (c + 1) * FFN_FC] = stage_gu[slot, :, FFN_FC:2 * FFN_FC].astype(BF16)
            wd_ref[c * FFN_FC:(c + 1) * FFN_FC, :] = stage_d[slot].astype(BF16)
        g = jnp.dot(h, wgu_ref[:, c * FFN_FC:(c + 1) * FFN_FC], preferred_element_type=F32)
        u = jnp.dot(h, wgu_ref[:, D_FF + c * FFN_FC:D_FF + (c + 1) * FFN_FC], preferred_element_type=F32)
        act = (g * jax.nn.sigmoid(g) * u).astype(BF16)
        part = jnp.dot(act, wd_ref[c * FFN_FC:(c + 1) * FFN_FC, :], preferred_element_type=F32)
        acc = part if acc is None else acc + part
    return acc


def _mod_spec(layer, tiles_a, tm, rows_per_cond):
    def index(i):
        row = jnp.where(i < tiles_a, 0, 1 + ((i - tiles_a) * tm) // rows_per_cond)
        return (layer, row, 0, 0)
    return pl.BlockSpec((None, None, N_MOD, D_MODEL), index)


def _stream_specs(width, tiles_a, tm):
    return (pl.BlockSpec((tm, width), lambda i: (jnp.minimum(i, tiles_a - 1), 0)),
            pl.BlockSpec((tm, width), lambda i: (jnp.maximum(i - tiles_a, 0), 0)))


def _resident(shape, layer):
    return pl.BlockSpec((None,) + shape, lambda i: (layer,) + (0,) * len(shape), pipeline_mode=pl.Buffered(1))


def _first_step_fetches(body, fetch):
    first = pl.program_id(0) == 0
    pl.when(first)(functools.partial(body, fetch))
    pl.when(jnp.logical_not(first))(functools.partial(body, None))


def _ffn_weight_scratch():
    return [
        pltpu.VMEM((D_MODEL, 2 * D_FF), BF16),
        pltpu.VMEM((D_FF, D_MODEL), BF16),
        pltpu.VMEM((2, D_MODEL, 2 * FFN_FC), F32),
        pltpu.VMEM((2, FFN_FC, D_MODEL), F32),
        pltpu.SemaphoreType.DMA((2, 3)),
    ]


def _ffn_in_kernel(xa_ref, xb_ref, mod_ref, n1_ref, nm_ref, wgu_hbm, wd_hbm, win_ref,
                   x1_ref, zu_ref, zh_ref, zq_ref, zkv_ref, wgu_ref, wd_ref, stage_gu, stage_d, sems, *, layer, tiles_a):
    def body(fetch):
        x = jnp.where(pl.program_id(0) < tiles_a, xa_ref[...], xb_ref[...])
        h = _mod_norm(x, n1_ref[...], mod_ref[0:1, :], mod_ref[1:2, :]).astype(BF16)
        x1 = x + 0.5 * mod_ref[2:3, :] * _swiglu_chunks(h, wgu_ref, wd_ref, fetch)
        x1_ref[...] = x1
        h2 = _mod_norm(x1, nm_ref[...], mod_ref[3:4, :], mod_ref[4:5, :]).astype(BF16)
        z = jnp.dot(h2, win_ref[...], preferred_element_type=F32)
        zu_ref[...] = z[:, :FOURIER_W]
        zh_ref[...] = z[:, FOURIER_W:FOURIER_W + ZH_W]
        zq_ref[...] = z[:, FOURIER_W + ZH_W:FOURIER_W + ZH_W + ATTN_W]
        zkv_ref[...] = z[:, FOURIER_W + ZH_W + ATTN_W:]

    _first_step_fetches(body, (wgu_hbm, wd_hbm, stage_gu, stage_d, sems, layer))


def _ffn_in_call(x_a, x_b, mod, n1, nm, w_gu, w_down, w_in, *, layer, rows_per_cond):
    tm = FFN_TM
    tiles_a = x_a.shape[0] // tm
    n = x_a.shape[0] + x_b.shape[0]
    widths = (D_MODEL, FOURIER_W, ZH_W, ATTN_W, 2 * KV_W)
    return pl.pallas_call(
        functools.partial(_ffn_in_kernel, layer=layer, tiles_a=tiles_a),
        grid=(n // tm,),
        in_specs=[
            *_stream_specs(D_MODEL, tiles_a, tm),
            _mod_spec(layer, tiles_a, tm, rows_per_cond),
            pl.BlockSpec((None, 1, D_MODEL), lambda i: (layer, 0, 0)),
            pl.BlockSpec((None, 1, D_MODEL), lambda i: (layer, 0, 0)),
            pl.BlockSpec(memory_space=pl.ANY),
            pl.BlockSpec(memory_space=pl.ANY),
            _resident((D_MODEL, IN_W), layer),
        ],
        out_specs=[pl.BlockSpec((tm, w), lambda i: (i, 0)) for w in widths],
        out_shape=[jax.ShapeDtypeStruct((n, w), F32) for w in widths],
        scratch_shapes=_ffn_weight_scratch(),
        compiler_params=_params("arbitrary"),
        name="ffn_in",
    )(x_a, x_b, mod, n1, nm, w_gu, w_down, w_in)


def _out_ffn_kernel(x_ref, mod_ref, n2_ref, foa_ref, fob_ref, hoa_ref, hob_ref, aoa_ref, aob_ref,
                    wout_ref, wgu_hbm, wd_hbm, oa_ref, ob_ref,
                    wgu_ref, wd_ref, stage_gu, stage_d, sems, *, layer, tiles_a):
    is_a = pl.program_id(0) < tiles_a

    def body(fetch):
        pick = lambda a_ref, b_ref: jnp.where(is_a, a_ref[...], b_ref[...]).astype(BF16)
        y = jnp.dot(pick(foa_ref, fob_ref), wout_ref[0:FOURIER_W, :], preferred_element_type=F32)
        y += jnp.dot(pick(hoa_ref, hob_ref), wout_ref[FOURIER_W:FOURIER_W + HGRN_W, :], preferred_element_type=F32)
        y += jnp.dot(pick(aoa_ref, aob_ref), wout_ref[FOURIER_W + HGRN_W:, :], preferred_element_type=F32)
        xm = x_ref[...] + mod_ref[5:6, :] * y
        h = _mod_norm(xm, n2_ref[...], mod_ref[6:7, :], mod_ref[7:8, :]).astype(BF16)
        out = xm + 0.5 * mod_ref[8:9, :] * _swiglu_chunks(h, wgu_ref, wd_ref, fetch)

        @pl.when(is_a)
        def _():
            oa_ref[...] = out

        @pl.when(jnp.logical_not(is_a))
        def _():
            ob_ref[...] = out

    _first_step_fetches(body, (wgu_hbm, wd_hbm, stage_gu, stage_d, sems, layer))


def _out_ffn_call(x, mod, n2, fo, ho, ao, w_out, w_gu, w_down, *, layer, rows_per_cond):
    tm = FFN_TM
    n_a, n_b = fo[0].shape[0], fo[1].shape[0]
    tiles_a = n_a // tm
    return pl.pallas_call(
        functools.partial(_out_ffn_kernel, layer=layer, tiles_a=tiles_a),
        grid=((n_a + n_b) // tm,),
        in_specs=[
            pl.BlockSpec((tm, D_MODEL), lambda i: (i, 0)),
            _mod_spec(layer, tiles_a, tm, rows_per_cond),
            pl.BlockSpec((None, 1, D_MODEL), lambda i: (layer, 0, 0)),
            *_stream_specs(FOURIER_W, tiles_a, tm),
            *_stream_specs(HGRN_W, tiles_a, tm),
            *_stream_specs(ATTN_W, tiles_a, tm),
            _resident((MIX_W, D_MODEL), layer),
            pl.BlockSpec(memory_space=pl.ANY),
            pl.BlockSpec(memory_space=pl.ANY),
        ],
        out_specs=list(_stream_specs(D_MODEL, tiles_a, tm)),
        out_shape=[jax.ShapeDtypeStruct((n_a, D_MODEL), F32), jax.ShapeDtypeStruct((n_b, D_MODEL), F32)],
        scratch_shapes=_ffn_weight_scratch(),
        compiler_params=_params("arbitrary"),
        name="out_ffn",
    )(x, mod, n2, fo[0], fo[1], ho[0], ho[1], ao[0], ao[1], w_out, w_gu, w_down)


def _dft_tables(seq_len):
    c = np.arange(HEAD_DIM)
    ang_c = 2.0 * np.pi * np.outer(c, c) / HEAD_DIM
    eye = np.eye(FOURIER_W // HEAD_DIM)
    chan = np.concatenate([np.kron(eye, np.cos(ang_c)), np.kron(eye, np.sin(ang_c))], axis=1) / np.sqrt(HEAD_DIM)
    n = np.arange(seq_len)
    ang_l = 2.0 * np.pi * (np.outer(n, n) % seq_len) / seq_len
    pos = np.concatenate([np.cos(ang_l), -np.sin(ang_l)], axis=1) / np.sqrt(seq_len)
    return jnp.asarray(chan, dtype=F32), jnp.asarray(pos, dtype=F32)


def _fourier_kernel(u_ref, chan_ref, pos_ref, o_ref, chan_bf, pos_bf, *, seq_len):
    @pl.when(pl.program_id(0) == 0)
    def _():
        chan_bf[...] = chan_ref[...].astype(BF16)
        pos_bf[...] = pos_ref[...].astype(BF16)

    t = jnp.dot(u_ref[...].astype(BF16), chan_bf[...], preferred_element_type=F32)
    pos = pos_bf[...]
    for r0 in range(0, u_ref.shape[0], seq_len):
        tb = t[r0:r0 + seq_len, :]
        stacked = jnp.concatenate([tb[:, :FOURIER_W], tb[:, FOURIER_W:]], axis=0).astype(BF16)
        o_ref[r0:r0 + seq_len, :] = jnp.dot(pos, stacked, preferred_element_type=F32)


def _fourier_call(zu, seq_len, span):
    row0, n = span
    chan, pos = _dft_tables(seq_len)
    rows = max(seq_len, SMALL_SEQ_ROWS)
    blk0 = row0 // rows
    return pl.pallas_call(
        functools.partial(_fourier_kernel, seq_len=seq_len),
        grid=(n // rows,),
        in_specs=[
            pl.BlockSpec((rows, FOURIER_W), lambda b: (b + blk0, 0)),
            pl.BlockSpec((FOURIER_W, 2 * FOURIER_W), lambda b: (0, 0), pipeline_mode=pl.Buffered(1)),
            pl.BlockSpec((seq_len, 2 * seq_len), lambda b: (0, 0), pipeline_mode=pl.Buffered(1)),
        ],
        out_specs=pl.BlockSpec((rows, FOURIER_W), lambda b: (b, 0)),
        out_shape=jax.ShapeDtypeStruct((n, FOURIER_W), F32),
        scratch_shapes=[pltpu.VMEM((FOURIER_W, 2 * FOURIER_W), BF16), pltpu.VMEM((seq_len, 2 * seq_len), BF16)],
        compiler_params=_params("arbitrary"),
        name="fourier",
    )(zu, chan, pos)


def _split3(x):
    hi = x.astype(BF16)
    r1 = x - hi.astype(F32)
    mid = r1.astype(BF16)
    lo = (r1 - mid.astype(F32)).astype(BF16)
    return hi, mid, lo


def _level_mid(b, s, pos):
    tile, lanes = b.shape
    if s >= SUBLANES:
        b3 = b.reshape(tile // s, s, lanes)
        return jnp.broadcast_to(b3[:, pos:pos + 1, :], b3.shape).reshape(tile, lanes)
    assert s == 4
    b3 = b.reshape(tile // SUBLANES, SUBLANES, lanes)
    sub = lax.broadcasted_iota(jnp.int32, (1, SUBLANES, 1), 1)
    return jnp.where(sub < 4, b3[:, pos:pos + 1, :], b3[:, 4 + pos:5 + pos, :]).reshape(tile, lanes)


def _hgrn_kernel(*refs, seq_len, layer, has_s0, emit_state):
    refs = list(refs)
    zh_ref, lb_ref, nw_ref, tri_ref = refs[:4]
    pos = 4
    s0_ref = sf_ref = None
    if has_s0:
        s0_ref = refs[pos]
        pos += 1
    if emit_state:
        pos += 1
    ho_ref = refs[pos]
    pos += 1
    if emit_state:
        sf_ref = refs[pos]
        pos += 1
    qin_ref, dec_ref, kvt_ref, stf_ref, stb_ref, oacc_ref = refs[pos:]

    tile = HGRN_TILE
    width = HGRN_W
    n_tiles = seq_len // tile
    c_q, c_v, c_z, c_g = 0, width, 2 * width, 4 * width

    raw = lb_ref[...]
    e = jnp.exp(raw - jnp.max(raw, axis=0, keepdims=True))
    soft = e / jnp.sum(e, axis=0, keepdims=True)
    cum = soft[0]
    for l in range(1, layer + 1):
        cum = cum + soft[l]
    lbd = cum - soft[0]
    lb_floor = jnp.maximum(lbd, LB_FLOOR)
    one_m_lb = 1.0 - lbd

    lane_bwd1 = lax.broadcasted_iota(jnp.int32, (1, HG_2), 1) >= width
    lane_bwd = lax.broadcasted_iota(jnp.int32, (tile, HG_2), 1) >= width
    lane_bwd_s = lax.broadcasted_iota(jnp.int32, (HEAD_DIM, HG_2), 1) >= width
    row = lax.broadcasted_iota(jnp.int32, (tile, 1), 0)
    pair_xor = (lax.broadcasted_iota(jnp.int32, (tile, HGRN_HEADS * tile), 0)
                ^ (lax.broadcasted_iota(jnp.int32, (tile, HGRN_HEADS * tile), 1) % tile))
    ev_mask = (lax.broadcasted_iota(jnp.int32, (HGRN_HEADS * tile, width), 0) // tile
               == lax.broadcasted_iota(jnp.int32, (HGRN_HEADS * tile, width), 1) // HEAD_DIM)
    bd_mask = (lax.broadcasted_iota(jnp.int32, (width, HG_2), 0) // HEAD_DIM
               == (lax.broadcasted_iota(jnp.int32, (width, HG_2), 1) // HEAD_DIM) % HGRN_HEADS)
    lane_head = (lax.broadcasted_iota(jnp.int32, (HEAD_DIM, HG_2), 1) // HEAD_DIM) % HGRN_HEADS
    ones_bd = _head_ones(width)

    def tile_body(t, carry):
        rows = pl.ds(pl.multiple_of(t * tile, tile), tile)
        q = zh_ref[rows, c_q:c_q + width] * (HEAD_DIM ** -0.5)
        v = zh_ref[rows, c_v:c_v + width]
        z = zh_ref[rows, c_z:c_z + HG_2]
        ez = jnp.exp(-jnp.abs(z))
        r = 1.0 / (1.0 + ez)
        tr = ez * r
        pos_z = z >= 0.0
        f = lb_floor + one_m_lb * jnp.where(pos_z, r, tr)
        logf = jnp.log(f)
        kk = one_m_lb * jnp.where(pos_z, tr, r)

        pieces = jnp.concatenate(_split3(logf), axis=1)
        acc = jnp.dot(tri_ref[...], pieces, preferred_element_type=F32)
        prefix = acc[:, :HG_2] + acc[:, HG_2:2 * HG_2] + acc[:, 2 * HG_2:]
        total = prefix[tile - 1:tile, :]
        b = jnp.where(lane_bwd, total - prefix + logf, prefix)
        kk_f, kk_b = kk[:, :width], kk[:, width:]

        a0 = jnp.dot((q * (kk_f + kk_b)).astype(BF16), ones_bd, preferred_element_type=F32)
        o = a0 * v

        b2 = b * LOG2_E
        bf2, bb2 = b2[:, :width], b2[:, width:]
        scores = None
        for s in HGRN_LEVELS:
            second = (row % s) >= s // 2
            k_role = jnp.where(second, kk_b, kk_f)
            if s == 2:
                q_op = q * jnp.where(second, f[:, :width], f[:, width:])
                k_op = k_role
            else:
                mid_f = _level_mid(bf2, s, s // 2 - 1)
                mid_b = _level_mid(bb2, s, s // 2)
                e_q = jnp.where(second, bf2, bb2) - jnp.where(second, mid_f, mid_b)
                e_k = jnp.where(second, mid_b, mid_f) - jnp.where(second, bb2, bf2)
                q_op = q * jnp.exp2(e_q)
                k_op = k_role * jnp.exp2(e_k)
            qt = q_op.astype(BF16)
            kt = k_op.T.astype(BF16)
            sc = jnp.concatenate(
                [jnp.dot(qt[:, h * HEAD_DIM:(h + 1) * HEAD_DIM], kt[h * HEAD_DIM:(h + 1) * HEAD_DIM, :],
                         preferred_element_type=F32) for h in range(HGRN_HEADS)], axis=1)
            scores = sc if scores is None else jnp.where(pair_xor < s, sc, scores)
        scores = jnp.where(pair_xor == 0, 0.0, scores)
        ev = jnp.where(ev_mask, jnp.concatenate([v] * HGRN_HEADS, axis=0), 0.0).astype(BF16)
        o = o + jnp.dot(scores.astype(BF16), ev, preferred_element_type=F32)
        oacc_ref[rows, :] = o

        q2 = jnp.concatenate([q, q], axis=1)
        qin_ref[rows, :] = (q2 * jnp.exp(b)).astype(BF16)
        kout = (kk * jnp.exp(total - b)).astype(BF16)
        dec_ref[t] = jnp.exp(total)
        full = jnp.dot(v.T.astype(BF16), kout, preferred_element_type=F32)
        kvt = full[0:HEAD_DIM, :]
        for hh in range(1, HGRN_HEADS):
            kvt = jnp.where(lane_head == hh, full[hh * HEAD_DIM:(hh + 1) * HEAD_DIM, :], kvt)
        kvt_ref[t] = kvt
        return carry

    lax.fori_loop(0, n_tiles, tile_body, 0)

    if has_s0:
        st0 = jnp.concatenate([s0_ref[d, h] for d in range(2) for h in range(HGRN_HEADS)], axis=0).T
    else:
        st0 = jnp.zeros((HEAD_DIM, HG_2), F32)

    def scan_step(i, st):
        tf, tb = i, n_tiles - 1 - i
        stf_ref[tf] = st
        stb_ref[tb] = st
        dec = jnp.where(lane_bwd1, dec_ref[tb], dec_ref[tf])
        kv = jnp.where(lane_bwd_s, kvt_ref[tb], kvt_ref[tf])
        return st * dec + kv

    st_fin = lax.fori_loop(0, n_tiles, scan_step, st0)
    if emit_state:
        st_t = st_fin.T
        for d in range(2):
            for h in range(HGRN_HEADS):
                r0 = (HGRN_HEADS * d + h) * HEAD_DIM
                sf_ref[d, h] = st_t[r0:r0 + HEAD_DIM, :]

    def out_tile(t):
        rows = pl.ds(pl.multiple_of(t * tile, tile), tile)
        st = jnp.where(lane_bwd_s, stb_ref[t], stf_ref[t])
        bd = jnp.where(bd_mask, jnp.concatenate([st] * HGRN_HEADS, axis=0), 0.0).T.astype(BF16)
        o = oacc_ref[rows, :] + jnp.dot(qin_ref[rows, :], bd, preferred_element_type=F32)
        hg = zh_ref[rows, c_g:c_g + width]
        ho_ref[rows, :] = _head_rmsnorm(o, nw_ref[...]) * (hg * jax.nn.sigmoid(hg))

    def out_body(i, carry):
        out_tile(2 * i)
        out_tile(2 * i + 1)
        return carry

    assert n_tiles % 2 == 0
    lax.fori_loop(0, n_tiles // 2, out_body, 0)


def _hgrn_tri():
    i = np.arange(HGRN_TILE)
    return jnp.asarray((i[None, :] <= i[:, None]).astype(np.float32), dtype=BF16)


def _hgrn_call(zh, lb2, norm_w, s0, final_states, *, seq_len, layer, span):
    row0, n = span
    blk0 = row0 // seq_len
    n_seq = n // seq_len
    has_s0 = s0 is not None
    emit_state = final_states is not None
    state_block = (2, HGRN_HEADS, HEAD_DIM, HEAD_DIM)
    n_tiles = seq_len // HGRN_TILE
    in_specs = [
        pl.BlockSpec((seq_len, ZH_W), lambda b: (b + blk0, 0)),
        pl.BlockSpec((DEPTH, 1, HG_2), lambda b: (0, 0, 0)),
        pl.BlockSpec((None, 1, HGRN_W), lambda b: (layer, 0, 0)),
        pl.BlockSpec((HGRN_TILE, HGRN_TILE), lambda b: (0, 0)),
    ]
    args = [zh, lb2, norm_w, _hgrn_tri()]
    if has_s0:
        in_specs.append(pl.BlockSpec((None, None) + state_block, lambda b: (b, layer, 0, 0, 0, 0)))
        args.append(s0)
    out_specs = [pl.BlockSpec((seq_len, HGRN_W), lambda b: (b, 0))]
    out_shape = [jax.ShapeDtypeStruct((n, HGRN_W), F32)]
    aliases = {}
    if emit_state:
        aliases[len(args)] = 1
        in_specs.append(pl.BlockSpec(memory_space=pl.ANY))
        args.append(final_states)
        out_specs.append(pl.BlockSpec((None, None) + state_block, lambda b: (b, layer, 0, 0, 0, 0)))
        out_shape.append(jax.ShapeDtypeStruct(final_states.shape, F32))
    return pl.pallas_call(
        functools.partial(_hgrn_kernel, seq_len=seq_len, layer=layer, has_s0=has_s0, emit_state=emit_state),
        grid=(n_seq,),
        in_specs=in_specs,
        out_specs=out_specs,
        out_shape=out_shape,
        input_output_aliases=aliases,
        scratch_shapes=[
            pltpu.VMEM((seq_len, HG_2), BF16),
            pltpu.VMEM((n_tiles, 1, HG_2), F32),
            pltpu.VMEM((n_tiles, HEAD_DIM, HG_2), F32),
            pltpu.VMEM((n_tiles, HEAD_DIM, HG_2), F32),
            pltpu.VMEM((n_tiles, HEAD_DIM, HG_2), F32),
            pltpu.VMEM((seq_len, HGRN_W), F32),
        ],
        compiler_params=_params("arbitrary"),
        name="hgrn",
    )(*args)


def _sink_row(sink_ref, kv_head, rows):
    grp = lax.broadcasted_iota(jnp.int32, (1, ATTN_GROUP * rows), 1) // rows
    base = kv_head * ATTN_GROUP
    out = jnp.broadcast_to(sink_ref[:, base:base + 1], (1, ATTN_GROUP * rows))
    for g in range(1, ATTN_GROUP):
        out = jnp.where(grp == g, sink_ref[:, base + g:base + g + 1], out)
    return out


def _group_queries_t(qt, kv_head):
    base = kv_head * ATTN_GROUP
    return jnp.concatenate([qt[(base + g) * HEAD_DIM:(base + g + 1) * HEAD_DIM, :] for g in range(ATTN_GROUP)], axis=1)


def _ctx_attn_kernel(zq_ref, zkv_ref, qn_ref, kn_ref, sink_ref, kc_all_ref, vc_all_ref, ao_ref, kc_ref, vc_ref,
                     *, seq_len):
    del kc_all_ref, vc_all_ref
    rows = seq_len
    q_all = _head_rmsnorm(zq_ref[...], qn_ref[...]) * (HEAD_DIM ** -0.5)
    k_all = _head_rmsnorm(zkv_ref[:, :KV_W], kn_ref[...])
    v_all = zkv_ref[:, KV_W:]
    kc_ref[...] = k_all.reshape(kc_ref.shape)
    vc_ref[...] = v_all.reshape(vc_ref.shape)
    for r0 in range(0, zq_ref.shape[0], seq_len):
        qt = q_all[r0:r0 + rows, :].T.astype(BF16)
        kb = k_all[r0:r0 + rows, :].astype(BF16)
        vt = v_all[r0:r0 + rows, :].T.astype(BF16)
        out_t = []
        for kh in range(ATTN_KV_HEADS):
            hs = slice(kh * HEAD_DIM, (kh + 1) * HEAD_DIM)
            s = jnp.dot(kb[:, hs], _group_queries_t(qt, kh), preferred_element_type=F32)
            sink = _sink_row(sink_ref, kh, rows)
            m = jnp.maximum(jnp.max(s, axis=0, keepdims=True), sink)
            p = jnp.exp(s - m)
            denom = jnp.sum(p, axis=0, keepdims=True) + jnp.exp(sink - m)
            o_t = jnp.dot(vt[hs, :], p.astype(BF16), preferred_element_type=F32) * (1.0 / denom)
            out_t.extend(o_t[:, g * rows:(g + 1) * rows] for g in range(ATTN_GROUP))
        ao_ref[r0:r0 + rows, :] = jnp.concatenate(out_t, axis=0).T


def _ctx_attn_call(zq, zkv, q_norm, k_norm, sink, k_cache, v_cache, *, seq_len, layer, span):
    row0, n = span
    rows = max(seq_len, SMALL_SEQ_ROWS)
    blk0 = row0 // rows
    seqs = rows // seq_len
    cache_spec = pl.BlockSpec((seqs, None, seq_len, KV_W), lambda b: (b, layer, 0, 0))
    return pl.pallas_call(
        functools.partial(_ctx_attn_kernel, seq_len=seq_len),
        grid=(n // rows,),
        in_specs=[
            pl.BlockSpec((rows, ATTN_W), lambda b: (b + blk0, 0)),
            pl.BlockSpec((rows, 2 * KV_W), lambda b: (b + blk0, 0)),
            pl.BlockSpec((None, 1, ATTN_W), lambda b: (layer, 0, 0)),
            pl.BlockSpec((None, 1, KV_W), lambda b: (layer, 0, 0)),
            pl.BlockSpec((None, 1, ATTN_HEADS), lambda b: (layer, 0, 0)),
            pl.BlockSpec(memory_space=pl.ANY),
            pl.BlockSpec(memory_space=pl.ANY),
        ],
        out_specs=[pl.BlockSpec((rows, ATTN_W), lambda b: (b, 0)), cache_spec, cache_spec],
        out_shape=[
            jax.ShapeDtypeStruct((n, ATTN_W), F32),
            jax.ShapeDtypeStruct(k_cache.shape, F32),
            jax.ShapeDtypeStruct(v_cache.shape, F32),
        ],
        input_output_aliases={5: 1, 6: 2},
        compiler_params=_params("arbitrary"),
        name="ctx_attn",
    )(zq, zkv, q_norm, k_norm, sink, k_cache, v_cache)


def _rope_tables(seq_len, n_heads):
    t = np.arange(seq_len)
    row, col = t // GRID_W, t % GRID_W
    half = HEAD_DIM // 2
    inv = (ROPE_BASE ** (-np.arange(0, half, 2, dtype=np.float32) / half)).astype(np.float32)

    def one(pos):
        ang = pos.astype(np.float32)[:, None] * inv[None]
        c, s = np.cos(ang), np.sin(ang)
        return np.concatenate([c, c], axis=1), np.concatenate([-s, s], axis=1)

    cr, sr = one(row)
    cc, sc = one(col)
    cos = np.tile(np.concatenate([cr, cc], axis=1), (1, n_heads)).astype(np.float32)
    sin = np.tile(np.concatenate([sr, sc], axis=1), (1, n_heads)).astype(np.float32)
    return jnp.asarray(cos), jnp.asarray(sin)


def _rope(x, cos, sin_signed):
    width = x.shape[-1]
    lane = lax.broadcasted_iota(jnp.int32, x.shape, 1)
    quarter = HEAD_DIM // 4
    up = pltpu.roll(x, width - quarter, axis=1)
    down = pltpu.roll(x, quarter, axis=1)
    partner = jnp.where((lane % (2 * quarter)) < quarter, up, down)
    return x * cos + partner * sin_signed


def _lat_attn_kernel(zq_ref, zkv_ref, kc_ref, vc_ref, qn_ref, kn_ref, sink_ref,
                     cosq_ref, sinq_ref, cosk_ref, sink_tab_ref, band_ref, ao_ref, kb_ref, kcb_ref, vct_ref,
                     *, seq_len):
    rows = ATTN_BLOCK

    k_all = _head_rmsnorm(zkv_ref[:, 0:KV_W], kn_ref[...])
    kb_ref[...] = _rope(k_all, cosk_ref[...], sink_tab_ref[...]).astype(BF16)
    kcb_ref[...] = kc_ref[...].astype(BF16)
    vct_ref[...] = vc_ref[...].T.astype(BF16)

    def query_block(blk, carry):
        qrows = pl.ds(pl.multiple_of(blk * ATTN_BLOCK, ATTN_BLOCK), ATTN_BLOCK)
        q = _head_rmsnorm(zq_ref[qrows, :], qn_ref[...])
        q = _rope(q, cosq_ref[qrows, :], sinq_ref[qrows, :]) * (HEAD_DIM ** -0.5)
        start = jnp.clip((blk - 1) * ATTN_BLOCK, 0, seq_len - WIN_KEYS)
        start = pl.multiple_of(start, ATTN_BLOCK)
        vwin = zkv_ref[pl.ds(start, WIN_KEYS), KV_W:2 * KV_W]

        band_bias = band_ref[blk - start // ATTN_BLOCK]

        qt = q.T.astype(BF16)
        kwb = kb_ref[pl.ds(start, WIN_KEYS), :]
        kcb = kcb_ref[...]
        vwt = vwin.T.astype(BF16)
        vct = vct_ref[...]
        out_t = []
        for kh in range(ATTN_KV_HEADS):
            hs = slice(kh * HEAD_DIM, (kh + 1) * HEAD_DIM)
            qg = _group_queries_t(qt, kh)
            s_loc = jnp.dot(kwb[:, hs], qg, preferred_element_type=F32) + band_bias
            s_ctx = jnp.dot(kcb[:, hs], qg, preferred_element_type=F32)
            sink = _sink_row(sink_ref, kh, rows)
            m = jnp.maximum(jnp.maximum(jnp.max(s_loc, axis=0, keepdims=True),
                                        jnp.max(s_ctx, axis=0, keepdims=True)), sink)
            p_loc = jnp.exp(s_loc - m)
            p_ctx = jnp.exp(s_ctx - m)
            denom = (jnp.sum(p_loc, axis=0, keepdims=True) + jnp.sum(p_ctx, axis=0, keepdims=True)
                     + jnp.exp(sink - m))
            o_t = (jnp.dot(vwt[hs, :], p_loc.astype(BF16), preferred_element_type=F32)
                   + jnp.dot(vct[hs, :], p_ctx.astype(BF16), preferred_element_type=F32)) * (1.0 / denom)
            out_t.extend(o_t[:, g * rows:(g + 1) * rows] for g in range(ATTN_GROUP))
        ao_ref[qrows, :] = jnp.concatenate(out_t, axis=0).T
        return carry

    lax.fori_loop(0, seq_len // ATTN_BLOCK, query_block, 0)


def _lat_attn_call(zq, zkv, cache_k, cache_v, q_norm, k_norm, sink, *, seq_len, layer, span):
    row0, n = span
    seq0 = row0 // seq_len
    past = cache_k.shape[2]
    cosq, sinq = _rope_tables(seq_len, ATTN_HEADS)
    cosk, sink_tab = _rope_tables(seq_len, ATTN_KV_HEADS)
    kr = np.arange(WIN_KEYS)[:, None]
    qi = np.arange(ATTN_GROUP * ATTN_BLOCK)[None, :] % ATTN_BLOCK
    band = np.stack([np.where(np.abs(qi + off * ATTN_BLOCK - kr) <= WINDOW, 0.0, NEG_BIG) for off in range(3)])
    band = jnp.asarray(band, dtype=F32)
    return pl.pallas_call(
        functools.partial(_lat_attn_kernel, seq_len=seq_len),
        grid=(n // seq_len,),
        in_specs=[
            pl.BlockSpec((seq_len, ATTN_W), lambda b: (seq0 + b, 0)),
            pl.BlockSpec((seq_len, 2 * KV_W), lambda b: (seq0 + b, 0)),
            pl.BlockSpec((None, None, past, KV_W), lambda b: (b, layer, 0, 0)),
            pl.BlockSpec((None, None, past, KV_W), lambda b: (b, layer, 0, 0)),
            pl.BlockSpec((None, 1, ATTN_W), lambda b: (layer, 0, 0)),
            pl.BlockSpec((None, 1, KV_W), lambda b: (layer, 0, 0)),
            pl.BlockSpec((None, 1, ATTN_HEADS), lambda b: (layer, 0, 0)),
            pl.BlockSpec((seq_len, ATTN_W), lambda b: (0, 0), pipeline_mode=pl.Buffered(1)),
            pl.BlockSpec((seq_len, ATTN_W), lambda b: (0, 0), pipeline_mode=pl.Buffered(1)),
            pl.BlockSpec((seq_len, KV_W), lambda b: (0, 0), pipeline_mode=pl.Buffered(1)),
            pl.BlockSpec((seq_len, KV_W), lambda b: (0, 0), pipeline_mode=pl.Buffered(1)),
            pl.BlockSpec((3, WIN_KEYS, ATTN_GROUP * ATTN_BLOCK), lambda b: (0, 0, 0), pipeline_mode=pl.Buffered(1)),
        ],
        out_specs=pl.BlockSpec((seq_len, ATTN_W), lambda b: (b, 0)),
        out_shape=jax.ShapeDtypeStruct((n, ATTN_W), F32),
        scratch_shapes=[
            pltpu.VMEM((seq_len, KV_W), BF16),
            pltpu.VMEM((past, KV_W), BF16),
            pltpu.VMEM((KV_W, past), BF16),
        ],
        compiler_params=_params("arbitrary"),
        name="lat_attn",
    )(zq, zkv, cache_k, cache_v, q_norm, k_norm, sink, cosq, sinq, cosk, sink_tab, band)


def kernel(x_prompt, x_sample, c, cache_attn_k, cache_attn_v, state_hgrn, c_ctx, w_ada, b_ada, norm_ffn1, norm_mix, norm_ffn2, ffn1_w_gate_up, ffn1_w_down, ffn2_w_gate_up, ffn2_w_down, w_in, w_out, hgrn_lower_bounds, hgrn_norm, q_norm, k_norm, attn_sink):
    batch, seq, _ = x_prompt.shape
    dec_batch, dec_seq, _ = x_sample.shape
    past = cache_attn_k.shape[2]
    assert dec_batch + 1 <= ADA_ROWS

    cond = jnp.concatenate([c_ctx[None], c, jnp.zeros((ADA_ROWS - 1 - dec_batch, D_MODEL), F32)], axis=0)
    mod = _ada_call(cond, w_ada, b_ada).reshape(DEPTH, ADA_ROWS, N_MOD, D_MODEL)

    bf = lambda w: w.astype(BF16)
    w_gu1, w_d1, w_gu2, w_d2 = ffn1_w_gate_up, ffn1_w_down, ffn2_w_gate_up, ffn2_w_down
    w_in_b, w_out_b = bf(w_in), bf(w_out)
    row3 = lambda w: w.reshape(DEPTH, 1, w.shape[-1])
    n1, nm, n2 = row3(norm_ffn1), row3(norm_mix), row3(norm_ffn2)
    hn = row3(hgrn_norm)
    qn = row3(jnp.tile(q_norm, (1, ATTN_HEADS)))
    kn = row3(jnp.tile(k_norm, (1, ATTN_KV_HEADS)))
    sink = row3(attn_sink)
    cache_k = cache_attn_k.reshape(dec_batch, DEPTH, past, KV_W)
    cache_v = cache_attn_v.reshape(dec_batch, DEPTH, past, KV_W)
    lb2 = hgrn_lower_bounds.reshape(DEPTH, 1, HG_2)

    n_ctx, n_lat = batch * seq, dec_batch * dec_seq
    span_c, span_l = (0, n_ctx), (n_ctx, n_lat)
    x_c, x_l = x_prompt.reshape(n_ctx, D_MODEL), x_sample.reshape(n_lat, D_MODEL)
    new_k = jnp.zeros((batch, DEPTH, seq, KV_W), F32)
    new_v = jnp.zeros((batch, DEPTH, seq, KV_W), F32)
    new_s = jnp.zeros((batch, DEPTH, 2, HGRN_HEADS, HEAD_DIM, HEAD_DIM), F32)
    for layer in range(DEPTH):
        x1, zu, zh, zq, zkv = _ffn_in_call(x_c, x_l, mod, n1, nm, w_gu1, w_d1, w_in_b, layer=layer, rows_per_cond=dec_seq)
        fo_c = _fourier_call(zu, seq, span_c)
        fo_l = _fourier_call(zu, dec_seq, span_l)
        ho_c, new_s = _hgrn_call(zh, lb2, hn, None, new_s, seq_len=seq, layer=layer, span=span_c)
        (ho_l,) = _hgrn_call(zh, lb2, hn, state_hgrn, None, seq_len=dec_seq, layer=layer, span=span_l)
        ao_c, new_k, new_v = _ctx_attn_call(zq, zkv, qn, kn, sink, new_k, new_v, seq_len=seq, layer=layer, span=span_c)
        ao_l = _lat_attn_call(zq, zkv, cache_k, cache_v, qn, kn, sink, seq_len=dec_seq, layer=layer, span=span_l)
        x_c, x_l = _out_ffn_call(x1, mod, n2, (fo_c, fo_l), (ho_c, ho_l), (ao_c, ao_l), w_out_b, w_gu2, w_d2,
                                 layer=layer, rows_per_cond=dec_seq)

    y_p = x_c.reshape(batch, seq, D_MODEL)
    y_s = x_l.reshape(dec_batch, dec_seq, D_MODEL)
    cache_shape = (batch, DEPTH, seq, ATTN_KV_HEADS, HEAD_DIM)
    return (y_p, y_s, new_k.reshape(cache_shape), new_v.reshape(cache_shape), new_s)
```

```python
import functools

import numpy as np
import jax
import jax.numpy as jnp
from jax import lax
from jax.experimental import pallas as pl
from jax.experimental.pallas import tpu as pltpu

F32 = jnp.float32
BF16 = jnp.bfloat16

D_MODEL = 1024
DEPTH = 2
HEAD_DIM = 64
FOURIER_W = 256
HGRN_W = 256
HGRN_HEADS = 4
ATTN_W = 512
ATTN_HEADS = 8
ATTN_KV_HEADS = 2
ATTN_GROUP = 4
KV_W = 128
MIX_W = 1024
WINDOW = 128
ATTN_BLOCK = 128
D_FF = 2816
GRID_W = 64
ROPE_BASE = 10000.0
EPS = 1e-6
N_MOD = 9
NEG_BIG = -1e30
LB_FLOOR = 1e-30
LOG2_E = 1.4426950408889634

HG_2 = 2 * HGRN_W
ZH_W = 5 * HGRN_W
IN_W = FOURIER_W + ZH_W + ATTN_W + 2 * KV_W

SUBLANES = 8
VMEM_LIMIT_BYTES = 63 * 1024 * 1024

ADA_ROWS = 8
ADA_TN = 1536
FFN_TM = 512
FFN_FC = 256
SMALL_SEQ_ROWS = 1024
HGRN_TILE = 128
HGRN_LEVELS = (128, 64, 32, 16, 8, 4, 2)
WIN_KEYS = 3 * ATTN_BLOCK


def _params(*semantics):
    return pltpu.CompilerParams(dimension_semantics=semantics, vmem_limit_bytes=VMEM_LIMIT_BYTES)


def _mod_norm(x, norm_w, shift, scale):
    ms = jnp.mean(x * x, axis=-1, keepdims=True)
    y = x * lax.rsqrt(ms + EPS) * norm_w
    return y * (1.0 + scale) + shift


def _head_ones(width):
    r = lax.broadcasted_iota(jnp.int32, (width, width), 0) // HEAD_DIM
    c = lax.broadcasted_iota(jnp.int32, (width, width), 1) // HEAD_DIM
    return jnp.where(r == c, 1.0, 0.0).astype(BF16)


def _head_rmsnorm(x, w):
    sq = (x * x).astype(BF16)
    ms = jnp.dot(sq, _head_ones(x.shape[-1]), preferred_element_type=F32) * (1.0 / HEAD_DIM)
    return x * lax.rsqrt(ms + EPS) * w


def _ada_kernel(cond_ref, w_ref, b_ref, o_ref):
    cnd = cond_ref[...]
    act = (cnd * jax.nn.sigmoid(cnd)).astype(BF16)
    o_ref[...] = jnp.dot(act, w_ref[...].astype(BF16), preferred_element_type=F32) + b_ref[...]


def _ada_call(cond, w_ada, b_ada):
    n_out = N_MOD * D_MODEL
    return pl.pallas_call(
        _ada_kernel,
        grid=(DEPTH, n_out // ADA_TN),
        in_specs=[
            pl.BlockSpec((ADA_ROWS, D_MODEL), lambda l, j: (0, 0)),
            pl.BlockSpec((None, D_MODEL, ADA_TN), lambda l, j: (l, 0, j)),
            pl.BlockSpec((None, 1, ADA_TN), lambda l, j: (l, 0, j)),
        ],
        out_specs=pl.BlockSpec((None, ADA_ROWS, ADA_TN), lambda l, j: (l, 0, j)),
        out_shape=jax.ShapeDtypeStruct((DEPTH, ADA_ROWS, n_out), F32),
        compiler_params=_params("arbitrary", "arbitrary"),
        name="adaln",
    )(cond, w_ada, b_ada.reshape(DEPTH, 1, n_out))


def _weight_chunk_copies(wgu_hbm, wd_hbm, stage_gu, stage_d, sems, layer, c, slot):
    lo, hi = c * FFN_FC, (c + 1) * FFN_FC
    return (
        pltpu.make_async_copy(wgu_hbm.at[layer, :, lo:hi], stage_gu.at[slot, :, 0:FFN_FC], sems.at[slot, 0]),
        pltpu.make_async_copy(wgu_hbm.at[layer, :, D_FF + lo:D_FF + hi], stage_gu.at[slot, :, FFN_FC:2 * FFN_FC],
                              sems.at[slot, 1]),
        pltpu.make_async_copy(wd_hbm.at[layer, lo:hi, :], stage_d.at[slot], sems.at[slot, 2]),
    )


def _swiglu_chunks(h, wgu_ref, wd_ref, fetch=None):
    n_chunks = D_FF // FFN_FC
    if fetch is not None:
        stage_gu, stage_d = fetch[2], fetch[3]
        for cp in _weight_chunk_copies(*fetch, 0, 0):
            cp.start()
    acc = None
    for c in range(n_chunks):
        if fetch is not None:
            slot = c % 2
            if c + 1 < n_chunks:
                for cp in _weight_chunk_copies(*fetch, c + 1, 1 - slot):
                    cp.start()
            for cp in _weight_chunk_copies(*fetch, c, slot):
                cp.wait()
            wgu_ref[:, c * FFN_FC:(c + 1) * FFN_FC] = stage_gu[slot, :, 0:FFN_FC].astype(BF16)
            wgu_ref[:, D_FF + c * FFN_FC:D_FF + (c + 1) * FFN_FC] = stage_gu[slot, :, FFN_FC:2 * FFN_FC].astype(BF16)
            wd_ref[c * FFN_FC:(c + 1) * FFN_FC, :] = stage_d[slot].astype(BF16)
        g = jnp.dot(h, wgu_ref[:, c * FFN_FC:(c + 1) * FFN_FC], preferred_element_type=F32)
        u = jnp.dot(h, wgu_ref[:, D_FF + c * FFN_FC:D_FF + (c + 1) * FFN_FC], preferred_element_type=F32)
        act = (g * jax.nn.sigmoid(g) * u).astype(BF16)
        part = jnp.dot(act, wd_ref[c * FFN_FC:(c + 1) * FFN_FC, :], preferred_element_type=F32)
        acc = part if acc is None else acc + part
    return acc


def _mod_spec(layer, tiles_a, tm, rows_per_cond):
    def index(i):
        row = jnp.where(i < tiles_a, 0, 1 + ((i - tiles_a) * tm) // rows_per_cond)
        return (layer, row, 0, 0)
    return pl.BlockSpec((None, None, N_MOD, D_MODEL), index)


def _stream_specs(width, tiles_a, tm):
    return (pl.BlockSpec((tm, width), lambda i: (jnp.minimum(i, tiles_a - 1), 0)),
            pl.BlockSpec((tm, width), lambda i: (jnp.maximum(i - tiles_a, 0), 0)))


def _resident(shape, layer):
    return pl.BlockSpec((None,) + shape, lambda i: (layer,) + (0,) * len(shape), pipeline_mode=pl.Buffered(1))


def _first_step_fetches(body, fetch):
    first = pl.program_id(0) == 0
    pl.when(first)(functools.partial(body, fetch))
    pl.when(jnp.logical_not(first))(functools.partial(body, None))


def _ffn_weight_scratch():
    return [
        pltpu.VMEM((D_MODEL, 2 * D_FF), BF16),
        pltpu.VMEM((D_FF, D_MODEL), BF16),
        pltpu.VMEM((2, D_MODEL, 2 * FFN_FC), F32),
        pltpu.VMEM((2, FFN_FC, D_MODEL), F32),
        pltpu.SemaphoreType.DMA((2, 3)),
    ]


def _ffn_in_kernel(xa_ref, xb_ref, mod_ref, n1_ref, nm_ref, wgu_hbm, wd_hbm, win_ref,
                   x1_ref, zu_ref, zh_ref, zq_ref, zkv_ref, wgu_ref, wd_ref, stage_gu, stage_d, sems, *, layer, tiles_a):
    def body(fetch):
        x = jnp.where(pl.program_id(0) < tiles_a, xa_ref[...], xb_ref[...])
        h = _mod_norm(x, n1_ref[...], mod_ref[0:1, :], mod_ref[1:2, :]).astype(BF16)
        x1 = x + 0.5 * mod_ref[2:3, :] * _swiglu_chunks(h, wgu_ref, wd_ref, fetch)
        x1_ref[...] = x1
        h2 = _mod_norm(x1, nm_ref[...], mod_ref[3:4, :], mod_ref[4:5, :]).astype(BF16)
        z = jnp.dot(h2, win_ref[...], preferred_element_type=F32)
        zu_ref[...] = z[:, :FOURIER_W]
        zh_ref[...] = z[:, FOURIER_W:FOURIER_W + ZH_W]
        zq_ref[...] = z[:, FOURIER_W + ZH_W:FOURIER_W + ZH_W + ATTN_W]
        zkv_ref[...] = z[:, FOURIER_W + ZH_W + ATTN_W:]

    _first_step_fetches(body, (wgu_hbm, wd_hbm, stage_gu, stage_d, sems, layer))


def _ffn_in_call(x_a, x_b, mod, n1, nm, w_gu, w_down, w_in, *, layer, rows_per_cond):
    tm = FFN_TM
    tiles_a = x_a.shape[0] // tm
    n = x_a.shape[0] + x_b.shape[0]
    widths = (D_MODEL, FOURIER_W, ZH_W, ATTN_W, 2 * KV_W)
    return pl.pallas_call(
        functools.partial(_ffn_in_kernel, layer=layer, tiles_a=tiles_a),
        grid=(n // tm,),
        in_specs=[
            *_stream_specs(D_MODEL, tiles_a, tm),
            _mod_spec(layer, tiles_a, tm, rows_per_cond),
            pl.BlockSpec((None, 1, D_MODEL), lambda i: (layer, 0, 0)),
            pl.BlockSpec((None, 1, D_MODEL), lambda i: (layer, 0, 0)),
            pl.BlockSpec(memory_space=pl.ANY),
            pl.BlockSpec(memory_space=pl.ANY),
            _resident((D_MODEL, IN_W), layer),
        ],
        out_specs=[pl.BlockSpec((tm, w), lambda i: (i, 0)) for w in widths],
        out_shape=[jax.ShapeDtypeStruct((n, w), F32) for w in widths],
        scratch_shapes=_ffn_weight_scratch(),
        compiler_params=_params("arbitrary"),
        name="ffn_in",
    )(x_a, x_b, mod, n1, nm, w_gu, w_down, w_in)


def _out_ffn_kernel(x_ref, mod_ref, n2_ref, foa_ref, fob_ref, hoa_ref, hob_ref, aoa_ref, aob_ref,
                    wout_ref, wgu_hbm, wd_hbm, oa_ref, ob_ref,
                    wgu_ref, wd_ref, stage_gu, stage_d, sems, *, layer, tiles_a):
    is_a = pl.program_id(0) < tiles_a

    def body(fetch):
        pick = lambda a_ref, b_ref: jnp.where(is_a, a_ref[...], b_ref[...]).astype(BF16)
        y = jnp.dot(pick(foa_ref, fob_ref), wout_ref[0:FOURIER_W, :], preferred_element_type=F32)
        y += jnp.dot(pick(hoa_ref, hob_ref), wout_ref[FOURIER_W:FOURIER_W + HGRN_W, :], preferred_element_type=F32)
        y += jnp.dot(pick(aoa_ref, aob_ref), wout_ref[FOURIER_W + HGRN_W:, :], preferred_element_type=F32)
        xm = x_ref[...] + mod_ref[5:6, :] * y
        h = _mod_norm(xm, n2_ref[...], mod_ref[6:7, :], mod_ref[7:8, :]).astype(BF16)
        out = xm + 0.5 * mod_ref[8:9, :] * _swiglu_chunks(h, wgu_ref, wd_ref, fetch)

        @pl.when(is_a)
        def _():
            oa_ref[...] = out

        @pl.when(jnp.logical_not(is_a))
        def _():
            ob_ref[...] = out

    _first_step_fetches(body, (wgu_hbm, wd_hbm, stage_gu, stage_d, sems, layer))


def _out_ffn_call(x, mod, n2, fo, ho, ao, w_out, w_gu, w_down, *, layer, rows_per_cond):
    tm = FFN_TM
    n_a, n_b = fo[0].shape[0], fo[1].shape[0]
    tiles_a = n_a // tm
    return pl.pallas_call(
        functools.partial(_out_ffn_kernel, layer=layer, tiles_a=tiles_a),
        grid=((n_a + n_b) // tm,),
        in_specs=[
            pl.BlockSpec((tm, D_MODEL), lambda i: (i, 0)),
            _mod_spec(layer, tiles_a, tm, rows_per_cond),
            pl.BlockSpec((None, 1, D_MODEL), lambda i: (layer, 0, 0)),
            *_stream_specs(FOURIER_W, tiles_a, tm),
            *_stream_specs(HGRN_W, tiles_a, tm),
            *_stream_specs(ATTN_W, tiles_a, tm),
            _resident((MIX_W, D_MODEL), layer),
            pl.BlockSpec(memory_space=pl.ANY),
            pl.BlockSpec(memory_space=pl.ANY),
        ],
        out_specs=list(_stream_specs(D_MODEL, tiles_a, tm)),
        out_shape=[jax.ShapeDtypeStruct((n_a, D_MODEL), F32), jax.ShapeDtypeStruct((n_b, D_MODEL), F32)],
        scratch_shapes=_ffn_weight_scratch(),
        compiler_params=_params("arbitrary"),
        name="out_ffn",
    )(x, mod, n2, fo[0], fo[1], ho[0], ho[1], ao[0], ao[1], w_out, w_gu, w_down)


def _dft_tables(seq_len):
    c = np.arange(HEAD_DIM)
    ang_c = 2.0 * np.pi * np.outer(c, c) / HEAD_DIM
    eye = np.eye(FOURIER_W // HEAD_DIM)
    chan = np.concatenate([np.kron(eye, np.cos(ang_c)), np.kron(eye, np.sin(ang_c))], axis=1) / np.sqrt(HEAD_DIM)
    n = np.arange(seq_len)
    ang_l = 2.0 * np.pi * (np.outer(n, n) % seq_len) / seq_len
    pos = np.concatenate([np.cos(ang_l), -np.sin(ang_l)], axis=1) / np.sqrt(seq_len)
    return jnp.asarray(chan, dtype=F32), jnp.asarray(pos, dtype=F32)


def _fourier_rows(u_ref, chan_ref, pos_ref, o_ref, chan_bf, pos_bf, seq_len):
    @pl.when(pl.program_id(0) == 0)
    def _():
        chan_bf[...] = chan_ref[...].astype(BF16)
        pos_bf[...] = pos_ref[...].astype(BF16)

    t = jnp.dot(u_ref[...].astype(BF16), chan_bf[...], preferred_element_type=F32)
    pos = pos_bf[...]
    for r0 in range(0, u_ref.shape[0], seq_len):
        tb = t[r0:r0 + seq_len, :]
        stacked = jnp.concatenate([tb[:, :FOURIER_W], tb[:, FOURIER_W:]], axis=0).astype(BF16)
        o_ref[r0:r0 + seq_len, :] = jnp.dot(pos, stacked, preferred_element_type=F32)


def _fourier_table_specs(seq_len, index):
    specs = [pl.BlockSpec((FOURIER_W, 2 * FOURIER_W), index, pipeline_mode=pl.Buffered(1)),
             pl.BlockSpec((seq_len, 2 * seq_len), index, pipeline_mode=pl.Buffered(1))]
    scratch = [pltpu.VMEM((FOURIER_W, 2 * FOURIER_W), BF16), pltpu.VMEM((seq_len, 2 * seq_len), BF16)]
    return specs, scratch


def _split3(x):
    hi = x.astype(BF16)
    r1 = x - hi.astype(F32)
    mid = r1.astype(BF16)
    lo = (r1 - mid.astype(F32)).astype(BF16)
    return hi, mid, lo


def _level_mid(b, s, pos):
    tile, lanes = b.shape
    if s >= SUBLANES:
        b3 = b.reshape(tile // s, s, lanes)
        return jnp.broadcast_to(b3[:, pos:pos + 1, :], b3.shape).reshape(tile, lanes)
    assert s == 4
    b3 = b.reshape(tile // SUBLANES, SUBLANES, lanes)
    sub = lax.broadcasted_iota(jnp.int32, (1, SUBLANES, 1), 1)
    return jnp.where(sub < 4, b3[:, pos:pos + 1, :], b3[:, 4 + pos:5 + pos, :]).reshape(tile, lanes)


def _hgrn_kernel(*refs, seq_len, layer, has_s0, emit_state):
    refs = list(refs)
    zh_ref, lb_ref, nw_ref, tri_ref = refs[:4]
    pos = 4
    s0_ref = sf_ref = None
    if has_s0:
        s0_ref = refs[pos]
        pos += 1
    if emit_state:
        pos += 1
    ho_ref = refs[pos]
    pos += 1
    if emit_state:
        sf_ref = refs[pos]
        pos += 1
    qin_ref, dec_ref, kvt_ref, stf_ref, stb_ref, oacc_ref = refs[pos:]

    tile = HGRN_TILE
    width = HGRN_W
    n_tiles = seq_len // tile
    c_q, c_v, c_z, c_g = 0, width, 2 * width, 4 * width

    raw = lb_ref[...]
    e = jnp.exp(raw - jnp.max(raw, axis=0, keepdims=True))
    soft = e / jnp.sum(e, axis=0, keepdims=True)
    cum = soft[0]
    for l in range(1, layer + 1):
        cum = cum + soft[l]
    lbd = cum - soft[0]
    lb_floor = jnp.maximum(lbd, LB_FLOOR)
    one_m_lb = 1.0 - lbd

    lane_bwd1 = lax.broadcasted_iota(jnp.int32, (1, HG_2), 1) >= width
    lane_bwd = lax.broadcasted_iota(jnp.int32, (tile, HG_2), 1) >= width
    lane_bwd_s = lax.broadcasted_iota(jnp.int32, (HEAD_DIM, HG_2), 1) >= width
    row = lax.broadcasted_iota(jnp.int32, (tile, 1), 0)
    pair_xor = (lax.broadcasted_iota(jnp.int32, (tile, HGRN_HEADS * tile), 0)
                ^ (lax.broadcasted_iota(jnp.int32, (tile, HGRN_HEADS * tile), 1) % tile))
    ev_mask = (lax.broadcasted_iota(jnp.int32, (HGRN_HEADS * tile, width), 0) // tile
               == lax.broadcasted_iota(jnp.int32, (HGRN_HEADS * tile, width), 1) // HEAD_DIM)
    bd_mask = (lax.broadcasted_iota(jnp.int32, (width, HG_2), 0) // HEAD_DIM
               == (lax.broadcasted_iota(jnp.int32, (width, HG_2), 1) // HEAD_DIM) % HGRN_HEADS)
    lane_head = (lax.broadcasted_iota(jnp.int32, (HEAD_DIM, HG_2), 1) // HEAD_DIM) % HGRN_HEADS
    ones_bd = _head_ones(width)

    def tile_body(t, carry):
        rows = pl.ds(pl.multiple_of(t * tile, tile), tile)
        q = zh_ref[rows, c_q:c_q + width] * (HEAD_DIM ** -0.5)
        v = zh_ref[rows, c_v:c_v + width]
        z = zh_ref[rows, c_z:c_z + HG_2]
        ez = jnp.exp(-jnp.abs(z))
        r = 1.0 / (1.0 + ez)
        tr = ez * r
        pos_z = z >= 0.0
        f = lb_floor + one_m_lb * jnp.where(pos_z, r, tr)
        logf = jnp.log(f)
        kk = one_m_lb * jnp.where(pos_z, tr, r)

        pieces = jnp.concatenate(_split3(logf), axis=1)
        acc = jnp.dot(tri_ref[...], pieces, preferred_element_type=F32)
        prefix = acc[:, :HG_2] + acc[:, HG_2:2 * HG_2] + acc[:, 2 * HG_2:]
        total = prefix[tile - 1:tile, :]
        b = jnp.where(lane_bwd, total - prefix + logf, prefix)
        kk_f, kk_b = kk[:, :width], kk[:, width:]

        a0 = jnp.dot((q * (kk_f + kk_b)).astype(BF16), ones_bd, preferred_element_type=F32)
        o = a0 * v

        b2 = b * LOG2_E
        bf2, bb2 = b2[:, :width], b2[:, width:]
        scores = None
        for s in HGRN_LEVELS:
            second = (row % s) >= s // 2
            k_role = jnp.where(second, kk_b, kk_f)
            if s == 2:
                q_op = q * jnp.where(second, f[:, :width], f[:, width:])
                k_op = k_role
            else:
                mid_f = _level_mid(bf2, s, s // 2 - 1)
                mid_b = _level_mid(bb2, s, s // 2)
                e_q = jnp.where(second, bf2, bb2) - jnp.where(second, mid_f, mid_b)
                e_k = jnp.where(second, mid_b, mid_f) - jnp.where(second, bb2, bf2)
                q_op = q * jnp.exp2(e_q)
                k_op = k_role * jnp.exp2(e_k)
            qt = q_op.astype(BF16)
            kt = k_op.T.astype(BF16)
            sc = jnp.concatenate(
                [jnp.dot(qt[:, h * HEAD_DIM:(h + 1) * HEAD_DIM], kt[h * HEAD_DIM:(h + 1) * HEAD_DIM, :],
                         preferred_element_type=F32) for h in range(HGRN_HEADS)], axis=1)
            scores = sc if scores is None else jnp.where(pair_xor < s, sc, scores)
        scores = jnp.where(pair_xor == 0, 0.0, scores)
        ev = jnp.where(ev_mask, jnp.concatenate([v] * HGRN_HEADS, axis=0), 0.0).astype(BF16)
        o = o + jnp.dot(scores.astype(BF16), ev, preferred_element_type=F32)
        oacc_ref[rows, :] = o

        q2 = jnp.concatenate([q, q], axis=1)
        qin_ref[rows, :] = (q2 * jnp.exp(b)).astype(BF16)
        kout = (kk * jnp.exp(total - b)).astype(BF16)
        dec_ref[t] = jnp.exp(total)
        full = jnp.dot(v.T.astype(BF16), kout, preferred_element_type=F32)
        kvt = full[0:HEAD_DIM, :]
        for hh in range(1, HGRN_HEADS):
            kvt = jnp.where(lane_head == hh, full[hh * HEAD_DIM:(hh + 1) * HEAD_DIM, :], kvt)
        kvt_ref[t] = kvt
        return carry

    lax.fori_loop(0, n_tiles, tile_body, 0)

    if has_s0:
        st0 = jnp.concatenate([s0_ref[d, h] for d in range(2) for h in range(HGRN_HEADS)], axis=0).T
    else:
        st0 = jnp.zeros((HEAD_DIM, HG_2), F32)

    def scan_step(i, st):
        tf, tb = i, n_tiles - 1 - i
        stf_ref[tf] = st
        stb_ref[tb] = st
        dec = jnp.where(lane_bwd1, dec_ref[tb], dec_ref[tf])
        kv = jnp.where(lane_bwd_s, kvt_ref[tb], kvt_ref[tf])
        return st * dec + kv

    st_fin = lax.fori_loop(0, n_tiles, scan_step, st0)
    if emit_state:
        st_t = st_fin.T
        for d in range(2):
            for h in range(HGRN_HEADS):
                r0 = (HGRN_HEADS * d + h) * HEAD_DIM
                sf_ref[d, h] = st_t[r0:r0 + HEAD_DIM, :]

    def out_tile(t):
        rows = pl.ds(pl.multiple_of(t * tile, tile), tile)
        st = jnp.where(lane_bwd_s, stb_ref[t], stf_ref[t])
        bd = jnp.where(bd_mask, jnp.concatenate([st] * HGRN_HEADS, axis=0), 0.0).T.astype(BF16)
        o = oacc_ref[rows, :] + jnp.dot(qin_ref[rows, :], bd, preferred_element_type=F32)
        hg = zh_ref[rows, c_g:c_g + width]
        ho_ref[rows, :] = _head_rmsnorm(o, nw_ref[...]) * (hg * jax.nn.sigmoid(hg))

    def out_body(i, carry):
        out_tile(2 * i)
        out_tile(2 * i + 1)
        return carry

    assert n_tiles % 2 == 0
    lax.fori_loop(0, n_tiles // 2, out_body, 0)


def _hgrn_tri():
    i = np.arange(HGRN_TILE)
    return jnp.asarray((i[None, :] <= i[:, None]).astype(np.float32), dtype=BF16)


def _hgrn_call(zh, lb2, norm_w, s0, final_states, *, seq_len, layer, span):
    row0, n = span
    blk0 = row0 // seq_len
    n_seq = n // seq_len
    has_s0 = s0 is not None
    emit_state = final_states is not None
    state_block = (2, HGRN_HEADS, HEAD_DIM, HEAD_DIM)
    n_tiles = seq_len // HGRN_TILE
    in_specs = [
        pl.BlockSpec((seq_len, ZH_W), lambda b: (b + blk0, 0)),
        pl.BlockSpec((DEPTH, 1, HG_2), lambda b: (0, 0, 0)),
        pl.BlockSpec((None, 1, HGRN_W), lambda b: (layer, 0, 0)),
        pl.BlockSpec((HGRN_TILE, HGRN_TILE), lambda b: (0, 0)),
    ]
    args = [zh, lb2, norm_w, _hgrn_tri()]
    if has_s0:
        in_specs.append(pl.BlockSpec((None, None) + state_block, lambda b: (b, layer, 0, 0, 0, 0)))
        args.append(s0)
    out_specs = [pl.BlockSpec((seq_len, HGRN_W), lambda b: (b, 0))]
    out_shape = [jax.ShapeDtypeStruct((n, HGRN_W), F32)]
    aliases = {}
    if emit_state:
        aliases[len(args)] = 1
        in_specs.append(pl.BlockSpec(memory_space=pl.ANY))
        args.append(final_states)
        out_specs.append(pl.BlockSpec((None, None) + state_block, lambda b: (b, layer, 0, 0, 0, 0)))
        out_shape.append(jax.ShapeDtypeStruct(final_states.shape, F32))
    return pl.pallas_call(
        functools.partial(_hgrn_kernel, seq_len=seq_len, layer=layer, has_s0=has_s0, emit_state=emit_state),
        grid=(n_seq,),
        in_specs=in_specs,
        out_specs=out_specs,
        out_shape=out_shape,
        input_output_aliases=aliases,
        scratch_shapes=[
            pltpu.VMEM((seq_len, HG_2), BF16),
            pltpu.VMEM((n_tiles, 1, HG_2), F32),
            pltpu.VMEM((n_tiles, HEAD_DIM, HG_2), F32),
            pltpu.VMEM((n_tiles, HEAD_DIM, HG_2), F32),
            pltpu.VMEM((n_tiles, HEAD_DIM, HG_2), F32),
            pltpu.VMEM((seq_len, HGRN_W), F32),
        ],
        compiler_params=_params("arbitrary"),
        name="hgrn",
    )(*args)


def _sink_row(sink_ref, kv_head, rows):
    grp = lax.broadcasted_iota(jnp.int32, (1, ATTN_GROUP * rows), 1) // rows
    base = kv_head * ATTN_GROUP
    out = jnp.broadcast_to(sink_ref[:, base:base + 1], (1, ATTN_GROUP * rows))
    for g in range(1, ATTN_GROUP):
        out = jnp.where(grp == g, sink_ref[:, base + g:base + g + 1], out)
    return out


def _group_queries_t(qt, kv_head):
    base = kv_head * ATTN_GROUP
    return jnp.concatenate([qt[(base + g) * HEAD_DIM:(base + g + 1) * HEAD_DIM, :] for g in range(ATTN_GROUP)], axis=1)


def _ctx_attn_kernel(zq_ref, zkv_ref, zu_ref, qn_ref, kn_ref, sink_ref, chan_ref, pos_ref, kc_all_ref, vc_all_ref,
                     ao_ref, fo_ref, kc_ref, vc_ref, chan_bf, pos_bf, *, seq_len):
    del kc_all_ref, vc_all_ref
    _fourier_rows(zu_ref, chan_ref, pos_ref, fo_ref, chan_bf, pos_bf, seq_len)
    rows = seq_len
    q_all = _head_rmsnorm(zq_ref[...], qn_ref[...]) * (HEAD_DIM ** -0.5)
    k_all = _head_rmsnorm(zkv_ref[:, :KV_W], kn_ref[...])
    v_all = zkv_ref[:, KV_W:]
    kc_ref[...] = k_all.reshape(kc_ref.shape)
    vc_ref[...] = v_all.reshape(vc_ref.shape)
    for r0 in range(0, zq_ref.shape[0], seq_len):
        qt = q_all[r0:r0 + rows, :].T.astype(BF16)
        kb = k_all[r0:r0 + rows, :].astype(BF16)
        vt = v_all[r0:r0 + rows, :].T.astype(BF16)
        out_t = []
        for kh in range(ATTN_KV_HEADS):
            hs = slice(kh * HEAD_DIM, (kh + 1) * HEAD_DIM)
            s = jnp.dot(kb[:, hs], _group_queries_t(qt, kh), preferred_element_type=F32)
            sink = _sink_row(sink_ref, kh, rows)
            m = jnp.maximum(jnp.max(s, axis=0, keepdims=True), sink)
            p = jnp.exp(s - m)
            denom = jnp.sum(p, axis=0, keepdims=True) + jnp.exp(sink - m)
            o_t = jnp.dot(vt[hs, :], p.astype(BF16), preferred_element_type=F32) * (1.0 / denom)
            out_t.extend(o_t[:, g * rows:(g + 1) * rows] for g in range(ATTN_GROUP))
        ao_ref[r0:r0 + rows, :] = jnp.concatenate(out_t, axis=0).T


def _ctx_attn_call(zq, zkv, zu, q_norm, k_norm, sink, k_cache, v_cache, *, seq_len, layer, span):
    row0, n = span
    rows = max(seq_len, SMALL_SEQ_ROWS)
    blk0 = row0 // rows
    seqs = rows // seq_len
    cache_spec = pl.BlockSpec((seqs, None, seq_len, KV_W), lambda b: (b, layer, 0, 0))
    table_specs, table_scratch = _fourier_table_specs(seq_len, lambda b: (0, 0))
    return pl.pallas_call(
        functools.partial(_ctx_attn_kernel, seq_len=seq_len),
        grid=(n // rows,),
        in_specs=[
            pl.BlockSpec((rows, ATTN_W), lambda b: (b + blk0, 0)),
            pl.BlockSpec((rows, 2 * KV_W), lambda b: (b + blk0, 0)),
            pl.BlockSpec((rows, FOURIER_W), lambda b: (b + blk0, 0)),
            pl.BlockSpec((None, 1, ATTN_W), lambda b: (layer, 0, 0)),
            pl.BlockSpec((None, 1, KV_W), lambda b: (layer, 0, 0)),
            pl.BlockSpec((None, 1, ATTN_HEADS), lambda b: (layer, 0, 0)),
            *table_specs,
            pl.BlockSpec(memory_space=pl.ANY),
            pl.BlockSpec(memory_space=pl.ANY),
        ],
        out_specs=[pl.BlockSpec((rows, ATTN_W), lambda b: (b, 0)), pl.BlockSpec((rows, FOURIER_W), lambda b: (b, 0)),
                   cache_spec, cache_spec],
        out_shape=[
            jax.ShapeDtypeStruct((n, ATTN_W), F32),
            jax.ShapeDtypeStruct((n, FOURIER_W), F32),
            jax.ShapeDtypeStruct(k_cache.shape, F32),
            jax.ShapeDtypeStruct(v_cache.shape, F32),
        ],
        input_output_aliases={8: 2, 9: 3},
        scratch_shapes=table_scratch,
        compiler_params=_params("arbitrary"),
        name="ctx_attn",
    )(zq, zkv, zu, q_norm, k_norm, sink, *_dft_tables(seq_len), k_cache, v_cache)


def _rope_tables(seq_len, n_heads):
    t = np.arange(seq_len)
    row, col = t // GRID_W, t % GRID_W
    half = HEAD_DIM // 2
    inv = (ROPE_BASE ** (-np.arange(0, half, 2, dtype=np.float32) / half)).astype(np.float32)

    def one(pos):
        ang = pos.astype(np.float32)[:, None] * inv[None]
        c, s = np.cos(ang), np.sin(ang)
        return np.concatenate([c, c], axis=1), np.concatenate([-s, s], axis=1)

    cr, sr = one(row)
    cc, sc = one(col)
    cos = np.tile(np.concatenate([cr, cc], axis=1), (1, n_heads)).astype(np.float32)
    sin = np.tile(np.concatenate([sr, sc], axis=1), (1, n_heads)).astype(np.float32)
    return jnp.asarray(cos), jnp.asarray(sin)


def _rope(x, cos, sin_signed):
    width = x.shape[-1]
    lane = lax.broadcasted_iota(jnp.int32, x.shape, 1)
    quarter = HEAD_DIM // 4
    up = pltpu.roll(x, width - quarter, axis=1)
    down = pltpu.roll(x, quarter, axis=1)
    partner = jnp.where((lane % (2 * quarter)) < quarter, up, down)
    return x * cos + partner * sin_signed


def _lat_attn_kernel(zq_ref, zkv_ref, zu_ref, kc_ref, vc_ref, qn_ref, kn_ref, sink_ref,
                     cosq_ref, sinq_ref, cosk_ref, sink_tab_ref, band_ref, chan_ref, pos_ref,
                     ao_ref, fo_ref, kb_ref, kcb_ref, vct_ref, chan_bf, pos_bf, *, seq_len):
    rows = ATTN_BLOCK
    _fourier_rows(zu_ref, chan_ref, pos_ref, fo_ref, chan_bf, pos_bf, seq_len)

    k_all = _head_rmsnorm(zkv_ref[:, 0:KV_W], kn_ref[...])
    kb_ref[...] = _rope(k_all, cosk_ref[...], sink_tab_ref[...]).astype(BF16)
    kcb_ref[...] = kc_ref[...].astype(BF16)
    vct_ref[...] = vc_ref[...].T.astype(BF16)

    def query_block(blk, carry):
        qrows = pl.ds(pl.multiple_of(blk * ATTN_BLOCK, ATTN_BLOCK), ATTN_BLOCK)
        q = _head_rmsnorm(zq_ref[qrows, :], qn_ref[...])
        q = _rope(q, cosq_ref[qrows, :], sinq_ref[qrows, :]) * (HEAD_DIM ** -0.5)
        start = jnp.clip((blk - 1) * ATTN_BLOCK, 0, seq_len - WIN_KEYS)
        start = pl.multiple_of(start, ATTN_BLOCK)
        vwin = zkv_ref[pl.ds(start, WIN_KEYS), KV_W:2 * KV_W]

        band_bias = band_ref[blk - start // ATTN_BLOCK]

        qt = q.T.astype(BF16)
        kwb = kb_ref[pl.ds(start, WIN_KEYS), :]
        kcb = kcb_ref[...]
        vwt = vwin.T.astype(BF16)
        vct = vct_ref[...]
        out_t = []
        for kh in range(ATTN_KV_HEADS):
            hs = slice(kh * HEAD_DIM, (kh + 1) * HEAD_DIM)
            qg = _group_queries_t(qt, kh)
            s_loc = jnp.dot(kwb[:, hs], qg, preferred_element_type=F32) + band_bias
            s_ctx = jnp.dot(kcb[:, hs], qg, preferred_element_type=F32)
            sink = _sink_row(sink_ref, kh, rows)
            m = jnp.maximum(jnp.maximum(jnp.max(s_loc, axis=0, keepdims=True),
                                        jnp.max(s_ctx, axis=0, keepdims=True)), sink)
            p_loc = jnp.exp(s_loc - m)
            p_ctx = jnp.exp(s_ctx - m)
            denom = (jnp.sum(p_loc, axis=0, keepdims=True) + jnp.sum(p_ctx, axis=0, keepdims=True)
                     + jnp.exp(sink - m))
            o_t = (jnp.dot(vwt[hs, :], p_loc.astype(BF16), preferred_element_type=F32)
                   + jnp.dot(vct[hs, :], p_ctx.astype(BF16), preferred_element_type=F32)) * (1.0 / denom)
            out_t.extend(o_t[:, g * rows:(g + 1) * rows] for g in range(ATTN_GROUP))
        ao_ref[qrows, :] = jnp.concatenate(out_t, axis=0).T
        return carry

    lax.fori_loop(0, seq_len // ATTN_BLOCK, query_block, 0)


def _lat_attn_call(zq, zkv, zu, cache_k, cache_v, q_norm, k_norm, sink, *, seq_len, layer, span):
    row0, n = span
    seq0 = row0 // seq_len
    past = cache_k.shape[2]
    cosq, sinq = _rope_tables(seq_len, ATTN_HEADS)
    cosk, sink_tab = _rope_tables(seq_len, ATTN_KV_HEADS)
    kr = np.arange(WIN_KEYS)[:, None]
    qi = np.arange(ATTN_GROUP * ATTN_BLOCK)[None, :] % ATTN_BLOCK
    band = np.stack([np.where(np.abs(qi + off * ATTN_BLOCK - kr) <= WINDOW, 0.0, NEG_BIG) for off in range(3)])
    band = jnp.asarray(band, dtype=F32)
    table_specs, table_scratch = _fourier_table_specs(seq_len, lambda b: (0, 0))
    return pl.pallas_call(
        functools.partial(_lat_attn_kernel, seq_len=seq_len),
        grid=(n // seq_len,),
        in_specs=[
            pl.BlockSpec((seq_len, ATTN_W), lambda b: (seq0 + b, 0)),
            pl.BlockSpec((seq_len, 2 * KV_W), lambda b: (seq0 + b, 0)),
            pl.BlockSpec((seq_len, FOURIER_W), lambda b: (seq0 + b, 0)),
            pl.BlockSpec((None, None, past, KV_W), lambda b: (b, layer, 0, 0)),
            pl.BlockSpec((None, None, past, KV_W), lambda b: (b, layer, 0, 0)),
            pl.BlockSpec((None, 1, ATTN_W), lambda b: (layer, 0, 0)),
            pl.BlockSpec((None, 1, KV_W), lambda b: (layer, 0, 0)),
            pl.BlockSpec((None, 1, ATTN_HEADS), lambda b: (layer, 0, 0)),
            pl.BlockSpec((seq_len, ATTN_W), lambda b: (0, 0), pipeline_mode=pl.Buffered(1)),
            pl.BlockSpec((seq_len, ATTN_W), lambda b: (0, 0), pipeline_mode=pl.Buffered(1)),
            pl.BlockSpec((seq_len, KV_W), lambda b: (0, 0), pipeline_mode=pl.Buffered(1)),
            pl.BlockSpec((seq_len, KV_W), lambda b: (0, 0), pipeline_mode=pl.Buffered(1)),
            pl.BlockSpec((3, WIN_KEYS, ATTN_GROUP * ATTN_BLOCK), lambda b: (0, 0, 0), pipeline_mode=pl.Buffered(1)),
            *table_specs,
        ],
        out_specs=[pl.BlockSpec((seq_len, ATTN_W), lambda b: (b, 0)), pl.BlockSpec((seq_len, FOURIER_W), lambda b: (b, 0))],
        out_shape=[jax.ShapeDtypeStruct((n, ATTN_W), F32), jax.ShapeDtypeStruct((n, FOURIER_W), F32)],
        scratch_shapes=[
            pltpu.VMEM((seq_len, KV_W), BF16),
            pltpu.VMEM((past, KV_W), BF16),
            pltpu.VMEM((KV_W, past), BF16),
            *table_scratch,
        ],
        compiler_params=_params("arbitrary"),
        name="lat_attn",
    )(zq, zkv, zu, cache_k, cache_v, q_norm, k_norm, sink, cosq, sinq, cosk, sink_tab, band, *_dft_tables(seq_len))


def kernel(x_prompt, x_sample, c, cache_attn_k, cache_attn_v, state_hgrn, c_ctx, w_ada, b_ada, norm_ffn1, norm_mix, norm_ffn2, ffn1_w_gate_up, ffn1_w_down, ffn2_w_gate_up, ffn2_w_down, w_in, w_out, hgrn_lower_bounds, hgrn_norm, q_norm, k_norm, attn_sink):
    batch, seq, _ = x_prompt.shape
    dec_batch, dec_seq, _ = x_sample.shape
    past = cache_attn_k.shape[2]
    assert dec_batch + 1 <= ADA_ROWS

    cond = jnp.concatenate([c_ctx[None], c, jnp.zeros((ADA_ROWS - 1 - dec_batch, D_MODEL), F32)], axis=0)
    mod = _ada_call(cond, w_ada, b_ada).reshape(DEPTH, ADA_ROWS, N_MOD, D_MODEL)

    bf = lambda w: w.astype(BF16)
    w_gu1, w_d1, w_gu2, w_d2 = ffn1_w_gate_up, ffn1_w_down, ffn2_w_gate_up, ffn2_w_down
    w_in_b, w_out_b = bf(w_in), bf(w_out)
    row3 = lambda w: w.reshape(DEPTH, 1, w.shape[-1])
    n1, nm, n2 = row3(norm_ffn1), row3(norm_mix), row3(norm_ffn2)
    hn = row3(hgrn_norm)
    qn = row3(jnp.tile(q_norm, (1, ATTN_HEADS)))
    kn = row3(jnp.tile(k_norm, (1, ATTN_KV_HEADS)))
    sink = row3(attn_sink)
    cache_k = cache_attn_k.reshape(dec_batch, DEPTH, past, KV_W)
    cache_v = cache_attn_v.reshape(dec_batch, DEPTH, past, KV_W)
    lb2 = hgrn_lower_bounds.reshape(DEPTH, 1, HG_2)

    n_ctx, n_lat = batch * seq, dec_batch * dec_seq
    span_c, span_l = (0, n_ctx), (n_ctx, n_lat)
    x_c, x_l = x_prompt.reshape(n_ctx, D_MODEL), x_sample.reshape(n_lat, D_MODEL)
    new_k = jnp.zeros((batch, DEPTH, seq, KV_W), F32)
    new_v = jnp.zeros((batch, DEPTH, seq, KV_W), F32)
    new_s = jnp.zeros((batch, DEPTH, 2, HGRN_HEADS, HEAD_DIM, HEAD_DIM), F32)
    for layer in range(DEPTH):
        x1, zu, zh, zq, zkv = _ffn_in_call(x_c, x_l, mod, n1, nm, w_gu1, w_d1, w_in_b, layer=layer, rows_per_cond=dec_seq)
        ho_c, new_s = _hgrn_call(zh, lb2, hn, None, new_s, seq_len=seq, layer=layer, span=span_c)
        (ho_l,) = _hgrn_call(zh, lb2, hn, state_hgrn, None, seq_len=dec_seq, layer=layer, span=span_l)
        ao_c, fo_c, new_k, new_v = _ctx_attn_call(zq, zkv, zu, qn, kn, sink, new_k, new_v,
                                                  seq_len=seq, layer=layer, span=span_c)
        ao_l, fo_l = _lat_attn_call(zq, zkv, zu, cache_k, cache_v, qn, kn, sink, seq_len=dec_seq, layer=layer, span=span_l)
        x_c, x_l = _out_ffn_call(x1, mod, n2, (fo_c, fo_l), (ho_c, ho_l), (ao_c, ao_l), w_out_b, w_gu2, w_d2,
                                 layer=layer, rows_per_cond=dec_seq)

    y_p = x_c.reshape(batch, seq, D_MODEL)
    y_s = x_l.reshape(dec_batch, dec_seq, D_MODEL)
    cache_shape = (batch, DEPTH, seq, ATTN_KV_HEADS, HEAD_DIM)
    return (y_p, y_s, new_k.reshape(cache_shape), new_v.reshape(cache_shape), new_s)
```

```python
import functools

import numpy as np
import jax
import jax.numpy as jnp
from jax import lax
from jax.experimental import pallas as pl
from jax.experimental.pallas import tpu as pltpu

F32 = jnp.float32
BF16 = jnp.bfloat16

D_MODEL = 1024
DEPTH = 2
HEAD_DIM = 64
FOURIER_W = 256
HGRN_W = 256
HGRN_HEADS = 4
ATTN_W = 512
ATTN_HEADS = 8
ATTN_KV_HEADS = 2
ATTN_GROUP = 4
KV_W = 128
MIX_W = 1024
WINDOW = 128
ATTN_BLOCK = 128
D_FF = 2816
GRID_W = 64
ROPE_BASE = 10000.0
EPS = 1e-6
N_MOD = 9
NEG_BIG = -1e30
LB_FLOOR = 1e-30
LOG2_E = 1.4426950408889634

HG_2 = 2 * HGRN_W
ZH_W = 5 * HGRN_W
IN_W = FOURIER_W + ZH_W + ATTN_W + 2 * KV_W

SUBLANES = 8
VMEM_LIMIT_BYTES = 63 * 1024 * 1024

ADA_ROWS = 8
ADA_TN = 1536
FFN_TM = 512
FFN_FC = 256
SMALL_SEQ_ROWS = 1024
HGRN_TILE = 128
HGRN_LEVELS = (128, 64, 32, 16, 8, 4, 2)
WIN_KEYS = 3 * ATTN_BLOCK


def _params(*semantics):
    return pltpu.CompilerParams(dimension_semantics=semantics, vmem_limit_bytes=VMEM_LIMIT_BYTES)


def _mod_norm(x, norm_w, shift, scale):
    ms = jnp.mean(x * x, axis=-1, keepdims=True)
    y = x * lax.rsqrt(ms + EPS) * norm_w
    return y * (1.0 + scale) + shift


def _head_ones(width):
    r = lax.broadcasted_iota(jnp.int32, (width, width), 0) // HEAD_DIM
    c = lax.broadcasted_iota(jnp.int32, (width, width), 1) // HEAD_DIM
    return jnp.where(r == c, 1.0, 0.0).astype(BF16)


def _head_rmsnorm(x, w):
    sq = (x * x).astype(BF16)
    ms = jnp.dot(sq, _head_ones(x.shape[-1]), preferred_element_type=F32) * (1.0 / HEAD_DIM)
    return x * lax.rsqrt(ms + EPS) * w


def _ada_kernel(cond_ref, w_ref, b_ref, o_ref):
    cnd = cond_ref[...]
    act = (cnd * jax.nn.sigmoid(cnd)).astype(BF16)
    o_ref[...] = jnp.dot(act, w_ref[...].astype(BF16), preferred_element_type=F32) + b_ref[...]


def _ada_call(cond, w_ada, b_ada):
    n_out = N_MOD * D_MODEL
    return pl.pallas_call(
        _ada_kernel,
        grid=(DEPTH, n_out // ADA_TN),
        in_specs=[
            pl.BlockSpec((ADA_ROWS, D_MODEL), lambda l, j: (0, 0)),
            pl.BlockSpec((None, D_MODEL, ADA_TN), lambda l, j: (l, 0, j)),
            pl.BlockSpec((None, 1, ADA_TN), lambda l, j: (l, 0, j)),
        ],
        out_specs=pl.BlockSpec((None, ADA_ROWS, ADA_TN), lambda l, j: (l, 0, j)),
        out_shape=jax.ShapeDtypeStruct((DEPTH, ADA_ROWS, n_out), F32),
        compiler_params=_params("arbitrary", "arbitrary"),
        name="adaln",
    )(cond, w_ada, b_ada.reshape(DEPTH, 1, n_out))


def _weight_chunk_copies(wgu_hbm, wd_hbm, stage_gu, stage_d, sems, layer, c, slot):
    lo, hi = c * FFN_FC, (c + 1) * FFN_FC
    return (
        pltpu.make_async_copy(wgu_hbm.at[layer, :, lo:hi], stage_gu.at[slot, :, 0:FFN_FC], sems.at[slot, 0]),
        pltpu.make_async_copy(wgu_hbm.at[layer, :, D_FF + lo:D_FF + hi], stage_gu.at[slot, :, FFN_FC:2 * FFN_FC],
                              sems.at[slot, 1]),
        pltpu.make_async_copy(wd_hbm.at[layer, lo:hi, :], stage_d.at[slot], sems.at[slot, 2]),
    )


def _swiglu_chunks(h, wgu_ref, wd_ref, fetch=None):
    n_chunks = D_FF // FFN_FC
    if fetch is not None:
        stage_gu, stage_d = fetch[2], fetch[3]
        for cp in _weight_chunk_copies(*fetch, 0, 0):
            cp.start()
    acc = None
    for c in range(n_chunks):
        if fetch is not None:
            slot = c % 2
            if c + 1 < n_chunks:
                for cp in _weight_chunk_copies(*fetch, c + 1, 1 - slot):
                    cp.start()
            for cp in _weight_chunk_copies(*fetch, c, slot):
                cp.wait()
            wgu_ref[:, c * FFN_FC:(c + 1) * FFN_FC] = stage_gu[slot, :, 0:FFN_FC].astype(BF16)
            wgu_ref[:, D_FF + c * FFN_FC:D_FF + (c + 1) * FFN_FC] = stage_gu[slot, :, FFN_FC:2 * FFN_FC].astype(BF16)
            wd_ref[c * FFN_FC:(c + 1) * FFN_FC, :] = stage_d[slot].astype(BF16)
        g = jnp.dot(h, wgu_ref[:, c * FFN_FC:(c + 1) * FFN_FC], preferred_element_type=F32)
        u = jnp.dot(h, wgu_ref[:, D_FF + c * FFN_FC:D_FF + (c + 1) * FFN_FC], preferred_element_type=F32)
        act = (g * jax.nn.sigmoid(g) * u).astype(BF16)
        part = jnp.dot(act, wd_ref[c * FFN_FC:(c + 1) * FFN_FC, :], preferred_element_type=F32)
        acc = part if acc is None else acc + part
    return acc


def _mod_spec(layer, tiles_a, tm, rows_per_cond):
    def index(i):
        row = jnp.where(i < tiles_a, 0, 1 + ((i - tiles_a) * tm) // rows_per_cond)
        return (layer, row, 0, 0)
    return pl.BlockSpec((None, None, N_MOD, D_MODEL), index)


def _stream_specs(width, tiles_a, tm):
    return (pl.BlockSpec((tm, width), lambda i: (jnp.minimum(i, tiles_a - 1), 0)),
            pl.BlockSpec((tm, width), lambda i: (jnp.maximum(i - tiles_a, 0), 0)))


def _resident(shape, layer):
    return pl.BlockSpec((None,) + shape, lambda i: (layer,) + (0,) * len(shape), pipeline_mode=pl.Buffered(1))


def _first_step_fetches(body, fetch):
    first = pl.program_id(0) == 0
    pl.when(first)(functools.partial(body, fetch))
    pl.when(jnp.logical_not(first))(functools.partial(body, None))


def _ffn_weight_scratch():
    return [
        pltpu.VMEM((D_MODEL, 2 * D_FF), BF16),
        pltpu.VMEM((D_FF, D_MODEL), BF16),
        pltpu.VMEM((2, D_MODEL, 2 * FFN_FC), F32),
        pltpu.VMEM((2, FFN_FC, D_MODEL), F32),
        pltpu.SemaphoreType.DMA((2, 3)),
    ]


def _ffn_in_kernel(xa_ref, xb_ref, mod_ref, n1_ref, nm_ref, wgu_hbm, wd_hbm, win_ref,
                   x1_ref, zu_ref, zh_ref, zq_ref, zkv_ref, wgu_ref, wd_ref, stage_gu, stage_d, sems, *, layer, tiles_a):
    def body(fetch):
        x = jnp.where(pl.program_id(0) < tiles_a, xa_ref[...], xb_ref[...])
        h = _mod_norm(x, n1_ref[...], mod_ref[0:1, :], mod_ref[1:2, :]).astype(BF16)
        x1 = x + 0.5 * mod_ref[2:3, :] * _swiglu_chunks(h, wgu_ref, wd_ref, fetch)
        x1_ref[...] = x1
        h2 = _mod_norm(x1, nm_ref[...], mod_ref[3:4, :], mod_ref[4:5, :]).astype(BF16)
        z = jnp.dot(h2, win_ref[...], preferred_element_type=F32)
        zu_ref[...] = z[:, :FOURIER_W]
        zh_ref[...] = z[:, FOURIER_W:FOURIER_W + ZH_W]
        zq_ref[...] = z[:, FOURIER_W + ZH_W:FOURIER_W + ZH_W + ATTN_W]
        zkv_ref[...] = z[:, FOURIER_W + ZH_W + ATTN_W:]

    _first_step_fetches(body, (wgu_hbm, wd_hbm, stage_gu, stage_d, sems, layer))


def _ffn_in_call(x_a, x_b, mod, n1, nm, w_gu, w_down, w_in, *, layer, rows_per_cond):
    tm = FFN_TM
    tiles_a = x_a.shape[0] // tm
    n = x_a.shape[0] + x_b.shape[0]
    widths = (D_MODEL, FOURIER_W, ZH_W, ATTN_W, 2 * KV_W)
    return pl.pallas_call(
        functools.partial(_ffn_in_kernel, layer=layer, tiles_a=tiles_a),
        grid=(n // tm,),
        in_specs=[
            *_stream_specs(D_MODEL, tiles_a, tm),
            _mod_spec(layer, tiles_a, tm, rows_per_cond),
            pl.BlockSpec((None, 1, D_MODEL), lambda i: (layer, 0, 0)),
            pl.BlockSpec((None, 1, D_MODEL), lambda i: (layer, 0, 0)),
            pl.BlockSpec(memory_space=pl.ANY),
            pl.BlockSpec(memory_space=pl.ANY),
            _resident((D_MODEL, IN_W), layer),
        ],
        out_specs=[pl.BlockSpec((tm, w), lambda i: (i, 0)) for w in widths],
        out_shape=[jax.ShapeDtypeStruct((n, w), F32) for w in widths],
        scratch_shapes=_ffn_weight_scratch(),
        compiler_params=_params("arbitrary"),
        name="ffn_in",
    )(x_a, x_b, mod, n1, nm, w_gu, w_down, w_in)


def _out_ffn_kernel(x_ref, mod_ref, n2_ref, foa_ref, fob_ref, hoa_ref, hob_ref, aoa_ref, aob_ref,
                    wout_ref, wgu_hbm, wd_hbm, oa_ref, ob_ref,
                    wgu_ref, wd_ref, stage_gu, stage_d, sems, *, layer, tiles_a):
    is_a = pl.program_id(0) < tiles_a

    def body(fetch):
        pick = lambda a_ref, b_ref: jnp.where(is_a, a_ref[...], b_ref[...]).astype(BF16)
        y = jnp.dot(pick(foa_ref, fob_ref), wout_ref[0:FOURIER_W, :], preferred_element_type=F32)
        y += jnp.dot(pick(hoa_ref, hob_ref), wout_ref[FOURIER_W:FOURIER_W + HGRN_W, :], preferred_element_type=F32)
        y += jnp.dot(pick(aoa_ref, aob_ref), wout_ref[FOURIER_W + HGRN_W:, :], preferred_element_type=F32)
        xm = x_ref[...] + mod_ref[5:6, :] * y
        h = _mod_norm(xm, n2_ref[...], mod_ref[6:7, :], mod_ref[7:8, :]).astype(BF16)
        out = xm + 0.5 * mod_ref[8:9, :] * _swiglu_chunks(h, wgu_ref, wd_ref, fetch)

        @pl.when(is_a)
        def _():
            oa_ref[...] = out

        @pl.when(jnp.logical_not(is_a))
        def _():
            ob_ref[...] = out

    _first_step_fetches(body, (wgu_hbm, wd_hbm, stage_gu, stage_d, sems, layer))


def _out_ffn_call(x, mod, n2, fo, ho, ao, w_out, w_gu, w_down, *, layer, rows_per_cond):
    tm = FFN_TM
    n_a, n_b = fo[0].shape[0], fo[1].shape[0]
    tiles_a = n_a // tm
    return pl.pallas_call(
        functools.partial(_out_ffn_kernel, layer=layer, tiles_a=tiles_a),
        grid=((n_a + n_b) // tm,),
        in_specs=[
            pl.BlockSpec((tm, D_MODEL), lambda i: (i, 0)),
            _mod_spec(layer, tiles_a, tm, rows_per_cond),
            pl.BlockSpec((None, 1, D_MODEL), lambda i: (layer, 0, 0)),
            *_stream_specs(FOURIER_W, tiles_a, tm),
            *_stream_specs(HGRN_W, tiles_a, tm),
            *_stream_specs(ATTN_W, tiles_a, tm),
            _resident((MIX_W, D_MODEL), layer),
            pl.BlockSpec(memory_space=pl.ANY),
            pl.BlockSpec(memory_space=pl.ANY),
        ],
        out_specs=list(_stream_specs(D_MODEL, tiles_a, tm)),
        out_shape=[jax.ShapeDtypeStruct((n_a, D_MODEL), F32), jax.ShapeDtypeStruct((n_b, D_MODEL), F32)],
        scratch_shapes=_ffn_weight_scratch(),
        compiler_params=_params("arbitrary"),
        name="out_ffn",
    )(x, mod, n2, fo[0], fo[1], ho[0], ho[1], ao[0], ao[1], w_out, w_gu, w_down)


def _dft_tables(seq_len):
    c = np.arange(HEAD_DIM)
    ang_c = 2.0 * np.pi * np.outer(c, c) / HEAD_DIM
    eye = np.eye(FOURIER_W // HEAD_DIM)
    chan = np.concatenate([np.kron(eye, np.cos(ang_c)), np.kron(eye, np.sin(ang_c))], axis=1) / np.sqrt(HEAD_DIM)
    n = np.arange(seq_len)
    ang_l = 2.0 * np.pi * (np.outer(n, n) % seq_len) / seq_len
    pos = np.concatenate([np.cos(ang_l), -np.sin(ang_l)], axis=1) / np.sqrt(seq_len)
    return jnp.asarray(chan, dtype=F32), jnp.asarray(pos, dtype=F32)


def _fourier_rows(u_ref, chan_ref, pos_ref, o_ref, chan_bf, pos_bf, seq_len):
    @pl.when(pl.program_id(0) == 0)
    def _():
        chan_bf[...] = chan_ref[...].astype(BF16)
        pos_bf[...] = pos_ref[...].astype(BF16)

    t = jnp.dot(u_ref[...].astype(BF16), chan_bf[...], preferred_element_type=F32)
    pos = pos_bf[...]
    for r0 in range(0, u_ref.shape[0], seq_len):
        tb = t[r0:r0 + seq_len, :]
        stacked = jnp.concatenate([tb[:, :FOURIER_W], tb[:, FOURIER_W:]], axis=0).astype(BF16)
        o_ref[r0:r0 + seq_len, :] = jnp.dot(pos, stacked, preferred_element_type=F32)


def _fourier_table_specs(seq_len, index):
    specs = [pl.BlockSpec((FOURIER_W, 2 * FOURIER_W), index, pipeline_mode=pl.Buffered(1)),
             pl.BlockSpec((seq_len, 2 * seq_len), index, pipeline_mode=pl.Buffered(1))]
    scratch = [pltpu.VMEM((FOURIER_W, 2 * FOURIER_W), BF16), pltpu.VMEM((seq_len, 2 * seq_len), BF16)]
    return specs, scratch


def _split3(x):
    hi = x.astype(BF16)
    r1 = x - hi.astype(F32)
    mid = r1.astype(BF16)
    lo = (r1 - mid.astype(F32)).astype(BF16)
    return hi, mid, lo


def _level_mid(b, s, pos):
    tile, lanes = b.shape
    if s >= SUBLANES:
        b3 = b.reshape(tile // s, s, lanes)
        return jnp.broadcast_to(b3[:, pos:pos + 1, :], b3.shape).reshape(tile, lanes)
    assert s == 4
    b3 = b.reshape(tile // SUBLANES, SUBLANES, lanes)
    sub = lax.broadcasted_iota(jnp.int32, (1, SUBLANES, 1), 1)
    return jnp.where(sub < 4, b3[:, pos:pos + 1, :], b3[:, 4 + pos:5 + pos, :]).reshape(tile, lanes)


def _hgrn_kernel(*refs, seq_len, layer, has_s0, emit_state):
    refs = list(refs)
    zh_ref, lb_ref, nw_ref, tri_ref = refs[:4]
    pos = 4
    s0_ref = sf_ref = None
    if has_s0:
        s0_ref = refs[pos]
        pos += 1
    if emit_state:
        pos += 1
    ho_ref = refs[pos]
    pos += 1
    if emit_state:
        sf_ref = refs[pos]
        pos += 1
    qin_ref, dec_ref, kvt_ref, stf_ref, stb_ref, oacc_ref = refs[pos:]

    tile = HGRN_TILE
    width = HGRN_W
    n_tiles = seq_len // tile
    c_q, c_v, c_z, c_g = 0, width, 2 * width, 4 * width

    raw = lb_ref[...]
    e = jnp.exp(raw - jnp.max(raw, axis=0, keepdims=True))
    soft = e / jnp.sum(e, axis=0, keepdims=True)
    cum = soft[0]
    for l in range(1, layer + 1):
        cum = cum + soft[l]
    lbd = cum - soft[0]
    lb_floor = jnp.maximum(lbd, LB_FLOOR)
    one_m_lb = 1.0 - lbd

    lane_bwd1 = lax.broadcasted_iota(jnp.int32, (1, HG_2), 1) >= width
    lane_bwd = lax.broadcasted_iota(jnp.int32, (tile, HG_2), 1) >= width
    lane_bwd_s = lax.broadcasted_iota(jnp.int32, (HEAD_DIM, HG_2), 1) >= width
    row = lax.broadcasted_iota(jnp.int32, (tile, 1), 0)
    pair_xor = (lax.broadcasted_iota(jnp.int32, (tile, HGRN_HEADS * tile), 0)
                ^ (lax.broadcasted_iota(jnp.int32, (tile, HGRN_HEADS * tile), 1) % tile))
    ev_mask = (lax.broadcasted_iota(jnp.int32, (HGRN_HEADS * tile, width), 0) // tile
               == lax.broadcasted_iota(jnp.int32, (HGRN_HEADS * tile, width), 1) // HEAD_DIM)
    bd_mask = (lax.broadcasted_iota(jnp.int32, (width, HG_2), 0) // HEAD_DIM
               == (lax.broadcasted_iota(jnp.int32, (width, HG_2), 1) // HEAD_DIM) % HGRN_HEADS)
    lane_head = (lax.broadcasted_iota(jnp.int32, (HEAD_DIM, HG_2), 1) // HEAD_DIM) % HGRN_HEADS
    ones_bd = _head_ones(width)

    def tile_body(t, carry):
        rows = pl.ds(pl.multiple_of(t * tile, tile), tile)
        q = zh_ref[rows, c_q:c_q + width] * (HEAD_DIM ** -0.5)
        v = zh_ref[rows, c_v:c_v + width]
        z = zh_ref[rows, c_z:c_z + HG_2]
        ez = jnp.exp(-jnp.abs(z))
        r = 1.0 / (1.0 + ez)
        tr = ez * r
        pos_z = z >= 0.0
        f = lb_floor + one_m_lb * jnp.where(pos_z, r, tr)
        logf = jnp.log(f)
        kk = one_m_lb * jnp.where(pos_z, tr, r)

        pieces = jnp.concatenate(_split3(logf), axis=1)
        acc = jnp.dot(tri_ref[...], pieces, preferred_element_type=F32)
        prefix = acc[:, :HG_2] + acc[:, HG_2:2 * HG_2] + acc[:, 2 * HG_2:]
        total = prefix[tile - 1:tile, :]
        b = jnp.where(lane_bwd, total - prefix + logf, prefix)
        kk_f, kk_b = kk[:, :width], kk[:, width:]

        a0 = jnp.dot((q * (kk_f + kk_b)).astype(BF16), ones_bd, preferred_element_type=F32)
        o = a0 * v

        b2 = b * LOG2_E
        bf2, bb2 = b2[:, :width], b2[:, width:]
        scores = None
        for s in HGRN_LEVELS:
            second = (row % s) >= s // 2
            k_role = jnp.where(second, kk_b, kk_f)
            if s == 2:
                q_op = q * jnp.where(second, f[:, :width], f[:, width:])
                k_op = k_role
            else:
                mid_f = _level_mid(bf2, s, s // 2 - 1)
                mid_b = _level_mid(bb2, s, s // 2)
                e_q = jnp.where(second, bf2, bb2) - jnp.where(second, mid_f, mid_b)
                e_k = jnp.where(second, mid_b, mid_f) - jnp.where(second, bb2, bf2)
                q_op = q * jnp.exp2(e_q)
                k_op = k_role * jnp.exp2(e_k)
            qt = q_op.astype(BF16)
            kt = k_op.T.astype(BF16)
            sc = jnp.concatenate(
                [jnp.dot(qt[:, h * HEAD_DIM:(h + 1) * HEAD_DIM], kt[h * HEAD_DIM:(h + 1) * HEAD_DIM, :],
                         preferred_element_type=F32) for h in range(HGRN_HEADS)], axis=1)
            scores = sc if scores is None else jnp.where(pair_xor < s, sc, scores)
        scores = jnp.where(pair_xor == 0, 0.0, scores)
        ev = jnp.where(ev_mask, jnp.concatenate([v] * HGRN_HEADS, axis=0), 0.0).astype(BF16)
        o = o + jnp.dot(scores.astype(BF16), ev, preferred_element_type=F32)
        oacc_ref[rows, :] = o

        q2 = jnp.concatenate([q, q], axis=1)
        qin_ref[rows, :] = (q2 * jnp.exp(b)).astype(BF16)
        kout = (kk * jnp.exp(total - b)).astype(BF16)
        dec_ref[t] = jnp.exp(total)
        full = jnp.dot(v.T.astype(BF16), kout, preferred_element_type=F32)
        kvt = full[0:HEAD_DIM, :]
        for hh in range(1, HGRN_HEADS):
            kvt = jnp.where(lane_head == hh, full[hh * HEAD_DIM:(hh + 1) * HEAD_DIM, :], kvt)
        kvt_ref[t] = kvt
        return carry

    lax.fori_loop(0, n_tiles, tile_body, 0)

    if has_s0:
        st0 = jnp.concatenate([s0_ref[d, h] for d in range(2) for h in range(HGRN_HEADS)], axis=0).T
    else:
        st0 = jnp.zeros((HEAD_DIM, HG_2), F32)

    def scan_step(i, st):
        tf, tb = i, n_tiles - 1 - i
        stf_ref[tf] = st
        stb_ref[tb] = st
        dec = jnp.where(lane_bwd1, dec_ref[tb], dec_ref[tf])
        kv = jnp.where(lane_bwd_s, kvt_ref[tb], kvt_ref[tf])
        return st * dec + kv

    st_fin = lax.fori_loop(0, n_tiles, scan_step, st0)
    if emit_state:
        st_t = st_fin.T
        for d in range(2):
            for h in range(HGRN_HEADS):
                r0 = (HGRN_HEADS * d + h) * HEAD_DIM
                sf_ref[d, h] = st_t[r0:r0 + HEAD_DIM, :]

    def out_tile(t):
        rows = pl.ds(pl.multiple_of(t * tile, tile), tile)
        st = jnp.where(lane_bwd_s, stb_ref[t], stf_ref[t])
        bd = jnp.where(bd_mask, jnp.concatenate([st] * HGRN_HEADS, axis=0), 0.0).T.astype(BF16)
        o = oacc_ref[rows, :] + jnp.dot(qin_ref[rows, :], bd, preferred_element_type=F32)
        hg = zh_ref[rows, c_g:c_g + width]
        ho_ref[rows, :] = _head_rmsnorm(o, nw_ref[...]) * (hg * jax.nn.sigmoid(hg))

    def out_body(i, carry):
        out_tile(2 * i)
        out_tile(2 * i + 1)
        return carry

    assert n_tiles % 2 == 0
    lax.fori_loop(0, n_tiles // 2, out_body, 0)


def _hgrn_tri():
    i = np.arange(HGRN_TILE)
    return jnp.asarray((i[None, :] <= i[:, None]).astype(np.float32), dtype=BF16)


def _hgrn_call(zh, lb2, norm_w, s0, final_states, *, seq_len, layer, span):
    row0, n = span
    blk0 = row0 // seq_len
    n_seq = n // seq_len
    has_s0 = s0 is not None
    emit_state = final_states is not None
    state_block = (2, HGRN_HEADS, HEAD_DIM, HEAD_DIM)
    n_tiles = seq_len // HGRN_TILE
    in_specs = [
        pl.BlockSpec((seq_len, ZH_W), lambda b: (b + blk0, 0)),
        pl.BlockSpec((DEPTH, 1, HG_2), lambda b: (0, 0, 0)),
        pl.BlockSpec((None, 1, HGRN_W), lambda b: (layer, 0, 0)),
        pl.BlockSpec((HGRN_TILE, HGRN_TILE), lambda b: (0, 0)),
    ]
    args = [zh, lb2, norm_w, _hgrn_tri()]
    if has_s0:
        in_specs.append(pl.BlockSpec((None, None) + state_block, lambda b: (b, layer, 0, 0, 0, 0)))
        args.append(s0)
    out_specs = [pl.BlockSpec((seq_len, HGRN_W), lambda b: (b, 0))]
    out_shape = [jax.ShapeDtypeStruct((n, HGRN_W), F32)]
    aliases = {}
    if emit_state:
        aliases[len(args)] = 1
        in_specs.append(pl.BlockSpec(memory_space=pl.ANY))
        args.append(final_states)
        out_specs.append(pl.BlockSpec((None, None) + state_block, lambda b: (b, layer, 0, 0, 0, 0)))
        out_shape.append(jax.ShapeDtypeStruct(final_states.shape, F32))
    return pl.pallas_call(
        functools.partial(_hgrn_kernel, seq_len=seq_len, layer=layer, has_s0=has_s0, emit_state=emit_state),
        grid=(n_seq,),
        in_specs=in_specs,
        out_specs=out_specs,
        out_shape=out_shape,
        input_output_aliases=aliases,
        scratch_shapes=[
            pltpu.VMEM((seq_len, HG_2), BF16),
            pltpu.VMEM((n_tiles, 1, HG_2), F32),
            pltpu.VMEM((n_tiles, HEAD_DIM, HG_2), F32),
            pltpu.VMEM((n_tiles, HEAD_DIM, HG_2), F32),
            pltpu.VMEM((n_tiles, HEAD_DIM, HG_2), F32),
            pltpu.VMEM((seq_len, HGRN_W), F32),
        ],
        compiler_params=_params("arbitrary"),
        name="hgrn",
    )(*args)


def _sink_row(sink_ref, kv_head, rows):
    grp = lax.broadcasted_iota(jnp.int32, (1, ATTN_GROUP * rows), 1) // rows
    base = kv_head * ATTN_GROUP
    out = jnp.broadcast_to(sink_ref[:, base:base + 1], (1, ATTN_GROUP * rows))
    for g in range(1, ATTN_GROUP):
        out = jnp.where(grp == g, sink_ref[:, base + g:base + g + 1], out)
    return out


def _group_queries_t(qt, kv_head):
    base = kv_head * ATTN_GROUP
    return jnp.concatenate([qt[(base + g) * HEAD_DIM:(base + g + 1) * HEAD_DIM, :] for g in range(ATTN_GROUP)], axis=1)


def _ctx_attn_kernel(zq_ref, zkv_ref, zu_ref, qn_ref, kn_ref, sink_ref, chan_ref, pos_ref, kc_all_ref, vc_all_ref,
                     ao_ref, fo_ref, kc_ref, vc_ref, chan_bf, pos_bf, *, seq_len):
    del kc_all_ref, vc_all_ref
    _fourier_rows(zu_ref, chan_ref, pos_ref, fo_ref, chan_bf, pos_bf, seq_len)
    rows = seq_len
    q_all = _head_rmsnorm(zq_ref[...], qn_ref[...]) * (HEAD_DIM ** -0.5)
    k_all = _head_rmsnorm(zkv_ref[:, :KV_W], kn_ref[...])
    v_all = zkv_ref[:, KV_W:]
    kc_ref[...] = k_all.reshape(kc_ref.shape)
    vc_ref[...] = v_all.reshape(vc_ref.shape)
    for r0 in range(0, zq_ref.shape[0], seq_len):
        qt = q_all[r0:r0 + rows, :].T.astype(BF16)
        kb = k_all[r0:r0 + rows, :].astype(BF16)
        vt = v_all[r0:r0 + rows, :].T.astype(BF16)
        heads = range(ATTN_KV_HEADS)
        hsl = [slice(kh * HEAD_DIM, (kh + 1) * HEAD_DIM) for kh in heads]
        s = [jnp.dot(kb[:, hsl[kh]], _group_queries_t(qt, kh), preferred_element_type=F32) for kh in heads]
        sink = [_sink_row(sink_ref, kh, rows) for kh in heads]
        m = [jnp.maximum(jnp.max(s[kh], axis=0, keepdims=True), sink[kh]) for kh in heads]
        p = [jnp.exp(s[kh] - m[kh]) for kh in heads]
        denom = [jnp.sum(p[kh], axis=0, keepdims=True) + jnp.exp(sink[kh] - m[kh]) for kh in heads]
        out_t = []
        for kh in heads:
            o_t = jnp.dot(vt[hsl[kh], :], p[kh].astype(BF16), preferred_element_type=F32) * (1.0 / denom[kh])
            out_t.extend(o_t[:, g * rows:(g + 1) * rows] for g in range(ATTN_GROUP))
        ao_ref[r0:r0 + rows, :] = jnp.concatenate(out_t, axis=0).T


def _ctx_attn_call(zq, zkv, zu, q_norm, k_norm, sink, k_cache, v_cache, *, seq_len, layer, span):
    row0, n = span
    rows = max(seq_len, SMALL_SEQ_ROWS)
    blk0 = row0 // rows
    seqs = rows // seq_len
    cache_spec = pl.BlockSpec((seqs, None, seq_len, KV_W), lambda b: (b, layer, 0, 0))
    table_specs, table_scratch = _fourier_table_specs(seq_len, lambda b: (0, 0))
    return pl.pallas_call(
        functools.partial(_ctx_attn_kernel, seq_len=seq_len),
        grid=(n // rows,),
        in_specs=[
            pl.BlockSpec((rows, ATTN_W), lambda b: (b + blk0, 0)),
            pl.BlockSpec((rows, 2 * KV_W), lambda b: (b + blk0, 0)),
            pl.BlockSpec((rows, FOURIER_W), lambda b: (b + blk0, 0)),
            pl.BlockSpec((None, 1, ATTN_W), lambda b: (layer, 0, 0)),
            pl.BlockSpec((None, 1, KV_W), lambda b: (layer, 0, 0)),
            pl.BlockSpec((None, 1, ATTN_HEADS), lambda b: (layer, 0, 0)),
            *table_specs,
            pl.BlockSpec(memory_space=pl.ANY),
            pl.BlockSpec(memory_space=pl.ANY),
        ],
        out_specs=[pl.BlockSpec((rows, ATTN_W), lambda b: (b, 0)), pl.BlockSpec((rows, FOURIER_W), lambda b: (b, 0)),
                   cache_spec, cache_spec],
        out_shape=[
            jax.ShapeDtypeStruct((n, ATTN_W), F32),
            jax.ShapeDtypeStruct((n, FOURIER_W), F32),
            jax.ShapeDtypeStruct(k_cache.shape, F32),
            jax.ShapeDtypeStruct(v_cache.shape, F32),
        ],
        input_output_aliases={8: 2, 9: 3},
        scratch_shapes=table_scratch,
        compiler_params=_params("arbitrary"),
        name="ctx_attn",
    )(zq, zkv, zu, q_norm, k_norm, sink, *_dft_tables(seq_len), k_cache, v_cache)


def _rope_tables(seq_len, n_heads):
    t = np.arange(seq_len)
    row, col = t // GRID_W, t % GRID_W
    half = HEAD_DIM // 2
    inv = (ROPE_BASE ** (-np.arange(0, half, 2, dtype=np.float32) / half)).astype(np.float32)

    def one(pos):
        ang = pos.astype(np.float32)[:, None] * inv[None]
        c, s = np.cos(ang), np.sin(ang)
        return np.concatenate([c, c], axis=1), np.concatenate([-s, s], axis=1)

    cr, sr = one(row)
    cc, sc = one(col)
    cos = np.tile(np.concatenate([cr, cc], axis=1), (1, n_heads)).astype(np.float32)
    sin = np.tile(np.concatenate([sr, sc], axis=1), (1, n_heads)).astype(np.float32)
    return jnp.asarray(cos), jnp.asarray(sin)


def _rope(x, cos, sin_signed):
    width = x.shape[-1]
    lane = lax.broadcasted_iota(jnp.int32, x.shape, 1)
    quarter = HEAD_DIM // 4
    up = pltpu.roll(x, width - quarter, axis=1)
    down = pltpu.roll(x, quarter, axis=1)
    partner = jnp.where((lane % (2 * quarter)) < quarter, up, down)
    return x * cos + partner * sin_signed


def _lat_attn_kernel(zq_ref, zkv_ref, zu_ref, kc_ref, vc_ref, qn_ref, kn_ref, sink_ref,
                     cosq_ref, sinq_ref, cosk_ref, sink_tab_ref, band_ref, chan_ref, pos_ref,
                     ao_ref, fo_ref, kb_ref, kcb_ref, vct_ref, chan_bf, pos_bf, *, seq_len):
    rows = ATTN_BLOCK
    _fourier_rows(zu_ref, chan_ref, pos_ref, fo_ref, chan_bf, pos_bf, seq_len)

    k_all = _head_rmsnorm(zkv_ref[:, 0:KV_W], kn_ref[...])
    kb_ref[...] = _rope(k_all, cosk_ref[...], sink_tab_ref[...]).astype(BF16)
    kcb_ref[...] = kc_ref[...].astype(BF16)
    vct_ref[...] = vc_ref[...].T.astype(BF16)

    def query_block(blk, carry):
        qrows = pl.ds(pl.multiple_of(blk * ATTN_BLOCK, ATTN_BLOCK), ATTN_BLOCK)
        q = _head_rmsnorm(zq_ref[qrows, :], qn_ref[...])
        q = _rope(q, cosq_ref[qrows, :], sinq_ref[qrows, :]) * (HEAD_DIM ** -0.5)
        start = jnp.clip((blk - 1) * ATTN_BLOCK, 0, seq_len - WIN_KEYS)
        start = pl.multiple_of(start, ATTN_BLOCK)
        vwin = zkv_ref[pl.ds(start, WIN_KEYS), KV_W:2 * KV_W]

        band_bias = band_ref[blk - start // ATTN_BLOCK]

        qt = q.T.astype(BF16)
        kwb = kb_ref[pl.ds(start, WIN_KEYS), :]
        kcb = kcb_ref[...]
        vwt = vwin.T.astype(BF16)
        vct = vct_ref[...]
        heads = range(ATTN_KV_HEADS)
        hsl = [slice(kh * HEAD_DIM, (kh + 1) * HEAD_DIM) for kh in heads]
        qg = [_group_queries_t(qt, kh) for kh in heads]
        s_loc = [jnp.dot(kwb[:, hsl[kh]], qg[kh], preferred_element_type=F32) + band_bias for kh in heads]
        s_ctx = [jnp.dot(kcb[:, hsl[kh]], qg[kh], preferred_element_type=F32) for kh in heads]
        sink = [_sink_row(sink_ref, kh, rows) for kh in heads]
        m = [jnp.maximum(jnp.maximum(jnp.max(s_loc[kh], axis=0, keepdims=True),
                                     jnp.max(s_ctx[kh], axis=0, keepdims=True)), sink[kh]) for kh in heads]
        p_loc = [jnp.exp(s_loc[kh] - m[kh]) for kh in heads]
        p_ctx = [jnp.exp(s_ctx[kh] - m[kh]) for kh in heads]
        denom = [jnp.sum(p_loc[kh], axis=0, keepdims=True) + jnp.sum(p_ctx[kh], axis=0, keepdims=True)
                 + jnp.exp(sink[kh] - m[kh]) for kh in heads]
        out_t = []
        for kh in heads:
            o_t = (jnp.dot(vwt[hsl[kh], :], p_loc[kh].astype(BF16), preferred_element_type=F32)
                   + jnp.dot(vct[hsl[kh], :], p_ctx[kh].astype(BF16), preferred_element_type=F32)) * (1.0 / denom[kh])
            out_t.extend(o_t[:, g * rows:(g + 1) * rows] for g in range(ATTN_GROUP))
        ao_ref[qrows, :] = jnp.concatenate(out_t, axis=0).T
        return carry

    lax.fori_loop(0, seq_len // ATTN_BLOCK, query_block, 0)


def _lat_attn_call(zq, zkv, zu, cache_k, cache_v, q_norm, k_norm, sink, *, seq_len, layer, span):
    row0, n = span
    seq0 = row0 // seq_len
    past = cache_k.shape[2]
    cosq, sinq = _rope_tables(seq_len, ATTN_HEADS)
    cosk, sink_tab = _rope_tables(seq_len, ATTN_KV_HEADS)
    kr = np.arange(WIN_KEYS)[:, None]
    qi = np.arange(ATTN_GROUP * ATTN_BLOCK)[None, :] % ATTN_BLOCK
    band = np.stack([np.where(np.abs(qi + off * ATTN_BLOCK - kr) <= WINDOW, 0.0, NEG_BIG) for off in range(3)])
    band = jnp.asarray(band, dtype=F32)
    table_specs, table_scratch = _fourier_table_specs(seq_len, lambda b: (0, 0))
    return pl.pallas_call(
        functools.partial(_lat_attn_kernel, seq_len=seq_len),
        grid=(n // seq_len,),
        in_specs=[
            pl.BlockSpec((seq_len, ATTN_W), lambda b: (seq0 + b, 0)),
            pl.BlockSpec((seq_len, 2 * KV_W), lambda b: (seq0 + b, 0)),
            pl.BlockSpec((seq_len, FOURIER_W), lambda b: (seq0 + b, 0)),
            pl.BlockSpec((None, None, past, KV_W), lambda b: (b, layer, 0, 0)),
            pl.BlockSpec((None, None, past, KV_W), lambda b: (b, layer, 0, 0)),
            pl.BlockSpec((None, 1, ATTN_W), lambda b: (layer, 0, 0)),
            pl.BlockSpec((None, 1, KV_W), lambda b: (layer, 0, 0)),
            pl.BlockSpec((None, 1, ATTN_HEADS), lambda b: (layer, 0, 0)),
            pl.BlockSpec((seq_len, ATTN_W), lambda b: (0, 0), pipeline_mode=pl.Buffered(1)),
            pl.BlockSpec((seq_len, ATTN_W), lambda b: (0, 0), pipeline_mode=pl.Buffered(1)),
            pl.BlockSpec((seq_len, KV_W), lambda b: (0, 0), pipeline_mode=pl.Buffered(1)),
            pl.BlockSpec((seq_len, KV_W), lambda b: (0, 0), pipeline_mode=pl.Buffered(1)),
            pl.BlockSpec((3, WIN_KEYS, ATTN_GROUP * ATTN_BLOCK), lambda b: (0, 0, 0), pipeline_mode=pl.Buffered(1)),
            *table_specs,
        ],
        out_specs=[pl.BlockSpec((seq_len, ATTN_W), lambda b: (b, 0)), pl.BlockSpec((seq_len, FOURIER_W), lambda b: (b, 0))],
        out_shape=[jax.ShapeDtypeStruct((n, ATTN_W), F32), jax.ShapeDtypeStruct((n, FOURIER_W), F32)],
        scratch_shapes=[
            pltpu.VMEM((seq_len, KV_W), BF16),
            pltpu.VMEM((past, KV_W), BF16),
            pltpu.VMEM((KV_W, past), BF16),
            *table_scratch,
        ],
        compiler_params=_params("arbitrary"),
        name="lat_attn",
    )(zq, zkv, zu, cache_k, cache_v, q_norm, k_norm, sink, cosq, sinq, cosk, sink_tab, band, *_dft_tables(seq_len))


def kernel(x_prompt, x_sample, c, cache_attn_k, cache_attn_v, state_hgrn, c_ctx, w_ada, b_ada, norm_ffn1, norm_mix, norm_ffn2, ffn1_w_gate_up, ffn1_w_down, ffn2_w_gate_up, ffn2_w_down, w_in, w_out, hgrn_lower_bounds, hgrn_norm, q_norm, k_norm, attn_sink):
    batch, seq, _ = x_prompt.shape
    dec_batch, dec_seq, _ = x_sample.shape
    past = cache_attn_k.shape[2]
    assert dec_batch + 1 <= ADA_ROWS

    cond = jnp.concatenate([c_ctx[None], c, jnp.zeros((ADA_ROWS - 1 - dec_batch, D_MODEL), F32)], axis=0)
    mod = _ada_call(cond, w_ada, b_ada).reshape(DEPTH, ADA_ROWS, N_MOD, D_MODEL)

    bf = lambda w: w.astype(BF16)
    w_gu1, w_d1, w_gu2, w_d2 = ffn1_w_gate_up, ffn1_w_down, ffn2_w_gate_up, ffn2_w_down
    w_in_b, w_out_b = bf(w_in), bf(w_out)
    row3 = lambda w: w.reshape(DEPTH, 1, w.shape[-1])
    n1, nm, n2 = row3(norm_ffn1), row3(norm_mix), row3(norm_ffn2)
    hn = row3(hgrn_norm)
    qn = row3(jnp.tile(q_norm, (1, ATTN_HEADS)))
    kn = row3(jnp.tile(k_norm, (1, ATTN_KV_HEADS)))
    sink = row3(attn_sink)
    cache_k = cache_attn_k.reshape(dec_batch, DEPTH, past, KV_W)
    cache_v = cache_attn_v.reshape(dec_batch, DEPTH, past, KV_W)
    lb2 = hgrn_lower_bounds.reshape(DEPTH, 1, HG_2)

    n_ctx, n_lat = batch * seq, dec_batch * dec_seq
    span_c, span_l = (0, n_ctx), (n_ctx, n_lat)
    x_c, x_l = x_prompt.reshape(n_ctx, D_MODEL), x_sample.reshape(n_lat, D_MODEL)
    new_k = jnp.zeros((batch, DEPTH, seq, KV_W), F32)
    new_v = jnp.zeros((batch, DEPTH, seq, KV_W), F32)
    new_s = jnp.zeros((batch, DEPTH, 2, HGRN_HEADS, HEAD_DIM, HEAD_DIM), F32)
    for layer in range(DEPTH):
        x1, zu, zh, zq, zkv = _ffn_in_call(x_c, x_l, mod, n1, nm, w_gu1, w_d1, w_in_b, layer=layer, rows_per_cond=dec_seq)
        ho_c, new_s = _hgrn_call(zh, lb2, hn, None, new_s, seq_len=seq, layer=layer, span=span_c)
        (ho_l,) = _hgrn_call(zh, lb2, hn, state_hgrn, None, seq_len=dec_seq, layer=layer, span=span_l)
        ao_c, fo_c, new_k, new_v = _ctx_attn_call(zq, zkv, zu, qn, kn, sink, new_k, new_v,
                                                  seq_len=seq, layer=layer, span=span_c)
        ao_l, fo_l = _lat_attn_call(zq, zkv, zu, cache_k, cache_v, qn, kn, sink, seq_len=dec_seq, layer=layer, span=span_l)
        x_c, x_l = _out_ffn_call(x1, mod, n2, (fo_c, fo_l), (ho_c, ho_l), (ao_c, ao_l), w_out_b, w_gu2, w_d2,
                                 layer=layer, rows_per_cond=dec_seq)

    y_p = x_c.reshape(batch, seq, D_MODEL)
    y_s = x_l.reshape(dec_batch, dec_seq, D_MODEL)
    cache_shape = (batch, DEPTH, seq, ATTN_KV_HEADS, HEAD_DIM)
    return (y_p, y_s, new_k.reshape(cache_shape), new_v.reshape(cache_shape), new_s)
```

```python
import functools

import numpy as np
import jax
import jax.numpy as jnp
from jax import lax
from jax.experimental import pallas as pl
from jax.experimental.pallas import tpu as pltpu

F32 = jnp.float32
BF16 = jnp.bfloat16

D_MODEL = 1024
DEPTH = 2
HEAD_DIM = 64
FOURIER_W = 256
HGRN_W = 256
HGRN_HEADS = 4
ATTN_W = 512
ATTN_HEADS = 8
ATTN_KV_HEADS = 2
ATTN_GROUP = 4
KV_W = 128
MIX_W = 1024
WINDOW = 128
ATTN_BLOCK = 128
D_FF = 2816
GRID_W = 64
ROPE_BASE = 10000.0
EPS = 1e-6
N_MOD = 9
NEG_BIG = -1e30
LB_FLOOR = 1e-30
LOG2_E = 1.4426950408889634

HG_2 = 2 * HGRN_W
ZH_W = 5 * HGRN_W
IN_W = FOURIER_W + ZH_W + ATTN_W + 2 * KV_W

SUBLANES = 8
VMEM_LIMIT_BYTES = 63 * 1024 * 1024

ADA_ROWS = 8
ADA_TN = 1536
FFN_TM = 512
FFN_FC = 256
SMALL_SEQ_ROWS = 1024
HGRN_TILE = 128
HGRN_LEVELS = (128, 64, 32, 16, 8, 4, 2)
WIN_KEYS = 3 * ATTN_BLOCK


def _params(*semantics):
    return pltpu.CompilerParams(dimension_semantics=semantics, vmem_limit_bytes=VMEM_LIMIT_BYTES)


def _mod_norm(x, norm_w, shift, scale):
    ms = jnp.mean(x * x, axis=-1, keepdims=True)
    y = x * lax.rsqrt(ms + EPS) * norm_w
    return y * (1.0 + scale) + shift


def _head_ones(width):
    r = lax.broadcasted_iota(jnp.int32, (width, width), 0) // HEAD_DIM
    c = lax.broadcasted_iota(jnp.int32, (width, width), 1) // HEAD_DIM
    return jnp.where(r == c, 1.0, 0.0).astype(BF16)


def _head_rmsnorm(x, w):
    sq = (x * x).astype(BF16)
    ms = jnp.dot(sq, _head_ones(x.shape[-1]), preferred_element_type=F32) * (1.0 / HEAD_DIM)
    return x * lax.rsqrt(ms + EPS) * w


def _ada_kernel(cond_ref, w_ref, b_ref, o_ref):
    cnd = cond_ref[...]
    act = (cnd * jax.nn.sigmoid(cnd)).astype(BF16)
    o_ref[...] = jnp.dot(act, w_ref[...].astype(BF16), preferred_element_type=F32) + b_ref[...]


def _ada_call(cond, w_ada, b_ada):
    n_out = N_MOD * D_MODEL
    return pl.pallas_call(
        _ada_kernel,
        grid=(DEPTH, n_out // ADA_TN),
        in_specs=[
            pl.BlockSpec((ADA_ROWS, D_MODEL), lambda l, j: (0, 0)),
            pl.BlockSpec((None, D_MODEL, ADA_TN), lambda l, j: (l, 0, j)),
            pl.BlockSpec((None, 1, ADA_TN), lambda l, j: (l, 0, j)),
        ],
        out_specs=pl.BlockSpec((None, ADA_ROWS, ADA_TN), lambda l, j: (l, 0, j)),
        out_shape=jax.ShapeDtypeStruct((DEPTH, ADA_ROWS, n_out), F32),
        compiler_params=_params("arbitrary", "arbitrary"),
        name="adaln",
    )(cond, w_ada, b_ada.reshape(DEPTH, 1, n_out))


def _weight_chunk_copies(wgu_hbm, wd_hbm, stage_gu, stage_d, sems, layer, c, slot):
    lo, hi = c * FFN_FC, (c + 1) * FFN_FC
    return (
        pltpu.make_async_copy(wgu_hbm.at[layer, :, lo:hi], stage_gu.at[slot, :, 0:FFN_FC], sems.at[slot, 0]),
        pltpu.make_async_copy(wgu_hbm.at[layer, :, D_FF + lo:D_FF + hi], stage_gu.at[slot, :, FFN_FC:2 * FFN_FC],
                              sems.at[slot, 1]),
        pltpu.make_async_copy(wd_hbm.at[layer, lo:hi, :], stage_d.at[slot], sems.at[slot, 2]),
    )


def _swiglu_chunks(h, wgu_ref, wd_ref, fetch=None):
    n_chunks = D_FF // FFN_FC
    if fetch is not None:
        stage_gu, stage_d = fetch[2], fetch[3]
        for cp in _weight_chunk_copies(*fetch, 0, 0):
            cp.start()
    acc = None
    for c in range(n_chunks):
        if fetch is not None:
            slot = c % 2
            if c + 1 < n_chunks:
                for cp in _weight_chunk_copies(*fetch, c + 1, 1 - slot):
                    cp.start()
            for cp in _weight_chunk_copies(*fetch, c, slot):
                cp.wait()
            wgu_ref[:, c * FFN_FC:(c + 1) * FFN_FC] = stage_gu[slot, :, 0:FFN_FC].astype(BF16)
            wgu_ref[:, D_FF + c * FFN_FC:D_FF + (c + 1) * FFN_FC] = stage_gu[slot, :, FFN_FC:2 * FFN_FC].astype(BF16)
            wd_ref[c * FFN_FC:(c + 1) * FFN_FC, :] = stage_d[slot].astype(BF16)
        g = jnp.dot(h, wgu_ref[:, c * FFN_FC:(c + 1) * FFN_FC], preferred_element_type=F32)
        u = jnp.dot(h, wgu_ref[:, D_FF + c * FFN_FC:D_FF + (c + 1) * FFN_FC], preferred_element_type=F32)
        act = (g * jax.nn.sigmoid(g) * u).astype(BF16)
        part = jnp.dot(act, wd_ref[c * FFN_FC:(c + 1) * FFN_FC, :], preferred_element_type=F32)
        acc = part if acc is None else acc + part
    return acc


def _mod_spec(layer, tiles_a, tm, rows_per_cond):
    def index(i):
        row = jnp.where(i < tiles_a, 0, 1 + ((i - tiles_a) * tm) // rows_per_cond)
        return (layer, row, 0, 0)
    return pl.BlockSpec((None, None, N_MOD, D_MODEL), index)


def _stream_specs(width, tiles_a, tm):
    return (pl.BlockSpec((tm, width), lambda i: (jnp.minimum(i, tiles_a - 1), 0)),
            pl.BlockSpec((tm, width), lambda i: (jnp.maximum(i - tiles_a, 0), 0)))


def _resident(shape, layer):
    return pl.BlockSpec((None,) + shape, lambda i: (layer,) + (0,) * len(shape), pipeline_mode=pl.Buffered(1))


def _first_step_fetches(body, fetch):
    first = pl.program_id(0) == 0
    pl.when(first)(functools.partial(body, fetch))
    pl.when(jnp.logical_not(first))(functools.partial(body, None))


def _ffn_weight_scratch():
    return [
        pltpu.VMEM((D_MODEL, 2 * D_FF), BF16),
        pltpu.VMEM((D_FF, D_MODEL), BF16),
        pltpu.VMEM((2, D_MODEL, 2 * FFN_FC), F32),
        pltpu.VMEM((2, FFN_FC, D_MODEL), F32),
        pltpu.SemaphoreType.DMA((2, 3)),
    ]


def _ffn_in_kernel(xa_ref, xb_ref, mod_ref, n1_ref, nm_ref, wgu_hbm, wd_hbm, win_ref,
                   x1_ref, zu_ref, zh_ref, zq_ref, zkv_ref, wgu_ref, wd_ref, stage_gu, stage_d, sems, *, layer, tiles_a):
    def body(fetch):
        x = jnp.where(pl.program_id(0) < tiles_a, xa_ref[...], xb_ref[...])
        h = _mod_norm(x, n1_ref[...], mod_ref[0:1, :], mod_ref[1:2, :]).astype(BF16)
        x1 = x + 0.5 * mod_ref[2:3, :] * _swiglu_chunks(h, wgu_ref, wd_ref, fetch)
        x1_ref[...] = x1
        h2 = _mod_norm(x1, nm_ref[...], mod_ref[3:4, :], mod_ref[4:5, :]).astype(BF16)
        z = jnp.dot(h2, win_ref[...], preferred_element_type=F32)
        zu_ref[...] = z[:, :FOURIER_W]
        zh_ref[...] = z[:, FOURIER_W:FOURIER_W + ZH_W]
        zq_ref[...] = z[:, FOURIER_W + ZH_W:FOURIER_W + ZH_W + ATTN_W]
        zkv_ref[...] = z[:, FOURIER_W + ZH_W + ATTN_W:]

    _first_step_fetches(body, (wgu_hbm, wd_hbm, stage_gu, stage_d, sems, layer))


def _ffn_in_call(x_a, x_b, mod, n1, nm, w_gu, w_down, w_in, *, layer, rows_per_cond):
    tm = FFN_TM
    tiles_a = x_a.shape[0] // tm
    n = x_a.shape[0] + x_b.shape[0]
    widths = (D_MODEL, FOURIER_W, ZH_W, ATTN_W, 2 * KV_W)
    return pl.pallas_call(
        functools.partial(_ffn_in_kernel, layer=layer, tiles_a=tiles_a),
        grid=(n // tm,),
        in_specs=[
            *_stream_specs(D_MODEL, tiles_a, tm),
            _mod_spec(layer, tiles_a, tm, rows_per_cond),
            pl.BlockSpec((None, 1, D_MODEL), lambda i: (layer, 0, 0)),
            pl.BlockSpec((None, 1, D_MODEL), lambda i: (layer, 0, 0)),
            pl.BlockSpec(memory_space=pl.ANY),
            pl.BlockSpec(memory_space=pl.ANY),
            _resident((D_MODEL, IN_W), layer),
        ],
        out_specs=[pl.BlockSpec((tm, w), lambda i: (i, 0)) for w in widths],
        out_shape=[jax.ShapeDtypeStruct((n, w), F32) for w in widths],
        scratch_shapes=_ffn_weight_scratch(),
        compiler_params=_params("arbitrary"),
        name="ffn_in",
    )(x_a, x_b, mod, n1, nm, w_gu, w_down, w_in)


def _out_ffn_kernel(x_ref, mod_ref, n2_ref, foa_ref, fob_ref, hoa_ref, hob_ref, aoa_ref, aob_ref,
                    wout_ref, wgu_hbm, wd_hbm, oa_ref, ob_ref,
                    wgu_ref, wd_ref, stage_gu, stage_d, sems, *, layer, tiles_a):
    is_a = pl.program_id(0) < tiles_a

    def body(fetch):
        pick = lambda a_ref, b_ref: jnp.where(is_a, a_ref[...], b_ref[...]).astype(BF16)
        y = jnp.dot(pick(foa_ref, fob_ref), wout_ref[0:FOURIER_W, :], preferred_element_type=F32)
        y += jnp.dot(pick(hoa_ref, hob_ref), wout_ref[FOURIER_W:FOURIER_W + HGRN_W, :], preferred_element_type=F32)
        y += jnp.dot(pick(aoa_ref, aob_ref), wout_ref[FOURIER_W + HGRN_W:, :], preferred_element_type=F32)
        xm = x_ref[...] + mod_ref[5:6, :] * y
        h = _mod_norm(xm, n2_ref[...], mod_ref[6:7, :], mod_ref[7:8, :]).astype(BF16)
        out = xm + 0.5 * mod_ref[8:9, :] * _swiglu_chunks(h, wgu_ref, wd_ref, fetch)

        @pl.when(is_a)
        def _():
            oa_ref[...] = out

        @pl.when(jnp.logical_not(is_a))
        def _():
            ob_ref[...] = out

    _first_step_fetches(body, (wgu_hbm, wd_hbm, stage_gu, stage_d, sems, layer))


def _out_ffn_call(x, mod, n2, fo, ho, ao, w_out, w_gu, w_down, *, layer, rows_per_cond):
    tm = FFN_TM
    n_a, n_b = fo[0].shape[0], fo[1].shape[0]
    tiles_a = n_a // tm
    return pl.pallas_call(
        functools.partial(_out_ffn_kernel, layer=layer, tiles_a=tiles_a),
        grid=((n_a + n_b) // tm,),
        in_specs=[
            pl.BlockSpec((tm, D_MODEL), lambda i: (i, 0)),
            _mod_spec(layer, tiles_a, tm, rows_per_cond),
            pl.BlockSpec((None, 1, D_MODEL), lambda i: (layer, 0, 0)),
            *_stream_specs(FOURIER_W, tiles_a, tm),
            *_stream_specs(HGRN_W, tiles_a, tm),
            *_stream_specs(ATTN_W, tiles_a, tm),
            _resident((MIX_W, D_MODEL), layer),
            pl.BlockSpec(memory_space=pl.ANY),
            pl.BlockSpec(memory_space=pl.ANY),
        ],
        out_specs=list(_stream_specs(D_MODEL, tiles_a, tm)),
        out_shape=[jax.ShapeDtypeStruct((n_a, D_MODEL), F32), jax.ShapeDtypeStruct((n_b, D_MODEL), F32)],
        scratch_shapes=_ffn_weight_scratch(),
        compiler_params=_params("arbitrary"),
        name="out_ffn",
    )(x, mod, n2, fo[0], fo[1], ho[0], ho[1], ao[0], ao[1], w_out, w_gu, w_down)


def _dft_tables(seq_len):
    c = np.arange(HEAD_DIM)
    ang_c = 2.0 * np.pi * np.outer(c, c) / HEAD_DIM
    eye = np.eye(FOURIER_W // HEAD_DIM)
    chan = np.concatenate([np.kron(eye, np.cos(ang_c)), np.kron(eye, np.sin(ang_c))], axis=1) / np.sqrt(HEAD_DIM)
    n = np.arange(seq_len)
    ang_l = 2.0 * np.pi * (np.outer(n, n) % seq_len) / seq_len
    pos = np.concatenate([np.cos(ang_l), -np.sin(ang_l)], axis=1) / np.sqrt(seq_len)
    return jnp.asarray(chan, dtype=F32), jnp.asarray(pos, dtype=F32)


def _fourier_rows(u_ref, chan_ref, pos_ref, o_ref, chan_bf, pos_bf, seq_len):
    @pl.when(pl.program_id(0) == 0)
    def _():
        chan_bf[...] = chan_ref[...].astype(BF16)
        pos_bf[...] = pos_ref[...].astype(BF16)

    t = jnp.dot(u_ref[...].astype(BF16), chan_bf[...], preferred_element_type=F32)
    pos = pos_bf[...]
    for r0 in range(0, u_ref.shape[0], seq_len):
        tb = t[r0:r0 + seq_len, :]
        stacked = jnp.concatenate([tb[:, :FOURIER_W], tb[:, FOURIER_W:]], axis=0).astype(BF16)
        o_ref[r0:r0 + seq_len, :] = jnp.dot(pos, stacked, preferred_element_type=F32)


def _fourier_table_specs(seq_len, index):
    specs = [pl.BlockSpec((FOURIER_W, 2 * FOURIER_W), index, pipeline_mode=pl.Buffered(1)),
             pl.BlockSpec((seq_len, 2 * seq_len), index, pipeline_mode=pl.Buffered(1))]
    scratch = [pltpu.VMEM((FOURIER_W, 2 * FOURIER_W), BF16), pltpu.VMEM((seq_len, 2 * seq_len), BF16)]
    return specs, scratch


def _split3(x):
    hi = x.astype(BF16)
    r1 = x - hi.astype(F32)
    mid = r1.astype(BF16)
    lo = (r1 - mid.astype(F32)).astype(BF16)
    return hi, mid, lo


def _level_mid(b, s, pos):
    tile, lanes = b.shape
    if s >= SUBLANES:
        b3 = b.reshape(tile // s, s, lanes)
        return jnp.broadcast_to(b3[:, pos:pos + 1, :], b3.shape).reshape(tile, lanes)
    assert s == 4
    b3 = b.reshape(tile // SUBLANES, SUBLANES, lanes)
    sub = lax.broadcasted_iota(jnp.int32, (1, SUBLANES, 1), 1)
    return jnp.where(sub < 4, b3[:, pos:pos + 1, :], b3[:, 4 + pos:5 + pos, :]).reshape(tile, lanes)


def _hgrn_kernel(*refs, seq_len, layer, has_s0, emit_state):
    refs = list(refs)
    zh_ref, lb_ref, nw_ref, tri_ref = refs[:4]
    pos = 4
    s0_ref = sf_ref = None
    if has_s0:
        s0_ref = refs[pos]
        pos += 1
    if emit_state:
        pos += 1
    ho_ref = refs[pos]
    pos += 1
    if emit_state:
        sf_ref = refs[pos]
        pos += 1
    qin_ref, dec_ref, kvt_ref, stf_ref, stb_ref, oacc_ref = refs[pos:]

    tile = HGRN_TILE
    width = HGRN_W
    n_tiles = seq_len // tile
    c_q, c_v, c_z, c_g = 0, width, 2 * width, 4 * width

    raw = lb_ref[...]
    e = jnp.exp(raw - jnp.max(raw, axis=0, keepdims=True))
    soft = e / jnp.sum(e, axis=0, keepdims=True)
    cum = soft[0]
    for l in range(1, layer + 1):
        cum = cum + soft[l]
    lbd = cum - soft[0]
    lb_floor = jnp.maximum(lbd, LB_FLOOR)
    one_m_lb = 1.0 - lbd

    lane_bwd1 = lax.broadcasted_iota(jnp.int32, (1, HG_2), 1) >= width
    lane_bwd = lax.broadcasted_iota(jnp.int32, (tile, HG_2), 1) >= width
    lane_bwd_s = lax.broadcasted_iota(jnp.int32, (HEAD_DIM, HG_2), 1) >= width
    row = lax.broadcasted_iota(jnp.int32, (tile, 1), 0)
    pair_xor = (lax.broadcasted_iota(jnp.int32, (tile, HGRN_HEADS * tile), 0)
                ^ (lax.broadcasted_iota(jnp.int32, (tile, HGRN_HEADS * tile), 1) % tile))
    ev_mask = (lax.broadcasted_iota(jnp.int32, (HGRN_HEADS * tile, width), 0) // tile
               == lax.broadcasted_iota(jnp.int32, (HGRN_HEADS * tile, width), 1) // HEAD_DIM)
    bd_mask = (lax.broadcasted_iota(jnp.int32, (width, HG_2), 0) // HEAD_DIM
               == (lax.broadcasted_iota(jnp.int32, (width, HG_2), 1) // HEAD_DIM) % HGRN_HEADS)
    lane_head = (lax.broadcasted_iota(jnp.int32, (HEAD_DIM, HG_2), 1) // HEAD_DIM) % HGRN_HEADS
    ones_bd = _head_ones(width)

    def tile_body(t, carry):
        rows = pl.ds(pl.multiple_of(t * tile, tile), tile)
        q = zh_ref[rows, c_q:c_q + width] * (HEAD_DIM ** -0.5)
        v = zh_ref[rows, c_v:c_v + width]
        z = zh_ref[rows, c_z:c_z + HG_2]
        ez = jnp.exp(-jnp.abs(z))
        r = 1.0 / (1.0 + ez)
        tr = ez * r
        pos_z = z >= 0.0
        f = lb_floor + one_m_lb * jnp.where(pos_z, r, tr)
        logf = jnp.log(f)
        kk = one_m_lb * jnp.where(pos_z, tr, r)

        pieces = jnp.concatenate(_split3(logf), axis=1)
        acc = jnp.dot(tri_ref[...], pieces, preferred_element_type=F32)
        prefix = acc[:, :HG_2] + acc[:, HG_2:2 * HG_2] + acc[:, 2 * HG_2:]
        total = prefix[tile - 1:tile, :]
        b = jnp.where(lane_bwd, total - prefix + logf, prefix)
        kk_f, kk_b = kk[:, :width], kk[:, width:]

        a0 = jnp.dot((q * (kk_f + kk_b)).astype(BF16), ones_bd, preferred_element_type=F32)
        o = a0 * v

        b2 = b * LOG2_E
        bf2, bb2 = b2[:, :width], b2[:, width:]
        scores = None
        for s in HGRN_LEVELS:
            second = (row % s) >= s // 2
            k_role = jnp.where(second, kk_b, kk_f)
            if s == 2:
                q_op = q * jnp.where(second, f[:, :width], f[:, width:])
                k_op = k_role
            else:
                mid_f = _level_mid(bf2, s, s // 2 - 1)
                mid_b = _level_mid(bb2, s, s // 2)
                e_q = jnp.where(second, bf2, bb2) - jnp.where(second, mid_f, mid_b)
                e_k = jnp.where(second, mid_b, mid_f) - jnp.where(second, bb2, bf2)
                q_op = q * jnp.exp2(e_q)
                k_op = k_role * jnp.exp2(e_k)
            qt = q_op.astype(BF16)
            kt = k_op.T.astype(BF16)
            sc = jnp.concatenate(
                [jnp.dot(qt[:, h * HEAD_DIM:(h + 1) * HEAD_DIM], kt[h * HEAD_DIM:(h + 1) * HEAD_DIM, :],
                         preferred_element_type=F32) for h in range(HGRN_HEADS)], axis=1)
            scores = sc if scores is None else jnp.where(pair_xor < s, sc, scores)
        scores = jnp.where(pair_xor == 0, 0.0, scores)
        ev = jnp.where(ev_mask, jnp.concatenate([v] * HGRN_HEADS, axis=0), 0.0).astype(BF16)
        o = o + jnp.dot(scores.astype(BF16), ev, preferred_element_type=F32)
        oacc_ref[rows, :] = o

        q2 = jnp.concatenate([q, q], axis=1)
        qin_ref[rows, :] = (q2 * jnp.exp(b)).astype(BF16)
        kout = (kk * jnp.exp(total - b)).astype(BF16)
        dec_ref[t] = jnp.exp(total)
        full = jnp.dot(v.T.astype(BF16), kout, preferred_element_type=F32)
        kvt = full[0:HEAD_DIM, :]
        for hh in range(1, HGRN_HEADS):
            kvt = jnp.where(lane_head == hh, full[hh * HEAD_DIM:(hh + 1) * HEAD_DIM, :], kvt)
        kvt_ref[t] = kvt
        return carry

    lax.fori_loop(0, n_tiles, tile_body, 0, unroll=2)

    if has_s0:
        st0 = jnp.concatenate([s0_ref[d, h] for d in range(2) for h in range(HGRN_HEADS)], axis=0).T
    else:
        st0 = jnp.zeros((HEAD_DIM, HG_2), F32)

    def scan_step(i, st):
        tf, tb = i, n_tiles - 1 - i
        stf_ref[tf] = st
        stb_ref[tb] = st
        dec = jnp.where(lane_bwd1, dec_ref[tb], dec_ref[tf])
        kv = jnp.where(lane_bwd_s, kvt_ref[tb], kvt_ref[tf])
        return st * dec + kv

    st_fin = lax.fori_loop(0, n_tiles, scan_step, st0)
    if emit_state:
        st_t = st_fin.T
        for d in range(2):
            for h in range(HGRN_HEADS):
                r0 = (HGRN_HEADS * d + h) * HEAD_DIM
                sf_ref[d, h] = st_t[r0:r0 + HEAD_DIM, :]

    def out_tile(t):
        rows = pl.ds(pl.multiple_of(t * tile, tile), tile)
        st = jnp.where(lane_bwd_s, stb_ref[t], stf_ref[t])
        bd = jnp.where(bd_mask, jnp.concatenate([st] * HGRN_HEADS, axis=0), 0.0).T.astype(BF16)
        o = oacc_ref[rows, :] + jnp.dot(qin_ref[rows, :], bd, preferred_element_type=F32)
        hg = zh_ref[rows, c_g:c_g + width]
        ho_ref[rows, :] = _head_rmsnorm(o, nw_ref[...]) * (hg * jax.nn.sigmoid(hg))

    def out_body(i, carry):
        out_tile(2 * i)
        out_tile(2 * i + 1)
        return carry

    assert n_tiles % 2 == 0
    lax.fori_loop(0, n_tiles // 2, out_body, 0)


def _hgrn_tri():
    i = np.arange(HGRN_TILE)
    return jnp.asarray((i[None, :] <= i[:, None]).astype(np.float32), dtype=BF16)


def _hgrn_call(zh, lb2, norm_w, s0, final_states, *, seq_len, layer, span):
    row0, n = span
    blk0 = row0 // seq_len
    n_seq = n // seq_len
    has_s0 = s0 is not None
    emit_state = final_states is not None
    state_block = (2, HGRN_HEADS, HEAD_DIM, HEAD_DIM)
    n_tiles = seq_len // HGRN_TILE
    in_specs = [
        pl.BlockSpec((seq_len, ZH_W), lambda b: (b + blk0, 0)),
        pl.BlockSpec((DEPTH, 1, HG_2), lambda b: (0, 0, 0)),
        pl.BlockSpec((None, 1, HGRN_W), lambda b: (layer, 0, 0)),
        pl.BlockSpec((HGRN_TILE, HGRN_TILE), lambda b: (0, 0)),
    ]
    args = [zh, lb2, norm_w, _hgrn_tri()]
    if has_s0:
        in_specs.append(pl.BlockSpec((None, None) + state_block, lambda b: (b, layer, 0, 0, 0, 0)))
        args.append(s0)
    out_specs = [pl.BlockSpec((seq_len, HGRN_W), lambda b: (b, 0))]
    out_shape = [jax.ShapeDtypeStruct((n, HGRN_W), F32)]
    aliases = {}
    if emit_state:
        aliases[len(args)] = 1
        in_specs.append(pl.BlockSpec(memory_space=pl.ANY))
        args.append(final_states)
        out_specs.append(pl.BlockSpec((None, None) + state_block, lambda b: (b, layer, 0, 0, 0, 0)))
        out_shape.append(jax.ShapeDtypeStruct(final_states.shape, F32))
    return pl.pallas_call(
        functools.partial(_hgrn_kernel, seq_len=seq_len, layer=layer, has_s0=has_s0, emit_state=emit_state),
        grid=(n_seq,),
        in_specs=in_specs,
        out_specs=out_specs,
        out_shape=out_shape,
        input_output_aliases=aliases,
        scratch_shapes=[
            pltpu.VMEM((seq_len, HG_2), BF16),
            pltpu.VMEM((n_tiles, 1, HG_2), F32),
            pltpu.VMEM((n_tiles, HEAD_DIM, HG_2), F32),
            pltpu.VMEM((n_tiles, HEAD_DIM, HG_2), F32),
            pltpu.VMEM((n_tiles, HEAD_DIM, HG_2), F32),
            pltpu.VMEM((seq_len, HGRN_W), F32),
        ],
        compiler_params=_params("arbitrary"),
        name="hgrn",
    )(*args)


def _sink_row(sink_ref, kv_head, rows):
    grp = lax.broadcasted_iota(jnp.int32, (1, ATTN_GROUP * rows), 1) // rows
    base = kv_head * ATTN_GROUP
    out = jnp.broadcast_to(sink_ref[:, base:base + 1], (1, ATTN_GROUP * rows))
    for g in range(1, ATTN_GROUP):
        out = jnp.where(grp == g, sink_ref[:, base + g:base + g + 1], out)
    return out


def _group_queries_t(qt, kv_head):
    base = kv_head * ATTN_GROUP
    return jnp.concatenate([qt[(base + g) * HEAD_DIM:(base + g + 1) * HEAD_DIM, :] for g in range(ATTN_GROUP)], axis=1)


def _ctx_attn_kernel(zq_ref, zkv_ref, zu_ref, qn_ref, kn_ref, sink_ref, chan_ref, pos_ref, kc_all_ref, vc_all_ref,
                     ao_ref, fo_ref, kc_ref, vc_ref, chan_bf, pos_bf, *, seq_len):
    del kc_all_ref, vc_all_ref
    _fourier_rows(zu_ref, chan_ref, pos_ref, fo_ref, chan_bf, pos_bf, seq_len)
    rows = seq_len
    q_all = _head_rmsnorm(zq_ref[...], qn_ref[...]) * (HEAD_DIM ** -0.5)
    k_all = _head_rmsnorm(zkv_ref[:, :KV_W], kn_ref[...])
    v_all = zkv_ref[:, KV_W:]
    kc_ref[...] = k_all.reshape(kc_ref.shape)
    vc_ref[...] = v_all.reshape(vc_ref.shape)
    for r0 in range(0, zq_ref.shape[0], seq_len):
        qt = q_all[r0:r0 + rows, :].T.astype(BF16)
        kb = k_all[r0:r0 + rows, :].astype(BF16)
        vt = v_all[r0:r0 + rows, :].T.astype(BF16)
        heads = range(ATTN_KV_HEADS)
        hsl = [slice(kh * HEAD_DIM, (kh + 1) * HEAD_DIM) for kh in heads]
        s = [jnp.dot(kb[:, hsl[kh]], _group_queries_t(qt, kh), preferred_element_type=F32) for kh in heads]
        sink = [_sink_row(sink_ref, kh, rows) for kh in heads]
        m = [jnp.maximum(jnp.max(s[kh], axis=0, keepdims=True), sink[kh]) for kh in heads]
        p = [jnp.exp(s[kh] - m[kh]) for kh in heads]
        denom = [jnp.sum(p[kh], axis=0, keepdims=True) + jnp.exp(sink[kh] - m[kh]) for kh in heads]
        out_t = []
        for kh in heads:
            o_t = jnp.dot(vt[hsl[kh], :], p[kh].astype(BF16), preferred_element_type=F32) * (1.0 / denom[kh])
            out_t.extend(o_t[:, g * rows:(g + 1) * rows] for g in range(ATTN_GROUP))
        ao_ref[r0:r0 + rows, :] = jnp.concatenate(out_t, axis=0).T


def _ctx_attn_call(zq, zkv, zu, q_norm, k_norm, sink, k_cache, v_cache, *, seq_len, layer, span):
    row0, n = span
    rows = max(seq_len, SMALL_SEQ_ROWS)
    blk0 = row0 // rows
    seqs = rows // seq_len
    cache_spec = pl.BlockSpec((seqs, None, seq_len, KV_W), lambda b: (b, layer, 0, 0))
    table_specs, table_scratch = _fourier_table_specs(seq_len, lambda b: (0, 0))
    return pl.pallas_call(
        functools.partial(_ctx_attn_kernel, seq_len=seq_len),
        grid=(n // rows,),
        in_specs=[
            pl.BlockSpec((rows, ATTN_W), lambda b: (b + blk0, 0)),
            pl.BlockSpec((rows, 2 * KV_W), lambda b: (b + blk0, 0)),
            pl.BlockSpec((rows, FOURIER_W), lambda b: (b + blk0, 0)),
            pl.BlockSpec((None, 1, ATTN_W), lambda b: (layer, 0, 0)),
            pl.BlockSpec((None, 1, KV_W), lambda b: (layer, 0, 0)),
            pl.BlockSpec((None, 1, ATTN_HEADS), lambda b: (layer, 0, 0)),
            *table_specs,
            pl.BlockSpec(memory_space=pl.ANY),
            pl.BlockSpec(memory_space=pl.ANY),
        ],
        out_specs=[pl.BlockSpec((rows, ATTN_W), lambda b: (b, 0)), pl.BlockSpec((rows, FOURIER_W), lambda b: (b, 0)),
                   cache_spec, cache_spec],
        out_shape=[
            jax.ShapeDtypeStruct((n, ATTN_W), F32),
            jax.ShapeDtypeStruct((n, FOURIER_W), F32),
            jax.ShapeDtypeStruct(k_cache.shape, F32),
            jax.ShapeDtypeStruct(v_cache.shape, F32),
        ],
        input_output_aliases={8: 2, 9: 3},
        scratch_shapes=table_scratch,
        compiler_params=_params("arbitrary"),
        name="ctx_attn",
    )(zq, zkv, zu, q_norm, k_norm, sink, *_dft_tables(seq_len), k_cache, v_cache)


def _rope_tables(seq_len, n_heads):
    t = np.arange(seq_len)
    row, col = t // GRID_W, t % GRID_W
    half = HEAD_DIM // 2
    inv = (ROPE_BASE ** (-np.arange(0, half, 2, dtype=np.float32) / half)).astype(np.float32)

    def one(pos):
        ang = pos.astype(np.float32)[:, None] * inv[None]
        c, s = np.cos(ang), np.sin(ang)
        return np.concatenate([c, c], axis=1), np.concatenate([-s, s], axis=1)

    cr, sr = one(row)
    cc, sc = one(col)
    cos = np.tile(np.concatenate([cr, cc], axis=1), (1, n_heads)).astype(np.float32)
    sin = np.tile(np.concatenate([sr, sc], axis=1), (1, n_heads)).astype(np.float32)
    return jnp.asarray(cos), jnp.asarray(sin)


def _rope(x, cos, sin_signed):
    width = x.shape[-1]
    lane = lax.broadcasted_iota(jnp.int32, x.shape, 1)
    quarter = HEAD_DIM // 4
    up = pltpu.roll(x, width - quarter, axis=1)
    down = pltpu.roll(x, quarter, axis=1)
    partner = jnp.where((lane % (2 * quarter)) < quarter, up, down)
    return x * cos + partner * sin_signed


def _lat_attn_kernel(zq_ref, zkv_ref, zu_ref, kc_ref, vc_ref, qn_ref, kn_ref, sink_ref,
                     cosq_ref, sinq_ref, cosk_ref, sink_tab_ref, band_ref, chan_ref, pos_ref,
                     ao_ref, fo_ref, kb_ref, kcb_ref, vct_ref, chan_bf, pos_bf, *, seq_len):
    rows = ATTN_BLOCK
    _fourier_rows(zu_ref, chan_ref, pos_ref, fo_ref, chan_bf, pos_bf, seq_len)

    k_all = _head_rmsnorm(zkv_ref[:, 0:KV_W], kn_ref[...])
    kb_ref[...] = _rope(k_all, cosk_ref[...], sink_tab_ref[...]).astype(BF16)
    kcb_ref[...] = kc_ref[...].astype(BF16)
    vct_ref[...] = vc_ref[...].T.astype(BF16)

    def query_block(blk, carry):
        qrows = pl.ds(pl.multiple_of(blk * ATTN_BLOCK, ATTN_BLOCK), ATTN_BLOCK)
        q = _head_rmsnorm(zq_ref[qrows, :], qn_ref[...])
        q = _rope(q, cosq_ref[qrows, :], sinq_ref[qrows, :]) * (HEAD_DIM ** -0.5)
        start = jnp.clip((blk - 1) * ATTN_BLOCK, 0, seq_len - WIN_KEYS)
        start = pl.multiple_of(start, ATTN_BLOCK)
        vwin = zkv_ref[pl.ds(start, WIN_KEYS), KV_W:2 * KV_W]

        band_bias = band_ref[blk - start // ATTN_BLOCK]

        qt = q.T.astype(BF16)
        kwb = kb_ref[pl.ds(start, WIN_KEYS), :]
        kcb = kcb_ref[...]
        vwt = vwin.T.astype(BF16)
        vct = vct_ref[...]
        heads = range(ATTN_KV_HEADS)
        hsl = [slice(kh * HEAD_DIM, (kh + 1) * HEAD_DIM) for kh in heads]
        qg = [_group_queries_t(qt, kh) for kh in heads]
        s_loc = [jnp.dot(kwb[:, hsl[kh]], qg[kh], preferred_element_type=F32) + band_bias for kh in heads]
        s_ctx = [jnp.dot(kcb[:, hsl[kh]], qg[kh], preferred_element_type=F32) for kh in heads]
        sink = [_sink_row(sink_ref, kh, rows) for kh in heads]
        m = [jnp.maximum(jnp.maximum(jnp.max(s_loc[kh], axis=0, keepdims=True),
                                     jnp.max(s_ctx[kh], axis=0, keepdims=True)), sink[kh]) for kh in heads]
        p_loc = [jnp.exp(s_loc[kh] - m[kh]) for kh in heads]
        p_ctx = [jnp.exp(s_ctx[kh] - m[kh]) for kh in heads]
        denom = [jnp.sum(p_loc[kh], axis=0, keepdims=True) + jnp.sum(p_ctx[kh], axis=0, keepdims=True)
                 + jnp.exp(sink[kh] - m[kh]) for kh in heads]
        out_t = []
        for kh in heads:
            o_t = (jnp.dot(vwt[hsl[kh], :], p_loc[kh].astype(BF16), preferred_element_type=F32)
                   + jnp.dot(vct[hsl[kh], :], p_ctx[kh].astype(BF16), preferred_element_type=F32)) * (1.0 / denom[kh])
            out_t.extend(o_t[:, g * rows:(g + 1) * rows] for g in range(ATTN_GROUP))
        ao_ref[qrows, :] = jnp.concatenate(out_t, axis=0).T
        return carry

    lax.fori_loop(0, seq_len // ATTN_BLOCK, query_block, 0, unroll=2)


def _lat_attn_call(zq, zkv, zu, cache_k, cache_v, q_norm, k_norm, sink, *, seq_len, layer, span):
    row0, n = span
    seq0 = row0 // seq_len
    past = cache_k.shape[2]
    cosq, sinq = _rope_tables(seq_len, ATTN_HEADS)
    cosk, sink_tab = _rope_tables(seq_len, ATTN_KV_HEADS)
    kr = np.arange(WIN_KEYS)[:, None]
    qi = np.arange(ATTN_GROUP * ATTN_BLOCK)[None, :] % ATTN_BLOCK
    band = np.stack([np.where(np.abs(qi + off * ATTN_BLOCK - kr) <= WINDOW, 0.0, NEG_BIG) for off in range(3)])
    band = jnp.asarray(band, dtype=F32)
    table_specs, table_scratch = _fourier_table_specs(seq_len, lambda b: (0, 0))
    return pl.pallas_call(
        functools.partial(_lat_attn_kernel, seq_len=seq_len),
        grid=(n // seq_len,),
        in_specs=[
            pl.BlockSpec((seq_len, ATTN_W), lambda b: (seq0 + b, 0)),
            pl.BlockSpec((seq_len, 2 * KV_W), lambda b: (seq0 + b, 0)),
            pl.BlockSpec((seq_len, FOURIER_W), lambda b: (seq0 + b, 0)),
            pl.BlockSpec((None, None, past, KV_W), lambda b: (b, layer, 0, 0)),
            pl.BlockSpec((None, None, past, KV_W), lambda b: (b, layer, 0, 0)),
            pl.BlockSpec((None, 1, ATTN_W), lambda b: (layer, 0, 0)),
            pl.BlockSpec((None, 1, KV_W), lambda b: (layer, 0, 0)),
            pl.BlockSpec((None, 1, ATTN_HEADS), lambda b: (layer, 0, 0)),
            pl.BlockSpec((seq_len, ATTN_W), lambda b: (0, 0), pipeline_mode=pl.Buffered(1)),
            pl.BlockSpec((seq_len, ATTN_W), lambda b: (0, 0), pipeline_mode=pl.Buffered(1)),
            pl.BlockSpec((seq_len, KV_W), lambda b: (0, 0), pipeline_mode=pl.Buffered(1)),
            pl.BlockSpec((seq_len, KV_W), lambda b: (0, 0), pipeline_mode=pl.Buffered(1)),
            pl.BlockSpec((3, WIN_KEYS, ATTN_GROUP * ATTN_BLOCK), lambda b: (0, 0, 0), pipeline_mode=pl.Buffered(1)),
            *table_specs,
        ],
        out_specs=[pl.BlockSpec((seq_len, ATTN_W), lambda b: (b, 0)), pl.BlockSpec((seq_len, FOURIER_W), lambda b: (b, 0))],
        out_shape=[jax.ShapeDtypeStruct((n, ATTN_W), F32), jax.ShapeDtypeStruct((n, FOURIER_W), F32)],
        scratch_shapes=[
            pltpu.VMEM((seq_len, KV_W), BF16),
            pltpu.VMEM((past, KV_W), BF16),
            pltpu.VMEM((KV_W, past), BF16),
            *table_scratch,
        ],
        compiler_params=_params("arbitrary"),
        name="lat_attn",
    )(zq, zkv, zu, cache_k, cache_v, q_norm, k_norm, sink, cosq, sinq, cosk, sink_tab, band, *_dft_tables(seq_len))


def kernel(x_prompt, x_sample, c, cache_attn_k, cache_attn_v, state_hgrn, c_ctx, w_ada, b_ada, norm_ffn1, norm_mix, norm_ffn2, ffn1_w_gate_up, ffn1_w_down, ffn2_w_gate_up, ffn2_w_down, w_in, w_out, hgrn_lower_bounds, hgrn_norm, q_norm, k_norm, attn_sink):
    batch, seq, _ = x_prompt.shape
    dec_batch, dec_seq, _ = x_sample.shape
    past = cache_attn_k.shape[2]
    assert dec_batch + 1 <= ADA_ROWS

    cond = jnp.concatenate([c_ctx[None], c, jnp.zeros((ADA_ROWS - 1 - dec_batch, D_MODEL), F32)], axis=0)
    mod = _ada_call(cond, w_ada, b_ada).reshape(DEPTH, ADA_ROWS, N_MOD, D_MODEL)

    bf = lambda w: w.astype(BF16)
    w_gu1, w_d1, w_gu2, w_d2 = ffn1_w_gate_up, ffn1_w_down, ffn2_w_gate_up, ffn2_w_down
    w_in_b, w_out_b = bf(w_in), bf(w_out)
    row3 = lambda w: w.reshape(DEPTH, 1, w.shape[-1])
    n1, nm, n2 = row3(norm_ffn1), row3(norm_mix), row3(norm_ffn2)
    hn = row3(hgrn_norm)
    qn = row3(jnp.tile(q_norm, (1, ATTN_HEADS)))
    kn = row3(jnp.tile(k_norm, (1, ATTN_KV_HEADS)))
    sink = row3(attn_sink)
    cache_k = cache_attn_k.reshape(dec_batch, DEPTH, past, KV_W)
    cache_v = cache_attn_v.reshape(dec_batch, DEPTH, past, KV_W)
    lb2 = hgrn_lower_bounds.reshape(DEPTH, 1, HG_2)

    n_ctx, n_lat = batch * seq, dec_batch * dec_seq
    span_c, span_l = (0, n_ctx), (n_ctx, n_lat)
    x_c, x_l = x_prompt.reshape(n_ctx, D_MODEL), x_sample.reshape(n_lat, D_MODEL)
    new_k = jnp.zeros((batch, DEPTH, seq, KV_W), F32)
    new_v = jnp.zeros((batch, DEPTH, seq, KV_W), F32)
    new_s = jnp.zeros((batch, DEPTH, 2, HGRN_HEADS, HEAD_DIM, HEAD_DIM), F32)
    for layer in range(DEPTH):
        x1, zu, zh, zq, zkv = _ffn_in_call(x_c, x_l, mod, n1, nm, w_gu1, w_d1, w_in_b, layer=layer, rows_per_cond=dec_seq)
        ho_c, new_s = _hgrn_call(zh, lb2, hn, None, new_s, seq_len=seq, layer=layer, span=span_c)
        (ho_l,) = _hgrn_call(zh, lb2, hn, state_hgrn, None, seq_len=dec_seq, layer=layer, span=span_l)
        ao_c, fo_c, new_k, new_v = _ctx_attn_call(zq, zkv, zu, qn, kn, sink, new_k, new_v,
                                                  seq_len=seq, layer=layer, span=span_c)
        ao_l, fo_l = _lat_attn_call(zq, zkv, zu, cache_k, cache_v, qn, kn, sink, seq_len=dec_seq, layer=layer, span=span_l)
        x_c, x_l = _out_ffn_call(x1, mod, n2, (fo_c, fo_l), (ho_c, ho_l), (ao_c, ao_l), w_out_b, w_gu2, w_d2,
                                 layer=layer, rows_per_cond=dec_seq)

    y_p = x_c.reshape(batch, seq, D_MODEL)
    y_s = x_l.reshape(dec_batch, dec_seq, D_MODEL)
    cache_shape = (batch, DEPTH, seq, ATTN_KV_HEADS, HEAD_DIM)
    return (y_p, y_s, new_k.reshape(cache_shape), new_v.reshape(cache_shape), new_s)
```

```python
import functools

import numpy as np
import jax
import jax.numpy as jnp
from jax import lax
from jax.experimental import pallas as pl
from jax.experimental.pallas import tpu as pltpu

F32 = jnp.float32
BF16 = jnp.bfloat16

D_MODEL = 1024
DEPTH = 2
HEAD_DIM = 64
FOURIER_W = 256
HGRN_W = 256
HGRN_HEADS = 4
ATTN_W = 512
ATTN_HEADS = 8
ATTN_KV_HEADS = 2
ATTN_GROUP = 4
KV_W = 128
MIX_W = 1024
WINDOW = 128
ATTN_BLOCK = 128
D_FF = 2816
GRID_W = 64
ROPE_BASE = 10000.0
EPS = 1e-6
N_MOD = 9
NEG_BIG = -1e30
LB_FLOOR = 1e-30
LOG2_E = 1.4426950408889634

HG_2 = 2 * HGRN_W
ZH_W = 5 * HGRN_W
IN_W = FOURIER_W + ZH_W + ATTN_W + 2 * KV_W

SUBLANES = 8
VMEM_LIMIT_BYTES = 63 * 1024 * 1024

ADA_ROWS = 8
ADA_TN = 1536
FFN_TM = 512
FFN_FC = 256
SMALL_SEQ_ROWS = 1024
HGRN_TILE = 128
HGRN_LEVELS = (128, 64, 32, 16, 8, 4, 2)
WIN_KEYS = 3 * ATTN_BLOCK


def _params(*semantics):
    return pltpu.CompilerParams(dimension_semantics=semantics, vmem_limit_bytes=VMEM_LIMIT_BYTES)


def _mod_norm(x, norm_w, shift, scale):
    ms = jnp.mean(x * x, axis=-1, keepdims=True)
    y = x * lax.rsqrt(ms + EPS) * norm_w
    return y * (1.0 + scale) + shift


def _head_ones(width):
    r = lax.broadcasted_iota(jnp.int32, (width, width), 0) // HEAD_DIM
    c = lax.broadcasted_iota(jnp.int32, (width, width), 1) // HEAD_DIM
    return jnp.where(r == c, 1.0, 0.0).astype(BF16)


def _head_rmsnorm(x, w):
    sq = (x * x).astype(BF16)
    ms = jnp.dot(sq, _head_ones(x.shape[-1]), preferred_element_type=F32) * (1.0 / HEAD_DIM)
    return x * lax.rsqrt(ms + EPS) * w


def _ada_kernel(cond_ref, w_ref, b_ref, o_ref):
    cnd = cond_ref[...]
    act = (cnd * jax.nn.sigmoid(cnd)).astype(BF16)
    o_ref[...] = jnp.dot(act, w_ref[...].astype(BF16), preferred_element_type=F32) + b_ref[...]


def _ada_call(cond, w_ada, b_ada):
    n_out = N_MOD * D_MODEL
    return pl.pallas_call(
        _ada_kernel,
        grid=(DEPTH, n_out // ADA_TN),
        in_specs=[
            pl.BlockSpec((ADA_ROWS, D_MODEL), lambda l, j: (0, 0)),
            pl.BlockSpec((None, D_MODEL, ADA_TN), lambda l, j: (l, 0, j)),
            pl.BlockSpec((None, 1, ADA_TN), lambda l, j: (l, 0, j)),
        ],
        out_specs=pl.BlockSpec((None, ADA_ROWS, ADA_TN), lambda l, j: (l, 0, j)),
        out_shape=jax.ShapeDtypeStruct((DEPTH, ADA_ROWS, n_out), F32),
        compiler_params=_params("arbitrary", "arbitrary"),
        name="adaln",
    )(cond, w_ada, b_ada.reshape(DEPTH, 1, n_out))


def _weight_chunk_copies(wgu_hbm, wd_hbm, stage_gu, stage_d, sems, layer, c, slot):
    lo, hi = c * FFN_FC, (c + 1) * FFN_FC
    return (
        pltpu.make_async_copy(wgu_hbm.at[layer, :, lo:hi], stage_gu.at[slot, :, 0:FFN_FC], sems.at[slot, 0]),
        pltpu.make_async_copy(wgu_hbm.at[layer, :, D_FF + lo:D_FF + hi], stage_gu.at[slot, :, FFN_FC:2 * FFN_FC],
                              sems.at[slot, 1]),
        pltpu.make_async_copy(wd_hbm.at[layer, lo:hi, :], stage_d.at[slot], sems.at[slot, 2]),
    )


def _swiglu_chunks(h, wgu_ref, wd_ref, fetch=None):
    n_chunks = D_FF // FFN_FC
    if fetch is not None:
        stage_gu, stage_d = fetch[2], fetch[3]
        for cp in _weight_chunk_copies(*fetch, 0, 0):
            cp.start()
    acc = None
    for c in range(n_chunks):
        if fetch is not None:
            slot = c % 2
            if c + 1 < n_chunks:
                for cp in _weight_chunk_copies(*fetch, c + 1, 1 - slot):
                    cp.start()
            for cp in _weight_chunk_copies(*fetch, c, slot):
                cp.wait()
            wgu_ref[:, c * FFN_FC:(c + 1) * FFN_FC] = stage_gu[slot, :, 0:FFN_FC].astype(BF16)
            wgu_ref[:, D_FF + c * FFN_FC:D_FF + (c + 1) * FFN_FC] = stage_gu[slot, :, FFN_FC:2 * FFN_FC].astype(BF16)
            wd_ref[c * FFN_FC:(c + 1) * FFN_FC, :] = stage_d[slot].astype(BF16)
        g = jnp.dot(h, wgu_ref[:, c * FFN_FC:(c + 1) * FFN_FC], preferred_element_type=F32)
        u = jnp.dot(h, wgu_ref[:, D_FF + c * FFN_FC:D_FF + (c + 1) * FFN_FC], preferred_element_type=F32)
        act = (g * jax.nn.sigmoid(g) * u).astype(BF16)
        part = jnp.dot(act, wd_ref[c * FFN_FC:(c + 1) * FFN_FC, :], preferred_element_type=F32)
        acc = part if acc is None else acc + part
    return acc


def _mod_spec(layer, tiles_a, tm, rows_per_cond):
    def index(i):
        row = jnp.where(i < tiles_a, 0, 1 + ((i - tiles_a) * tm) // rows_per_cond)
        return (layer, row, 0, 0)
    return pl.BlockSpec((None, None, N_MOD, D_MODEL), index)


def _stream_specs(width, tiles_a, tm):
    return (pl.BlockSpec((tm, width), lambda i: (jnp.minimum(i, tiles_a - 1), 0)),
            pl.BlockSpec((tm, width), lambda i: (jnp.maximum(i - tiles_a, 0), 0)))


def _resident(shape, layer):
    return pl.BlockSpec((None,) + shape, lambda i: (layer,) + (0,) * len(shape), pipeline_mode=pl.Buffered(1))


def _first_step_fetches(body, fetch):
    first = pl.program_id(0) == 0
    pl.when(first)(functools.partial(body, fetch))
    pl.when(jnp.logical_not(first))(functools.partial(body, None))


def _ffn_weight_scratch():
    return [
        pltpu.VMEM((D_MODEL, 2 * D_FF), BF16),
        pltpu.VMEM((D_FF, D_MODEL), BF16),
        pltpu.VMEM((2, D_MODEL, 2 * FFN_FC), F32),
        pltpu.VMEM((2, FFN_FC, D_MODEL), F32),
        pltpu.SemaphoreType.DMA((2, 3)),
    ]


def _ffn_in_kernel(*refs, layer, tiles_a, n_x):
    x_refs = refs[:n_x]
    (mod_ref, n1_ref, nm_ref, wgu_hbm, wd_hbm, win_ref, x1_ref, zu_ref, zh_ref, zq_ref, zkv_ref,
     wgu_ref, wd_ref, stage_gu, stage_d, sems) = refs[n_x:]

    def body(fetch):
        if n_x == 1:
            x = x_refs[0][...]
        else:
            x = jnp.where(pl.program_id(0) < tiles_a, x_refs[0][...], x_refs[1][...])
        h = _mod_norm(x, n1_ref[...], mod_ref[0:1, :], mod_ref[1:2, :]).astype(BF16)
        x1 = x + 0.5 * mod_ref[2:3, :] * _swiglu_chunks(h, wgu_ref, wd_ref, fetch)
        x1_ref[...] = x1
        h2 = _mod_norm(x1, nm_ref[...], mod_ref[3:4, :], mod_ref[4:5, :]).astype(BF16)
        z = jnp.dot(h2, win_ref[...], preferred_element_type=F32)
        zu_ref[...] = z[:, :FOURIER_W]
        zh_ref[...] = z[:, FOURIER_W:FOURIER_W + ZH_W]
        zq_ref[...] = z[:, FOURIER_W + ZH_W:FOURIER_W + ZH_W + ATTN_W]
        zkv_ref[...] = z[:, FOURIER_W + ZH_W + ATTN_W:]

    _first_step_fetches(body, (wgu_hbm, wd_hbm, stage_gu, stage_d, sems, layer))


def _ffn_in_call(xs, mod, n1, nm, w_gu, w_down, w_in, *, layer, rows_a, rows_per_cond):
    tm = FFN_TM
    tiles_a = rows_a // tm
    n = sum(x.shape[0] for x in xs)
    widths = (D_MODEL, FOURIER_W, ZH_W, ATTN_W, 2 * KV_W)
    x_specs = _stream_specs(D_MODEL, tiles_a, tm) if len(xs) == 2 else [pl.BlockSpec((tm, D_MODEL), lambda i: (i, 0))]
    return pl.pallas_call(
        functools.partial(_ffn_in_kernel, layer=layer, tiles_a=tiles_a, n_x=len(xs)),
        grid=(n // tm,),
        in_specs=[
            *x_specs,
            _mod_spec(layer, tiles_a, tm, rows_per_cond),
            pl.BlockSpec((None, 1, D_MODEL), lambda i: (layer, 0, 0)),
            pl.BlockSpec((None, 1, D_MODEL), lambda i: (layer, 0, 0)),
            pl.BlockSpec(memory_space=pl.ANY),
            pl.BlockSpec(memory_space=pl.ANY),
            _resident((D_MODEL, IN_W), layer),
        ],
        out_specs=[pl.BlockSpec((tm, w), lambda i: (i, 0)) for w in widths],
        out_shape=[jax.ShapeDtypeStruct((n, w), F32) for w in widths],
        scratch_shapes=_ffn_weight_scratch(),
        compiler_params=_params("arbitrary"),
        name="ffn_in",
    )(*xs, mod, n1, nm, w_gu, w_down, w_in)


def _out_ffn_kernel(x_ref, mod_ref, n2_ref, foa_ref, fob_ref, hoa_ref, hob_ref, aoa_ref, aob_ref,
                    wout_ref, wgu_hbm, wd_hbm, *rest, layer, tiles_a, split):
    o_refs = rest[:2 if split else 1]
    wgu_ref, wd_ref, stage_gu, stage_d, sems = rest[len(o_refs):]
    is_a = pl.program_id(0) < tiles_a

    def body(fetch):
        pick = lambda a_ref, b_ref: jnp.where(is_a, a_ref[...], b_ref[...]).astype(BF16)
        y = jnp.dot(pick(foa_ref, fob_ref), wout_ref[0:FOURIER_W, :], preferred_element_type=F32)
        y += jnp.dot(pick(hoa_ref, hob_ref), wout_ref[FOURIER_W:FOURIER_W + HGRN_W, :], preferred_element_type=F32)
        y += jnp.dot(pick(aoa_ref, aob_ref), wout_ref[FOURIER_W + HGRN_W:, :], preferred_element_type=F32)
        xm = x_ref[...] + mod_ref[5:6, :] * y
        h = _mod_norm(xm, n2_ref[...], mod_ref[6:7, :], mod_ref[7:8, :]).astype(BF16)
        out = xm + 0.5 * mod_ref[8:9, :] * _swiglu_chunks(h, wgu_ref, wd_ref, fetch)

        if not split:
            o_refs[0][...] = out
            return

        @pl.when(is_a)
        def _():
            o_refs[0][...] = out

        @pl.when(jnp.logical_not(is_a))
        def _():
            o_refs[1][...] = out

    _first_step_fetches(body, (wgu_hbm, wd_hbm, stage_gu, stage_d, sems, layer))


def _out_ffn_call(x, mod, n2, fo, ho, ao, w_out, w_gu, w_down, *, layer, rows_per_cond, split):
    tm = FFN_TM
    n_a, n_b = fo[0].shape[0], fo[1].shape[0]
    tiles_a = n_a // tm
    if split:
        out_specs = list(_stream_specs(D_MODEL, tiles_a, tm))
        out_shape = [jax.ShapeDtypeStruct((n_a, D_MODEL), F32), jax.ShapeDtypeStruct((n_b, D_MODEL), F32)]
    else:
        out_specs = pl.BlockSpec((tm, D_MODEL), lambda i: (i, 0))
        out_shape = jax.ShapeDtypeStruct((n_a + n_b, D_MODEL), F32)
    return pl.pallas_call(
        functools.partial(_out_ffn_kernel, layer=layer, tiles_a=tiles_a, split=split),
        grid=((n_a + n_b) // tm,),
        in_specs=[
            pl.BlockSpec((tm, D_MODEL), lambda i: (i, 0)),
            _mod_spec(layer, tiles_a, tm, rows_per_cond),
            pl.BlockSpec((None, 1, D_MODEL), lambda i: (layer, 0, 0)),
            *_stream_specs(FOURIER_W, tiles_a, tm),
            *_stream_specs(HGRN_W, tiles_a, tm),
            *_stream_specs(ATTN_W, tiles_a, tm),
            _resident((MIX_W, D_MODEL), layer),
            pl.BlockSpec(memory_space=pl.ANY),
            pl.BlockSpec(memory_space=pl.ANY),
        ],
        out_specs=out_specs,
        out_shape=out_shape,
        scratch_shapes=_ffn_weight_scratch(),
        compiler_params=_params("arbitrary"),
        name="out_ffn",
    )(x, mod, n2, fo[0], fo[1], ho[0], ho[1], ao[0], ao[1], w_out, w_gu, w_down)


def _dft_tables(seq_len):
    c = np.arange(HEAD_DIM)
    ang_c = 2.0 * np.pi * np.outer(c, c) / HEAD_DIM
    eye = np.eye(FOURIER_W // HEAD_DIM)
    chan = np.concatenate([np.kron(eye, np.cos(ang_c)), np.kron(eye, np.sin(ang_c))], axis=1) / np.sqrt(HEAD_DIM)
    n = np.arange(seq_len)
    ang_l = 2.0 * np.pi * (np.outer(n, n) % seq_len) / seq_len
    pos = np.concatenate([np.cos(ang_l), -np.sin(ang_l)], axis=1) / np.sqrt(seq_len)
    return jnp.asarray(chan, dtype=F32), jnp.asarray(pos, dtype=F32)


def _fourier_rows(u_ref, chan_ref, pos_ref, o_ref, chan_bf, pos_bf, seq_len):
    @pl.when(pl.program_id(0) == 0)
    def _():
        chan_bf[...] = chan_ref[...].astype(BF16)
        pos_bf[...] = pos_ref[...].astype(BF16)

    t = jnp.dot(u_ref[...].astype(BF16), chan_bf[...], preferred_element_type=F32)
    pos = pos_bf[...]
    for r0 in range(0, u_ref.shape[0], seq_len):
        tb = t[r0:r0 + seq_len, :]
        stacked = jnp.concatenate([tb[:, :FOURIER_W], tb[:, FOURIER_W:]], axis=0).astype(BF16)
        o_ref[r0:r0 + seq_len, :] = jnp.dot(pos, stacked, preferred_element_type=F32)


def _fourier_table_specs(seq_len, index):
    specs = [pl.BlockSpec((FOURIER_W, 2 * FOURIER_W), index, pipeline_mode=pl.Buffered(1)),
             pl.BlockSpec((seq_len, 2 * seq_len), index, pipeline_mode=pl.Buffered(1))]
    scratch = [pltpu.VMEM((FOURIER_W, 2 * FOURIER_W), BF16), pltpu.VMEM((seq_len, 2 * seq_len), BF16)]
    return specs, scratch


def _split3(x):
    hi = x.astype(BF16)
    r1 = x - hi.astype(F32)
    mid = r1.astype(BF16)
    lo = (r1 - mid.astype(F32)).astype(BF16)
    return hi, mid, lo


def _level_mid(b, s, pos):
    tile, lanes = b.shape
    if s >= SUBLANES:
        b3 = b.reshape(tile // s, s, lanes)
        return jnp.broadcast_to(b3[:, pos:pos + 1, :], b3.shape).reshape(tile, lanes)
    assert s == 4
    b3 = b.reshape(tile // SUBLANES, SUBLANES, lanes)
    sub = lax.broadcasted_iota(jnp.int32, (1, SUBLANES, 1), 1)
    return jnp.where(sub < 4, b3[:, pos:pos + 1, :], b3[:, 4 + pos:5 + pos, :]).reshape(tile, lanes)


def _hgrn_kernel(*refs, seq_len, layer, has_s0, emit_state):
    refs = list(refs)
    zh_ref, lb_ref, nw_ref, tri_ref = refs[:4]
    pos = 4
    s0_ref = sf_ref = None
    if has_s0:
        s0_ref = refs[pos]
        pos += 1
    if emit_state:
        pos += 1
    ho_ref = refs[pos]
    pos += 1
    if emit_state:
        sf_ref = refs[pos]
        pos += 1
    qin_ref, dec_ref, kvt_ref, stf_ref, stb_ref, oacc_ref = refs[pos:]

    tile = HGRN_TILE
    width = HGRN_W
    n_tiles = seq_len // tile
    c_q, c_v, c_z, c_g = 0, width, 2 * width, 4 * width

    raw = lb_ref[...]
    e = jnp.exp(raw - jnp.max(raw, axis=0, keepdims=True))
    soft = e / jnp.sum(e, axis=0, keepdims=True)
    cum = soft[0]
    for l in range(1, layer + 1):
        cum = cum + soft[l]
    lbd = cum - soft[0]
    lb_floor = jnp.maximum(lbd, LB_FLOOR)
    one_m_lb = 1.0 - lbd

    lane_bwd1 = lax.broadcasted_iota(jnp.int32, (1, HG_2), 1) >= width
    lane_bwd = lax.broadcasted_iota(jnp.int32, (tile, HG_2), 1) >= width
    lane_bwd_s = lax.broadcasted_iota(jnp.int32, (HEAD_DIM, HG_2), 1) >= width
    row = lax.broadcasted_iota(jnp.int32, (tile, 1), 0)
    pair_xor = (lax.broadcasted_iota(jnp.int32, (tile, HGRN_HEADS * tile), 0)
                ^ (lax.broadcasted_iota(jnp.int32, (tile, HGRN_HEADS * tile), 1) % tile))
    ev_mask = (lax.broadcasted_iota(jnp.int32, (HGRN_HEADS * tile, width), 0) // tile
               == lax.broadcasted_iota(jnp.int32, (HGRN_HEADS * tile, width), 1) // HEAD_DIM)
    bd_mask = (lax.broadcasted_iota(jnp.int32, (width, HG_2), 0) // HEAD_DIM
               == (lax.broadcasted_iota(jnp.int32, (width, HG_2), 1) // HEAD_DIM) % HGRN_HEADS)
    lane_head = (lax.broadcasted_iota(jnp.int32, (HEAD_DIM, HG_2), 1) // HEAD_DIM) % HGRN_HEADS
    ones_bd = _head_ones(width)

    def tile_body(t, carry):
        rows = pl.ds(pl.multiple_of(t * tile, tile), tile)
        q = zh_ref[rows, c_q:c_q + width] * (HEAD_DIM ** -0.5)
        v = zh_ref[rows, c_v:c_v + width]
        z = zh_ref[rows, c_z:c_z + HG_2]
        ez = jnp.exp(-jnp.abs(z))
        r = 1.0 / (1.0 + ez)
        tr = ez * r
        pos_z = z >= 0.0
        f = lb_floor + one_m_lb * jnp.where(pos_z, r, tr)
        logf = jnp.log(f)
        kk = one_m_lb * jnp.where(pos_z, tr, r)

        pieces = jnp.concatenate(_split3(logf), axis=1)
        acc = jnp.dot(tri_ref[...], pieces, preferred_element_type=F32)
        prefix = acc[:, :HG_2] + acc[:, HG_2:2 * HG_2] + acc[:, 2 * HG_2:]
        total = prefix[tile - 1:tile, :]
        b = jnp.where(lane_bwd, total - prefix + logf, prefix)
        kk_f, kk_b = kk[:, :width], kk[:, width:]

        a0 = jnp.dot((q * (kk_f + kk_b)).astype(BF16), ones_bd, preferred_element_type=F32)
        o = a0 * v

        b2 = b * LOG2_E
        bf2, bb2 = b2[:, :width], b2[:, width:]
        scores = None
        for s in HGRN_LEVELS:
            second = (row % s) >= s // 2
            k_role = jnp.where(second, kk_b, kk_f)
            if s == 2:
                q_op = q * jnp.where(second, f[:, :width], f[:, width:])
                k_op = k_role
            else:
                mid_f = _level_mid(bf2, s, s // 2 - 1)
                mid_b = _level_mid(bb2, s, s // 2)
                e_q = jnp.where(second, bf2, bb2) - jnp.where(second, mid_f, mid_b)
                e_k = jnp.where(second, mid_b, mid_f) - jnp.where(second, bb2, bf2)
                q_op = q * jnp.exp2(e_q)
                k_op = k_role * jnp.exp2(e_k)
            qt = q_op.astype(BF16)
            kt = k_op.T.astype(BF16)
            sc = jnp.concatenate(
                [jnp.dot(qt[:, h * HEAD_DIM:(h + 1) * HEAD_DIM], kt[h * HEAD_DIM:(h + 1) * HEAD_DIM, :],
                         preferred_element_type=F32) for h in range(HGRN_HEADS)], axis=1)
            scores = sc if scores is None else jnp.where(pair_xor < s, sc, scores)
        scores = jnp.where(pair_xor == 0, 0.0, scores)
        ev = jnp.where(ev_mask, jnp.concatenate([v] * HGRN_HEADS, axis=0), 0.0).astype(BF16)
        o = o + jnp.dot(scores.astype(BF16), ev, preferred_element_type=F32)
        oacc_ref[rows, :] = o

        q2 = jnp.concatenate([q, q], axis=1)
        qin_ref[rows, :] = (q2 * jnp.exp(b)).astype(BF16)
        kout = (kk * jnp.exp(total - b)).astype(BF16)
        dec_ref[t] = jnp.exp(total)
        full = jnp.dot(v.T.astype(BF16), kout, preferred_element_type=F32)
        kvt = full[0:HEAD_DIM, :]
        for hh in range(1, HGRN_HEADS):
            kvt = jnp.where(lane_head == hh, full[hh * HEAD_DIM:(hh + 1) * HEAD_DIM, :], kvt)
        kvt_ref[t] = kvt
        return carry

    lax.fori_loop(0, n_tiles, tile_body, 0, unroll=2)

    if has_s0:
        st0 = jnp.concatenate([s0_ref[d, h] for d in range(2) for h in range(HGRN_HEADS)], axis=0).T
    else:
        st0 = jnp.zeros((HEAD_DIM, HG_2), F32)

    def scan_step(i, st):
        tf, tb = i, n_tiles - 1 - i
        stf_ref[tf] = st
        stb_ref[tb] = st
        dec = jnp.where(lane_bwd1, dec_ref[tb], dec_ref[tf])
        kv = jnp.where(lane_bwd_s, kvt_ref[tb], kvt_ref[tf])
        return st * dec + kv

    st_fin = lax.fori_loop(0, n_tiles, scan_step, st0)
    if emit_state:
        st_t = st_fin.T
        for d in range(2):
            for h in range(HGRN_HEADS):
                r0 = (HGRN_HEADS * d + h) * HEAD_DIM
                sf_ref[d, h] = st_t[r0:r0 + HEAD_DIM, :]

    def out_tile(t):
        rows = pl.ds(pl.multiple_of(t * tile, tile), tile)
        st = jnp.where(lane_bwd_s, stb_ref[t], stf_ref[t])
        bd = jnp.where(bd_mask, jnp.concatenate([st] * HGRN_HEADS, axis=0), 0.0).T.astype(BF16)
        o = oacc_ref[rows, :] + jnp.dot(qin_ref[rows, :], bd, preferred_element_type=F32)
        hg = zh_ref[rows, c_g:c_g + width]
        ho_ref[rows, :] = _head_rmsnorm(o, nw_ref[...]) * (hg * jax.nn.sigmoid(hg))

    group = 4 if n_tiles % 4 == 0 else 2

    def out_body(i, carry):
        for j in range(group):
            out_tile(group * i + j)
        return carry

    assert n_tiles % group == 0
    lax.fori_loop(0, n_tiles // group, out_body, 0)


def _hgrn_tri():
    i = np.arange(HGRN_TILE)
    return jnp.asarray((i[None, :] <= i[:, None]).astype(np.float32), dtype=BF16)


def _hgrn_call(zh, lb2, norm_w, s0, final_states, *, seq_len, layer, span):
    row0, n = span
    blk0 = row0 // seq_len
    n_seq = n // seq_len
    has_s0 = s0 is not None
    emit_state = final_states is not None
    state_block = (2, HGRN_HEADS, HEAD_DIM, HEAD_DIM)
    n_tiles = seq_len // HGRN_TILE
    in_specs = [
        pl.BlockSpec((seq_len, ZH_W), lambda b: (b + blk0, 0)),
        pl.BlockSpec((DEPTH, 1, HG_2), lambda b: (0, 0, 0)),
        pl.BlockSpec((None, 1, HGRN_W), lambda b: (layer, 0, 0)),
        pl.BlockSpec((HGRN_TILE, HGRN_TILE), lambda b: (0, 0)),
    ]
    args = [zh, lb2, norm_w, _hgrn_tri()]
    if has_s0:
        in_specs.append(pl.BlockSpec((None, None) + state_block, lambda b: (b, layer, 0, 0, 0, 0)))
        args.append(s0)
    out_specs = [pl.BlockSpec((seq_len, HGRN_W), lambda b: (b, 0))]
    out_shape = [jax.ShapeDtypeStruct((n, HGRN_W), F32)]
    aliases = {}
    if emit_state:
        aliases[len(args)] = 1
        in_specs.append(pl.BlockSpec(memory_space=pl.ANY))
        args.append(final_states)
        out_specs.append(pl.BlockSpec((None, None) + state_block, lambda b: (b, layer, 0, 0, 0, 0)))
        out_shape.append(jax.ShapeDtypeStruct(final_states.shape, F32))
    return pl.pallas_call(
        functools.partial(_hgrn_kernel, seq_len=seq_len, layer=layer, has_s0=has_s0, emit_state=emit_state),
        grid=(n_seq,),
        in_specs=in_specs,
        out_specs=out_specs,
        out_shape=out_shape,
        input_output_aliases=aliases,
        scratch_shapes=[
            pltpu.VMEM((seq_len, HG_2), BF16),
            pltpu.VMEM((n_tiles, 1, HG_2), F32),
            pltpu.VMEM((n_tiles, HEAD_DIM, HG_2), F32),
            pltpu.VMEM((n_tiles, HEAD_DIM, HG_2), F32),
            pltpu.VMEM((n_tiles, HEAD_DIM, HG_2), F32),
            pltpu.VMEM((seq_len, HGRN_W), F32),
        ],
        compiler_params=_params("arbitrary"),
        name="hgrn",
    )(*args)


def _sink_row(sink_ref, kv_head, rows):
    grp = lax.broadcasted_iota(jnp.int32, (1, ATTN_GROUP * rows), 1) // rows
    base = kv_head * ATTN_GROUP
    out = jnp.broadcast_to(sink_ref[:, base:base + 1], (1, ATTN_GROUP * rows))
    for g in range(1, ATTN_GROUP):
        out = jnp.where(grp == g, sink_ref[:, base + g:base + g + 1], out)
    return out


def _group_queries_t(qt, kv_head):
    base = kv_head * ATTN_GROUP
    return jnp.concatenate([qt[(base + g) * HEAD_DIM:(base + g + 1) * HEAD_DIM, :] for g in range(ATTN_GROUP)], axis=1)


def _ctx_attn_kernel(zq_ref, zkv_ref, zu_ref, qn_ref, kn_ref, sink_ref, chan_ref, pos_ref, kc_all_ref, vc_all_ref,
                     ao_ref, fo_ref, kc_ref, vc_ref, chan_bf, pos_bf, *, seq_len):
    del kc_all_ref, vc_all_ref
    _fourier_rows(zu_ref, chan_ref, pos_ref, fo_ref, chan_bf, pos_bf, seq_len)
    rows = seq_len
    q_all = _head_rmsnorm(zq_ref[...], qn_ref[...]) * (HEAD_DIM ** -0.5)
    k_all = _head_rmsnorm(zkv_ref[:, :KV_W], kn_ref[...])
    v_all = zkv_ref[:, KV_W:]
    kc_ref[...] = k_all.reshape(kc_ref.shape)
    vc_ref[...] = v_all.reshape(vc_ref.shape)
    starts = list(range(0, zq_ref.shape[0], seq_len))
    heads = range(ATTN_KV_HEADS)
    hsl = [slice(kh * HEAD_DIM, (kh + 1) * HEAD_DIM) for kh in heads]
    qt = [q_all[r0:r0 + rows, :].T.astype(BF16) for r0 in starts]
    kb = [k_all[r0:r0 + rows, :].astype(BF16) for r0 in starts]
    vt = [v_all[r0:r0 + rows, :].T.astype(BF16) for r0 in starts]
    pairs = [(i, kh) for i in range(len(starts)) for kh in heads]
    sink = [_sink_row(sink_ref, kh, rows) for kh in heads]
    s = {pk: jnp.dot(kb[pk[0]][:, hsl[pk[1]]], _group_queries_t(qt[pk[0]], pk[1]), preferred_element_type=F32)
         for pk in pairs}
    m = {pk: jnp.maximum(jnp.max(s[pk], axis=0, keepdims=True), sink[pk[1]]) for pk in pairs}
    p = {pk: jnp.exp(s[pk] - m[pk]) for pk in pairs}
    denom = {pk: jnp.sum(p[pk], axis=0, keepdims=True) + jnp.exp(sink[pk[1]] - m[pk]) for pk in pairs}
    for i, r0 in enumerate(starts):
        out_t = []
        for kh in heads:
            o_t = jnp.dot(vt[i][hsl[kh], :], p[(i, kh)].astype(BF16), preferred_element_type=F32) * (1.0 / denom[(i, kh)])
            out_t.extend(o_t[:, g * rows:(g + 1) * rows] for g in range(ATTN_GROUP))
        ao_ref[r0:r0 + rows, :] = jnp.concatenate(out_t, axis=0).T


def _ctx_attn_call(zq, zkv, zu, q_norm, k_norm, sink, k_cache, v_cache, *, seq_len, layer, span):
    row0, n = span
    rows = max(seq_len, SMALL_SEQ_ROWS)
    blk0 = row0 // rows
    seqs = rows // seq_len
    cache_spec = pl.BlockSpec((seqs, None, seq_len, KV_W), lambda b: (b, layer, 0, 0))
    table_specs, table_scratch = _fourier_table_specs(seq_len, lambda b: (0, 0))
    return pl.pallas_call(
        functools.partial(_ctx_attn_kernel, seq_len=seq_len),
        grid=(n // rows,),
        in_specs=[
            pl.BlockSpec((rows, ATTN_W), lambda b: (b + blk0, 0)),
            pl.BlockSpec((rows, 2 * KV_W), lambda b: (b + blk0, 0)),
            pl.BlockSpec((rows, FOURIER_W), lambda b: (b + blk0, 0)),
            pl.BlockSpec((None, 1, ATTN_W), lambda b: (layer, 0, 0)),
            pl.BlockSpec((None, 1, KV_W), lambda b: (layer, 0, 0)),
            pl.BlockSpec((None, 1, ATTN_HEADS), lambda b: (layer, 0, 0)),
            *table_specs,
            pl.BlockSpec(memory_space=pl.ANY),
            pl.BlockSpec(memory_space=pl.ANY),
        ],
        out_specs=[pl.BlockSpec((rows, ATTN_W), lambda b: (b, 0)), pl.BlockSpec((rows, FOURIER_W), lambda b: (b, 0)),
                   cache_spec, cache_spec],
        out_shape=[
            jax.ShapeDtypeStruct((n, ATTN_W), F32),
            jax.ShapeDtypeStruct((n, FOURIER_W), F32),
            jax.ShapeDtypeStruct(k_cache.shape, F32),
            jax.ShapeDtypeStruct(v_cache.shape, F32),
        ],
        input_output_aliases={8: 2, 9: 3},
        scratch_shapes=table_scratch,
        compiler_params=_params("arbitrary"),
        name="ctx_attn",
    )(zq, zkv, zu, q_norm, k_norm, sink, *_dft_tables(seq_len), k_cache, v_cache)


def _rope_tables(seq_len, n_heads):
    t = np.arange(seq_len)
    row, col = t // GRID_W, t % GRID_W
    half = HEAD_DIM // 2
    inv = (ROPE_BASE ** (-np.arange(0, half, 2, dtype=np.float32) / half)).astype(np.float32)

    def one(pos):
        ang = pos.astype(np.float32)[:, None] * inv[None]
        c, s = np.cos(ang), np.sin(ang)
        return np.concatenate([c, c], axis=1), np.concatenate([-s, s], axis=1)

    cr, sr = one(row)
    cc, sc = one(col)
    cos = np.tile(np.concatenate([cr, cc], axis=1), (1, n_heads)).astype(np.float32)
    sin = np.tile(np.concatenate([sr, sc], axis=1), (1, n_heads)).astype(np.float32)
    return jnp.asarray(cos), jnp.asarray(sin)


def _rope(x, cos, sin_signed):
    width = x.shape[-1]
    lane = lax.broadcasted_iota(jnp.int32, x.shape, 1)
    quarter = HEAD_DIM // 4
    up = pltpu.roll(x, width - quarter, axis=1)
    down = pltpu.roll(x, quarter, axis=1)
    partner = jnp.where((lane % (2 * quarter)) < quarter, up, down)
    return x * cos + partner * sin_signed


def _lat_attn_kernel(zq_ref, zkv_ref, zu_ref, kc_ref, vc_ref, qn_ref, kn_ref, sink_ref,
                     cosq_ref, sinq_ref, cosk_ref, sink_tab_ref, band_ref, chan_ref, pos_ref,
                     ao_ref, fo_ref, kb_ref, kcb_ref, vct_ref, chan_bf, pos_bf, *, seq_len):
    rows = ATTN_BLOCK
    _fourier_rows(zu_ref, chan_ref, pos_ref, fo_ref, chan_bf, pos_bf, seq_len)

    k_all = _head_rmsnorm(zkv_ref[:, 0:KV_W], kn_ref[...])
    kb_ref[...] = _rope(k_all, cosk_ref[...], sink_tab_ref[...]).astype(BF16)
    kcb_ref[...] = kc_ref[...].astype(BF16)
    vct_ref[...] = vc_ref[...].T.astype(BF16)

    def query_block(blk, carry):
        qrows = pl.ds(pl.multiple_of(blk * ATTN_BLOCK, ATTN_BLOCK), ATTN_BLOCK)
        q = _head_rmsnorm(zq_ref[qrows, :], qn_ref[...])
        q = _rope(q, cosq_ref[qrows, :], sinq_ref[qrows, :]) * (HEAD_DIM ** -0.5)
        start = jnp.clip((blk - 1) * ATTN_BLOCK, 0, seq_len - WIN_KEYS)
        start = pl.multiple_of(start, ATTN_BLOCK)
        vwin = zkv_ref[pl.ds(start, WIN_KEYS), KV_W:2 * KV_W]

        band_bias = band_ref[blk - start // ATTN_BLOCK]

        qt = q.T.astype(BF16)
        kwb = kb_ref[pl.ds(start, WIN_KEYS), :]
        kcb = kcb_ref[...]
        vwt = vwin.T.astype(BF16)
        vct = vct_ref[...]
        heads = range(ATTN_KV_HEADS)
        hsl = [slice(kh * HEAD_DIM, (kh + 1) * HEAD_DIM) for kh in heads]
        qg = [_group_queries_t(qt, kh) for kh in heads]
        s_loc = [jnp.dot(kwb[:, hsl[kh]], qg[kh], preferred_element_type=F32) + band_bias for kh in heads]
        s_ctx = [jnp.dot(kcb[:, hsl[kh]], qg[kh], preferred_element_type=F32) for kh in heads]
        sink = [_sink_row(sink_ref, kh, rows) for kh in heads]
        m = [jnp.maximum(jnp.maximum(jnp.max(s_loc[kh], axis=0, keepdims=True),
                                     jnp.max(s_ctx[kh], axis=0, keepdims=True)), sink[kh]) for kh in heads]
        p_loc = [jnp.exp(s_loc[kh] - m[kh]) for kh in heads]
        p_ctx = [jnp.exp(s_ctx[kh] - m[kh]) for kh in heads]
        denom = [jnp.sum(p_loc[kh], axis=0, keepdims=True) + jnp.sum(p_ctx[kh], axis=0, keepdims=True)
                 + jnp.exp(sink[kh] - m[kh]) for kh in heads]
        out_t = []
        for kh in heads:
            o_t = (jnp.dot(vwt[hsl[kh], :], p_loc[kh].astype(BF16), preferred_element_type=F32)
                   + jnp.dot(vct[hsl[kh], :], p_ctx[kh].astype(BF16), preferred_element_type=F32)) * (1.0 / denom[kh])
            out_t.extend(o_t[:, g * rows:(g + 1) * rows] for g in range(ATTN_GROUP))
        ao_ref[qrows, :] = jnp.concatenate(out_t, axis=0).T
        return carry

    lax.fori_loop(0, seq_len // ATTN_BLOCK, query_block, 0, unroll=4)


def _lat_attn_call(zq, zkv, zu, cache_k, cache_v, q_norm, k_norm, sink, *, seq_len, layer, span):
    row0, n = span
    seq0 = row0 // seq_len
    past = cache_k.shape[2]
    cosq, sinq = _rope_tables(seq_len, ATTN_HEADS)
    cosk, sink_tab = _rope_tables(seq_len, ATTN_KV_HEADS)
    kr = np.arange(WIN_KEYS)[:, None]
    qi = np.arange(ATTN_GROUP * ATTN_BLOCK)[None, :] % ATTN_BLOCK
    band = np.stack([np.where(np.abs(qi + off * ATTN_BLOCK - kr) <= WINDOW, 0.0, NEG_BIG) for off in range(3)])
    band = jnp.asarray(band, dtype=F32)
    table_specs, table_scratch = _fourier_table_specs(seq_len, lambda b: (0, 0))
    return pl.pallas_call(
        functools.partial(_lat_attn_kernel, seq_len=seq_len),
        grid=(n // seq_len,),
        in_specs=[
            pl.BlockSpec((seq_len, ATTN_W), lambda b: (seq0 + b, 0)),
            pl.BlockSpec((seq_len, 2 * KV_W), lambda b: (seq0 + b, 0)),
            pl.BlockSpec((seq_len, FOURIER_W), lambda b: (seq0 + b, 0)),
            pl.BlockSpec((None, None, past, KV_W), lambda b: (b, layer, 0, 0)),
            pl.BlockSpec((None, None, past, KV_W), lambda b: (b, layer, 0, 0)),
            pl.BlockSpec((None, 1, ATTN_W), lambda b: (layer, 0, 0)),
            pl.BlockSpec((None, 1, KV_W), lambda b: (layer, 0, 0)),
            pl.BlockSpec((None, 1, ATTN_HEADS), lambda b: (layer, 0, 0)),
            pl.BlockSpec((seq_len, ATTN_W), lambda b: (0, 0), pipeline_mode=pl.Buffered(1)),
            pl.BlockSpec((seq_len, ATTN_W), lambda b: (0, 0), pipeline_mode=pl.Buffered(1)),
            pl.BlockSpec((seq_len, KV_W), lambda b: (0, 0), pipeline_mode=pl.Buffered(1)),
            pl.BlockSpec((seq_len, KV_W), lambda b: (0, 0), pipeline_mode=pl.Buffered(1)),
            pl.BlockSpec((3, WIN_KEYS, ATTN_GROUP * ATTN_BLOCK), lambda b: (0, 0, 0), pipeline_mode=pl.Buffered(1)),
            *table_specs,
        ],
        out_specs=[pl.BlockSpec((seq_len, ATTN_W), lambda b: (b, 0)), pl.BlockSpec((seq_len, FOURIER_W), lambda b: (b, 0))],
        out_shape=[jax.ShapeDtypeStruct((n, ATTN_W), F32), jax.ShapeDtypeStruct((n, FOURIER_W), F32)],
        scratch_shapes=[
            pltpu.VMEM((seq_len, KV_W), BF16),
            pltpu.VMEM((past, KV_W), BF16),
            pltpu.VMEM((KV_W, past), BF16),
            *table_scratch,
        ],
        compiler_params=_params("arbitrary"),
        name="lat_attn",
    )(zq, zkv, zu, cache_k, cache_v, q_norm, k_norm, sink, cosq, sinq, cosk, sink_tab, band, *_dft_tables(seq_len))


def kernel(x_prompt, x_sample, c, cache_attn_k, cache_attn_v, state_hgrn, c_ctx, w_ada, b_ada, norm_ffn1, norm_mix, norm_ffn2, ffn1_w_gate_up, ffn1_w_down, ffn2_w_gate_up, ffn2_w_down, w_in, w_out, hgrn_lower_bounds, hgrn_norm, q_norm, k_norm, attn_sink):
    batch, seq, _ = x_prompt.shape
    dec_batch, dec_seq, _ = x_sample.shape
    past = cache_attn_k.shape[2]
    assert dec_batch + 1 <= ADA_ROWS

    cond = jnp.concatenate([c_ctx[None], c, jnp.zeros((ADA_ROWS - 1 - dec_batch, D_MODEL), F32)], axis=0)
    mod = _ada_call(cond, w_ada, b_ada).reshape(DEPTH, ADA_ROWS, N_MOD, D_MODEL)

    bf = lambda w: w.astype(BF16)
    w_gu1, w_d1, w_gu2, w_d2 = ffn1_w_gate_up, ffn1_w_down, ffn2_w_gate_up, ffn2_w_down
    w_in_b, w_out_b = bf(w_in), bf(w_out)
    row3 = lambda w: w.reshape(DEPTH, 1, w.shape[-1])
    n1, nm, n2 = row3(norm_ffn1), row3(norm_mix), row3(norm_ffn2)
    hn = row3(hgrn_norm)
    qn = row3(jnp.tile(q_norm, (1, ATTN_HEADS)))
    kn = row3(jnp.tile(k_norm, (1, ATTN_KV_HEADS)))
    sink = row3(attn_sink)
    cache_k = cache_attn_k.reshape(dec_batch, DEPTH, past, KV_W)
    cache_v = cache_attn_v.reshape(dec_batch, DEPTH, past, KV_W)
    lb2 = hgrn_lower_bounds.reshape(DEPTH, 1, HG_2)

    n_ctx, n_lat = batch * seq, dec_batch * dec_seq
    span_c, span_l = (0, n_ctx), (n_ctx, n_lat)
    xs = (x_prompt.reshape(n_ctx, D_MODEL), x_sample.reshape(n_lat, D_MODEL))
    new_k = jnp.zeros((batch, DEPTH, seq, KV_W), F32)
    new_v = jnp.zeros((batch, DEPTH, seq, KV_W), F32)
    new_s = jnp.zeros((batch, DEPTH, 2, HGRN_HEADS, HEAD_DIM, HEAD_DIM), F32)
    for layer in range(DEPTH):
        x1, zu, zh, zq, zkv = _ffn_in_call(xs, mod, n1, nm, w_gu1, w_d1, w_in_b,
                                           layer=layer, rows_a=n_ctx, rows_per_cond=dec_seq)
        ho_c, new_s = _hgrn_call(zh, lb2, hn, None, new_s, seq_len=seq, layer=layer, span=span_c)
        (ho_l,) = _hgrn_call(zh, lb2, hn, state_hgrn, None, seq_len=dec_seq, layer=layer, span=span_l)
        ao_c, fo_c, new_k, new_v = _ctx_attn_call(zq, zkv, zu, qn, kn, sink, new_k, new_v,
                                                  seq_len=seq, layer=layer, span=span_c)
        ao_l, fo_l = _lat_attn_call(zq, zkv, zu, cache_k, cache_v, qn, kn, sink, seq_len=dec_seq, layer=layer, span=span_l)
        last = layer == DEPTH - 1
        xs = _out_ffn_call(x1, mod, n2, (fo_c, fo_l), (ho_c, ho_l), (ao_c, ao_l), w_out_b, w_gu2, w_d2,
                           layer=layer, rows_per_cond=dec_seq, split=last)
        xs = tuple(xs) if last else (xs,)

    y_p = xs[0].reshape(batch, seq, D_MODEL)
    y_s = xs[1].reshape(dec_batch, dec_seq, D_MODEL)
    cache_shape = (batch, DEPTH, seq, ATTN_KV_HEADS, HEAD_DIM)
    return (y_p, y_s, new_k.reshape(cache_shape), new_v.reshape(cache_shape), new_s)
```

```python
import functools

import numpy as np
import jax
import jax.numpy as jnp
from jax import lax
from jax.experimental import pallas as pl
from jax.experimental.pallas import tpu as pltpu

F32 = jnp.float32
BF16 = jnp.bfloat16

D_MODEL = 1024
DEPTH = 2
HEAD_DIM = 64
FOURIER_W = 256
HGRN_W = 256
HGRN_HEADS = 4
ATTN_W = 512
ATTN_HEADS = 8
ATTN_KV_HEADS = 2
ATTN_GROUP = 4
KV_W = 128
MIX_W = 1024
WINDOW = 128
ATTN_BLOCK = 128
D_FF = 2816
GRID_W = 64
ROPE_BASE = 10000.0
EPS = 1e-6
N_MOD = 9
NEG_BIG = -1e30
LB_FLOOR = 1e-30
LOG2_E = 1.4426950408889634

HG_2 = 2 * HGRN_W
ZH_W = 5 * HGRN_W
IN_W = FOURIER_W + ZH_W + ATTN_W + 2 * KV_W

SUBLANES = 8
VMEM_LIMIT_BYTES = 63 * 1024 * 1024

ADA_ROWS = 8
ADA_TN = 1536
FFN_TM = 512
FFN_FC = 256
SMALL_SEQ_ROWS = 1024
HGRN_TILE = 128
HGRN_LEVELS = (128, 64, 32, 16, 8, 4, 2)
WIN_KEYS = 3 * ATTN_BLOCK


def _params(*semantics):
    return pltpu.CompilerParams(dimension_semantics=semantics, vmem_limit_bytes=VMEM_LIMIT_BYTES)


def _mod_norm(x, norm_w, shift, scale):
    ms = jnp.mean(x * x, axis=-1, keepdims=True)
    y = x * lax.rsqrt(ms + EPS) * norm_w
    return y * (1.0 + scale) + shift


def _head_ones(width):
    r = lax.broadcasted_iota(jnp.int32, (width, width), 0) // HEAD_DIM
    c = lax.broadcasted_iota(jnp.int32, (width, width), 1) // HEAD_DIM
    return jnp.where(r == c, 1.0, 0.0).astype(BF16)


def _head_rmsnorm(x, w):
    sq = (x * x).astype(BF16)
    ms = jnp.dot(sq, _head_ones(x.shape[-1]), preferred_element_type=F32) * (1.0 / HEAD_DIM)
    return x * lax.rsqrt(ms + EPS) * w


def _ada_kernel(cond_ref, w_ref, b_ref, o_ref):
    cnd = cond_ref[...]
    act = (cnd * jax.nn.sigmoid(cnd)).astype(BF16)
    o_ref[...] = jnp.dot(act, w_ref[...].astype(BF16), preferred_element_type=F32) + b_ref[...]


def _ada_call(cond, w_ada, b_ada):
    n_out = N_MOD * D_MODEL
    return pl.pallas_call(
        _ada_kernel,
        grid=(DEPTH, n_out // ADA_TN),
        in_specs=[
            pl.BlockSpec((ADA_ROWS, D_MODEL), lambda l, j: (0, 0)),
            pl.BlockSpec((None, D_MODEL, ADA_TN), lambda l, j: (l, 0, j)),
            pl.BlockSpec((None, 1, ADA_TN), lambda l, j: (l, 0, j)),
        ],
        out_specs=pl.BlockSpec((None, ADA_ROWS, ADA_TN), lambda l, j: (l, 0, j)),
        out_shape=jax.ShapeDtypeStruct((DEPTH, ADA_ROWS, n_out), F32),
        compiler_params=_params("arbitrary", "arbitrary"),
        name="adaln",
    )(cond, w_ada, b_ada.reshape(DEPTH, 1, n_out))


def _weight_chunk_copies(wgu_hbm, wd_hbm, stage_gu, stage_d, sems, layer, c, slot):
    lo, hi = c * FFN_FC, (c + 1) * FFN_FC
    return (
        pltpu.make_async_copy(wgu_hbm.at[layer, :, lo:hi], stage_gu.at[slot, :, 0:FFN_FC], sems.at[slot, 0]),
        pltpu.make_async_copy(wgu_hbm.at[layer, :, D_FF + lo:D_FF + hi], stage_gu.at[slot, :, FFN_FC:2 * FFN_FC],
                              sems.at[slot, 1]),
        pltpu.make_async_copy(wd_hbm.at[layer, lo:hi, :], stage_d.at[slot], sems.at[slot, 2]),
    )


def _swiglu_chunks(h, wgu_ref, wd_ref, fetch=None):
    n_chunks = D_FF // FFN_FC
    if fetch is not None:
        stage_gu, stage_d = fetch[2], fetch[3]
        for cp in _weight_chunk_copies(*fetch, 0, 0):
            cp.start()
    acc = None
    for c in range(n_chunks):
        if fetch is not None:
            slot = c % 2
            if c + 1 < n_chunks:
                for cp in _weight_chunk_copies(*fetch, c + 1, 1 - slot):
                    cp.start()
            for cp in _weight_chunk_copies(*fetch, c, slot):
                cp.wait()
            wgu_ref[:, c * FFN_FC:(c + 1) * FFN_FC] = stage_gu[slot, :, 0:FFN_FC].astype(BF16)
            wgu_ref[:, D_FF + c * FFN_FC:D_FF + (c + 1) * FFN_FC] = stage_gu[slot, :, FFN_FC:2 * FFN_FC].astype(BF16)
            wd_ref[c * FFN_FC:(c + 1) * FFN_FC, :] = stage_d[slot].astype(BF16)
        g = jnp.dot(h, wgu_ref[:, c * FFN_FC:(c + 1) * FFN_FC], preferred_element_type=F32)
        u = jnp.dot(h, wgu_ref[:, D_FF + c * FFN_FC:D_FF + (c + 1) * FFN_FC], preferred_element_type=F32)
        act = (g * jax.nn.sigmoid(g) * u).astype(BF16)
        part = jnp.dot(act, wd_ref[c * FFN_FC:(c + 1) * FFN_FC, :], preferred_element_type=F32)
        acc = part if acc is None else acc + part
    return acc


def _mod_spec(layer, tiles_a, tm, rows_per_cond):
    def index(i):
        row = jnp.where(i < tiles_a, 0, 1 + ((i - tiles_a) * tm) // rows_per_cond)
        return (layer, row, 0, 0)
    return pl.BlockSpec((None, None, N_MOD, D_MODEL), index)


def _stream_specs(width, tiles_a, tm):
    return (pl.BlockSpec((tm, width), lambda i: (jnp.minimum(i, tiles_a - 1), 0)),
            pl.BlockSpec((tm, width), lambda i: (jnp.maximum(i - tiles_a, 0), 0)))


def _resident(shape, layer):
    return pl.BlockSpec((None,) + shape, lambda i: (layer,) + (0,) * len(shape), pipeline_mode=pl.Buffered(1))


def _first_step_fetches(body, fetch):
    first = pl.program_id(0) == 0
    pl.when(first)(functools.partial(body, fetch))
    pl.when(jnp.logical_not(first))(functools.partial(body, None))


def _ffn_weight_scratch():
    return [
        pltpu.VMEM((D_MODEL, 2 * D_FF), BF16),
        pltpu.VMEM((D_FF, D_MODEL), BF16),
        pltpu.VMEM((2, D_MODEL, 2 * FFN_FC), F32),
        pltpu.VMEM((2, FFN_FC, D_MODEL), F32),
        pltpu.SemaphoreType.DMA((2, 3)),
    ]


def _ffn_in_kernel(*refs, layer, tiles_a, n_x):
    x_refs = refs[:n_x]
    (mod_ref, n1_ref, nm_ref, wgu_hbm, wd_hbm, win_ref, x1_ref, zu_ref, zh_ref, zq_ref, zkv_ref,
     wgu_ref, wd_ref, stage_gu, stage_d, sems) = refs[n_x:]

    def body(fetch):
        if n_x == 1:
            x = x_refs[0][...]
        else:
            x = jnp.where(pl.program_id(0) < tiles_a, x_refs[0][...], x_refs[1][...])
        h = _mod_norm(x, n1_ref[...], mod_ref[0:1, :], mod_ref[1:2, :]).astype(BF16)
        x1 = x + 0.5 * mod_ref[2:3, :] * _swiglu_chunks(h, wgu_ref, wd_ref, fetch)
        x1_ref[...] = x1
        h2 = _mod_norm(x1, nm_ref[...], mod_ref[3:4, :], mod_ref[4:5, :]).astype(BF16)
        z = jnp.dot(h2, win_ref[...], preferred_element_type=F32)
        zu_ref[...] = z[:, :FOURIER_W]
        zh_ref[...] = z[:, FOURIER_W:FOURIER_W + ZH_W]
        zq_ref[...] = z[:, FOURIER_W + ZH_W:FOURIER_W + ZH_W + ATTN_W]
        zkv_ref[...] = z[:, FOURIER_W + ZH_W + ATTN_W:]

    _first_step_fetches(body, (wgu_hbm, wd_hbm, stage_gu, stage_d, sems, layer))


def _ffn_in_call(xs, mod, n1, nm, w_gu, w_down, w_in, *, layer, rows_a, rows_per_cond):
    tm = FFN_TM
    tiles_a = rows_a // tm
    n = sum(x.shape[0] for x in xs)
    widths = (D_MODEL, FOURIER_W, ZH_W, ATTN_W, 2 * KV_W)
    x_specs = _stream_specs(D_MODEL, tiles_a, tm) if len(xs) == 2 else [pl.BlockSpec((tm, D_MODEL), lambda i: (i, 0))]
    return pl.pallas_call(
        functools.partial(_ffn_in_kernel, layer=layer, tiles_a=tiles_a, n_x=len(xs)),
        grid=(n // tm,),
        in_specs=[
            *x_specs,
            _mod_spec(layer, tiles_a, tm, rows_per_cond),
            pl.BlockSpec((None, 1, D_MODEL), lambda i: (layer, 0, 0)),
            pl.BlockSpec((None, 1, D_MODEL), lambda i: (layer, 0, 0)),
            pl.BlockSpec(memory_space=pl.ANY),
            pl.BlockSpec(memory_space=pl.ANY),
            _resident((D_MODEL, IN_W), layer),
        ],
        out_specs=[pl.BlockSpec((tm, w), lambda i: (i, 0)) for w in widths],
        out_shape=[jax.ShapeDtypeStruct((n, w), F32) for w in widths],
        scratch_shapes=_ffn_weight_scratch(),
        compiler_params=_params("arbitrary"),
        name="ffn_in",
    )(*xs, mod, n1, nm, w_gu, w_down, w_in)


def _out_ffn_kernel(x_ref, mod_ref, n2_ref, foa_ref, fob_ref, hoa_ref, hob_ref, aoa_ref, aob_ref,
                    wout_ref, wgu_hbm, wd_hbm, *rest, layer, tiles_a, split):
    o_refs = rest[:2 if split else 1]
    wgu_ref, wd_ref, stage_gu, stage_d, sems = rest[len(o_refs):]
    is_a = pl.program_id(0) < tiles_a

    def body(fetch):
        pick = lambda a_ref, b_ref: jnp.where(is_a, a_ref[...], b_ref[...]).astype(BF16)
        y = jnp.dot(pick(foa_ref, fob_ref), wout_ref[0:FOURIER_W, :], preferred_element_type=F32)
        y += jnp.dot(pick(hoa_ref, hob_ref), wout_ref[FOURIER_W:FOURIER_W + HGRN_W, :], preferred_element_type=F32)
        y += jnp.dot(pick(aoa_ref, aob_ref), wout_ref[FOURIER_W + HGRN_W:, :], preferred_element_type=F32)
        xm = x_ref[...] + mod_ref[5:6, :] * y
        h = _mod_norm(xm, n2_ref[...], mod_ref[6:7, :], mod_ref[7:8, :]).astype(BF16)
        out = xm + 0.5 * mod_ref[8:9, :] * _swiglu_chunks(h, wgu_ref, wd_ref, fetch)

        if not split:
            o_refs[0][...] = out
            return

        @pl.when(is_a)
        def _():
            o_refs[0][...] = out

        @pl.when(jnp.logical_not(is_a))
        def _():
            o_refs[1][...] = out

    _first_step_fetches(body, (wgu_hbm, wd_hbm, stage_gu, stage_d, sems, layer))


def _out_ffn_call(x, mod, n2, fo, ho, ao, w_out, w_gu, w_down, *, layer, rows_per_cond, split):
    tm = FFN_TM
    n_a, n_b = fo[0].shape[0], fo[1].shape[0]
    tiles_a = n_a // tm
    if split:
        out_specs = list(_stream_specs(D_MODEL, tiles_a, tm))
        out_shape = [jax.ShapeDtypeStruct((n_a, D_MODEL), F32), jax.ShapeDtypeStruct((n_b, D_MODEL), F32)]
    else:
        out_specs = pl.BlockSpec((tm, D_MODEL), lambda i: (i, 0))
        out_shape = jax.ShapeDtypeStruct((n_a + n_b, D_MODEL), F32)
    return pl.pallas_call(
        functools.partial(_out_ffn_kernel, layer=layer, tiles_a=tiles_a, split=split),
        grid=((n_a + n_b) // tm,),
        in_specs=[
            pl.BlockSpec((tm, D_MODEL), lambda i: (i, 0)),
            _mod_spec(layer, tiles_a, tm, rows_per_cond),
            pl.BlockSpec((None, 1, D_MODEL), lambda i: (layer, 0, 0)),
            *_stream_specs(FOURIER_W, tiles_a, tm),
            *_stream_specs(HGRN_W, tiles_a, tm),
            *_stream_specs(ATTN_W, tiles_a, tm),
            _resident((MIX_W, D_MODEL), layer),
            pl.BlockSpec(memory_space=pl.ANY),
            pl.BlockSpec(memory_space=pl.ANY),
        ],
        out_specs=out_specs,
        out_shape=out_shape,
        scratch_shapes=_ffn_weight_scratch(),
        compiler_params=_params("arbitrary"),
        name="out_ffn",
    )(x, mod, n2, fo[0], fo[1], ho[0], ho[1], ao[0], ao[1], w_out, w_gu, w_down)


def _dft_tables(seq_len):
    c = np.arange(HEAD_DIM)
    ang_c = 2.0 * np.pi * np.outer(c, c) / HEAD_DIM
    eye = np.eye(FOURIER_W // HEAD_DIM)
    chan = np.concatenate([np.kron(eye, np.cos(ang_c)), np.kron(eye, np.sin(ang_c))], axis=1) / np.sqrt(HEAD_DIM)
    n = np.arange(seq_len)
    ang_l = 2.0 * np.pi * (np.outer(n, n) % seq_len) / seq_len
    pos = np.concatenate([np.cos(ang_l), -np.sin(ang_l)], axis=1) / np.sqrt(seq_len)
    return jnp.asarray(chan, dtype=F32), jnp.asarray(pos, dtype=F32)


def _fourier_rows(u_ref, chan_ref, pos_ref, o_ref, chan_bf, pos_bf, seq_len):
    @pl.when(pl.program_id(0) == 0)
    def _():
        chan_bf[...] = chan_ref[...].astype(BF16)
        pos_bf[...] = pos_ref[...].astype(BF16)

    t = jnp.dot(u_ref[...].astype(BF16), chan_bf[...], preferred_element_type=F32)
    pos = pos_bf[...]
    for r0 in range(0, u_ref.shape[0], seq_len):
        tb = t[r0:r0 + seq_len, :]
        stacked = jnp.concatenate([tb[:, :FOURIER_W], tb[:, FOURIER_W:]], axis=0).astype(BF16)
        o_ref[r0:r0 + seq_len, :] = jnp.dot(pos, stacked, preferred_element_type=F32)


def _fourier_table_specs(seq_len, index):
    specs = [pl.BlockSpec((FOURIER_W, 2 * FOURIER_W), index, pipeline_mode=pl.Buffered(1)),
             pl.BlockSpec((seq_len, 2 * seq_len), index, pipeline_mode=pl.Buffered(1))]
    scratch = [pltpu.VMEM((FOURIER_W, 2 * FOURIER_W), BF16), pltpu.VMEM((seq_len, 2 * seq_len), BF16)]
    return specs, scratch


def _split3(x):
    hi = x.astype(BF16)
    r1 = x - hi.astype(F32)
    mid = r1.astype(BF16)
    lo = (r1 - mid.astype(F32)).astype(BF16)
    return hi, mid, lo


def _level_mid(b, s, pos):
    tile, lanes = b.shape
    if s >= SUBLANES:
        b3 = b.reshape(tile // s, s, lanes)
        return jnp.broadcast_to(b3[:, pos:pos + 1, :], b3.shape).reshape(tile, lanes)
    assert s == 4
    b3 = b.reshape(tile // SUBLANES, SUBLANES, lanes)
    sub = lax.broadcasted_iota(jnp.int32, (1, SUBLANES, 1), 1)
    return jnp.where(sub < 4, b3[:, pos:pos + 1, :], b3[:, 4 + pos:5 + pos, :]).reshape(tile, lanes)


def _hgrn_kernel(*refs, seq_len, layer, has_s0, emit_state):
    refs = list(refs)
    zh_ref, lb_ref, nw_ref, tri_ref = refs[:4]
    pos = 4
    s0_ref = sf_ref = None
    if has_s0:
        s0_ref = refs[pos]
        pos += 1
    if emit_state:
        pos += 1
    ho_ref = refs[pos]
    pos += 1
    if emit_state:
        sf_ref = refs[pos]
        pos += 1
    qin_ref, dec_ref, kvt_ref, stf_ref, stb_ref, oacc_ref = refs[pos:]

    tile = HGRN_TILE
    width = HGRN_W
    n_tiles = seq_len // tile
    c_q, c_v, c_z, c_g = 0, width, 2 * width, 4 * width

    raw = lb_ref[...]
    e = jnp.exp(raw - jnp.max(raw, axis=0, keepdims=True))
    soft = e / jnp.sum(e, axis=0, keepdims=True)
    cum = soft[0]
    for l in range(1, layer + 1):
        cum = cum + soft[l]
    lbd = cum - soft[0]
    lb_floor = jnp.maximum(lbd, LB_FLOOR)
    one_m_lb = 1.0 - lbd

    lane_bwd1 = lax.broadcasted_iota(jnp.int32, (1, HG_2), 1) >= width
    lane_bwd = lax.broadcasted_iota(jnp.int32, (tile, HG_2), 1) >= width
    lane_bwd_s = lax.broadcasted_iota(jnp.int32, (HEAD_DIM, HG_2), 1) >= width
    row = lax.broadcasted_iota(jnp.int32, (tile, 1), 0)
    pair_xor = (lax.broadcasted_iota(jnp.int32, (tile, HGRN_HEADS * tile), 0)
                ^ (lax.broadcasted_iota(jnp.int32, (tile, HGRN_HEADS * tile), 1) % tile))
    ev_mask = (lax.broadcasted_iota(jnp.int32, (HGRN_HEADS * tile, width), 0) // tile
               == lax.broadcasted_iota(jnp.int32, (HGRN_HEADS * tile, width), 1) // HEAD_DIM)
    bd_mask = (lax.broadcasted_iota(jnp.int32, (width, HG_2), 0) // HEAD_DIM
               == (lax.broadcasted_iota(jnp.int32, (width, HG_2), 1) // HEAD_DIM) % HGRN_HEADS)
    lane_head = (lax.broadcasted_iota(jnp.int32, (HEAD_DIM, HG_2), 1) // HEAD_DIM) % HGRN_HEADS
    ones_bd = _head_ones(width)

    def tile_body(t, carry):
        rows = pl.ds(pl.multiple_of(t * tile, tile), tile)
        q = zh_ref[rows, c_q:c_q + width] * (HEAD_DIM ** -0.5)
        v = zh_ref[rows, c_v:c_v + width]
        z = zh_ref[rows, c_z:c_z + HG_2]
        ez = jnp.exp(-jnp.abs(z))
        r = 1.0 / (1.0 + ez)
        tr = ez * r
        pos_z = z >= 0.0
        f = lb_floor + one_m_lb * jnp.where(pos_z, r, tr)
        logf = jnp.log(f)
        kk = one_m_lb * jnp.where(pos_z, tr, r)

        pieces = jnp.concatenate(_split3(logf), axis=1)
        acc = jnp.dot(tri_ref[...], pieces, preferred_element_type=F32)
        prefix = acc[:, :HG_2] + acc[:, HG_2:2 * HG_2] + acc[:, 2 * HG_2:]
        total = prefix[tile - 1:tile, :]
        b = jnp.where(lane_bwd, total - prefix + logf, prefix)
        kk_f, kk_b = kk[:, :width], kk[:, width:]

        a0 = jnp.dot((q * (kk_f + kk_b)).astype(BF16), ones_bd, preferred_element_type=F32)
        o = a0 * v

        b2 = b * LOG2_E
        bf2, bb2 = b2[:, :width], b2[:, width:]
        scores = None
        for s in HGRN_LEVELS:
            second = (row % s) >= s // 2
            k_role = jnp.where(second, kk_b, kk_f)
            if s == 2:
                q_op = q * jnp.where(second, f[:, :width], f[:, width:])
                k_op = k_role
            else:
                mid_f = _level_mid(bf2, s, s // 2 - 1)
                mid_b = _level_mid(bb2, s, s // 2)
                e_q = jnp.where(second, bf2, bb2) - jnp.where(second, mid_f, mid_b)
                e_k = jnp.where(second, mid_b, mid_f) - jnp.where(second, bb2, bf2)
                q_op = q * jnp.exp2(e_q)
                k_op = k_role * jnp.exp2(e_k)
            qt = q_op.astype(BF16)
            kt = k_op.T.astype(BF16)
            sc = jnp.concatenate(
                [jnp.dot(qt[:, h * HEAD_DIM:(h + 1) * HEAD_DIM], kt[h * HEAD_DIM:(h + 1) * HEAD_DIM, :],
                         preferred_element_type=F32) for h in range(HGRN_HEADS)], axis=1)
            scores = sc if scores is None else jnp.where(pair_xor < s, sc, scores)
        scores = jnp.where(pair_xor == 0, 0.0, scores)
        ev = jnp.where(ev_mask, jnp.concatenate([v] * HGRN_HEADS, axis=0), 0.0).astype(BF16)
        o = o + jnp.dot(scores.astype(BF16), ev, preferred_element_type=F32)
        oacc_ref[rows, :] = o

        q2 = jnp.concatenate([q, q], axis=1)
        qin_ref[rows, :] = (q2 * jnp.exp(b)).astype(BF16)
        kout = (kk * jnp.exp(total - b)).astype(BF16)
        dec_ref[t] = jnp.exp(total)
        full = jnp.dot(v.T.astype(BF16), kout, preferred_element_type=F32)
        kvt = full[0:HEAD_DIM, :]
        for hh in range(1, HGRN_HEADS):
            kvt = jnp.where(lane_head == hh, full[hh * HEAD_DIM:(hh + 1) * HEAD_DIM, :], kvt)
        kvt_ref[t] = kvt
        return carry

    lax.fori_loop(0, n_tiles, tile_body, 0, unroll=2)

    if has_s0:
        st0 = jnp.concatenate([s0_ref[d, h] for d in range(2) for h in range(HGRN_HEADS)], axis=0).T
    else:
        st0 = jnp.zeros((HEAD_DIM, HG_2), F32)

    def scan_step(i, st):
        tf, tb = i, n_tiles - 1 - i
        stf_ref[tf] = st
        stb_ref[tb] = st
        dec = jnp.where(lane_bwd1, dec_ref[tb], dec_ref[tf])
        kv = jnp.where(lane_bwd_s, kvt_ref[tb], kvt_ref[tf])
        return st * dec + kv

    st_fin = lax.fori_loop(0, n_tiles, scan_step, st0)
    if emit_state:
        st_t = st_fin.T
        for d in range(2):
            for h in range(HGRN_HEADS):
                r0 = (HGRN_HEADS * d + h) * HEAD_DIM
                sf_ref[d, h] = st_t[r0:r0 + HEAD_DIM, :]

    def out_tile(t):
        rows = pl.ds(pl.multiple_of(t * tile, tile), tile)
        st = jnp.where(lane_bwd_s, stb_ref[t], stf_ref[t])
        bd = jnp.where(bd_mask, jnp.concatenate([st] * HGRN_HEADS, axis=0), 0.0).T.astype(BF16)
        o = oacc_ref[rows, :] + jnp.dot(qin_ref[rows, :], bd, preferred_element_type=F32)
        hg = zh_ref[rows, c_g:c_g + width]
        ho_ref[rows, :] = _head_rmsnorm(o, nw_ref[...]) * (hg * jax.nn.sigmoid(hg))

    group = 4 if n_tiles % 4 == 0 else 2

    def out_body(i, carry):
        for j in range(group):
            out_tile(group * i + j)
        return carry

    assert n_tiles % group == 0
    lax.fori_loop(0, n_tiles // group, out_body, 0)


def _hgrn_tri():
    i = np.arange(HGRN_TILE)
    return jnp.asarray((i[None, :] <= i[:, None]).astype(np.float32), dtype=BF16)


def _hgrn_call(zh, lb2, norm_w, s0, final_states, *, seq_len, layer, span):
    row0, n = span
    blk0 = row0 // seq_len
    n_seq = n // seq_len
    has_s0 = s0 is not None
    emit_state = final_states is not None
    state_block = (2, HGRN_HEADS, HEAD_DIM, HEAD_DIM)
    n_tiles = seq_len // HGRN_TILE
    in_specs = [
        pl.BlockSpec((seq_len, ZH_W), lambda b: (b + blk0, 0)),
        pl.BlockSpec((DEPTH, 1, HG_2), lambda b: (0, 0, 0)),
        pl.BlockSpec((None, 1, HGRN_W), lambda b: (layer, 0, 0)),
        pl.BlockSpec((HGRN_TILE, HGRN_TILE), lambda b: (0, 0)),
    ]
    args = [zh, lb2, norm_w, _hgrn_tri()]
    if has_s0:
        in_specs.append(pl.BlockSpec((None, None) + state_block, lambda b: (b, layer, 0, 0, 0, 0)))
        args.append(s0)
    out_specs = [pl.BlockSpec((seq_len, HGRN_W), lambda b: (b, 0))]
    out_shape = [jax.ShapeDtypeStruct((n, HGRN_W), F32)]
    aliases = {}
    if emit_state:
        aliases[len(args)] = 1
        in_specs.append(pl.BlockSpec(memory_space=pl.ANY))
        args.append(final_states)
        out_specs.append(pl.BlockSpec((None, None) + state_block, lambda b: (b, layer, 0, 0, 0, 0)))
        out_shape.append(jax.ShapeDtypeStruct(final_states.shape, F32))
    return pl.pallas_call(
        functools.partial(_hgrn_kernel, seq_len=seq_len, layer=layer, has_s0=has_s0, emit_state=emit_state),
        grid=(n_seq,),
        in_specs=in_specs,
        out_specs=out_specs,
        out_shape=out_shape,
        input_output_aliases=aliases,
        scratch_shapes=[
            pltpu.VMEM((seq_len, HG_2), BF16),
            pltpu.VMEM((n_tiles, 1, HG_2), F32),
            pltpu.VMEM((n_tiles, HEAD_DIM, HG_2), F32),
            pltpu.VMEM((n_tiles, HEAD_DIM, HG_2), F32),
            pltpu.VMEM((n_tiles, HEAD_DIM, HG_2), F32),
            pltpu.VMEM((seq_len, HGRN_W), F32),
        ],
        compiler_params=_params("arbitrary"),
        name="hgrn",
    )(*args)


def _sink_row(sink_ref, kv_head, rows):
    grp = lax.broadcasted_iota(jnp.int32, (1, ATTN_GROUP * rows), 1) // rows
    base = kv_head * ATTN_GROUP
    out = jnp.broadcast_to(sink_ref[:, base:base + 1], (1, ATTN_GROUP * rows))
    for g in range(1, ATTN_GROUP):
        out = jnp.where(grp == g, sink_ref[:, base + g:base + g + 1], out)
    return out


def _group_queries_t(qt, kv_head):
    base = kv_head * ATTN_GROUP
    return jnp.concatenate([qt[(base + g) * HEAD_DIM:(base + g + 1) * HEAD_DIM, :] for g in range(ATTN_GROUP)], axis=1)


def _ctx_attn_kernel(zq_ref, zkv_ref, zu_ref, qn_ref, kn_ref, sink_ref, chan_ref, pos_ref, kc_all_ref, vc_all_ref,
                     ao_ref, fo_ref, kc_ref, vc_ref, chan_bf, pos_bf, *, seq_len):
    del kc_all_ref, vc_all_ref
    _fourier_rows(zu_ref, chan_ref, pos_ref, fo_ref, chan_bf, pos_bf, seq_len)
    rows = seq_len
    q_all = _head_rmsnorm(zq_ref[...], qn_ref[...]) * (HEAD_DIM ** -0.5)
    k_all = _head_rmsnorm(zkv_ref[:, :KV_W], kn_ref[...])
    v_all = zkv_ref[:, KV_W:]
    kc_ref[...] = k_all.reshape(kc_ref.shape)
    vc_ref[...] = v_all.reshape(vc_ref.shape)
    starts = list(range(0, zq_ref.shape[0], seq_len))
    heads = range(ATTN_KV_HEADS)
    hsl = [slice(kh * HEAD_DIM, (kh + 1) * HEAD_DIM) for kh in heads]
    qt = [q_all[r0:r0 + rows, :].T.astype(BF16) for r0 in starts]
    kb = [k_all[r0:r0 + rows, :].astype(BF16) for r0 in starts]
    vt = [v_all[r0:r0 + rows, :].T.astype(BF16) for r0 in starts]
    pairs = [(i, kh) for i in range(len(starts)) for kh in heads]
    sink = [_sink_row(sink_ref, kh, rows) for kh in heads]
    s = {pk: jnp.dot(kb[pk[0]][:, hsl[pk[1]]], _group_queries_t(qt[pk[0]], pk[1]), preferred_element_type=F32)
         for pk in pairs}
    m = {pk: jnp.maximum(jnp.max(s[pk], axis=0, keepdims=True), sink[pk[1]]) for pk in pairs}
    p = {pk: jnp.exp(s[pk] - m[pk]) for pk in pairs}
    denom = {pk: jnp.sum(p[pk], axis=0, keepdims=True) + jnp.exp(sink[pk[1]] - m[pk]) for pk in pairs}
    for i, r0 in enumerate(starts):
        out_t = []
        for kh in heads:
            o_t = jnp.dot(vt[i][hsl[kh], :], p[(i, kh)].astype(BF16), preferred_element_type=F32) * (1.0 / denom[(i, kh)])
            out_t.extend(o_t[:, g * rows:(g + 1) * rows] for g in range(ATTN_GROUP))
        ao_ref[r0:r0 + rows, :] = jnp.concatenate(out_t, axis=0).T


def _ctx_attn_call(zq, zkv, zu, q_norm, k_norm, sink, k_cache, v_cache, *, seq_len, layer, span):
    row0, n = span
    rows = max(seq_len, SMALL_SEQ_ROWS)
    blk0 = row0 // rows
    seqs = rows // seq_len
    cache_spec = pl.BlockSpec((seqs, None, seq_len, KV_W), lambda b: (b, layer, 0, 0))
    table_specs, table_scratch = _fourier_table_specs(seq_len, lambda b: (0, 0))
    return pl.pallas_call(
        functools.partial(_ctx_attn_kernel, seq_len=seq_len),
        grid=(n // rows,),
        in_specs=[
            pl.BlockSpec((rows, ATTN_W), lambda b: (b + blk0, 0)),
            pl.BlockSpec((rows, 2 * KV_W), lambda b: (b + blk0, 0)),
            pl.BlockSpec((rows, FOURIER_W), lambda b: (b + blk0, 0)),
            pl.BlockSpec((None, 1, ATTN_W), lambda b: (layer, 0, 0)),
            pl.BlockSpec((None, 1, KV_W), lambda b: (layer, 0, 0)),
            pl.BlockSpec((None, 1, ATTN_HEADS), lambda b: (layer, 0, 0)),
            *table_specs,
            pl.BlockSpec(memory_space=pl.ANY),
            pl.BlockSpec(memory_space=pl.ANY),
        ],
        out_specs=[pl.BlockSpec((rows, ATTN_W), lambda b: (b, 0)), pl.BlockSpec((rows, FOURIER_W), lambda b: (b, 0)),
                   cache_spec, cache_spec],
        out_shape=[
            jax.ShapeDtypeStruct((n, ATTN_W), F32),
            jax.ShapeDtypeStruct((n, FOURIER_W), F32),
            jax.ShapeDtypeStruct(k_cache.shape, F32),
            jax.ShapeDtypeStruct(v_cache.shape, F32),
        ],
        input_output_aliases={8: 2, 9: 3},
        scratch_shapes=table_scratch,
        compiler_params=_params("arbitrary"),
        name="ctx_attn",
    )(zq, zkv, zu, q_norm, k_norm, sink, *_dft_tables(seq_len), k_cache, v_cache)


def _rope_tables(seq_len, n_heads):
    t = np.arange(seq_len)
    row, col = t // GRID_W, t % GRID_W
    half = HEAD_DIM // 2
    inv = (ROPE_BASE ** (-np.arange(0, half, 2, dtype=np.float32) / half)).astype(np.float32)

    def one(pos):
        ang = pos.astype(np.float32)[:, None] * inv[None]
        c, s = np.cos(ang), np.sin(ang)
        return np.concatenate([c, c], axis=1), np.concatenate([-s, s], axis=1)

    cr, sr = one(row)
    cc, sc = one(col)
    cos = np.tile(np.concatenate([cr, cc], axis=1), (1, n_heads)).astype(np.float32)
    sin = np.tile(np.concatenate([sr, sc], axis=1), (1, n_heads)).astype(np.float32)
    return jnp.asarray(cos), jnp.asarray(sin)


def _rope(x, cos, sin_signed):
    width = x.shape[-1]
    lane = lax.broadcasted_iota(jnp.int32, x.shape, 1)
    quarter = HEAD_DIM // 4
    up = pltpu.roll(x, width - quarter, axis=1)
    down = pltpu.roll(x, quarter, axis=1)
    partner = jnp.where((lane % (2 * quarter)) < quarter, up, down)
    return x * cos + partner * sin_signed


def _lat_attn_kernel(zq_ref, zkv_ref, zu_ref, kc_ref, vc_ref, qn_ref, kn_ref, sink_ref,
                     cosq_ref, sinq_ref, cosk_ref, sink_tab_ref, band_ref, chan_ref, pos_ref,
                     ao_ref, fo_ref, kb_ref, kcb_ref, vct_ref, chan_bf, pos_bf, stk_ref, *, seq_len):
    rows = ATTN_BLOCK

    @pl.when(pl.program_id(0) == 0)
    def _():
        chan_bf[...] = chan_ref[...].astype(BF16)
        pos_bf[...] = pos_ref[...].astype(BF16)

    t = jnp.dot(zu_ref[...].astype(BF16), chan_bf[...], preferred_element_type=F32)
    stk_ref[...] = jnp.concatenate([t[:, :FOURIER_W], t[:, FOURIER_W:]], axis=0).astype(BF16)

    k_all = _head_rmsnorm(zkv_ref[:, 0:KV_W], kn_ref[...])
    kb_ref[...] = _rope(k_all, cosk_ref[...], sink_tab_ref[...]).astype(BF16)
    kcb_ref[...] = kc_ref[...].astype(BF16)
    vct_ref[...] = vc_ref[...].T.astype(BF16)

    def query_block(blk, carry):
        qrows = pl.ds(pl.multiple_of(blk * ATTN_BLOCK, ATTN_BLOCK), ATTN_BLOCK)
        fo_ref[qrows, :] = jnp.dot(pos_bf[qrows, :], stk_ref[...], preferred_element_type=F32)
        q = _head_rmsnorm(zq_ref[qrows, :], qn_ref[...])
        q = _rope(q, cosq_ref[qrows, :], sinq_ref[qrows, :]) * (HEAD_DIM ** -0.5)
        start = jnp.clip((blk - 1) * ATTN_BLOCK, 0, seq_len - WIN_KEYS)
        start = pl.multiple_of(start, ATTN_BLOCK)
        vwin = zkv_ref[pl.ds(start, WIN_KEYS), KV_W:2 * KV_W]

        band_bias = band_ref[blk - start // ATTN_BLOCK]

        qt = q.T.astype(BF16)
        kwb = kb_ref[pl.ds(start, WIN_KEYS), :]
        kcb = kcb_ref[...]
        vwt = vwin.T.astype(BF16)
        vct = vct_ref[...]
        heads = range(ATTN_KV_HEADS)
        hsl = [slice(kh * HEAD_DIM, (kh + 1) * HEAD_DIM) for kh in heads]
        qg = [_group_queries_t(qt, kh) for kh in heads]
        s_loc = [jnp.dot(kwb[:, hsl[kh]], qg[kh], preferred_element_type=F32) + band_bias for kh in heads]
        s_ctx = [jnp.dot(kcb[:, hsl[kh]], qg[kh], preferred_element_type=F32) for kh in heads]
        sink = [_sink_row(sink_ref, kh, rows) for kh in heads]
        m = [jnp.maximum(jnp.maximum(jnp.max(s_loc[kh], axis=0, keepdims=True),
                                     jnp.max(s_ctx[kh], axis=0, keepdims=True)), sink[kh]) for kh in heads]
        p_loc = [jnp.exp(s_loc[kh] - m[kh]) for kh in heads]
        p_ctx = [jnp.exp(s_ctx[kh] - m[kh]) for kh in heads]
        denom = [jnp.sum(p_loc[kh], axis=0, keepdims=True) + jnp.sum(p_ctx[kh], axis=0, keepdims=True)
                 + jnp.exp(sink[kh] - m[kh]) for kh in heads]
        out_t = []
        for kh in heads:
            o_t = (jnp.dot(vwt[hsl[kh], :], p_loc[kh].astype(BF16), preferred_element_type=F32)
                   + jnp.dot(vct[hsl[kh], :], p_ctx[kh].astype(BF16), preferred_element_type=F32)) * (1.0 / denom[kh])
            out_t.extend(o_t[:, g * rows:(g + 1) * rows] for g in range(ATTN_GROUP))
        ao_ref[qrows, :] = jnp.concatenate(out_t, axis=0).T
        return carry

    lax.fori_loop(0, seq_len // ATTN_BLOCK, query_block, 0, unroll=4)


def _lat_attn_call(zq, zkv, zu, cache_k, cache_v, q_norm, k_norm, sink, *, seq_len, layer, span):
    row0, n = span
    seq0 = row0 // seq_len
    past = cache_k.shape[2]
    cosq, sinq = _rope_tables(seq_len, ATTN_HEADS)
    cosk, sink_tab = _rope_tables(seq_len, ATTN_KV_HEADS)
    kr = np.arange(WIN_KEYS)[:, None]
    qi = np.arange(ATTN_GROUP * ATTN_BLOCK)[None, :] % ATTN_BLOCK
    band = np.stack([np.where(np.abs(qi + off * ATTN_BLOCK - kr) <= WINDOW, 0.0, NEG_BIG) for off in range(3)])
    band = jnp.asarray(band, dtype=F32)
    table_specs, table_scratch = _fourier_table_specs(seq_len, lambda b: (0, 0))
    return pl.pallas_call(
        functools.partial(_lat_attn_kernel, seq_len=seq_len),
        grid=(n // seq_len,),
        in_specs=[
            pl.BlockSpec((seq_len, ATTN_W), lambda b: (seq0 + b, 0)),
            pl.BlockSpec((seq_len, 2 * KV_W), lambda b: (seq0 + b, 0)),
            pl.BlockSpec((seq_len, FOURIER_W), lambda b: (seq0 + b, 0)),
            pl.BlockSpec((None, None, past, KV_W), lambda b: (b, layer, 0, 0)),
            pl.BlockSpec((None, None, past, KV_W), lambda b: (b, layer, 0, 0)),
            pl.BlockSpec((None, 1, ATTN_W), lambda b: (layer, 0, 0)),
            pl.BlockSpec((None, 1, KV_W), lambda b: (layer, 0, 0)),
            pl.BlockSpec((None, 1, ATTN_HEADS), lambda b: (layer, 0, 0)),
            pl.BlockSpec((seq_len, ATTN_W), lambda b: (0, 0), pipeline_mode=pl.Buffered(1)),
            pl.BlockSpec((seq_len, ATTN_W), lambda b: (0, 0), pipeline_mode=pl.Buffered(1)),
            pl.BlockSpec((seq_len, KV_W), lambda b: (0, 0), pipeline_mode=pl.Buffered(1)),
            pl.BlockSpec((seq_len, KV_W), lambda b: (0, 0), pipeline_mode=pl.Buffered(1)),
            pl.BlockSpec((3, WIN_KEYS, ATTN_GROUP * ATTN_BLOCK), lambda b: (0, 0, 0), pipeline_mode=pl.Buffered(1)),
            *table_specs,
        ],
        out_specs=[pl.BlockSpec((seq_len, ATTN_W), lambda b: (b, 0)), pl.BlockSpec((seq_len, FOURIER_W), lambda b: (b, 0))],
        out_shape=[jax.ShapeDtypeStruct((n, ATTN_W), F32), jax.ShapeDtypeStruct((n, FOURIER_W), F32)],
        scratch_shapes=[
            pltpu.VMEM((seq_len, KV_W), BF16),
            pltpu.VMEM((past, KV_W), BF16),
            pltpu.VMEM((KV_W, past), BF16),
            *table_scratch,
            pltpu.VMEM((2 * seq_len, FOURIER_W), BF16),
        ],
        compiler_params=_params("arbitrary"),
        name="lat_attn",
    )(zq, zkv, zu, cache_k, cache_v, q_norm, k_norm, sink, cosq, sinq, cosk, sink_tab, band, *_dft_tables(seq_len))


def kernel(x_prompt, x_sample, c, cache_attn_k, cache_attn_v, state_hgrn, c_ctx, w_ada, b_ada, norm_ffn1, norm_mix, norm_ffn2, ffn1_w_gate_up, ffn1_w_down, ffn2_w_gate_up, ffn2_w_down, w_in, w_out, hgrn_lower_bounds, hgrn_norm, q_norm, k_norm, attn_sink):
    batch, seq, _ = x_prompt.shape
    dec_batch, dec_seq, _ = x_sample.shape
    past = cache_attn_k.shape[2]
    assert dec_batch + 1 <= ADA_ROWS

    cond = jnp.concatenate([c_ctx[None], c, jnp.zeros((ADA_ROWS - 1 - dec_batch, D_MODEL), F32)], axis=0)
    mod = _ada_call(cond, w_ada, b_ada).reshape(DEPTH, ADA_ROWS, N_MOD, D_MODEL)

    bf = lambda w: w.astype(BF16)
    w_gu1, w_d1, w_gu2, w_d2 = ffn1_w_gate_up, ffn1_w_down, ffn2_w_gate_up, ffn2_w_down
    w_in_b, w_out_b = bf(w_in), bf(w_out)
    row3 = lambda w: w.reshape(DEPTH, 1, w.shape[-1])
    n1, nm, n2 = row3(norm_ffn1), row3(norm_mix), row3(norm_ffn2)
    hn = row3(hgrn_norm)
    qn = row3(jnp.tile(q_norm, (1, ATTN_HEADS)))
    kn = row3(jnp.tile(k_norm, (1, ATTN_KV_HEADS)))
    sink = row3(attn_sink)
    cache_k = cache_attn_k.reshape(dec_batch, DEPTH, past, KV_W)
    cache_v = cache_attn_v.reshape(dec_batch, DEPTH, past, KV_W)
    lb2 = hgrn_lower_bounds.reshape(DEPTH, 1, HG_2)

    n_ctx, n_lat = batch * seq, dec_batch * dec_seq
    span_c, span_l = (0, n_ctx), (n_ctx, n_lat)
    xs = (x_prompt.reshape(n_ctx, D_MODEL), x_sample.reshape(n_lat, D_MODEL))
    new_k = jnp.zeros((batch, DEPTH, seq, KV_W), F32)
    new_v = jnp.zeros((batch, DEPTH, seq, KV_W), F32)
    new_s = jnp.zeros((batch, DEPTH, 2, HGRN_HEADS, HEAD_DIM, HEAD_DIM), F32)
    for layer in range(DEPTH):
        x1, zu, zh, zq, zkv = _ffn_in_call(xs, mod, n1, nm, w_gu1, w_d1, w_in_b,
                                           layer=layer, rows_a=n_ctx, rows_per_cond=dec_seq)
        ho_c, new_s = _hgrn_call(zh, lb2, hn, None, new_s, seq_len=seq, layer=layer, span=span_c)
        (ho_l,) = _hgrn_call(zh, lb2, hn, state_hgrn, None, seq_len=dec_seq, layer=layer, span=span_l)
        ao_c, fo_c, new_k, new_v = _ctx_attn_call(zq, zkv, zu, qn, kn, sink, new_k, new_v,
                                                  seq_len=seq, layer=layer, span=span_c)
        ao_l, fo_l = _lat_attn_call(zq, zkv, zu, cache_k, cache_v, qn, kn, sink, seq_len=dec_seq, layer=layer, span=span_l)
        last = layer == DEPTH - 1
        xs = _out_ffn_call(x1, mod, n2, (fo_c, fo_l), (ho_c, ho_l), (ao_c, ao_l), w_out_b, w_gu2, w_d2,
                           layer=layer, rows_per_cond=dec_seq, split=last)
        xs = tuple(xs) if last else (xs,)

    y_p = xs[0].reshape(batch, seq, D_MODEL)
    y_s = xs[1].reshape(dec_batch, dec_seq, D_MODEL)
    cache_shape = (batch, DEPTH, seq, ATTN_KV_HEADS, HEAD_DIM)
    return (y_p, y_s, new_k.reshape(cache_shape), new_v.reshape(cache_shape), new_s)
```
